```python
import math
import jax, jax.numpy as jnp
from jax import lax
import numpy as np


D_MODEL = 1024
BATCH = 8
SEQ = 4096
DEPTH = 1

A_HEADS = 8
A_HEAD_DIM = 64
A_WIDTH = A_HEADS * A_HEAD_DIM
IDX_HEADS = 16
IDX_DIM = 32
TOPK_MAX = 256
QBLK = 128
REL_BUCKETS = 32
REL_MAX_DIST = 128
G_HEADS = 4
G_DK = 64
G_DV = 128
G_KW = G_HEADS * G_DK
G_VW = G_HEADS * G_DV
G_RANK = 16
G_TAU = 16.0
G_CHUNK = 64
N_GROUPS = 4
EXPERTS_PER_GROUP = 8
N_EXPERTS = N_GROUPS * EXPERTS_PER_GROUP
EXPERT_TOP_K = 2
D_EXPERT = 256
DN_ALPHA = (2.0 * DEPTH) ** 0.25
DN_BETA = (8.0 * DEPTH) ** -0.25
LN_EPS = 1e-5

SPLIT_SIZES = (A_WIDTH, A_WIDTH, A_WIDTH, IDX_HEADS * IDX_DIM, IDX_DIM, IDX_HEADS,
               G_KW, G_KW, G_VW, G_VW, G_RANK, D_MODEL, D_MODEL)
VALUE_SEGMENTS = (2, 8)
SPLIT_OFFSETS = tuple(int(v) for v in np.cumsum(SPLIT_SIZES)[:-1])
D_IN_PROJ = int(sum(SPLIT_SIZES))

kernel_name = "hybrid_dsa_gla_hmoe_deepnorm_adaln"


def layer_norm(x, g=None, b=None):
    xf = x.astype(jnp.float32)
    mu = jnp.mean(xf, axis=-1, keepdims=True)
    var = jnp.mean(jnp.square(xf - mu), axis=-1, keepdims=True)
    y = (xf - mu) * lax.rsqrt(var + LN_EPS)
    if g is not None:
        y = y * g.astype(jnp.float32) + b.astype(jnp.float32)
    return y.astype(x.dtype)


def modulate(x, shift, scale):
    return layer_norm(x) * (1.0 + scale[:, None, :]) + shift[:, None, :]


def t5_bucket(dist):
    max_exact = REL_BUCKETS // 2
    d_f = jnp.maximum(dist, 1).astype(jnp.float32)
    large = max_exact + (jnp.log(d_f / max_exact) / math.log(REL_MAX_DIST / max_exact)
                         * (REL_BUCKETS - max_exact)).astype(jnp.int32)
    large = jnp.minimum(large, REL_BUCKETS - 1)
    return jnp.where(dist < max_exact, dist, large)


def dsa_sparse_attention(q, k, v, iq, ik, iw, rel_bias):
    B, S, H, Dh = q.shape
    topk = min(TOPK_MAX, S // 4)
    n_blk = S // QBLK
    scale = Dh ** -0.5
    iw = iw * (IDX_HEADS ** -0.5 * IDX_DIM ** -0.5)
    key_pos = jnp.arange(S, dtype=jnp.int32)
    gather = jax.vmap(lambda arr, idx: arr[idx])

    def block(i):
        q0 = i * QBLK
        qb = lax.dynamic_slice_in_dim(q, q0, QBLK, axis=1)
        iqb = lax.dynamic_slice_in_dim(iq, q0, QBLK, axis=1)
        iwb = lax.dynamic_slice_in_dim(iw, q0, QBLK, axis=1)
        q_pos = q0 + jnp.arange(QBLK, dtype=jnp.int32)
        idx_logits = jnp.einsum('bqhd,bsd->bqhs', iqb, ik)
        score = jnp.einsum('bqh,bqhs->bqs', iwb, jax.nn.relu(idx_logits)).astype(jnp.float32)
        causal = key_pos[None, :] <= q_pos[:, None]
        score = jnp.where(causal[None], score, -jnp.inf)
        _, sel = lax.top_k(score, topk)
        valid = sel <= q_pos[None, :, None]
        k_sel = gather(k, sel)
        v_sel = gather(v, sel)
        logits = jnp.einsum('bqhd,bqkhd->bhqk', qb, k_sel).astype(jnp.float32) * scale
        bucket = t5_bucket(jnp.maximum(q_pos[None, :, None] - sel, 0))
        bias = rel_bias[bucket]
        logits = logits + jnp.transpose(bias, (0, 3, 1, 2)).astype(jnp.float32)
        logits = jnp.where(valid[:, None], logits, -jnp.inf)
        p = jax.nn.softmax(logits, axis=-1).astype(v.dtype)
        return jnp.einsum('bhqk,bqkhd->bqhd', p, v_sel)

    out = lax.map(block, jnp.arange(n_blk, dtype=jnp.int32))
    return jnp.transpose(out, (1, 0, 2, 3, 4)).reshape(B, S, H, Dh)


def gla_chunked(q, k, v, log_g):
    B, S, H, Dk = q.shape
    Dv = v.shape[-1]
    C = G_CHUNK
    N = S // C

    def chunks(t):
        return jnp.transpose(t.astype(jnp.float32).reshape(B, N, C, H, -1), (0, 3, 1, 2, 4))

    qc = chunks(q) * (Dk ** -0.5)
    kc = chunks(k)
    vc = chunks(v)
    b = jnp.cumsum(chunks(log_g), axis=3)
    b_last = b[:, :, :, -1:, :]
    q_in = qc * jnp.exp(b)
    k_st = kc * jnp.exp(b_last - b)
    q_rel = qc * jnp.exp(b - b_last)
    tril = jnp.tril(jnp.ones((C, C), dtype=bool))
    att = jnp.einsum('bhncd,bhnjd->bhncj', q_rel, k_st)
    att = jnp.where(tril, att, 0.0)
    o_intra = jnp.einsum('bhncj,bhnje->bhnce', att, vc)
    u = jnp.einsum('bhncd,bhnce->bhnde', k_st, vc)
    decay = jnp.exp(b_last[:, :, :, 0, :])

    def step(state, inp):
        dec, uu = inp
        return dec[..., None] * state + uu, state

    _, s_prev = lax.scan(step, jnp.zeros((B, H, Dk, Dv), jnp.float32),
                         (jnp.moveaxis(decay, 2, 0), jnp.moveaxis(u, 2, 0)))
    s_prev = jnp.moveaxis(s_prev, 0, 2)
    o_inter = jnp.einsum('bhncd,bhnde->bhnce', q_in, s_prev)
    o = o_intra + o_inter
    return jnp.transpose(o, (0, 2, 3, 1, 4)).reshape(B, S, H, Dv)


def hier_moe(h, w_rg, b_rg, w_re, b_re, w1, w3, w2):
    B, S, D = h.shape
    t = h.reshape(B * S, D)
    g_prob = jax.nn.softmax((t @ w_rg + b_rg).astype(jnp.float32), axis=-1)
    g_w, g_idx = lax.top_k(g_prob, 1)
    e_logits = (t @ w_re + b_re).astype(jnp.float32).reshape(B * S, N_GROUPS, EXPERTS_PER_GROUP)
    g_onehot = jax.nn.one_hot(g_idx[:, 0], N_GROUPS, dtype=jnp.float32)
    e_in_group = jnp.einsum('tg,tge->te', g_onehot, e_logits)
    e_top, e_idx = lax.top_k(e_in_group, EXPERT_TOP_K)
    e_w = jax.nn.softmax(e_top, axis=-1) * g_w
    expert_id = g_idx * EXPERTS_PER_GROUP + e_idx
    gate = jnp.sum(jax.nn.one_hot(expert_id, N_EXPERTS, dtype=jnp.float32) * e_w[..., None], axis=1)
    gate = gate.astype(h.dtype)
    y = jnp.zeros_like(t)
    for e in range(N_EXPERTS):
        hid = jax.nn.silu(t @ w1[e]) * (t @ w3[e])
        y = y + gate[:, e:e + 1] * (hid @ w2[e])
    return y.reshape(B, S, D)


def setup_inputs(seed: int = 0) -> dict:
    key = jax.random.key(seed)
    ks = jax.random.split(key, 24)
    f32 = jnp.float32
    D = D_MODEL
    nrm = lambda k, shape, s: jax.random.normal(k, shape, f32) * s
    col_scale = jnp.concatenate([jnp.full((n,), DN_BETA if i in VALUE_SEGMENTS else 1.0, f32)
                                 for i, n in enumerate(SPLIT_SIZES)])
    return {
        'x': nrm(ks[0], (BATCH, SEQ, D), 1.0),
        'c': nrm(ks[1], (BATCH, D), 1.0),
        'rel_bias': nrm(ks[2], (REL_BUCKETS, A_HEADS), 0.5),
        'w_ada': nrm(ks[3], (DEPTH, D, 6 * D), 0.5 * D ** -0.5),
        'b_ada': nrm(ks[4], (DEPTH, 6 * D), 0.01),
        'w_in': nrm(ks[5], (DEPTH, D, D_IN_PROJ), D ** -0.5) * col_scale,
        'gla_w_gate': nrm(ks[6], (DEPTH, G_RANK, G_KW), G_RANK ** -0.5),
        'gla_b_gate': nrm(ks[7], (DEPTH, G_KW), 0.1),
        'gla_norm_g': 1.0 + nrm(ks[8], (DEPTH, G_VW), 0.02),
        'w_branch_a': nrm(ks[9], (DEPTH, A_WIDTH, D), A_WIDTH ** -0.5 * DN_BETA),
        'w_branch_b': nrm(ks[10], (DEPTH, G_VW, D), G_VW ** -0.5 * DN_BETA),
        'w_out': nrm(ks[11], (DEPTH, D, D), D ** -0.5 * DN_BETA),
        'ln1_g': 1.0 + nrm(ks[12], (DEPTH, D), 0.02),
        'ln1_b': nrm(ks[13], (DEPTH, D), 0.02),
        'w_router_group': nrm(ks[14], (DEPTH, D, N_GROUPS), D ** -0.5),
        'b_router_group': nrm(ks[15], (DEPTH, N_GROUPS), 0.01),
        'w_router_expert': nrm(ks[16], (DEPTH, D, N_EXPERTS), D ** -0.5),
        'b_router_expert': nrm(ks[17], (DEPTH, N_EXPERTS), 0.01),
        'w_exp_gate': nrm(ks[18], (DEPTH, N_EXPERTS, D, D_EXPERT), D ** -0.5),
        'w_exp_up': nrm(ks[19], (DEPTH, N_EXPERTS, D, D_EXPERT), D ** -0.5),
        'w_exp_down': nrm(ks[20], (DEPTH, N_EXPERTS, D_EXPERT, D), D_EXPERT ** -0.5 * DN_BETA),
        'ln2_g': 1.0 + nrm(ks[21], (DEPTH, D), 0.02),
        'ln2_b': nrm(ks[22], (DEPTH, D), 0.02),
    }


def reference(x, c, rel_bias, w_ada, b_ada, w_in, gla_w_gate, gla_b_gate, gla_norm_g,
              w_branch_a, w_branch_b, w_out, ln1_g, ln1_b, w_router_group, b_router_group,
              w_router_expert, b_router_expert, w_exp_gate, w_exp_up, w_exp_down, ln2_g, ln2_b):
    B, S, D = x.shape
    cond = jax.nn.silu(c)
    for l in range(DEPTH):
        ada = cond @ w_ada[l] + b_ada[l]
        sh1, sc1, gt1, sh2, sc2, gt2 = jnp.split(ada, 6, axis=-1)
        h = modulate(x, sh1, sc1)
        proj = h @ w_in[l]
        (aq, ak, av, iq, ik, iw, gq, gk, gv, gr, glr, gate_a, gate_b) = jnp.split(proj, SPLIT_OFFSETS, axis=-1)
        o_a = dsa_sparse_attention(aq.reshape(B, S, A_HEADS, A_HEAD_DIM),
                                   ak.reshape(B, S, A_HEADS, A_HEAD_DIM),
                                   av.reshape(B, S, A_HEADS, A_HEAD_DIM),
                                   iq.reshape(B, S, IDX_HEADS, IDX_DIM), ik, iw,
                                   rel_bias).reshape(B, S, A_WIDTH)
        log_g = jax.nn.log_sigmoid((glr @ gla_w_gate[l] + gla_b_gate[l]).astype(jnp.float32)) / G_TAU
        o_b = gla_chunked(gq.reshape(B, S, G_HEADS, G_DK), gk.reshape(B, S, G_HEADS, G_DK),
                          gv.reshape(B, S, G_HEADS, G_DV), log_g.reshape(B, S, G_HEADS, G_DK))
        o_b = (layer_norm(o_b) * gla_norm_g[l].reshape(G_HEADS, G_DV).astype(jnp.float32)).astype(x.dtype)
        o_b = (o_b * jax.nn.silu(gr.reshape(B, S, G_HEADS, G_DV))).reshape(B, S, G_VW)
        merged = (jax.nn.sigmoid(gate_a) * (o_a @ w_branch_a[l])
                  + jax.nn.sigmoid(gate_b) * (o_b @ w_branch_b[l]))
        y = merged @ w_out[l]
        x = layer_norm(DN_ALPHA * x + gt1[:, None, :] * y, ln1_g[l], ln1_b[l])
        h2 = modulate(x, sh2, sc2)
        f = hier_moe(h2, w_router_group[l], b_router_group[l], w_router_expert[l], b_router_expert[l],
                     w_exp_gate[l], w_exp_up[l], w_exp_down[l])
        x = layer_norm(DN_ALPHA * x + gt2[:, None, :] * f, ln2_g[l], ln2_b[l])
    return x
```

```python
import functools
import math

import numpy as np
import jax
import jax.numpy as jnp
from jax import lax
from jax.experimental import pallas as pl
from jax.experimental.pallas import tpu as pltpu

F32 = jnp.float32
BF16 = jnp.bfloat16
I32 = jnp.int32
HIGHEST = lax.Precision.HIGHEST

A_HEADS = 8
A_HEAD_DIM = 64
A_WIDTH = A_HEADS * A_HEAD_DIM
IDX_HEADS = 16
IDX_DIM = 32
TOPK_MAX = 256
QBLK = 128
REL_BUCKETS = 32
REL_MAX_DIST = 128
G_HEADS = 4
G_DK = 64
G_DV = 128
G_KW = G_HEADS * G_DK
G_VW = G_HEADS * G_DV
G_RANK = 16
G_TAU = 16.0
G_CHUNK = 64
N_GROUPS = 4
EXPERTS_PER_GROUP = 8
N_EXPERTS = N_GROUPS * EXPERTS_PER_GROUP
D_EXPERT = 256
DEPTH = 1
DN_ALPHA = (2.0 * DEPTH) ** 0.25
LN_EPS = 1e-5
SPLIT_SIZES = (A_WIDTH, A_WIDTH, A_WIDTH, IDX_HEADS * IDX_DIM, IDX_DIM, IDX_HEADS,
               G_KW, G_KW, G_VW, G_VW, G_RANK, 1024, 1024)

LANES = 128
VMEM_LIMIT_BYTES = 56 * 1024 * 1024

NEG = -1e30
INT_MIN = -2 ** 31
INT_MAX = 2 ** 31 - 1

NT = (((1,), (1,)), ((), ()))
TN = (((0,), (0,)), ((), ()))


def _ln(x):
    mu = jnp.mean(x, axis=-1, keepdims=True)
    xc = x - mu
    var = jnp.mean(xc * xc, axis=-1, keepdims=True)
    return xc * lax.rsqrt(var + LN_EPS)


def _sigmoid(x):
    return 1.0 / (1.0 + jnp.exp(-x))


def _params(*sem):
    return pltpu.CompilerParams(dimension_semantics=sem, vmem_limit_bytes=VMEM_LIMIT_BYTES)


def _ada_body(c_ref, w_ref, b_ref, o_ref):
    c = c_ref[...]
    cond = c * _sigmoid(c)
    o_ref[...] = jnp.dot(cond, w_ref[...], preferred_element_type=F32, precision=HIGHEST) + b_ref[...]


def _ada_call(c, w, b):
    B, D = c.shape
    N = w.shape[1]
    tn = 1536
    return pl.pallas_call(
        _ada_body,
        grid=(N // tn,),
        in_specs=[pl.BlockSpec((B, D), lambda j: (0, 0)),
                  pl.BlockSpec((D, tn), lambda j: (0, j)),
                  pl.BlockSpec((1, tn), lambda j: (0, j))],
        out_specs=pl.BlockSpec((B, tn), lambda j: (0, j)),
        out_shape=jax.ShapeDtypeStruct((B, N), F32),
        compiler_params=_params("arbitrary"),
        name="ada",
    )(c, w, b.reshape(1, N))


TOK_K = (0, 512)
TOK_GLA = (512, 2048)
TOK_GATES = (2048, 4096)
TOK_SMALL = (4096, 4224)
CH_Q = (0, 512)
CH_V = (512, 1024)
CH_IQ = (1024, 1536)
CH_IW = (1536, 1552)
IW_SCALE = IDX_HEADS ** -0.5 * IDX_DIM ** -0.5


def _inproj_body(x_ref, sh_ref, sc_ref, wtok_ref, wch_ref,
                 k_ref, gla_ref, gates_ref, ik_ref, glr_ref, qT_ref, vT_ref, iqT_ref, iwT_ref):
    tm = x_ref.shape[1]
    h = (_ln(x_ref[0]) * (1.0 + sc_ref[0]) + sh_ref[0]).astype(BF16)

    def tok(ab):
        return jnp.dot(h, wtok_ref[:, ab[0]:ab[1]], preferred_element_type=F32)

    def ch(ab):
        return lax.dot_general(wch_ref[ab[0]:ab[1], :], h, NT, preferred_element_type=F32)

    kres = tok(TOK_K)
    for p in range(A_WIDTH // LANES):
        k_ref[0, p] = kres[:, p * LANES:(p + 1) * LANES].astype(BF16)
    gla_ref[0] = tok(TOK_GLA).astype(BF16)
    gates_ref[0] = tok(TOK_GATES).astype(BF16)
    small = tok(TOK_SMALL)
    ik_ref[0] = small[:, :IDX_DIM].astype(BF16)
    glr_ref[0] = small[:, IDX_DIM:IDX_DIM + G_RANK]

    qT_ref[0] = (ch(CH_Q) * (A_HEAD_DIM ** -0.5)).astype(BF16)
    vres = ch(CH_V).astype(BF16)
    for j in range(tm // LANES):
        vT_ref[0, j] = vres[:, j * LANES:(j + 1) * LANES]
    iqT_ref[0] = ch(CH_IQ).astype(BF16)
    iwT_ref[0] = ch(CH_IW) * IW_SCALE


def _inproj_call(x, sh1, sc1, w_tok, w_ch, tm):
    B, S, D = x.shape
    nt = S // tm
    const = lambda b, t: (0, 0)
    out_shape = (
        jax.ShapeDtypeStruct((B, A_WIDTH // LANES, S, LANES), BF16),
        jax.ShapeDtypeStruct((B, S, 1536), BF16),
        jax.ShapeDtypeStruct((B, S, 2048), BF16),
        jax.ShapeDtypeStruct((B, S, IDX_DIM), BF16),
        jax.ShapeDtypeStruct((B, S, G_RANK), F32),
        jax.ShapeDtypeStruct((B, A_WIDTH, S), BF16),
        jax.ShapeDtypeStruct((B, S // LANES, A_WIDTH, LANES), BF16),
        jax.ShapeDtypeStruct((B, IDX_HEADS * IDX_DIM, S), BF16),
        jax.ShapeDtypeStruct((B, IDX_HEADS, S), F32),
    )
    out_specs = (
        pl.BlockSpec((1, A_WIDTH // LANES, tm, LANES), lambda b, t: (b, 0, t, 0)),
        pl.BlockSpec((1, tm, 1536), lambda b, t: (b, t, 0)),
        pl.BlockSpec((1, tm, 2048), lambda b, t: (b, t, 0)),
        pl.BlockSpec((1, tm, IDX_DIM), lambda b, t: (b, t, 0)),
        pl.BlockSpec((1, tm, G_RANK), lambda b, t: (b, t, 0)),
        pl.BlockSpec((1, A_WIDTH, tm), lambda b, t: (b, 0, t)),
        pl.BlockSpec((1, tm // LANES, A_WIDTH, LANES), lambda b, t: (b, t, 0, 0)),
        pl.BlockSpec((1, IDX_HEADS * IDX_DIM, tm), lambda b, t: (b, 0, t)),
        pl.BlockSpec((1, IDX_HEADS, tm), lambda b, t: (b, 0, t)),
    )
    return pl.pallas_call(
        _inproj_body,
        grid=(B, nt),
        in_specs=[pl.BlockSpec((1, tm, D), lambda b, t: (b, t, 0)),
                  pl.BlockSpec((1, 1, D), lambda b, t: (b, 0, 0)),
                  pl.BlockSpec((1, 1, D), lambda b, t: (b, 0, 0)),
                  pl.BlockSpec(w_tok.shape, const),
                  pl.BlockSpec(w_ch.shape, const)],
        out_specs=out_specs,
        out_shape=out_shape,
        compiler_params=_params("parallel", "parallel"),
        name="inproj",
    )(x, sh1, sc1, w_tok, w_ch)


IDX_CHUNK = 256
FAR_CHUNK = 512


def _rel_bucket_table():
    s = np.arange(2 * QBLK)[:, None]
    t = np.arange(QBLK)[None, :]
    dist = np.maximum(t + QBLK - s, 0)
    max_exact = REL_BUCKETS // 2
    d_f = np.maximum(dist, 1).astype(np.float32)
    large = max_exact + (np.log(d_f / max_exact) / math.log(REL_MAX_DIST / max_exact)
                         * (REL_BUCKETS - max_exact)).astype(np.int32)
    large = np.minimum(large, REL_BUCKETS - 1)
    return np.where(dist < max_exact, dist, large).astype(np.int32)


def _far_bucket():
    max_exact = REL_BUCKETS // 2
    v = max_exact + int(np.float32(np.log(np.float32(QBLK + 1) / max_exact) / math.log(REL_MAX_DIST / max_exact)
                                   * (REL_BUCKETS - max_exact)))
    assert min(v, REL_BUCKETS - 1) == REL_BUCKETS - 1
    return REL_BUCKETS - 1


def _dsa_body(rb_ref, bkt_ref, ik_ref, kk_ref, vT_ref, qT_ref, iqT_ref, iwT_ref, o_ref,
              key_s, madd_s, tbl_s, oT_s, xcut_s, *, topk, idx_bits):
    i = pl.program_id(1)
    nck = (i + 2) // 2
    t_idx = i * QBLK + lax.broadcasted_iota(I32, (1, QBLK), 1)

    @pl.when(i == 0)
    def _():
        bkt = bkt_ref[...]
        for h in range(A_HEADS):
            t = jnp.zeros((2 * QBLK, QBLK), F32)
            for k in range(REL_BUCKETS):
                t = jnp.where(bkt == k, rb_ref[k, h], t)
            tbl_s[h] = t

    def score_chunk(c, carry):
        s0 = pl.multiple_of(c * IDX_CHUNK, IDX_CHUNK)
        kc = ik_ref[0, pl.ds(s0, IDX_CHUNK), :]
        acc = jnp.zeros((IDX_CHUNK, QBLK), F32)
        for hp in range(IDX_HEADS // 2):
            r0 = hp * 2 * IDX_DIM
            rhs = jnp.concatenate([iqT_ref[0, r0:r0 + IDX_DIM, :],
                                   iqT_ref[0, r0 + IDX_DIM:r0 + 2 * IDX_DIM, :]], axis=1)
            z = jnp.dot(kc, rhs, preferred_element_type=F32)
            acc = acc + jnp.maximum(z[:, :QBLK], 0.0) * iwT_ref[0, 2 * hp:2 * hp + 1, :]
            acc = acc + jnp.maximum(z[:, QBLK:], 0.0) * iwT_ref[0, 2 * hp + 1:2 * hp + 2, :]
        s_idx = s0 + lax.broadcasted_iota(I32, (IDX_CHUNK, QBLK), 0)
        bits = pltpu.bitcast(acc, I32)
        key = jnp.where(bits < 0, bits ^ INT_MAX, bits)
        key_s[pl.ds(s0, IDX_CHUNK), :] = jnp.where(s_idx <= t_idx, key, INT_MIN)
        return carry

    lax.fori_loop(0, nck, score_chunk, 0)

    def count(pred):
        def body(c, cnt):
            s0 = pl.multiple_of(c * IDX_CHUNK, IDX_CHUNK)
            k = key_s[pl.ds(s0, IDX_CHUNK), :]
            s_idx = s0 + lax.broadcasted_iota(I32, (IDX_CHUNK, QBLK), 0)
            m = jnp.where(pred(k, s_idx), 1, 0)
            return cnt + jnp.sum(m.reshape(IDX_CHUNK // 8, 8, QBLK), axis=0)
        cnt = lax.fori_loop(0, nck, body, jnp.zeros((8, QBLK), I32))
        return jnp.sum(cnt, axis=0, keepdims=True)

    c0 = count(lambda k, s: k >= 0)
    T = jnp.where(c0 >= topk, 0, INT_MIN).astype(I32)

    def bit_body(j, T):
        cand = T | jnp.left_shift(jnp.int32(1), 30 - j)
        c = count(lambda k, s: k >= cand)
        return jnp.where(c >= topk, cand, T)

    T = lax.fori_loop(0, 31, bit_body, T)

    cnt_gt = count(lambda k, s: k > T)
    cnt_ge = count(lambda k, s: k >= T)
    need = topk - cnt_gt
    excess = jnp.where((cnt_ge - cnt_gt > need) & (T > INT_MIN), 1.0, 0.0)
    xcut_s[...] = jnp.full((1, QBLK), INT_MAX, I32)

    @pl.when(jnp.max(excess) > 0.0)
    def _():
        X = jnp.zeros((1, QBLK), I32)
        for b in range(idx_bits - 1, -1, -1):
            cand = X | (1 << b)
            f = count(lambda k, s: (k == T) & (s < cand))
            X = jnp.where(f < need, cand, X)
        xcut_s[...] = X

    xcut = xcut_s[...]

    def mask_chunk(c, carry):
        s0 = pl.multiple_of(c * IDX_CHUNK, IDX_CHUNK)
        k = key_s[pl.ds(s0, IDX_CHUNK), :]
        s_idx = s0 + lax.broadcasted_iota(I32, (IDX_CHUNK, QBLK), 0)
        sel = ((k > T) | ((k == T) & (s_idx <= xcut))) & (s_idx <= t_idx)
        madd_s[pl.ds(s0, IDX_CHUNK), :] = jnp.where(sel, 0.0, NEG)
        return carry

    lax.fori_loop(0, nck, mask_chunk, 0)

    n_far = jnp.maximum(i - 1, 0)
    blocks_per_far = FAR_CHUNK // QBLK
    n1 = n_far // blocks_per_far
    n2 = n_far - n1 * blocks_per_far
    row_head = lax.broadcasted_iota(I32, (LANES, QBLK), 0) // A_HEAD_DIM

    def head_body(h, carry0):
        pair = h // 2
        sub = h - 2 * pair
        qp = qT_ref[0, pl.ds(pl.multiple_of(pair * LANES, LANES), LANES), :]
        qm = jnp.where(row_head == sub, qp, jnp.zeros_like(qp))
        hrow = pl.multiple_of(h * A_HEAD_DIM, A_HEAD_DIM)
        far_bias = rb_ref[_far_bucket(), h]

        def step(blk0, nblk, bias, carry):
            m, l, acc = carry
            n = nblk * QBLK
            s0 = pl.multiple_of(blk0 * QBLK, QBLK)
            kc = kk_ref[0, pair, pl.ds(s0, n), :]
            s = jnp.dot(kc, qm, preferred_element_type=F32) + madd_s[pl.ds(s0, n), :] + bias
            m_new = jnp.maximum(m, jnp.max(s, axis=0, keepdims=True))
            alpha = jnp.exp(m - m_new)
            p = jnp.exp(s - m_new)
            l = alpha * l + jnp.sum(p, axis=0, keepdims=True)
            vt = jnp.concatenate([vT_ref[0, blk0 + u, pl.ds(hrow, A_HEAD_DIM), :] for u in range(nblk)], axis=1)
            acc = alpha * acc + jnp.dot(vt, p.astype(BF16), preferred_element_type=F32)
            return m_new, l, acc

        carry = (jnp.full((1, QBLK), NEG, F32), jnp.zeros((1, QBLK), F32), jnp.zeros((A_HEAD_DIM, QBLK), F32))
        carry = lax.fori_loop(0, n1, lambda c, cr: step(c * blocks_per_far, blocks_per_far, far_bias, cr), carry)
        carry = lax.fori_loop(0, n2, lambda c, cr: step(n1 * blocks_per_far + c, 1, far_bias, cr), carry)
        carry = lax.fori_loop(0, jnp.minimum(i, 1), lambda c, cr: step(i - 1, 1, tbl_s[h, 0:QBLK, :], cr), carry)
        m, l, acc = step(i, 1, tbl_s[h, QBLK:2 * QBLK, :], carry)
        oT_s[pl.ds(hrow, A_HEAD_DIM), :] = acc / l
        return carry0

    lax.fori_loop(0, A_HEADS, head_body, 0)
    o_ref[0] = oT_s[...].T.astype(BF16)


def _dsa_call(rel_bias, ik, kk, vT, qT, iqT, iwT):
    B, S, _ = ik.shape
    nb = S // QBLK
    topk = min(TOPK_MAX, S // 4)
    bkt = jnp.asarray(_rel_bucket_table())
    body = functools.partial(_dsa_body, topk=topk, idx_bits=int(math.log2(S)))
    return pl.pallas_call(
        body,
        grid=(B, nb),
        in_specs=[pl.BlockSpec(memory_space=pltpu.SMEM),
                  pl.BlockSpec((2 * QBLK, QBLK), lambda b, i: (0, 0)),
                  pl.BlockSpec((1, S, IDX_DIM), lambda b, i: (b, 0, 0)),
                  pl.BlockSpec((1, A_WIDTH // LANES, S, LANES), lambda b, i: (b, 0, 0, 0)),
                  pl.BlockSpec((1, S // LANES, A_WIDTH, LANES), lambda b, i: (b, 0, 0, 0)),
                  pl.BlockSpec((1, A_WIDTH, QBLK), lambda b, i: (b, 0, i)),
                  pl.BlockSpec((1, IDX_HEADS * IDX_DIM, QBLK), lambda b, i: (b, 0, i)),
                  pl.BlockSpec((1, IDX_HEADS, QBLK), lambda b, i: (b, 0, i))],
        out_specs=pl.BlockSpec((1, QBLK, A_WIDTH), lambda b, i: (b, i, 0)),
        out_shape=jax.ShapeDtypeStruct((B, S, A_WIDTH), BF16),
        scratch_shapes=[pltpu.VMEM((S, QBLK), I32),
                        pltpu.VMEM((S, QBLK), F32),
                        pltpu.VMEM((A_HEADS, 2 * QBLK, QBLK), F32),
                        pltpu.VMEM((A_WIDTH, QBLK), F32),
                        pltpu.VMEM((1, QBLK), I32)],
        compiler_params=_params("parallel", "arbitrary"),
        name="dsa",
    )(rel_bias, bkt, ik, kk, vT, qT, iqT, iwT)


GLA_Q = (0, 256)
GLA_K = (256, 512)
GLA_V = (512, 1024)
GLA_R = (1024, 1536)


def _gla_body(gla_ref, glr_ref, wg_ref, bg_ref, ng_ref, o_ref, st_s):
    tg = gla_ref.shape[1]
    C = G_CHUNK

    @pl.when(pl.program_id(1) == 0)
    def _():
        st_s[...] = jnp.zeros_like(st_s)

    xg = jnp.dot(glr_ref[0], wg_ref[...], preferred_element_type=F32, precision=HIGHEST) + bg_ref[...]
    logg = -(jnp.maximum(-xg, 0.0) + jnp.log1p(jnp.exp(-jnp.abs(xg)))) * (1.0 / G_TAU)

    ri = lax.broadcasted_iota(I32, (C, C), 0)
    ci = lax.broadcasted_iota(I32, (C, C), 1)
    tril = ri >= ci
    tril_f = jnp.where(tril, 1.0, 0.0).astype(F32)
    lane_head = lax.broadcasted_iota(I32, (C, LANES), 1) // G_DK
    st_rows = lax.broadcasted_iota(I32, (2 * G_DV, LANES), 0) // G_DV
    st_cols = lax.broadcasted_iota(I32, (2 * G_DV, LANES), 1) // G_DK
    st_diag = st_rows == st_cols

    for ck in range(tg // C):
        r0 = ck * C
        for p in range(G_HEADS // 2):
            lg = logg[r0:r0 + C, p * LANES:(p + 1) * LANES]
            bc = jnp.dot(tril_f, lg, preferred_element_type=F32, precision=HIGHEST)
            bl = bc[C - 1:C, :]
            q = gla_ref[0, r0:r0 + C, GLA_Q[0] + p * LANES:GLA_Q[0] + (p + 1) * LANES].astype(F32) * (G_DK ** -0.5)
            k = gla_ref[0, r0:r0 + C, GLA_K[0] + p * LANES:GLA_K[0] + (p + 1) * LANES].astype(F32)
            v = gla_ref[0, r0:r0 + C, GLA_V[0] + p * 2 * G_DV:GLA_V[0] + (p + 1) * 2 * G_DV]
            q_in = (q * jnp.exp(bc)).astype(BF16)
            k_st = (k * jnp.exp(bl - bc)).astype(BF16)
            q_rel = q * jnp.exp(bc - bl)
            o_intra = []
            for sub in range(2):
                qm = jnp.where(lane_head == sub, q_rel, 0.0).astype(BF16)
                att = lax.dot_general(qm, k_st, NT, preferred_element_type=F32)
                att = jnp.where(tril, att, 0.0).astype(BF16)
                o_intra.append(jnp.dot(att, v[:, sub * G_DV:(sub + 1) * G_DV], preferred_element_type=F32))
            st = st_s[p]
            o_inter = lax.dot_general(q_in, st.astype(BF16), NT, preferred_element_type=F32)
            uT = lax.dot_general(v, k_st, TN, preferred_element_type=F32)
            st_s[p] = st * jnp.exp(bl) + jnp.where(st_diag, uT, 0.0)
            for sub in range(2):
                hd = 2 * p + sub
                o = o_intra[sub] + o_inter[:, sub * G_DV:(sub + 1) * G_DV]
                y = _ln(o) * ng_ref[:, hd * G_DV:(hd + 1) * G_DV]
                g = gla_ref[0, r0:r0 + C, GLA_R[0] + hd * G_DV:GLA_R[0] + (hd + 1) * G_DV].astype(F32)
                o_ref[0, r0:r0 + C, hd * G_DV:(hd + 1) * G_DV] = (y * (g * _sigmoid(g))).astype(BF16)


def _gla_call(gla, glr, wg, bg, ng, tg):
    B, S, _ = gla.shape
    const = lambda b, j: (0, 0)
    return pl.pallas_call(
        _gla_body,
        grid=(B, S // tg),
        in_specs=[pl.BlockSpec((1, tg, 1536), lambda b, j: (b, j, 0)),
                  pl.BlockSpec((1, tg, G_RANK), lambda b, j: (b, j, 0)),
                  pl.BlockSpec((G_RANK, G_KW), const),
                  pl.BlockSpec((1, G_KW), const),
                  pl.BlockSpec((1, G_VW), const)],
        out_specs=pl.BlockSpec((1, tg, G_VW), lambda b, j: (b, j, 0)),
        out_shape=jax.ShapeDtypeStruct((B, S, G_VW), BF16),
        scratch_shapes=[pltpu.VMEM((G_HEADS // 2, 2 * G_DV, LANES), F32)],
        compiler_params=_params("parallel", "arbitrary"),
        name="gla",
    )(gla, glr, wg, bg, ng)


ROUTER_ROWS = 40
ROUTER_E0 = 8


def _post_body(oa_ref, ob_ref, gates_ref, x_ref, gt1_ref, sh2_ref, sc2_ref, wa_ref, wb_ref, wo_ref,
               g1_ref, b1_ref, wr_ref, br_ref, x1_ref, h2_ref, gate_ref):
    tm = x_ref.shape[1]
    D = x_ref.shape[2]
    ya = jnp.dot(oa_ref[0], wa_ref[...], preferred_element_type=F32)
    yb = jnp.dot(ob_ref[0], wb_ref[...], preferred_element_type=F32)
    ga = gates_ref[0, :, 0:D].astype(F32)
    gb = gates_ref[0, :, D:2 * D].astype(F32)
    merged = _sigmoid(ga) * ya + _sigmoid(gb) * yb
    y = jnp.dot(merged.astype(BF16), wo_ref[...], preferred_element_type=F32)
    x1 = _ln(DN_ALPHA * x_ref[0] + gt1_ref[0] * y) * g1_ref[...] + b1_ref[...]
    x1_ref[0] = x1
    h2 = _ln(x1) * (1.0 + sc2_ref[0]) + sh2_ref[0]
    h2_ref[0] = h2.astype(BF16)

    lt = lax.dot_general(wr_ref[...], h2, NT, preferred_element_type=F32, precision=HIGHEST) + br_ref[...]
    gl = lt[0:N_GROUPS]
    gmax = jnp.max(gl, axis=0, keepdims=True)
    g_w = 1.0 / jnp.sum(jnp.exp(gl - gmax), axis=0, keepdims=True)
    r4 = lax.broadcasted_iota(I32, (N_GROUPS, tm), 0)
    g_idx = jnp.min(jnp.where(gl == gmax, r4, N_GROUPS), axis=0, keepdims=True)
    eg = jnp.zeros((EXPERTS_PER_GROUP, tm), F32)
    for g in range(N_GROUPS):
        lo = ROUTER_E0 + g * EXPERTS_PER_GROUP
        eg = jnp.where(g_idx == g, lt[lo:lo + EXPERTS_PER_GROUP], eg)
    r8 = lax.broadcasted_iota(I32, (EXPERTS_PER_GROUP, tm), 0)
    e1 = jnp.max(eg, axis=0, keepdims=True)
    i1 = jnp.min(jnp.where(eg == e1, r8, EXPERTS_PER_GROUP), axis=0, keepdims=True)
    eg2 = jnp.where(r8 == i1, -jnp.inf, eg)
    e2 = jnp.max(eg2, axis=0, keepdims=True)
    i2 = jnp.min(jnp.where(eg2 == e2, r8, EXPERTS_PER_GROUP), axis=0, keepdims=True)
    d = jnp.exp(e2 - e1)
    w1 = g_w / (1.0 + d)
    w2 = g_w * d / (1.0 + d)
    in_group = jnp.where(r8 == i1, w1, 0.0) + jnp.where(r8 == i2, w2, 0.0)
    blocks = [jnp.where(g_idx == g, in_group, 0.0) for g in range(N_GROUPS)]
    blocks.append(jnp.zeros((LANES - N_EXPERTS, tm), F32))
    gate_ref[...] = jnp.concatenate(blocks, axis=0).T


def _post_call(o_a, o_b, gates, x, gt1, sh2, sc2, wa, wb, wo, g1, b1, wr, br, tm):
    B, S, D = x.shape
    nt = S // tm
    const = lambda b, t: (0, 0)
    row = lambda b, t: (b, 0, 0)
    tile = lambda b, t: (b, t, 0)
    return pl.pallas_call(
        _post_body,
        grid=(B, nt),
        in_specs=[pl.BlockSpec((1, tm, A_WIDTH), tile),
                  pl.BlockSpec((1, tm, G_VW), tile),
                  pl.BlockSpec((1, tm, 2 * D), tile),
                  pl.BlockSpec((1, tm, D), tile),
                  pl.BlockSpec((1, 1, D), row),
                  pl.BlockSpec((1, 1, D), row),
                  pl.BlockSpec((1, 1, D), row),
                  pl.BlockSpec(wa.shape, const),
                  pl.BlockSpec(wb.shape, const),
                  pl.BlockSpec(wo.shape, const),
                  pl.BlockSpec((1, D), const),
                  pl.BlockSpec((1, D), const),
                  pl.BlockSpec(wr.shape, const),
                  pl.BlockSpec(br.shape, const)],
        out_specs=(pl.BlockSpec((1, tm, D), tile),
                   pl.BlockSpec((1, tm, D), tile),
                   pl.BlockSpec((tm, LANES), lambda b, t: (b * nt + t, 0))),
        out_shape=(jax.ShapeDtypeStruct((B, S, D), F32),
                   jax.ShapeDtypeStruct((B, S, D), BF16),
                   jax.ShapeDtypeStruct((B * S, LANES), F32)),
        compiler_params=_params("parallel", "parallel"),
        name="post",
    )(o_a, o_b, gates, x, gt1, sh2, sc2, wa, wb, wo, g1, b1, wr, br)


MOE_EXPERTS_PER_STEP = 4


def _moe_body(h2_ref, gate_ref, x1_ref, gt2_ref, w1_ref, w3_ref, w2_ref, g2_ref, b2_ref, o_ref, acc_s):
    c = pl.program_id(1)
    ne = MOE_EXPERTS_PER_STEP

    @pl.when(c == 0)
    def _():
        acc_s[...] = jnp.zeros_like(acc_s)

    h = h2_ref[...]
    gate = pltpu.roll(gate_ref[...], (LANES - c * ne) % LANES, axis=1)
    hid = []
    for j in range(ne):
        a = jnp.dot(h, w1_ref[j], preferred_element_type=F32)
        b = jnp.dot(h, w3_ref[j], preferred_element_type=F32)
        hid.append((a * _sigmoid(a) * b * gate[:, j:j + 1]).astype(BF16))
    acc_s[...] += jnp.dot(jnp.concatenate(hid, axis=1), w2_ref[...], preferred_element_type=F32)

    @pl.when(c == pl.num_programs(1) - 1)
    def _():
        z = DN_ALPHA * x1_ref[...] + gt2_ref[0] * acc_s[...]
        o_ref[...] = _ln(z) * g2_ref[...] + b2_ref[...]


def _moe_call(h2, gate, x1, gt2, w1, w3, w2, g2, b2, tm, S):
    T, D = h2.shape
    ne = MOE_EXPERTS_PER_STEP
    nc = N_EXPERTS // ne
    tiles_per_seq = S // tm
    tile = lambda t, c: (t, 0)
    const = lambda t, c: (0, 0)
    return pl.pallas_call(
        _moe_body,
        grid=(T // tm, nc),
        in_specs=[pl.BlockSpec((tm, D), tile),
                  pl.BlockSpec((tm, LANES), tile),
                  pl.BlockSpec((tm, D), tile),
                  pl.BlockSpec((1, 1, D), lambda t, c: (t // tiles_per_seq, 0, 0)),
                  pl.BlockSpec((ne, D, D_EXPERT), lambda t, c: (c, 0, 0)),
                  pl.BlockSpec((ne, D, D_EXPERT), lambda t, c: (c, 0, 0)),
                  pl.BlockSpec((ne * D_EXPERT, D), lambda t, c: (c, 0)),
                  pl.BlockSpec((1, D), const),
                  pl.BlockSpec((1, D), const)],
        out_specs=pl.BlockSpec((tm, D), tile),
        out_shape=jax.ShapeDtypeStruct((T, D), F32),
        scratch_shapes=[pltpu.VMEM((tm, D), F32)],
        compiler_params=_params("parallel", "arbitrary"),
        name="moe",
    )(h2, gate, x1, gt2, w1, w3, w2, g2, b2)


def _pick(n, pref):
    return pref if n % pref == 0 else n


def kernel(x, c, rel_bias, w_ada, b_ada, w_in, gla_w_gate, gla_b_gate, gla_norm_g, w_branch_a, w_branch_b, w_out, ln1_g, ln1_b, w_router_group, b_router_group, w_router_expert, b_router_expert, w_exp_gate, w_exp_up, w_exp_down, ln2_g, ln2_b):
    B, S, D = x.shape
    assert S % (2 * QBLK) == 0 and D == 1024 and w_ada.shape[0] == DEPTH == 1
    l = 0

    ada = _ada_call(c, w_ada[l], b_ada[l])
    sh1, sc1, gt1, sh2, sc2, gt2 = [ada[:, i * D:(i + 1) * D].reshape(B, 1, D) for i in range(6)]

    offs = np.concatenate([[0], np.cumsum(SPLIT_SIZES)])
    seg = lambda i: w_in[l][:, offs[i]:offs[i + 1]]
    (w_aq, w_ak, w_av, w_iq, w_ik, w_iw, w_gq, w_gk, w_gv, w_gr, w_glr, w_ga, w_gb) = [seg(i) for i in range(13)]
    pad = jnp.zeros((D, TOK_SMALL[1] - TOK_SMALL[0] - IDX_DIM - G_RANK), F32)
    w_tok = jnp.concatenate([w_ak, w_gq, w_gk, w_gv, w_gr, w_ga, w_gb, w_ik, w_glr, pad], axis=1).astype(BF16)
    w_ch = jnp.concatenate([w_aq, w_av, w_iq, w_iw], axis=1).T.astype(BF16)

    tm = _pick(S, 512)
    kk, gla, gates, ik, glr, qT, vT, iqT, iwT = _inproj_call(x, sh1, sc1, w_tok, w_ch, tm)

    o_a = _dsa_call(rel_bias, ik, kk, vT, qT, iqT, iwT)
    o_b = _gla_call(gla, glr, gla_w_gate[l], gla_b_gate[l].reshape(1, G_KW), gla_norm_g[l].reshape(1, G_VW),
                    _pick(S, 256))

    wr = jnp.zeros((ROUTER_ROWS, D), F32)
    wr = wr.at[0:N_GROUPS].set(w_router_group[l].T).at[ROUTER_E0:ROUTER_E0 + N_EXPERTS].set(w_router_expert[l].T)
    br = jnp.zeros((ROUTER_ROWS, 1), F32)
    br = br.at[0:N_GROUPS, 0].set(b_router_group[l]).at[ROUTER_E0:ROUTER_E0 + N_EXPERTS, 0].set(b_router_expert[l])
    x1, h2, gate = _post_call(o_a, o_b, gates, x, gt1, sh2, sc2,
                              w_branch_a[l].astype(BF16), w_branch_b[l].astype(BF16), w_out[l].astype(BF16),
                              ln1_g[l].reshape(1, D), ln1_b[l].reshape(1, D), wr, br, tm)

    tm5 = _pick(S, 1024)
    out = _moe_call(h2.reshape(B * S, D), gate, x1.reshape(B * S, D), gt2,
                    w_exp_gate[l].astype(BF16), w_exp_up[l].astype(BF16),
                    w_exp_down[l].astype(BF16).reshape(N_EXPERTS * D_EXPERT, D),
                    ln2_g[l].reshape(1, D), ln2_b[l].reshape(1, D), tm5, S)
    return out.reshape(B, S, D)
```

```python
import functools
import math

import numpy as np
import jax
import jax.numpy as jnp
from jax import lax
from jax.experimental import pallas as pl
from jax.experimental.pallas import tpu as pltpu

F32 = jnp.float32
BF16 = jnp.bfloat16
I32 = jnp.int32
HIGHEST = lax.Precision.HIGHEST

A_HEADS = 8
A_HEAD_DIM = 64
A_WIDTH = A_HEADS * A_HEAD_DIM
IDX_HEADS = 16
IDX_DIM = 32
TOPK_MAX = 256
QBLK = 128
REL_BUCKETS = 32
REL_MAX_DIST = 128
G_HEADS = 4
G_DK = 64
G_DV = 128
G_KW = G_HEADS * G_DK
G_VW = G_HEADS * G_DV
G_RANK = 16
G_TAU = 16.0
G_CHUNK = 64
N_GROUPS = 4
EXPERTS_PER_GROUP = 8
N_EXPERTS = N_GROUPS * EXPERTS_PER_GROUP
D_EXPERT = 256
DEPTH = 1
DN_ALPHA = (2.0 * DEPTH) ** 0.25
LN_EPS = 1e-5
SPLIT_SIZES = (A_WIDTH, A_WIDTH, A_WIDTH, IDX_HEADS * IDX_DIM, IDX_DIM, IDX_HEADS,
               G_KW, G_KW, G_VW, G_VW, G_RANK, 1024, 1024)

LANES = 128
VMEM_LIMIT_BYTES = 56 * 1024 * 1024

NEG = -1e30
INT_MIN = -2 ** 31
INT_MAX = 2 ** 31 - 1

NT = (((1,), (1,)), ((), ()))
TN = (((0,), (0,)), ((), ()))


def _ln(x):
    mu = jnp.mean(x, axis=-1, keepdims=True)
    xc = x - mu
    var = jnp.mean(xc * xc, axis=-1, keepdims=True)
    return xc * lax.rsqrt(var + LN_EPS)


def _sigmoid(x):
    return 1.0 / (1.0 + jnp.exp(-x))


def _params(*sem):
    return pltpu.CompilerParams(dimension_semantics=sem, vmem_limit_bytes=VMEM_LIMIT_BYTES)


def _ada_body(c_ref, w_ref, b_ref, o_ref):
    c = c_ref[...]
    cond = c * _sigmoid(c)
    o_ref[...] = jnp.dot(cond, w_ref[...], preferred_element_type=F32, precision=HIGHEST) + b_ref[...]


def _ada_call(c, w, b):
    B, D = c.shape
    N = w.shape[1]
    tn = 1536
    return pl.pallas_call(
        _ada_body,
        grid=(N // tn,),
        in_specs=[pl.BlockSpec((B, D), lambda j: (0, 0)),
                  pl.BlockSpec((D, tn), lambda j: (0, j)),
                  pl.BlockSpec((1, tn), lambda j: (0, j))],
        out_specs=pl.BlockSpec((B, tn), lambda j: (0, j)),
        out_shape=jax.ShapeDtypeStruct((B, N), F32),
        compiler_params=_params("arbitrary"),
        name="ada",
    )(c, w, b.reshape(1, N))


TOK_K = (0, 512)
TOK_GLA = (512, 2048)
TOK_GATES = (2048, 4096)
TOK_SMALL = (4096, 4224)
CH_Q = (0, 512)
CH_V = (512, 1024)
CH_IQ = (1024, 1536)
CH_IW = (1536, 1552)
IW_SCALE = IDX_HEADS ** -0.5 * IDX_DIM ** -0.5


def _inproj_body(x_ref, sh_ref, sc_ref, wtok_ref, wch_ref,
                 k_ref, gla_ref, gates_ref, ik_ref, glr_ref, qT_ref, vT_ref, iqT_ref, iwT_ref):
    tm = x_ref.shape[1]
    h = (_ln(x_ref[0]) * (1.0 + sc_ref[0]) + sh_ref[0]).astype(BF16)

    def tok(ab):
        return jnp.dot(h, wtok_ref[:, ab[0]:ab[1]], preferred_element_type=F32)

    def ch(ab):
        return lax.dot_general(wch_ref[ab[0]:ab[1], :], h, NT, preferred_element_type=F32)

    kres = tok(TOK_K)
    for p in range(A_WIDTH // LANES):
        k_ref[0, p] = kres[:, p * LANES:(p + 1) * LANES].astype(BF16)
    gla_ref[0] = tok(TOK_GLA).astype(BF16)
    gates_ref[0] = tok(TOK_GATES).astype(BF16)
    small = tok(TOK_SMALL)
    ik_ref[0] = small[:, :IDX_DIM].astype(BF16)
    glr_ref[0] = small[:, IDX_DIM:IDX_DIM + G_RANK]

    qT_ref[0] = (ch(CH_Q) * (A_HEAD_DIM ** -0.5)).astype(BF16)
    vres = ch(CH_V).astype(BF16)
    for j in range(tm // LANES):
        vT_ref[0, j] = vres[:, j * LANES:(j + 1) * LANES]
    iqT_ref[0] = ch(CH_IQ).astype(BF16)
    iwT_ref[0] = ch(CH_IW) * IW_SCALE


def _inproj_call(x, sh1, sc1, w_tok, w_ch, tm):
    B, S, D = x.shape
    nt = S // tm
    const = lambda b, t: (0, 0)
    out_shape = (
        jax.ShapeDtypeStruct((B, A_WIDTH // LANES, S, LANES), BF16),
        jax.ShapeDtypeStruct((B, S, 1536), BF16),
        jax.ShapeDtypeStruct((B, S, 2048), BF16),
        jax.ShapeDtypeStruct((B, S, IDX_DIM), BF16),
        jax.ShapeDtypeStruct((B, S, G_RANK), F32),
        jax.ShapeDtypeStruct((B, A_WIDTH, S), BF16),
        jax.ShapeDtypeStruct((B, S // LANES, A_WIDTH, LANES), BF16),
        jax.ShapeDtypeStruct((B, IDX_HEADS * IDX_DIM, S), BF16),
        jax.ShapeDtypeStruct((B, IDX_HEADS, S), F32),
    )
    out_specs = (
        pl.BlockSpec((1, A_WIDTH // LANES, tm, LANES), lambda b, t: (b, 0, t, 0)),
        pl.BlockSpec((1, tm, 1536), lambda b, t: (b, t, 0)),
        pl.BlockSpec((1, tm, 2048), lambda b, t: (b, t, 0)),
        pl.BlockSpec((1, tm, IDX_DIM), lambda b, t: (b, t, 0)),
        pl.BlockSpec((1, tm, G_RANK), lambda b, t: (b, t, 0)),
        pl.BlockSpec((1, A_WIDTH, tm), lambda b, t: (b, 0, t)),
        pl.BlockSpec((1, tm // LANES, A_WIDTH, LANES), lambda b, t: (b, t, 0, 0)),
        pl.BlockSpec((1, IDX_HEADS * IDX_DIM, tm), lambda b, t: (b, 0, t)),
        pl.BlockSpec((1, IDX_HEADS, tm), lambda b, t: (b, 0, t)),
    )
    return pl.pallas_call(
        _inproj_body,
        grid=(B, nt),
        in_specs=[pl.BlockSpec((1, tm, D), lambda b, t: (b, t, 0)),
                  pl.BlockSpec((1, 1, D), lambda b, t: (b, 0, 0)),
                  pl.BlockSpec((1, 1, D), lambda b, t: (b, 0, 0)),
                  pl.BlockSpec(w_tok.shape, const),
                  pl.BlockSpec(w_ch.shape, const)],
        out_specs=out_specs,
        out_shape=out_shape,
        compiler_params=_params("parallel", "parallel"),
        name="inproj",
    )(x, sh1, sc1, w_tok, w_ch)


IDX_CHUNK = 256
FAR_CHUNK = 512


def _rel_bucket_table():
    s = np.arange(2 * QBLK)[:, None]
    t = np.arange(QBLK)[None, :]
    dist = np.maximum(t + QBLK - s, 0)
    max_exact = REL_BUCKETS // 2
    d_f = np.maximum(dist, 1).astype(np.float32)
    large = max_exact + (np.log(d_f / max_exact) / math.log(REL_MAX_DIST / max_exact)
                         * (REL_BUCKETS - max_exact)).astype(np.int32)
    large = np.minimum(large, REL_BUCKETS - 1)
    return np.where(dist < max_exact, dist, large).astype(np.int32)


def _far_bucket():
    max_exact = REL_BUCKETS // 2
    v = max_exact + int(np.float32(np.log(np.float32(QBLK + 1) / max_exact) / math.log(REL_MAX_DIST / max_exact)
                                   * (REL_BUCKETS - max_exact)))
    assert min(v, REL_BUCKETS - 1) == REL_BUCKETS - 1
    return REL_BUCKETS - 1


def _dsa_body(rb_ref, bkt_ref, ik_ref, kk_ref, vT_ref, qT_ref, iqT_ref, iwT_ref, o_ref,
              key_s, madd_s, tbl_s, oT_s, xcut_s, qm_s, s_s, *, topk, idx_bits):
    i = pl.program_id(1)
    nck = (i + 2) // 2
    t_idx = i * QBLK + lax.broadcasted_iota(I32, (1, QBLK), 1)

    @pl.when(i == 0)
    def _():
        bkt = bkt_ref[...]
        for h in range(A_HEADS):
            t = jnp.zeros((2 * QBLK, QBLK), F32)
            for k in range(REL_BUCKETS):
                t = jnp.where(bkt == k, rb_ref[k, h], t)
            tbl_s[h] = t - rb_ref[_far_bucket(), h]

    def score_chunk(c, carry):
        s0 = pl.multiple_of(c * IDX_CHUNK, IDX_CHUNK)
        kc = ik_ref[0, pl.ds(s0, IDX_CHUNK), :]
        acc = jnp.zeros((IDX_CHUNK, QBLK), F32)
        for hp in range(IDX_HEADS // 2):
            r0 = hp * 2 * IDX_DIM
            rhs = jnp.concatenate([iqT_ref[0, r0:r0 + IDX_DIM, :],
                                   iqT_ref[0, r0 + IDX_DIM:r0 + 2 * IDX_DIM, :]], axis=1)
            z = jnp.dot(kc, rhs, preferred_element_type=F32)
            acc = acc + jnp.maximum(z[:, :QBLK], 0.0) * iwT_ref[0, 2 * hp:2 * hp + 1, :]
            acc = acc + jnp.maximum(z[:, QBLK:], 0.0) * iwT_ref[0, 2 * hp + 1:2 * hp + 2, :]
        s_idx = s0 + lax.broadcasted_iota(I32, (IDX_CHUNK, QBLK), 0)
        bits = pltpu.bitcast(acc, I32)
        key = jnp.where(bits < 0, bits ^ INT_MAX, bits)
        key_s[pl.ds(s0, IDX_CHUNK), :] = jnp.where(s_idx <= t_idx, key, INT_MIN)
        return carry

    lax.fori_loop(0, nck, score_chunk, 0)

    def count(pred):
        def body(c, cnt):
            s0 = pl.multiple_of(c * IDX_CHUNK, IDX_CHUNK)
            k = key_s[pl.ds(s0, IDX_CHUNK), :]
            s_idx = s0 + lax.broadcasted_iota(I32, (IDX_CHUNK, QBLK), 0)
            m = jnp.where(pred(k, s_idx), 1, 0)
            return cnt + jnp.sum(m.reshape(IDX_CHUNK // 8, 8, QBLK), axis=0)
        cnt = lax.fori_loop(0, nck, body, jnp.zeros((8, QBLK), I32))
        return jnp.sum(cnt, axis=0, keepdims=True)

    c0 = count(lambda k, s: k >= 0)
    T = jnp.where(c0 >= topk, 0, INT_MIN).astype(I32)

    def bit_body(j, T):
        cand = T | jnp.left_shift(jnp.int32(1), 30 - j)
        c = count(lambda k, s: k >= cand)
        return jnp.where(c >= topk, cand, T)

    T = lax.fori_loop(0, 31, bit_body, T)

    cnt_gt = count(lambda k, s: k > T)
    cnt_ge = count(lambda k, s: k >= T)
    need = topk - cnt_gt
    excess = jnp.where((cnt_ge - cnt_gt > need) & (T > INT_MIN), 1.0, 0.0)
    xcut_s[...] = jnp.full((1, QBLK), INT_MAX, I32)

    @pl.when(jnp.max(excess) > 0.0)
    def _():
        X = jnp.zeros((1, QBLK), I32)
        for b in range(idx_bits - 1, -1, -1):
            cand = X | (1 << b)
            f = count(lambda k, s: (k == T) & (s < cand))
            X = jnp.where(f < need, cand, X)
        xcut_s[...] = X

    xcut = xcut_s[...]

    def mask_chunk(c, carry):
        s0 = pl.multiple_of(c * IDX_CHUNK, IDX_CHUNK)
        k = key_s[pl.ds(s0, IDX_CHUNK), :]
        s_idx = s0 + lax.broadcasted_iota(I32, (IDX_CHUNK, QBLK), 0)
        sel = ((k > T) | ((k == T) & (s_idx <= xcut))) & (s_idx <= t_idx)
        madd_s[pl.ds(s0, IDX_CHUNK), :] = jnp.where(sel, 0.0, NEG)
        return carry

    lax.fori_loop(0, nck, mask_chunk, 0)

    n_far = jnp.maximum(i - 1, 0)
    blocks_per_far = FAR_CHUNK // QBLK
    n1 = n_far // blocks_per_far
    n2 = n_far - n1 * blocks_per_far
    row_head = lax.broadcasted_iota(I32, (LANES, QBLK), 0) // A_HEAD_DIM
    for h in range(A_HEADS):
        qp = qT_ref[0, (h // 2) * LANES:(h // 2 + 1) * LANES, :]
        qm_s[h] = jnp.where(row_head == h % 2, qp, jnp.zeros_like(qp))
    oT_s[...] = jnp.zeros_like(oT_s)

    def step(blk0, nblk, near_row, carry):
        m_all, l_all = carry
        n = nblk * QBLK
        s0 = pl.multiple_of(blk0 * QBLK, QBLK)
        madd = madd_s[pl.ds(s0, n), :]
        m_blk = []
        for h in range(A_HEADS):
            kc = kk_ref[0, h // 2, pl.ds(s0, n), :]
            s = jnp.dot(kc, qm_s[h], preferred_element_type=F32) + madd
            if near_row is not None:
                s = s + tbl_s[h, pl.ds(near_row, QBLK), :]
            s_s[h, 0:n, :] = s
            m_blk.append(jnp.max(s, axis=0, keepdims=True))
        m_new = jnp.maximum(m_all, jnp.concatenate(m_blk, axis=0))
        alpha = jnp.exp(m_all - m_new)
        l_blk = []
        for h in range(A_HEADS):
            rows = slice(h * A_HEAD_DIM, (h + 1) * A_HEAD_DIM)
            p = jnp.exp(s_s[h, 0:n, :] - m_new[h:h + 1])
            l_blk.append(jnp.sum(p, axis=0, keepdims=True))
            vt = jnp.concatenate([vT_ref[0, blk0 + u, rows, :] for u in range(nblk)], axis=1)
            oT_s[rows, :] = alpha[h:h + 1] * oT_s[rows, :] + jnp.dot(vt, p.astype(BF16), preferred_element_type=F32)
        return m_new, alpha * l_all + jnp.concatenate(l_blk, axis=0)

    carry = (jnp.full((A_HEADS, QBLK), NEG, F32), jnp.zeros((A_HEADS, QBLK), F32))
    carry = lax.fori_loop(0, n1, lambda c, cr: step(c * blocks_per_far, blocks_per_far, None, cr), carry)
    carry = lax.fori_loop(0, n2, lambda c, cr: step(n1 * blocks_per_far + c, 1, None, cr), carry)
    m_all, l_all = lax.fori_loop(1 - jnp.minimum(i, 1), 2,
                                 lambda j, cr: step(i - 1 + j, 1, pl.multiple_of(j * QBLK, QBLK), cr), carry)
    for h in range(A_HEADS):
        rows = slice(h * A_HEAD_DIM, (h + 1) * A_HEAD_DIM)
        oT_s[rows, :] = oT_s[rows, :] / l_all[h:h + 1]
    o_ref[0] = oT_s[...].T.astype(BF16)


def _dsa_call(rel_bias, ik, kk, vT, qT, iqT, iwT):
    B, S, _ = ik.shape
    nb = S // QBLK
    topk = min(TOPK_MAX, S // 4)
    bkt = jnp.asarray(_rel_bucket_table())
    body = functools.partial(_dsa_body, topk=topk, idx_bits=int(math.log2(S)))
    return pl.pallas_call(
        body,
        grid=(B, nb),
        in_specs=[pl.BlockSpec(memory_space=pltpu.SMEM),
                  pl.BlockSpec((2 * QBLK, QBLK), lambda b, i: (0, 0)),
                  pl.BlockSpec((1, S, IDX_DIM), lambda b, i: (b, 0, 0)),
                  pl.BlockSpec((1, A_WIDTH // LANES, S, LANES), lambda b, i: (b, 0, 0, 0)),
                  pl.BlockSpec((1, S // LANES, A_WIDTH, LANES), lambda b, i: (b, 0, 0, 0)),
                  pl.BlockSpec((1, A_WIDTH, QBLK), lambda b, i: (b, 0, i)),
                  pl.BlockSpec((1, IDX_HEADS * IDX_DIM, QBLK), lambda b, i: (b, 0, i)),
                  pl.BlockSpec((1, IDX_HEADS, QBLK), lambda b, i: (b, 0, i))],
        out_specs=pl.BlockSpec((1, QBLK, A_WIDTH), lambda b, i: (b, i, 0)),
        out_shape=jax.ShapeDtypeStruct((B, S, A_WIDTH), BF16),
        scratch_shapes=[pltpu.VMEM((S, QBLK), I32),
                        pltpu.VMEM((S, QBLK), F32),
                        pltpu.VMEM((A_HEADS, 2 * QBLK, QBLK), F32),
                        pltpu.VMEM((A_WIDTH, QBLK), F32),
                        pltpu.VMEM((1, QBLK), I32),
                        pltpu.VMEM((A_HEADS, LANES, QBLK), BF16),
                        pltpu.VMEM((A_HEADS, FAR_CHUNK, QBLK), F32)],
        compiler_params=_params("parallel", "arbitrary"),
        name="dsa",
    )(rel_bias, bkt, ik, kk, vT, qT, iqT, iwT)


GLA_Q = (0, 256)
GLA_K = (256, 512)
GLA_V = (512, 1024)
GLA_R = (1024, 1536)


def _gla_body(gla_ref, glr_ref, wg_ref, bg_ref, ng_ref, o_ref, st_s):
    tg = gla_ref.shape[1]
    C = G_CHUNK

    @pl.when(pl.program_id(1) == 0)
    def _():
        st_s[...] = jnp.zeros_like(st_s)

    xg = jnp.dot(glr_ref[0], wg_ref[...], preferred_element_type=F32, precision=HIGHEST) + bg_ref[...]
    logg = -(jnp.maximum(-xg, 0.0) + jnp.log1p(jnp.exp(-jnp.abs(xg)))) * (1.0 / G_TAU)

    ri = lax.broadcasted_iota(I32, (C, C), 0)
    ci = lax.broadcasted_iota(I32, (C, C), 1)
    tril = ri >= ci
    tril_f = jnp.where(tril, 1.0, 0.0).astype(F32)
    lane_head = lax.broadcasted_iota(I32, (C, LANES), 1) // G_DK
    st_rows = lax.broadcasted_iota(I32, (2 * G_DV, LANES), 0) // G_DV
    st_cols = lax.broadcasted_iota(I32, (2 * G_DV, LANES), 1) // G_DK
    st_diag = st_rows == st_cols

    for ck in range(tg // C):
        r0 = ck * C
        for p in range(G_HEADS // 2):
            lg = logg[r0:r0 + C, p * LANES:(p + 1) * LANES]
            bc = jnp.dot(tril_f, lg, preferred_element_type=F32, precision=HIGHEST)
            bl = bc[C - 1:C, :]
            q = gla_ref[0, r0:r0 + C, GLA_Q[0] + p * LANES:GLA_Q[0] + (p + 1) * LANES].astype(F32) * (G_DK ** -0.5)
            k = gla_ref[0, r0:r0 + C, GLA_K[0] + p * LANES:GLA_K[0] + (p + 1) * LANES].astype(F32)
            v = gla_ref[0, r0:r0 + C, GLA_V[0] + p * 2 * G_DV:GLA_V[0] + (p + 1) * 2 * G_DV]
            q_in = (q * jnp.exp(bc)).astype(BF16)
            k_st = (k * jnp.exp(bl - bc)).astype(BF16)
            q_rel = q * jnp.exp(bc - bl)
            o_intra = []
            for sub in range(2):
                qm = jnp.where(lane_head == sub, q_rel, 0.0).astype(BF16)
                att = lax.dot_general(qm, k_st, NT, preferred_element_type=F32)
                att = jnp.where(tril, att, 0.0).astype(BF16)
                o_intra.append(jnp.dot(att, v[:, sub * G_DV:(sub + 1) * G_DV], preferred_element_type=F32))
            st = st_s[p]
            o_inter = lax.dot_general(q_in, st.astype(BF16), NT, preferred_element_type=F32)
            uT = lax.dot_general(v, k_st, TN, preferred_element_type=F32)
            st_s[p] = st * jnp.exp(bl) + jnp.where(st_diag, uT, 0.0)
            for sub in range(2):
                hd = 2 * p + sub
                o = o_intra[sub] + o_inter[:, sub * G_DV:(sub + 1) * G_DV]
                y = _ln(o) * ng_ref[:, hd * G_DV:(hd + 1) * G_DV]
                g = gla_ref[0, r0:r0 + C, GLA_R[0] + hd * G_DV:GLA_R[0] + (hd + 1) * G_DV].astype(F32)
                o_ref[0, r0:r0 + C, hd * G_DV:(hd + 1) * G_DV] = (y * (g * _sigmoid(g))).astype(BF16)


def _gla_call(gla, glr, wg, bg, ng, tg):
    B, S, _ = gla.shape
    const = lambda b, j: (0, 0)
    return pl.pallas_call(
        _gla_body,
        grid=(B, S // tg),
        in_specs=[pl.BlockSpec((1, tg, 1536), lambda b, j: (b, j, 0)),
                  pl.BlockSpec((1, tg, G_RANK), lambda b, j: (b, j, 0)),
                  pl.BlockSpec((G_RANK, G_KW), const),
                  pl.BlockSpec((1, G_KW), const),
                  pl.BlockSpec((1, G_VW), const)],
        out_specs=pl.BlockSpec((1, tg, G_VW), lambda b, j: (b, j, 0)),
        out_shape=jax.ShapeDtypeStruct((B, S, G_VW), BF16),
        scratch_shapes=[pltpu.VMEM((G_HEADS // 2, 2 * G_DV, LANES), F32)],
        compiler_params=_params("parallel", "arbitrary"),
        name="gla",
    )(gla, glr, wg, bg, ng)


ROUTER_ROWS = 40
ROUTER_E0 = 8


def _post_body(oa_ref, ob_ref, gates_ref, x_ref, gt1_ref, sh2_ref, sc2_ref, wa_ref, wb_ref, wo_ref,
               g1_ref, b1_ref, wr_ref, br_ref, x1_ref, h2_ref, gate_ref):
    tm = x_ref.shape[1]
    D = x_ref.shape[2]
    ya = jnp.dot(oa_ref[0], wa_ref[...], preferred_element_type=F32)
    yb = jnp.dot(ob_ref[0], wb_ref[...], preferred_element_type=F32)
    ga = gates_ref[0, :, 0:D].astype(F32)
    gb = gates_ref[0, :, D:2 * D].astype(F32)
    merged = _sigmoid(ga) * ya + _sigmoid(gb) * yb
    y = jnp.dot(merged.astype(BF16), wo_ref[...], preferred_element_type=F32)
    x1 = _ln(DN_ALPHA * x_ref[0] + gt1_ref[0] * y) * g1_ref[...] + b1_ref[...]
    x1_ref[0] = x1
    h2 = _ln(x1) * (1.0 + sc2_ref[0]) + sh2_ref[0]
    h2_ref[0] = h2.astype(BF16)

    lt = lax.dot_general(wr_ref[...], h2, NT, preferred_element_type=F32, precision=HIGHEST) + br_ref[...]
    gl = lt[0:N_GROUPS]
    gmax = jnp.max(gl, axis=0, keepdims=True)
    g_w = 1.0 / jnp.sum(jnp.exp(gl - gmax), axis=0, keepdims=True)
    r4 = lax.broadcasted_iota(I32, (N_GROUPS, tm), 0)
    g_idx = jnp.min(jnp.where(gl == gmax, r4, N_GROUPS), axis=0, keepdims=True)
    eg = jnp.zeros((EXPERTS_PER_GROUP, tm), F32)
    for g in range(N_GROUPS):
        lo = ROUTER_E0 + g * EXPERTS_PER_GROUP
        eg = jnp.where(g_idx == g, lt[lo:lo + EXPERTS_PER_GROUP], eg)
    r8 = lax.broadcasted_iota(I32, (EXPERTS_PER_GROUP, tm), 0)
    e1 = jnp.max(eg, axis=0, keepdims=True)
    i1 = jnp.min(jnp.where(eg == e1, r8, EXPERTS_PER_GROUP), axis=0, keepdims=True)
    eg2 = jnp.where(r8 == i1, -jnp.inf, eg)
    e2 = jnp.max(eg2, axis=0, keepdims=True)
    i2 = jnp.min(jnp.where(eg2 == e2, r8, EXPERTS_PER_GROUP), axis=0, keepdims=True)
    d = jnp.exp(e2 - e1)
    w1 = g_w / (1.0 + d)
    w2 = g_w * d / (1.0 + d)
    in_group = jnp.where(r8 == i1, w1, 0.0) + jnp.where(r8 == i2, w2, 0.0)
    blocks = [jnp.where(g_idx == g, in_group, 0.0) for g in range(N_GROUPS)]
    blocks.append(jnp.zeros((LANES - N_EXPERTS, tm), F32))
    gate_ref[...] = jnp.concatenate(blocks, axis=0).T


def _post_call(o_a, o_b, gates, x, gt1, sh2, sc2, wa, wb, wo, g1, b1, wr, br, tm):
    B, S, D = x.shape
    nt = S // tm
    const = lambda b, t: (0, 0)
    row = lambda b, t: (b, 0, 0)
    tile = lambda b, t: (b, t, 0)
    return pl.pallas_call(
        _post_body,
        grid=(B, nt),
        in_specs=[pl.BlockSpec((1, tm, A_WIDTH), tile),
                  pl.BlockSpec((1, tm, G_VW), tile),
                  pl.BlockSpec((1, tm, 2 * D), tile),
                  pl.BlockSpec((1, tm, D), tile),
                  pl.BlockSpec((1, 1, D), row),
                  pl.BlockSpec((1, 1, D), row),
                  pl.BlockSpec((1, 1, D), row),
                  pl.BlockSpec(wa.shape, const),
                  pl.BlockSpec(wb.shape, const),
                  pl.BlockSpec(wo.shape, const),
                  pl.BlockSpec((1, D), const),
                  pl.BlockSpec((1, D), const),
                  pl.BlockSpec(wr.shape, const),
                  pl.BlockSpec(br.shape, const)],
        out_specs=(pl.BlockSpec((1, tm, D), tile),
                   pl.BlockSpec((1, tm, D), tile),
                   pl.BlockSpec((tm, LANES), lambda b, t: (b * nt + t, 0))),
        out_shape=(jax.ShapeDtypeStruct((B, S, D), F32),
                   jax.ShapeDtypeStruct((B, S, D), BF16),
                   jax.ShapeDtypeStruct((B * S, LANES), F32)),
        compiler_params=_params("parallel", "parallel"),
        name="post",
    )(o_a, o_b, gates, x, gt1, sh2, sc2, wa, wb, wo, g1, b1, wr, br)


MOE_EXPERTS_PER_STEP = 4


def _moe_body(h2_ref, gate_ref, x1_ref, gt2_ref, w1_ref, w3_ref, w2_ref, g2_ref, b2_ref, o_ref, acc_s):
    c = pl.program_id(1)
    ne = MOE_EXPERTS_PER_STEP

    @pl.when(c == 0)
    def _():
        acc_s[...] = jnp.zeros_like(acc_s)

    h = h2_ref[...]
    gate = pltpu.roll(gate_ref[...], (LANES - c * ne) % LANES, axis=1)
    hid = []
    for j in range(ne):
        a = jnp.dot(h, w1_ref[j], preferred_element_type=F32)
        b = jnp.dot(h, w3_ref[j], preferred_element_type=F32)
        hid.append((a * _sigmoid(a) * b * gate[:, j:j + 1]).astype(BF16))
    acc_s[...] += jnp.dot(jnp.concatenate(hid, axis=1), w2_ref[...], preferred_element_type=F32)

    @pl.when(c == pl.num_programs(1) - 1)
    def _():
        z = DN_ALPHA * x1_ref[...] + gt2_ref[0] * acc_s[...]
        o_ref[...] = _ln(z) * g2_ref[...] + b2_ref[...]


def _moe_call(h2, gate, x1, gt2, w1, w3, w2, g2, b2, tm, S):
    T, D = h2.shape
    ne = MOE_EXPERTS_PER_STEP
    nc = N_EXPERTS // ne
    tiles_per_seq = S // tm
    tile = lambda t, c: (t, 0)
    const = lambda t, c: (0, 0)
    return pl.pallas_call(
        _moe_body,
        grid=(T // tm, nc),
        in_specs=[pl.BlockSpec((tm, D), tile),
                  pl.BlockSpec((tm, LANES), tile),
                  pl.BlockSpec((tm, D), tile),
                  pl.BlockSpec((1, 1, D), lambda t, c: (t // tiles_per_seq, 0, 0)),
                  pl.BlockSpec((ne, D, D_EXPERT), lambda t, c: (c, 0, 0)),
                  pl.BlockSpec((ne, D, D_EXPERT), lambda t, c: (c, 0, 0)),
                  pl.BlockSpec((ne * D_EXPERT, D), lambda t, c: (c, 0)),
                  pl.BlockSpec((1, D), const),
                  pl.BlockSpec((1, D), const)],
        out_specs=pl.BlockSpec((tm, D), tile),
        out_shape=jax.ShapeDtypeStruct((T, D), F32),
        scratch_shapes=[pltpu.VMEM((tm, D), F32)],
        compiler_params=_params("parallel", "arbitrary"),
        name="moe",
    )(h2, gate, x1, gt2, w1, w3, w2, g2, b2)


def _pick(n, pref):
    return pref if n % pref == 0 else n


def kernel(x, c, rel_bias, w_ada, b_ada, w_in, gla_w_gate, gla_b_gate, gla_norm_g, w_branch_a, w_branch_b, w_out, ln1_g, ln1_b, w_router_group, b_router_group, w_router_expert, b_router_expert, w_exp_gate, w_exp_up, w_exp_down, ln2_g, ln2_b):
    B, S, D = x.shape
    assert S % (2 * QBLK) == 0 and D == 1024 and w_ada.shape[0] == DEPTH == 1
    l = 0

    ada = _ada_call(c, w_ada[l], b_ada[l])
    sh1, sc1, gt1, sh2, sc2, gt2 = [ada[:, i * D:(i + 1) * D].reshape(B, 1, D) for i in range(6)]

    offs = np.concatenate([[0], np.cumsum(SPLIT_SIZES)])
    seg = lambda i: w_in[l][:, offs[i]:offs[i + 1]]
    (w_aq, w_ak, w_av, w_iq, w_ik, w_iw, w_gq, w_gk, w_gv, w_gr, w_glr, w_ga, w_gb) = [seg(i) for i in range(13)]
    pad = jnp.zeros((D, TOK_SMALL[1] - TOK_SMALL[0] - IDX_DIM - G_RANK), F32)
    w_tok = jnp.concatenate([w_ak, w_gq, w_gk, w_gv, w_gr, w_ga, w_gb, w_ik, w_glr, pad], axis=1).astype(BF16)
    w_ch = jnp.concatenate([w_aq, w_av, w_iq, w_iw], axis=1).T.astype(BF16)

    tm = _pick(S, 512)
    kk, gla, gates, ik, glr, qT, vT, iqT, iwT = _inproj_call(x, sh1, sc1, w_tok, w_ch, tm)

    o_a = _dsa_call(rel_bias, ik, kk, vT, qT, iqT, iwT)
    o_b = _gla_call(gla, glr, gla_w_gate[l], gla_b_gate[l].reshape(1, G_KW), gla_norm_g[l].reshape(1, G_VW),
                    _pick(S, 256))

    wr = jnp.zeros((ROUTER_ROWS, D), F32)
    wr = wr.at[0:N_GROUPS].set(w_router_group[l].T).at[ROUTER_E0:ROUTER_E0 + N_EXPERTS].set(w_router_expert[l].T)
    br = jnp.zeros((ROUTER_ROWS, 1), F32)
    br = br.at[0:N_GROUPS, 0].set(b_router_group[l]).at[ROUTER_E0:ROUTER_E0 + N_EXPERTS, 0].set(b_router_expert[l])
    x1, h2, gate = _post_call(o_a, o_b, gates, x, gt1, sh2, sc2,
                              w_branch_a[l].astype(BF16), w_branch_b[l].astype(BF16), w_out[l].astype(BF16),
                              ln1_g[l].reshape(1, D), ln1_b[l].reshape(1, D), wr, br, tm)

    tm5 = _pick(S, 1024)
    out = _moe_call(h2.reshape(B * S, D), gate, x1.reshape(B * S, D), gt2,
                    w_exp_gate[l].astype(BF16), w_exp_up[l].astype(BF16),
                    w_exp_down[l].astype(BF16).reshape(N_EXPERTS * D_EXPERT, D),
                    ln2_g[l].reshape(1, D), ln2_b[l].reshape(1, D), tm5, S)
    return out.reshape(B, S, D)
```

```python
import functools
import math

import numpy as np
import jax
import jax.numpy as jnp
from jax import lax
from jax.experimental import pallas as pl
from jax.experimental.pallas import tpu as pltpu

F32 = jnp.float32
BF16 = jnp.bfloat16
I32 = jnp.int32
I16 = jnp.int16
HIGHEST = lax.Precision.HIGHEST

A_HEADS = 8
A_HEAD_DIM = 64
A_WIDTH = A_HEADS * A_HEAD_DIM
IDX_HEADS = 16
IDX_DIM = 32
TOPK_MAX = 256
QBLK = 128
REL_BUCKETS = 32
REL_MAX_DIST = 128
G_HEADS = 4
G_DK = 64
G_DV = 128
G_KW = G_HEADS * G_DK
G_VW = G_HEADS * G_DV
G_RANK = 16
G_TAU = 16.0
G_CHUNK = 64
N_GROUPS = 4
EXPERTS_PER_GROUP = 8
N_EXPERTS = N_GROUPS * EXPERTS_PER_GROUP
D_EXPERT = 256
DEPTH = 1
DN_ALPHA = (2.0 * DEPTH) ** 0.25
LN_EPS = 1e-5
SPLIT_SIZES = (A_WIDTH, A_WIDTH, A_WIDTH, IDX_HEADS * IDX_DIM, IDX_DIM, IDX_HEADS,
               G_KW, G_KW, G_VW, G_VW, G_RANK, 1024, 1024)

LANES = 128
VMEM_LIMIT_BYTES = 56 * 1024 * 1024

NEG = -1e30
LOG2E = math.log2(math.e)
INT_MIN = -2 ** 31
INT_MAX = 2 ** 31 - 1
I16_MIN = -2 ** 15
I16_ROWS = 16

NT = (((1,), (1,)), ((), ()))
TN = (((0,), (0,)), ((), ()))


def _ln(x):
    mu = jnp.mean(x, axis=-1, keepdims=True)
    xc = x - mu
    var = jnp.mean(xc * xc, axis=-1, keepdims=True)
    return xc * lax.rsqrt(var + LN_EPS)


def _sigmoid(x):
    return 1.0 / (1.0 + jnp.exp(-x))


def _params(*sem):
    return pltpu.CompilerParams(dimension_semantics=sem, vmem_limit_bytes=VMEM_LIMIT_BYTES)


def _ada_body(c_ref, w_ref, b_ref, o_ref):
    c = c_ref[...]
    cond = c * _sigmoid(c)
    o_ref[...] = jnp.dot(cond, w_ref[...], preferred_element_type=F32, precision=HIGHEST) + b_ref[...]


def _ada_call(c, w, b):
    B, D = c.shape
    N = w.shape[1]
    tn = 1536
    return pl.pallas_call(
        _ada_body,
        grid=(N // tn,),
        in_specs=[pl.BlockSpec((B, D), lambda j: (0, 0)),
                  pl.BlockSpec((D, tn), lambda j: (0, j)),
                  pl.BlockSpec((1, tn), lambda j: (0, j))],
        out_specs=pl.BlockSpec((B, tn), lambda j: (0, j)),
        out_shape=jax.ShapeDtypeStruct((B, N), F32),
        compiler_params=_params("arbitrary"),
        name="ada",
    )(c, w, b.reshape(1, N))


TOK_K = (0, 512)
TOK_GLA = (512, 2048)
TOK_GATES = (2048, 4096)
TOK_SMALL = (4096, 4224)
CH_Q = (0, 512)
CH_V = (512, 1024)
CH_IQ = (1024, 1536)
CH_IW = (1536, 1552)
IW_SCALE = IDX_HEADS ** -0.5 * IDX_DIM ** -0.5


def _inproj_body(x_ref, sh_ref, sc_ref, wtok_ref, wch_ref,
                 k_ref, gla_ref, gates_ref, ik_ref, glr_ref, qT_ref, vT_ref, iqT_ref, iwT_ref):
    tm = x_ref.shape[1]
    h = (_ln(x_ref[0]) * (1.0 + sc_ref[0]) + sh_ref[0]).astype(BF16)

    def tok(ab):
        return jnp.dot(h, wtok_ref[:, ab[0]:ab[1]], preferred_element_type=F32)

    def ch(ab):
        return lax.dot_general(wch_ref[ab[0]:ab[1], :], h, NT, preferred_element_type=F32)

    kres = tok(TOK_K)
    for p in range(A_WIDTH // LANES):
        k_ref[0, p] = kres[:, p * LANES:(p + 1) * LANES].astype(BF16)
    gla_ref[0] = tok(TOK_GLA).astype(BF16)
    gates_ref[0] = tok(TOK_GATES).astype(BF16)
    small = tok(TOK_SMALL)
    ik_ref[0] = small[:, :IDX_DIM].astype(BF16)
    glr_ref[0] = small[:, IDX_DIM:IDX_DIM + G_RANK]

    qT_ref[0] = (ch(CH_Q) * (A_HEAD_DIM ** -0.5 * LOG2E)).astype(BF16)
    vres = ch(CH_V).astype(BF16)
    for j in range(tm // LANES):
        vT_ref[0, j] = vres[:, j * LANES:(j + 1) * LANES]
    iqT_ref[0] = ch(CH_IQ).astype(BF16)
    iwT_ref[0] = ch(CH_IW) * IW_SCALE


def _inproj_call(x, sh1, sc1, w_tok, w_ch, tm):
    B, S, D = x.shape
    nt = S // tm
    const = lambda b, t: (0, 0)
    out_shape = (
        jax.ShapeDtypeStruct((B, A_WIDTH // LANES, S, LANES), BF16),
        jax.ShapeDtypeStruct((B, S, 1536), BF16),
        jax.ShapeDtypeStruct((B, S, 2048), BF16),
        jax.ShapeDtypeStruct((B, S, IDX_DIM), BF16),
        jax.ShapeDtypeStruct((B, S, G_RANK), F32),
        jax.ShapeDtypeStruct((B, A_WIDTH, S), BF16),
        jax.ShapeDtypeStruct((B, S // LANES, A_WIDTH, LANES), BF16),
        jax.ShapeDtypeStruct((B, IDX_HEADS * IDX_DIM, S), BF16),
        jax.ShapeDtypeStruct((B, IDX_HEADS, S), F32),
    )
    out_specs = (
        pl.BlockSpec((1, A_WIDTH // LANES, tm, LANES), lambda b, t: (b, 0, t, 0)),
        pl.BlockSpec((1, tm, 1536), lambda b, t: (b, t, 0)),
        pl.BlockSpec((1, tm, 2048), lambda b, t: (b, t, 0)),
        pl.BlockSpec((1, tm, IDX_DIM), lambda b, t: (b, t, 0)),
        pl.BlockSpec((1, tm, G_RANK), lambda b, t: (b, t, 0)),
        pl.BlockSpec((1, A_WIDTH, tm), lambda b, t: (b, 0, t)),
        pl.BlockSpec((1, tm // LANES, A_WIDTH, LANES), lambda b, t: (b, t, 0, 0)),
        pl.BlockSpec((1, IDX_HEADS * IDX_DIM, tm), lambda b, t: (b, 0, t)),
        pl.BlockSpec((1, IDX_HEADS, tm), lambda b, t: (b, 0, t)),
    )
    return pl.pallas_call(
        _inproj_body,
        grid=(B, nt),
        in_specs=[pl.BlockSpec((1, tm, D), lambda b, t: (b, t, 0)),
                  pl.BlockSpec((1, 1, D), lambda b, t: (b, 0, 0)),
                  pl.BlockSpec((1, 1, D), lambda b, t: (b, 0, 0)),
                  pl.BlockSpec(w_tok.shape, const),
                  pl.BlockSpec(w_ch.shape, const)],
        out_specs=out_specs,
        out_shape=out_shape,
        compiler_params=_params("parallel", "parallel"),
        name="inproj",
    )(x, sh1, sc1, w_tok, w_ch)


IDX_CHUNK = 256
CNT_CHUNK = 512
ATT_CHUNK = 256
TBL_PAD = 2 * QBLK
ACC_ROWS = A_HEAD_DIM + 16


def _rel_bucket_table():
    s = np.arange(2 * QBLK)[:, None]
    t = np.arange(QBLK)[None, :]
    dist = np.maximum(t + QBLK - s, 0)
    max_exact = REL_BUCKETS // 2
    d_f = np.maximum(dist, 1).astype(np.float32)
    large = max_exact + (np.log(d_f / max_exact) / math.log(REL_MAX_DIST / max_exact)
                         * (REL_BUCKETS - max_exact)).astype(np.int32)
    large = np.minimum(large, REL_BUCKETS - 1)
    return np.where(dist < max_exact, dist, large).astype(np.int32)


def _far_bucket():
    max_exact = REL_BUCKETS // 2
    v = max_exact + int(np.float32(np.log(np.float32(QBLK + 1) / max_exact) / math.log(REL_MAX_DIST / max_exact)
                                   * (REL_BUCKETS - max_exact)))
    assert min(v, REL_BUCKETS - 1) == REL_BUCKETS - 1
    return REL_BUCKETS - 1


def _dsa_body(rb_ref, bkt_ref, ik_ref, kk_ref, vT_ref, qT_ref, iqT_ref, iwT_ref, o_ref,
              key_s, hi_s, lo_s, lo2_s, madd_s, tbl_s, oT_s, xcut_s, qm_s, sa_s, sb_s, acc_s, mall_s, mblk_s,
              *, topk, idx_bits):
    i = pl.program_id(1)
    nck = (i + 2) // 2
    t_idx = i * QBLK + lax.broadcasted_iota(I32, (1, QBLK), 1)

    @pl.when(i == 0)
    def _():
        bkt = bkt_ref[...]
        tbl_s[...] = jnp.zeros_like(tbl_s)
        for h in range(A_HEADS):
            t = jnp.zeros((2 * QBLK, QBLK), F32)
            for k in range(REL_BUCKETS):
                t = jnp.where(bkt == k, rb_ref[k, h], t)
            tbl_s[h, TBL_PAD:TBL_PAD + 2 * QBLK, :] = (t - rb_ref[_far_bucket(), h]) * LOG2E

    def score_chunk(c, carry):
        s0 = pl.multiple_of(c * IDX_CHUNK, IDX_CHUNK)
        kc = ik_ref[0, pl.ds(s0, IDX_CHUNK), :]
        acc = jnp.zeros((IDX_CHUNK, QBLK), F32)
        for hp in range(IDX_HEADS // 2):
            r0 = hp * 2 * IDX_DIM
            rhs = jnp.concatenate([iqT_ref[0, r0:r0 + IDX_DIM, :],
                                   iqT_ref[0, r0 + IDX_DIM:r0 + 2 * IDX_DIM, :]], axis=1)
            z = jnp.dot(kc, rhs, preferred_element_type=F32)
            acc = acc + jnp.maximum(z[:, :QBLK], 0.0) * iwT_ref[0, 2 * hp:2 * hp + 1, :]
            acc = acc + jnp.maximum(z[:, QBLK:], 0.0) * iwT_ref[0, 2 * hp + 1:2 * hp + 2, :]
        s_idx = s0 + lax.broadcasted_iota(I32, (IDX_CHUNK, QBLK), 0)
        acc = jnp.where(acc == 0.0, 0.0, acc)
        bits = pltpu.bitcast(acc, I32)
        key = jnp.where(bits < 0, bits ^ INT_MAX, bits)
        key = jnp.where(s_idx <= t_idx, key, INT_MIN)
        key_s[pl.ds(s0, IDX_CHUNK), :] = key
        hi_s[pl.ds(s0, IDX_CHUNK), :] = (key >> 16).astype(I16)
        lo_s[pl.ds(s0, IDX_CHUNK), :] = (key ^ 0x8000).astype(I16)
        return carry

    lax.fori_loop(0, nck, score_chunk, 0)

    @pl.when(nck % 2 == 1)
    def _():
        pad0 = pl.multiple_of(nck * IDX_CHUNK, IDX_CHUNK)
        key_s[pl.ds(pad0, IDX_CHUNK), :] = jnp.full((IDX_CHUNK, QBLK), INT_MIN, I32)
        hi_s[pl.ds(pad0, IDX_CHUNK), :] = jnp.full((IDX_CHUNK, QBLK), I16_MIN, I16)
        lo_s[pl.ds(pad0, IDX_CHUNK), :] = jnp.full((IDX_CHUNK, QBLK), I16_MIN, I16)

    ncnt = (nck + 1) // 2

    def count(pred):
        def body(c, cnt):
            s0 = pl.multiple_of(c * CNT_CHUNK, CNT_CHUNK)
            k = key_s[pl.ds(s0, CNT_CHUNK), :]
            s_idx = s0 + lax.broadcasted_iota(I32, (CNT_CHUNK, QBLK), 0)
            m = jnp.where(pred(k, s_idx), 1, 0)
            return cnt + jnp.sum(m.reshape(CNT_CHUNK // 8, 8, QBLK), axis=0)
        cnt = lax.fori_loop(0, ncnt, body, jnp.zeros((8, QBLK), I32))
        return jnp.sum(cnt, axis=0, keepdims=True)

    def tile16(v):
        return jnp.broadcast_to(v, (I16_ROWS, QBLK)).astype(I16)

    def count16(buf, pred):
        def body(c, cnt):
            s0 = pl.multiple_of(c * CNT_CHUNK, CNT_CHUNK)
            k = buf[pl.ds(s0, CNT_CHUNK), :]
            parts = [jnp.where(pred(k[r * I16_ROWS:(r + 1) * I16_ROWS]), jnp.int16(1), jnp.int16(0))
                     for r in range(CNT_CHUNK // I16_ROWS)]
            while len(parts) > 1:
                parts = [parts[j] + parts[j + 1] for j in range(0, len(parts), 2)]
            return cnt + parts[0]
        cnt = lax.fori_loop(0, ncnt, body, jnp.zeros((I16_ROWS, QBLK), I16))
        return jnp.sum(cnt.astype(I32), axis=0, keepdims=True)

    def search16(buf, k_need, cnt_floor):
        c0 = count16(buf, lambda k: k >= jnp.int16(0))
        ok = c0 >= k_need
        T = jnp.where(ok, 0, I16_MIN).astype(I32)
        cnt_T = jnp.where(ok, c0, cnt_floor)

        def bit_body(j, carry):
            T, cnt_T = carry
            cand = T | jnp.left_shift(jnp.int32(1), 14 - j)
            cand16 = tile16(cand)
            c = count16(buf, lambda k: k >= cand16)
            ok = c >= k_need
            return jnp.where(ok, cand, T), jnp.where(ok, c, cnt_T)

        return lax.fori_loop(0, 15, bit_body, (T, cnt_T))

    n_all = jnp.full((1, QBLK), ncnt * CNT_CHUNK, I32)
    T_hi, cnt_hi = search16(hi_s, topk, n_all)
    t_hi16 = tile16(T_hi)
    gt_hi = count16(hi_s, lambda k: k > t_hi16)

    def low_half(c, carry):
        s0 = pl.multiple_of(c * CNT_CHUNK, CNT_CHUNK)
        for r in range(CNT_CHUNK // I16_ROWS):
            rows = pl.ds(s0 + r * I16_ROWS, I16_ROWS)
            lo2_s[rows, :] = jnp.where(hi_s[rows, :] == t_hi16, lo_s[rows, :], jnp.int16(I16_MIN))
        return carry

    lax.fori_loop(0, ncnt, low_half, 0)
    T_lo, cnt_lo = search16(lo2_s, topk - gt_hi, cnt_hi - gt_hi)
    t_lo16 = tile16(T_lo)
    gt_lo = count16(lo2_s, lambda k: k > t_lo16)
    T = jnp.left_shift(T_hi, 16) | ((T_lo - I16_MIN) & 0xFFFF)
    cnt_ge = gt_hi + cnt_lo
    cnt_gt = gt_hi + gt_lo

    need = topk - cnt_gt
    excess = jnp.where((cnt_ge - cnt_gt > need) & (T > INT_MIN), 1.0, 0.0)
    xcut_s[...] = jnp.full((1, QBLK), INT_MAX, I32)

    @pl.when(jnp.max(excess) > 0.0)
    def _():
        X = jnp.zeros((1, QBLK), I32)
        for b in range(idx_bits - 1, -1, -1):
            cand = X | (1 << b)
            f = count(lambda k, s: (k == T) & (s < cand))
            X = jnp.where(f < need, cand, X)
        xcut_s[...] = X

    xcut = xcut_s[...]

    def mask_chunk(c, carry):
        s0 = pl.multiple_of(c * IDX_CHUNK, IDX_CHUNK)
        k = key_s[pl.ds(s0, IDX_CHUNK), :]
        s_idx = s0 + lax.broadcasted_iota(I32, (IDX_CHUNK, QBLK), 0)
        sel = ((k > T) | ((k == T) & (s_idx <= xcut))) & (s_idx <= t_idx)
        madd_s[pl.ds(s0, IDX_CHUNK), :] = jnp.where(sel, 0.0, NEG)
        return carry

    lax.fori_loop(0, nck, mask_chunk, 0)

    c_last = i // 2
    even = 1 - (i - 2 * c_last)
    row_head = lax.broadcasted_iota(I32, (LANES, QBLK), 0) // A_HEAD_DIM
    for h in range(A_HEADS):
        qp = qT_ref[0, (h // 2) * LANES:(h // 2 + 1) * LANES, :]
        qm_s[h] = jnp.where(row_head == h % 2, qp, jnp.zeros_like(qp))
    acc_s[...] = jnp.zeros_like(acc_s)
    ones = jnp.ones((ACC_ROWS - A_HEAD_DIM, ATT_CHUNK), BF16)

    def logits(c, s_buf):
        s0 = pl.multiple_of(c * ATT_CHUNK, ATT_CHUNK)
        madd = madd_s[pl.ds(s0, ATT_CHUNK), :]
        off = jnp.where(c == c_last, 2 * QBLK + QBLK * even, jnp.where(c == c_last - 1, QBLK * even, 0))
        off = pl.multiple_of(off, QBLK)
        m_blk = []
        for h in range(A_HEADS):
            kc = kk_ref[0, h // 2, pl.ds(s0, ATT_CHUNK), :]
            s = jnp.dot(kc, qm_s[h], preferred_element_type=F32) + madd + tbl_s[h, pl.ds(off, ATT_CHUNK), :]
            s_buf[h] = s
            m_blk.append(jnp.max(s, axis=0, keepdims=True))
        return jnp.concatenate(m_blk, axis=0)

    def accumulate(c, s_buf, m_all, m_blk):
        m_new = jnp.maximum(m_all, m_blk)
        alpha = jnp.exp2(m_all - m_new)
        for h in range(A_HEADS):
            rows = slice(h * A_HEAD_DIM, (h + 1) * A_HEAD_DIM)
            p = jnp.exp2(s_buf[h] - m_new[h:h + 1]).astype(BF16)
            vt = jnp.concatenate([vT_ref[0, 2 * c + u, rows, :] for u in range(ATT_CHUNK // QBLK)], axis=1)
            vt = jnp.concatenate([vt, ones], axis=0)
            acc_s[h] = alpha[h:h + 1] * acc_s[h] + jnp.dot(vt, p, preferred_element_type=F32)
        return m_new

    def att_body(pair, carry):
        m_all, m_blk = carry
        c = 2 * pair
        m_b = logits(c + 1, sb_s)
        m_all = accumulate(c, sa_s, m_all, m_blk)
        m_a = logits(c + 2, sa_s)
        return accumulate(c + 1, sb_s, m_all, m_b), m_a

    n_pairs = c_last // 2
    carry = (jnp.full((A_HEADS, QBLK), NEG, F32), logits(0, sa_s))
    m_all, m_blk = lax.fori_loop(0, n_pairs, att_body, carry)
    mall_s[...] = m_all
    mblk_s[...] = m_blk

    @pl.when(c_last % 2 == 1)
    def _():
        m_b = logits(c_last, sb_s)
        m_all = accumulate(c_last - 1, sa_s, mall_s[...], mblk_s[...])
        accumulate(c_last, sb_s, m_all, m_b)

    @pl.when(c_last % 2 == 0)
    def _():
        accumulate(c_last, sa_s, mall_s[...], mblk_s[...])

    for h in range(A_HEADS):
        rows = slice(h * A_HEAD_DIM, (h + 1) * A_HEAD_DIM)
        oT_s[rows, :] = acc_s[h, 0:A_HEAD_DIM, :] / acc_s[h, A_HEAD_DIM:A_HEAD_DIM + 1, :]
    o_ref[0] = oT_s[...].T.astype(BF16)


def _dsa_call(rel_bias, ik, kk, vT, qT, iqT, iwT):
    B, S, _ = ik.shape
    nb = S // QBLK
    topk = min(TOPK_MAX, S // 4)
    bkt = jnp.asarray(_rel_bucket_table())
    body = functools.partial(_dsa_body, topk=topk, idx_bits=int(math.log2(S)))
    return pl.pallas_call(
        body,
        grid=(B, nb),
        in_specs=[pl.BlockSpec(memory_space=pltpu.SMEM),
                  pl.BlockSpec((2 * QBLK, QBLK), lambda b, i: (0, 0)),
                  pl.BlockSpec((1, S, IDX_DIM), lambda b, i: (b, 0, 0)),
                  pl.BlockSpec((1, A_WIDTH // LANES, S, LANES), lambda b, i: (b, 0, 0, 0)),
                  pl.BlockSpec((1, S // LANES, A_WIDTH, LANES), lambda b, i: (b, 0, 0, 0)),
                  pl.BlockSpec((1, A_WIDTH, QBLK), lambda b, i: (b, 0, i)),
                  pl.BlockSpec((1, IDX_HEADS * IDX_DIM, QBLK), lambda b, i: (b, 0, i)),
                  pl.BlockSpec((1, IDX_HEADS, QBLK), lambda b, i: (b, 0, i))],
        out_specs=pl.BlockSpec((1, QBLK, A_WIDTH), lambda b, i: (b, i, 0)),
        out_shape=jax.ShapeDtypeStruct((B, S, A_WIDTH), BF16),
        scratch_shapes=[pltpu.VMEM((S, QBLK), I32),
                        pltpu.VMEM((S, QBLK), I16),
                        pltpu.VMEM((S, QBLK), I16),
                        pltpu.VMEM((S, QBLK), I16),
                        pltpu.VMEM((S, QBLK), F32),
                        pltpu.VMEM((A_HEADS, TBL_PAD + 3 * QBLK, QBLK), F32),
                        pltpu.VMEM((A_WIDTH, QBLK), F32),
                        pltpu.VMEM((1, QBLK), I32),
                        pltpu.VMEM((A_HEADS, LANES, QBLK), BF16),
                        pltpu.VMEM((A_HEADS, ATT_CHUNK, QBLK), F32),
                        pltpu.VMEM((A_HEADS, ATT_CHUNK, QBLK), F32),
                        pltpu.VMEM((A_HEADS, ACC_ROWS, QBLK), F32),
                        pltpu.VMEM((A_HEADS, QBLK), F32),
                        pltpu.VMEM((A_HEADS, QBLK), F32)],
        compiler_params=_params("parallel", "arbitrary"),
        name="dsa",
    )(rel_bias, bkt, ik, kk, vT, qT, iqT, iwT)


GLA_Q = (0, 256)
GLA_K = (256, 512)
GLA_V = (512, 1024)
GLA_R = (1024, 1536)


def _gla_body(gla_ref, glr_ref, wg_ref, bg_ref, ng_ref, o_ref, st_s):
    tg = gla_ref.shape[1]
    C = G_CHUNK

    @pl.when(pl.program_id(1) == 0)
    def _():
        st_s[...] = jnp.zeros_like(st_s)

    xg = jnp.dot(glr_ref[0], wg_ref[...], preferred_element_type=F32, precision=HIGHEST) + bg_ref[...]
    logg = -(jnp.maximum(-xg, 0.0) + jnp.log1p(jnp.exp(-jnp.abs(xg)))) * (1.0 / G_TAU)

    ri = lax.broadcasted_iota(I32, (C, C), 0)
    ci = lax.broadcasted_iota(I32, (C, C), 1)
    tril = ri >= ci
    tril_f = jnp.where(tril, 1.0, 0.0).astype(F32)
    lane_head = lax.broadcasted_iota(I32, (C, LANES), 1) // G_DK
    st_rows = lax.broadcasted_iota(I32, (2 * G_DV, LANES), 0) // G_DV
    st_cols = lax.broadcasted_iota(I32, (2 * G_DV, LANES), 1) // G_DK
    st_diag = st_rows == st_cols

    for ck in range(tg // C):
        r0 = ck * C
        for p in range(G_HEADS // 2):
            lg = logg[r0:r0 + C, p * LANES:(p + 1) * LANES]
            bc = jnp.dot(tril_f, lg, preferred_element_type=F32, precision=HIGHEST)
            bl = bc[C - 1:C, :]
            q = gla_ref[0, r0:r0 + C, GLA_Q[0] + p * LANES:GLA_Q[0] + (p + 1) * LANES].astype(F32) * (G_DK ** -0.5)
            k = gla_ref[0, r0:r0 + C, GLA_K[0] + p * LANES:GLA_K[0] + (p + 1) * LANES].astype(F32)
            v = gla_ref[0, r0:r0 + C, GLA_V[0] + p * 2 * G_DV:GLA_V[0] + (p + 1) * 2 * G_DV]
            q_in = (q * jnp.exp(bc)).astype(BF16)
            k_st = (k * jnp.exp(bl - bc)).astype(BF16)
            q_rel = q * jnp.exp(bc - bl)
            o_intra = []
            for sub in range(2):
                qm = jnp.where(lane_head == sub, q_rel, 0.0).astype(BF16)
                att = lax.dot_general(qm, k_st, NT, preferred_element_type=F32)
                att = jnp.where(tril, att, 0.0).astype(BF16)
                o_intra.append(jnp.dot(att, v[:, sub * G_DV:(sub + 1) * G_DV], preferred_element_type=F32))
            st = st_s[p]
            o_inter = lax.dot_general(q_in, st.astype(BF16), NT, preferred_element_type=F32)
            uT = lax.dot_general(v, k_st, TN, preferred_element_type=F32)
            st_s[p] = st * jnp.exp(bl) + jnp.where(st_diag, uT, 0.0)
            for sub in range(2):
                hd = 2 * p + sub
                o = o_intra[sub] + o_inter[:, sub * G_DV:(sub + 1) * G_DV]
                y = _ln(o) * ng_ref[:, hd * G_DV:(hd + 1) * G_DV]
                g = gla_ref[0, r0:r0 + C, GLA_R[0] + hd * G_DV:GLA_R[0] + (hd + 1) * G_DV].astype(F32)
                o_ref[0, r0:r0 + C, hd * G_DV:(hd + 1) * G_DV] = (y * (g * _sigmoid(g))).astype(BF16)


def _gla_call(gla, glr, wg, bg, ng, tg):
    B, S, _ = gla.shape
    const = lambda b, j: (0, 0)
    return pl.pallas_call(
        _gla_body,
        grid=(B, S // tg),
        in_specs=[pl.BlockSpec((1, tg, 1536), lambda b, j: (b, j, 0)),
                  pl.BlockSpec((1, tg, G_RANK), lambda b, j: (b, j, 0)),
                  pl.BlockSpec((G_RANK, G_KW), const),
                  pl.BlockSpec((1, G_KW), const),
                  pl.BlockSpec((1, G_VW), const)],
        out_specs=pl.BlockSpec((1, tg, G_VW), lambda b, j: (b, j, 0)),
        out_shape=jax.ShapeDtypeStruct((B, S, G_VW), BF16),
        scratch_shapes=[pltpu.VMEM((G_HEADS // 2, 2 * G_DV, LANES), F32)],
        compiler_params=_params("parallel", "arbitrary"),
        name="gla",
    )(gla, glr, wg, bg, ng)


ROUTER_ROWS = 40
ROUTER_E0 = 8


def _post_body(oa_ref, ob_ref, gates_ref, x_ref, gt1_ref, sh2_ref, sc2_ref, wa_ref, wb_ref, wo_ref,
               g1_ref, b1_ref, wr_ref, br_ref, x1_ref, h2_ref, gate_ref):
    tm = x_ref.shape[1]
    D = x_ref.shape[2]
    ya = jnp.dot(oa_ref[0], wa_ref[...], preferred_element_type=F32)
    yb = jnp.dot(ob_ref[0], wb_ref[...], preferred_element_type=F32)
    ga = gates_ref[0, :, 0:D].astype(F32)
    gb = gates_ref[0, :, D:2 * D].astype(F32)
    merged = _sigmoid(ga) * ya + _sigmoid(gb) * yb
    y = jnp.dot(merged.astype(BF16), wo_ref[...], preferred_element_type=F32)
    x1 = _ln(DN_ALPHA * x_ref[0] + gt1_ref[0] * y) * g1_ref[...] + b1_ref[...]
    x1_ref[0] = x1
    h2 = _ln(x1) * (1.0 + sc2_ref[0]) + sh2_ref[0]
    h2_ref[0] = h2.astype(BF16)

    lt = lax.dot_general(wr_ref[...], h2, NT, preferred_element_type=F32, precision=HIGHEST) + br_ref[...]
    gl = lt[0:N_GROUPS]
    gmax = jnp.max(gl, axis=0, keepdims=True)
    g_w = 1.0 / jnp.sum(jnp.exp(gl - gmax), axis=0, keepdims=True)
    r4 = lax.broadcasted_iota(I32, (N_GROUPS, tm), 0)
    g_idx = jnp.min(jnp.where(gl == gmax, r4, N_GROUPS), axis=0, keepdims=True)
    eg = jnp.zeros((EXPERTS_PER_GROUP, tm), F32)
    for g in range(N_GROUPS):
        lo = ROUTER_E0 + g * EXPERTS_PER_GROUP
        eg = jnp.where(g_idx == g, lt[lo:lo + EXPERTS_PER_GROUP], eg)
    r8 = lax.broadcasted_iota(I32, (EXPERTS_PER_GROUP, tm), 0)
    e1 = jnp.max(eg, axis=0, keepdims=True)
    i1 = jnp.min(jnp.where(eg == e1, r8, EXPERTS_PER_GROUP), axis=0, keepdims=True)
    eg2 = jnp.where(r8 == i1, -jnp.inf, eg)
    e2 = jnp.max(eg2, axis=0, keepdims=True)
    i2 = jnp.min(jnp.where(eg2 == e2, r8, EXPERTS_PER_GROUP), axis=0, keepdims=True)
    d = jnp.exp(e2 - e1)
    w1 = g_w / (1.0 + d)
    w2 = g_w * d / (1.0 + d)
    in_group = jnp.where(r8 == i1, w1, 0.0) + jnp.where(r8 == i2, w2, 0.0)
    blocks = [jnp.where(g_idx == g, in_group, 0.0) for g in range(N_GROUPS)]
    blocks.append(jnp.zeros((LANES - N_EXPERTS, tm), F32))
    gate_ref[...] = jnp.concatenate(blocks, axis=0).T


def _post_call(o_a, o_b, gates, x, gt1, sh2, sc2, wa, wb, wo, g1, b1, wr, br, tm):
    B, S, D = x.shape
    nt = S // tm
    const = lambda b, t: (0, 0)
    row = lambda b, t: (b, 0, 0)
    tile = lambda b, t: (b, t, 0)
    return pl.pallas_call(
        _post_body,
        grid=(B, nt),
        in_specs=[pl.BlockSpec((1, tm, A_WIDTH), tile),
                  pl.BlockSpec((1, tm, G_VW), tile),
                  pl.BlockSpec((1, tm, 2 * D), tile),
                  pl.BlockSpec((1, tm, D), tile),
                  pl.BlockSpec((1, 1, D), row),
                  pl.BlockSpec((1, 1, D), row),
                  pl.BlockSpec((1, 1, D), row),
                  pl.BlockSpec(wa.shape, const),
                  pl.BlockSpec(wb.shape, const),
                  pl.BlockSpec(wo.shape, const),
                  pl.BlockSpec((1, D), const),
                  pl.BlockSpec((1, D), const),
                  pl.BlockSpec(wr.shape, const),
                  pl.BlockSpec(br.shape, const)],
        out_specs=(pl.BlockSpec((1, tm, D), tile),
                   pl.BlockSpec((1, tm, D), tile),
                   pl.BlockSpec((tm, LANES), lambda b, t: (b * nt + t, 0))),
        out_shape=(jax.ShapeDtypeStruct((B, S, D), F32),
                   jax.ShapeDtypeStruct((B, S, D), BF16),
                   jax.ShapeDtypeStruct((B * S, LANES), F32)),
        compiler_params=_params("parallel", "parallel"),
        name="post",
    )(o_a, o_b, gates, x, gt1, sh2, sc2, wa, wb, wo, g1, b1, wr, br)


MOE_EXPERTS_PER_STEP = 4


def _moe_body(h2_ref, gate_ref, x1_ref, gt2_ref, w1_ref, w3_ref, w2_ref, g2_ref, b2_ref, o_ref, acc_s):
    c = pl.program_id(1)
    ne = MOE_EXPERTS_PER_STEP

    @pl.when(c == 0)
    def _():
        acc_s[...] = jnp.zeros_like(acc_s)

    h = h2_ref[...]
    gate = pltpu.roll(gate_ref[...], (LANES - c * ne) % LANES, axis=1)
    hid = []
    for j in range(ne):
        a = jnp.dot(h, w1_ref[j], preferred_element_type=F32)
        b = jnp.dot(h, w3_ref[j], preferred_element_type=F32)
        hid.append((a * _sigmoid(a) * b * gate[:, j:j + 1]).astype(BF16))
    acc_s[...] += jnp.dot(jnp.concatenate(hid, axis=1), w2_ref[...], preferred_element_type=F32)

    @pl.when(c == pl.num_programs(1) - 1)
    def _():
        z = DN_ALPHA * x1_ref[...] + gt2_ref[0] * acc_s[...]
        o_ref[...] = _ln(z) * g2_ref[...] + b2_ref[...]


def _moe_call(h2, gate, x1, gt2, w1, w3, w2, g2, b2, tm, S):
    T, D = h2.shape
    ne = MOE_EXPERTS_PER_STEP
    nc = N_EXPERTS // ne
    tiles_per_seq = S // tm
    tile = lambda t, c: (t, 0)
    const = lambda t, c: (0, 0)
    return pl.pallas_call(
        _moe_body,
        grid=(T // tm, nc),
        in_specs=[pl.BlockSpec((tm, D), tile),
                  pl.BlockSpec((tm, LANES), tile),
                  pl.BlockSpec((tm, D), tile),
                  pl.BlockSpec((1, 1, D), lambda t, c: (t // tiles_per_seq, 0, 0)),
                  pl.BlockSpec((ne, D, D_EXPERT), lambda t, c: (c, 0, 0)),
                  pl.BlockSpec((ne, D, D_EXPERT), lambda t, c: (c, 0, 0)),
                  pl.BlockSpec((ne * D_EXPERT, D), lambda t, c: (c, 0)),
                  pl.BlockSpec((1, D), const),
                  pl.BlockSpec((1, D), const)],
        out_specs=pl.BlockSpec((tm, D), tile),
        out_shape=jax.ShapeDtypeStruct((T, D), F32),
        scratch_shapes=[pltpu.VMEM((tm, D), F32)],
        compiler_params=_params("parallel", "arbitrary"),
        name="moe",
    )(h2, gate, x1, gt2, w1, w3, w2, g2, b2)


def _pick(n, pref):
    return pref if n % pref == 0 else n


def kernel(x, c, rel_bias, w_ada, b_ada, w_in, gla_w_gate, gla_b_gate, gla_norm_g, w_branch_a, w_branch_b, w_out, ln1_g, ln1_b, w_router_group, b_router_group, w_router_expert, b_router_expert, w_exp_gate, w_exp_up, w_exp_down, ln2_g, ln2_b):
    B, S, D = x.shape
    assert S % (2 * QBLK) == 0 and D == 1024 and w_ada.shape[0] == DEPTH == 1
    l = 0

    ada = _ada_call(c, w_ada[l], b_ada[l])
    sh1, sc1, gt1, sh2, sc2, gt2 = [ada[:, i * D:(i + 1) * D].reshape(B, 1, D) for i in range(6)]

    offs = np.concatenate([[0], np.cumsum(SPLIT_SIZES)])
    seg = lambda i: w_in[l][:, offs[i]:offs[i + 1]]
    (w_aq, w_ak, w_av, w_iq, w_ik, w_iw, w_gq, w_gk, w_gv, w_gr, w_glr, w_ga, w_gb) = [seg(i) for i in range(13)]
    pad = jnp.zeros((D, TOK_SMALL[1] - TOK_SMALL[0] - IDX_DIM - G_RANK), F32)
    w_tok = jnp.concatenate([w_ak, w_gq, w_gk, w_gv, w_gr, w_ga, w_gb, w_ik, w_glr, pad], axis=1).astype(BF16)
    w_ch = jnp.concatenate([w_aq, w_av, w_iq, w_iw], axis=1).T.astype(BF16)

    tm = _pick(S, 512)
    kk, gla, gates, ik, glr, qT, vT, iqT, iwT = _inproj_call(x, sh1, sc1, w_tok, w_ch, tm)

    o_a = _dsa_call(rel_bias, ik, kk, vT, qT, iqT, iwT)
    o_b = _gla_call(gla, glr, gla_w_gate[l], gla_b_gate[l].reshape(1, G_KW), gla_norm_g[l].reshape(1, G_VW),
                    _pick(S, 256))

    wr = jnp.zeros((ROUTER_ROWS, D), F32)
    wr = wr.at[0:N_GROUPS].set(w_router_group[l].T).at[ROUTER_E0:ROUTER_E0 + N_EXPERTS].set(w_router_expert[l].T)
    br = jnp.zeros((ROUTER_ROWS, 1), F32)
    br = br.at[0:N_GROUPS, 0].set(b_router_group[l]).at[ROUTER_E0:ROUTER_E0 + N_EXPERTS, 0].set(b_router_expert[l])
    x1, h2, gate = _post_call(o_a, o_b, gates, x, gt1, sh2, sc2,
                              w_branch_a[l].astype(BF16), w_branch_b[l].astype(BF16), w_out[l].astype(BF16),
                              ln1_g[l].reshape(1, D), ln1_b[l].reshape(1, D), wr, br, tm)

    tm5 = _pick(S, 1024)
    out = _moe_call(h2.reshape(B * S, D), gate, x1.reshape(B * S, D), gt2,
                    w_exp_gate[l].astype(BF16), w_exp_up[l].astype(BF16),
                    w_exp_down[l].astype(BF16).reshape(N_EXPERTS * D_EXPERT, D),
                    ln2_g[l].reshape(1, D), ln2_b[l].reshape(1, D), tm5, S)
    return out.reshape(B, S, D)
```

```python
import functools
import math

import numpy as np
import jax
import jax.numpy as jnp
from jax import lax
from jax.experimental import pallas as pl
from jax.experimental.pallas import tpu as pltpu

F32 = jnp.float32
BF16 = jnp.bfloat16
I32 = jnp.int32
I16 = jnp.int16
HIGHEST = lax.Precision.HIGHEST

A_HEADS = 8
A_HEAD_DIM = 64
A_WIDTH = A_HEADS * A_HEAD_DIM
IDX_HEADS = 16
IDX_DIM = 32
TOPK_MAX = 256
QBLK = 128
REL_BUCKETS = 32
REL_MAX_DIST = 128
G_HEADS = 4
G_DK = 64
G_DV = 128
G_KW = G_HEADS * G_DK
G_VW = G_HEADS * G_DV
G_RANK = 16
G_TAU = 16.0
G_CHUNK = 64
N_GROUPS = 4
EXPERTS_PER_GROUP = 8
N_EXPERTS = N_GROUPS * EXPERTS_PER_GROUP
D_EXPERT = 256
DEPTH = 1
DN_ALPHA = (2.0 * DEPTH) ** 0.25
LN_EPS = 1e-5
SPLIT_SIZES = (A_WIDTH, A_WIDTH, A_WIDTH, IDX_HEADS * IDX_DIM, IDX_DIM, IDX_HEADS,
               G_KW, G_KW, G_VW, G_VW, G_RANK, 1024, 1024)

LANES = 128
VMEM_LIMIT_BYTES = 56 * 1024 * 1024

NEG = -1e30
LOG2E = math.log2(math.e)
INT_MIN = -2 ** 31
INT_MAX = 2 ** 31 - 1
I16_MIN = -2 ** 15
I16_ROWS = 16

NT = (((1,), (1,)), ((), ()))
TN = (((0,), (0,)), ((), ()))


def _ln(x):
    mu = jnp.mean(x, axis=-1, keepdims=True)
    xc = x - mu
    var = jnp.mean(xc * xc, axis=-1, keepdims=True)
    return xc * lax.rsqrt(var + LN_EPS)


def _sigmoid(x):
    return 1.0 / (1.0 + jnp.exp(-x))


def _params(*sem):
    return pltpu.CompilerParams(dimension_semantics=sem, vmem_limit_bytes=VMEM_LIMIT_BYTES)


def _ada_body(c_ref, w_ref, b_ref, o_ref):
    c = c_ref[...]
    cond = c * _sigmoid(c)
    o_ref[...] = jnp.dot(cond, w_ref[...], preferred_element_type=F32, precision=HIGHEST) + b_ref[...]


def _ada_call(c, w, b):
    B, D = c.shape
    N = w.shape[1]
    tn = 1536
    return pl.pallas_call(
        _ada_body,
        grid=(N // tn,),
        in_specs=[pl.BlockSpec((B, D), lambda j: (0, 0)),
                  pl.BlockSpec((D, tn), lambda j: (0, j)),
                  pl.BlockSpec((1, tn), lambda j: (0, j))],
        out_specs=pl.BlockSpec((B, tn), lambda j: (0, j)),
        out_shape=jax.ShapeDtypeStruct((B, N), F32),
        compiler_params=_params("arbitrary"),
        name="ada",
    )(c, w, b.reshape(1, N))


TOK_K = (0, 512)
TOK_GLA = (512, 2048)
TOK_GATES = (2048, 4096)
TOK_SMALL = (4096, 4224)
CH_Q = (0, 512)
CH_V = (512, 1024)
CH_IQ = (1024, 1536)
CH_IW = (1536, 1552)
IW_SCALE = IDX_HEADS ** -0.5 * IDX_DIM ** -0.5


def _inproj_body(x_ref, sh_ref, sc_ref, wtok_ref, wch_ref,
                 k_ref, gla_ref, gates_ref, ik_ref, glr_ref, qT_ref, vT_ref, iqT_ref, iwT_ref):
    tm = x_ref.shape[1]
    h = (_ln(x_ref[0]) * (1.0 + sc_ref[0]) + sh_ref[0]).astype(BF16)

    def tok(ab):
        return jnp.dot(h, wtok_ref[:, ab[0]:ab[1]], preferred_element_type=F32)

    def ch(ab):
        return lax.dot_general(wch_ref[ab[0]:ab[1], :], h, NT, preferred_element_type=F32)

    kres = tok(TOK_K)
    for p in range(A_WIDTH // LANES):
        k_ref[0, p] = kres[:, p * LANES:(p + 1) * LANES].astype(BF16)
    gla_ref[0] = tok(TOK_GLA).astype(BF16)
    gates_ref[0] = tok(TOK_GATES).astype(BF16)
    small = tok(TOK_SMALL)
    ik_ref[0] = small[:, :IDX_DIM].astype(BF16)
    glr_ref[0] = small[:, IDX_DIM:IDX_DIM + G_RANK]

    qT_ref[0] = (ch(CH_Q) * (A_HEAD_DIM ** -0.5 * LOG2E)).astype(BF16)
    vres = ch(CH_V).astype(BF16)
    for j in range(tm // LANES):
        vT_ref[0, j] = vres[:, j * LANES:(j + 1) * LANES]
    iqT_ref[0] = ch(CH_IQ).astype(BF16)
    iwT_ref[0] = ch(CH_IW) * IW_SCALE


def _inproj_call(x, sh1, sc1, w_tok, w_ch, tm):
    B, S, D = x.shape
    nt = S // tm
    const = lambda b, t: (0, 0)
    out_shape = (
        jax.ShapeDtypeStruct((B, A_WIDTH // LANES, S, LANES), BF16),
        jax.ShapeDtypeStruct((B, S, 1536), BF16),
        jax.ShapeDtypeStruct((B, S, 2048), BF16),
        jax.ShapeDtypeStruct((B, S, IDX_DIM), BF16),
        jax.ShapeDtypeStruct((B, S, G_RANK), F32),
        jax.ShapeDtypeStruct((B, A_WIDTH, S), BF16),
        jax.ShapeDtypeStruct((B, S // LANES, A_WIDTH, LANES), BF16),
        jax.ShapeDtypeStruct((B, IDX_HEADS * IDX_DIM, S), BF16),
        jax.ShapeDtypeStruct((B, IDX_HEADS, S), F32),
    )
    out_specs = (
        pl.BlockSpec((1, A_WIDTH // LANES, tm, LANES), lambda b, t: (b, 0, t, 0)),
        pl.BlockSpec((1, tm, 1536), lambda b, t: (b, t, 0)),
        pl.BlockSpec((1, tm, 2048), lambda b, t: (b, t, 0)),
        pl.BlockSpec((1, tm, IDX_DIM), lambda b, t: (b, t, 0)),
        pl.BlockSpec((1, tm, G_RANK), lambda b, t: (b, t, 0)),
        pl.BlockSpec((1, A_WIDTH, tm), lambda b, t: (b, 0, t)),
        pl.BlockSpec((1, tm // LANES, A_WIDTH, LANES), lambda b, t: (b, t, 0, 0)),
        pl.BlockSpec((1, IDX_HEADS * IDX_DIM, tm), lambda b, t: (b, 0, t)),
        pl.BlockSpec((1, IDX_HEADS, tm), lambda b, t: (b, 0, t)),
    )
    return pl.pallas_call(
        _inproj_body,
        grid=(B, nt),
        in_specs=[pl.BlockSpec((1, tm, D), lambda b, t: (b, t, 0)),
                  pl.BlockSpec((1, 1, D), lambda b, t: (b, 0, 0)),
                  pl.BlockSpec((1, 1, D), lambda b, t: (b, 0, 0)),
                  pl.BlockSpec(w_tok.shape, const),
                  pl.BlockSpec(w_ch.shape, const)],
        out_specs=out_specs,
        out_shape=out_shape,
        compiler_params=_params("parallel", "parallel"),
        name="inproj",
    )(x, sh1, sc1, w_tok, w_ch)


IDX_CHUNK = 256
CNT_CHUNK = 512
ATT_CHUNK = 256
TBL_PAD = 2 * QBLK
ACC_ROWS = A_HEAD_DIM + 16


def _rel_bucket_table():
    s = np.arange(2 * QBLK)[:, None]
    t = np.arange(QBLK)[None, :]
    dist = np.maximum(t + QBLK - s, 0)
    max_exact = REL_BUCKETS // 2
    d_f = np.maximum(dist, 1).astype(np.float32)
    large = max_exact + (np.log(d_f / max_exact) / math.log(REL_MAX_DIST / max_exact)
                         * (REL_BUCKETS - max_exact)).astype(np.int32)
    large = np.minimum(large, REL_BUCKETS - 1)
    return np.where(dist < max_exact, dist, large).astype(np.int32)


def _far_bucket():
    max_exact = REL_BUCKETS // 2
    v = max_exact + int(np.float32(np.log(np.float32(QBLK + 1) / max_exact) / math.log(REL_MAX_DIST / max_exact)
                                   * (REL_BUCKETS - max_exact)))
    assert min(v, REL_BUCKETS - 1) == REL_BUCKETS - 1
    return REL_BUCKETS - 1


def _dsa_body(rb_ref, bkt_ref, ik_ref, kk_ref, vT_ref, qT_ref, iqT_ref, iwT_ref, o_ref,
              key_s, hi_s, lo_s, lo2_s, thr_s, madd_s, tbl_s, oT_s, xcut_s, qm_s, sa_s, sb_s, acc_s, mall_s, mblk_s,
              *, topk, idx_bits, max_cnt):
    i = pl.program_id(1)
    nck = (i + 2) // 2
    t_idx = i * QBLK + lax.broadcasted_iota(I32, (1, QBLK), 1)

    @pl.when(i == 0)
    def _():
        bkt = bkt_ref[...]
        tbl_s[...] = jnp.zeros_like(tbl_s)
        for h in range(A_HEADS):
            t = jnp.zeros((2 * QBLK, QBLK), F32)
            for k in range(REL_BUCKETS):
                t = jnp.where(bkt == k, rb_ref[k, h], t)
            tbl_s[h, TBL_PAD:TBL_PAD + 2 * QBLK, :] = (t - rb_ref[_far_bucket(), h]) * LOG2E

    def score_chunk(c, carry):
        s0 = pl.multiple_of(c * IDX_CHUNK, IDX_CHUNK)
        kc = ik_ref[0, pl.ds(s0, IDX_CHUNK), :]
        acc = jnp.zeros((IDX_CHUNK, QBLK), F32)
        for hp in range(IDX_HEADS // 2):
            r0 = hp * 2 * IDX_DIM
            rhs = jnp.concatenate([iqT_ref[0, r0:r0 + IDX_DIM, :],
                                   iqT_ref[0, r0 + IDX_DIM:r0 + 2 * IDX_DIM, :]], axis=1)
            z = jnp.dot(kc, rhs, preferred_element_type=F32)
            acc = acc + jnp.maximum(z[:, :QBLK], 0.0) * iwT_ref[0, 2 * hp:2 * hp + 1, :]
            acc = acc + jnp.maximum(z[:, QBLK:], 0.0) * iwT_ref[0, 2 * hp + 1:2 * hp + 2, :]
        s_idx = s0 + lax.broadcasted_iota(I32, (IDX_CHUNK, QBLK), 0)
        acc = jnp.where(acc == 0.0, 0.0, acc)
        bits = pltpu.bitcast(acc, I32)
        key = jnp.where(bits < 0, bits ^ INT_MAX, bits)
        key = jnp.where(s_idx <= t_idx, key, INT_MIN)
        key_s[pl.ds(s0, IDX_CHUNK), :] = key
        hi_s[pl.ds(s0, IDX_CHUNK), :] = (key >> 16).astype(I16)
        lo_s[pl.ds(s0, IDX_CHUNK), :] = (key ^ 0x8000).astype(I16)
        return carry

    lax.fori_loop(0, nck, score_chunk, 0)

    @pl.when(nck % 2 == 1)
    def _():
        pad0 = pl.multiple_of(nck * IDX_CHUNK, IDX_CHUNK)
        key_s[pl.ds(pad0, IDX_CHUNK), :] = jnp.full((IDX_CHUNK, QBLK), INT_MIN, I32)
        hi_s[pl.ds(pad0, IDX_CHUNK), :] = jnp.full((IDX_CHUNK, QBLK), I16_MIN, I16)
        lo_s[pl.ds(pad0, IDX_CHUNK), :] = jnp.full((IDX_CHUNK, QBLK), I16_MIN, I16)

    ncnt = (nck + 1) // 2

    def count(pred):
        def body(c, cnt):
            s0 = pl.multiple_of(c * CNT_CHUNK, CNT_CHUNK)
            k = key_s[pl.ds(s0, CNT_CHUNK), :]
            s_idx = s0 + lax.broadcasted_iota(I32, (CNT_CHUNK, QBLK), 0)
            m = jnp.where(pred(k, s_idx), 1, 0)
            return cnt + jnp.sum(m.reshape(CNT_CHUNK // 8, 8, QBLK), axis=0)
        cnt = lax.fori_loop(0, ncnt, body, jnp.zeros((8, QBLK), I32))
        return jnp.sum(cnt, axis=0, keepdims=True)

    def tile16(v):
        return jnp.broadcast_to(v, (I16_ROWS, QBLK)).astype(I16)

    def search_block(n):
        tiles = n * CNT_CHUNK // I16_ROWS

        def tile(buf, r):
            return buf[r * I16_ROWS:(r + 1) * I16_ROWS, :]

        def count16(buf, pred):
            parts = [jnp.where(pred(tile(buf, r)), jnp.int16(1), jnp.int16(0)) for r in range(tiles)]
            while len(parts) > 1:
                odd = parts[len(parts) & ~1:]
                parts = [parts[j] + parts[j + 1] for j in range(0, len(parts) - 1, 2)] + odd
            return jnp.sum(parts[0].astype(I32), axis=0, keepdims=True)

        def search16(buf, k_need, cnt_floor):
            c0 = count16(buf, lambda k: k >= jnp.int16(0))
            ok = c0 >= k_need
            T = jnp.where(ok, 0, I16_MIN).astype(I32)
            cnt_T = jnp.where(ok, c0, cnt_floor)

            def bit_body(j, carry):
                T, cnt_T = carry
                cand = T | jnp.left_shift(jnp.int32(1), 14 - j)
                cand16 = tile16(cand)
                c = count16(buf, lambda k: k >= cand16)
                ok = c >= k_need
                return jnp.where(ok, cand, T), jnp.where(ok, c, cnt_T)

            return lax.fori_loop(0, 15, bit_body, (T, cnt_T))

        n_all = jnp.full((1, QBLK), n * CNT_CHUNK, I32)
        T_hi, cnt_hi = search16(hi_s, topk, n_all)
        t_hi16 = tile16(T_hi)
        gt_hi = count16(hi_s, lambda k: k > t_hi16)
        for r in range(tiles):
            rows = slice(r * I16_ROWS, (r + 1) * I16_ROWS)
            lo2_s[rows, :] = jnp.where(hi_s[rows, :] == t_hi16, lo_s[rows, :], jnp.int16(I16_MIN))
        T_lo, cnt_lo = search16(lo2_s, topk - gt_hi, cnt_hi - gt_hi)
        t_lo16 = tile16(T_lo)
        gt_lo = count16(lo2_s, lambda k: k > t_lo16)
        thr_s[0:1, :] = jnp.left_shift(T_hi, 16) | ((T_lo - I16_MIN) & 0xFFFF)
        thr_s[1:2, :] = gt_hi + cnt_lo
        thr_s[2:3, :] = gt_hi + gt_lo

    for n in range(1, max_cnt + 1):
        pl.when(ncnt == n)(functools.partial(search_block, n))
    T = thr_s[0:1, :]
    cnt_ge = thr_s[1:2, :]
    cnt_gt = thr_s[2:3, :]

    need = topk - cnt_gt
    excess = jnp.where((cnt_ge - cnt_gt > need) & (T > INT_MIN), 1.0, 0.0)
    xcut_s[...] = jnp.full((1, QBLK), INT_MAX, I32)

    @pl.when(jnp.max(excess) > 0.0)
    def _():
        X = jnp.zeros((1, QBLK), I32)
        for b in range(idx_bits - 1, -1, -1):
            cand = X | (1 << b)
            f = count(lambda k, s: (k == T) & (s < cand))
            X = jnp.where(f < need, cand, X)
        xcut_s[...] = X

    xcut = xcut_s[...]

    def mask_chunk(c, carry):
        s0 = pl.multiple_of(c * IDX_CHUNK, IDX_CHUNK)
        k = key_s[pl.ds(s0, IDX_CHUNK), :]
        s_idx = s0 + lax.broadcasted_iota(I32, (IDX_CHUNK, QBLK), 0)
        sel = ((k > T) | ((k == T) & (s_idx <= xcut))) & (s_idx <= t_idx)
        madd_s[pl.ds(s0, IDX_CHUNK), :] = jnp.where(sel, 0.0, NEG)
        return carry

    lax.fori_loop(0, nck, mask_chunk, 0)

    c_last = i // 2
    even = 1 - (i - 2 * c_last)
    row_head = lax.broadcasted_iota(I32, (LANES, QBLK), 0) // A_HEAD_DIM
    for h in range(A_HEADS):
        qp = qT_ref[0, (h // 2) * LANES:(h // 2 + 1) * LANES, :]
        qm_s[h] = jnp.where(row_head == h % 2, qp, jnp.zeros_like(qp))
    acc_s[...] = jnp.zeros_like(acc_s)
    ones = jnp.ones((ACC_ROWS - A_HEAD_DIM, ATT_CHUNK), BF16)

    def logits(c, s_buf):
        s0 = pl.multiple_of(c * ATT_CHUNK, ATT_CHUNK)
        madd = madd_s[pl.ds(s0, ATT_CHUNK), :]
        off = jnp.where(c == c_last, 2 * QBLK + QBLK * even, jnp.where(c == c_last - 1, QBLK * even, 0))
        off = pl.multiple_of(off, QBLK)
        m_blk = []
        for h in range(A_HEADS):
            kc = kk_ref[0, h // 2, pl.ds(s0, ATT_CHUNK), :]
            s = jnp.dot(kc, qm_s[h], preferred_element_type=F32) + madd + tbl_s[h, pl.ds(off, ATT_CHUNK), :]
            s_buf[h] = s
            m_blk.append(jnp.max(s, axis=0, keepdims=True))
        return jnp.concatenate(m_blk, axis=0)

    def accumulate(c, s_buf, m_all, m_blk):
        m_new = jnp.maximum(m_all, m_blk)
        alpha = jnp.exp2(m_all - m_new)
        for h in range(A_HEADS):
            rows = slice(h * A_HEAD_DIM, (h + 1) * A_HEAD_DIM)
            p = jnp.exp2(s_buf[h] - m_new[h:h + 1]).astype(BF16)
            vt = jnp.concatenate([vT_ref[0, 2 * c + u, rows, :] for u in range(ATT_CHUNK // QBLK)], axis=1)
            vt = jnp.concatenate([vt, ones], axis=0)
            acc_s[h] = alpha[h:h + 1] * acc_s[h] + jnp.dot(vt, p, preferred_element_type=F32)
        return m_new

    def att_body(pair, carry):
        m_all, m_blk = carry
        c = 2 * pair
        m_b = logits(c + 1, sb_s)
        m_all = accumulate(c, sa_s, m_all, m_blk)
        m_a = logits(c + 2, sa_s)
        return accumulate(c + 1, sb_s, m_all, m_b), m_a

    n_pairs = c_last // 2
    carry = (jnp.full((A_HEADS, QBLK), NEG, F32), logits(0, sa_s))
    m_all, m_blk = lax.fori_loop(0, n_pairs, att_body, carry)
    mall_s[...] = m_all
    mblk_s[...] = m_blk

    @pl.when(c_last % 2 == 1)
    def _():
        m_b = logits(c_last, sb_s)
        m_all = accumulate(c_last - 1, sa_s, mall_s[...], mblk_s[...])
        accumulate(c_last, sb_s, m_all, m_b)

    @pl.when(c_last % 2 == 0)
    def _():
        accumulate(c_last, sa_s, mall_s[...], mblk_s[...])

    for h in range(A_HEADS):
        rows = slice(h * A_HEAD_DIM, (h + 1) * A_HEAD_DIM)
        oT_s[rows, :] = acc_s[h, 0:A_HEAD_DIM, :] / acc_s[h, A_HEAD_DIM:A_HEAD_DIM + 1, :]
    o_ref[0] = oT_s[...].T.astype(BF16)


def _dsa_call(rel_bias, ik, kk, vT, qT, iqT, iwT):
    B, S, _ = ik.shape
    nb = S // QBLK
    topk = min(TOPK_MAX, S // 4)
    bkt = jnp.asarray(_rel_bucket_table())
    body = functools.partial(_dsa_body, topk=topk, idx_bits=int(math.log2(S)), max_cnt=S // CNT_CHUNK)
    return pl.pallas_call(
        body,
        grid=(B, nb),
        in_specs=[pl.BlockSpec(memory_space=pltpu.SMEM),
                  pl.BlockSpec((2 * QBLK, QBLK), lambda b, i: (0, 0)),
                  pl.BlockSpec((1, S, IDX_DIM), lambda b, i: (b, 0, 0)),
                  pl.BlockSpec((1, A_WIDTH // LANES, S, LANES), lambda b, i: (b, 0, 0, 0)),
                  pl.BlockSpec((1, S // LANES, A_WIDTH, LANES), lambda b, i: (b, 0, 0, 0)),
                  pl.BlockSpec((1, A_WIDTH, QBLK), lambda b, i: (b, 0, i)),
                  pl.BlockSpec((1, IDX_HEADS * IDX_DIM, QBLK), lambda b, i: (b, 0, i)),
                  pl.BlockSpec((1, IDX_HEADS, QBLK), lambda b, i: (b, 0, i))],
        out_specs=pl.BlockSpec((1, QBLK, A_WIDTH), lambda b, i: (b, i, 0)),
        out_shape=jax.ShapeDtypeStruct((B, S, A_WIDTH), BF16),
        scratch_shapes=[pltpu.VMEM((S, QBLK), I32),
                        pltpu.VMEM((S, QBLK), I16),
                        pltpu.VMEM((S, QBLK), I16),
                        pltpu.VMEM((S, QBLK), I16),
                        pltpu.VMEM((8, QBLK), I32),
                        pltpu.VMEM((S, QBLK), F32),
                        pltpu.VMEM((A_HEADS, TBL_PAD + 3 * QBLK, QBLK), F32),
                        pltpu.VMEM((A_WIDTH, QBLK), F32),
                        pltpu.VMEM((1, QBLK), I32),
                        pltpu.VMEM((A_HEADS, LANES, QBLK), BF16),
                        pltpu.VMEM((A_HEADS, ATT_CHUNK, QBLK), F32),
                        pltpu.VMEM((A_HEADS, ATT_CHUNK, QBLK), F32),
                        pltpu.VMEM((A_HEADS, ACC_ROWS, QBLK), F32),
                        pltpu.VMEM((A_HEADS, QBLK), F32),
                        pltpu.VMEM((A_HEADS, QBLK), F32)],
        compiler_params=_params("parallel", "arbitrary"),
        name="dsa",
    )(rel_bias, bkt, ik, kk, vT, qT, iqT, iwT)


GLA_Q = (0, 256)
GLA_K = (256, 512)
GLA_V = (512, 1024)
GLA_R = (1024, 1536)


def _gla_body(gla_ref, glr_ref, wg_ref, bg_ref, ng_ref, o_ref, st_s):
    tg = gla_ref.shape[1]
    C = G_CHUNK

    @pl.when(pl.program_id(1) == 0)
    def _():
        st_s[...] = jnp.zeros_like(st_s)

    xg = jnp.dot(glr_ref[0], wg_ref[...], preferred_element_type=F32, precision=HIGHEST) + bg_ref[...]
    logg = -(jnp.maximum(-xg, 0.0) + jnp.log1p(jnp.exp(-jnp.abs(xg)))) * (1.0 / G_TAU)

    ri = lax.broadcasted_iota(I32, (C, C), 0)
    ci = lax.broadcasted_iota(I32, (C, C), 1)
    tril = ri >= ci
    tril_f = jnp.where(tril, 1.0, 0.0).astype(F32)
    lane_head = lax.broadcasted_iota(I32, (C, LANES), 1) // G_DK
    st_rows = lax.broadcasted_iota(I32, (2 * G_DV, LANES), 0) // G_DV
    st_cols = lax.broadcasted_iota(I32, (2 * G_DV, LANES), 1) // G_DK
    st_diag = st_rows == st_cols

    for ck in range(tg // C):
        r0 = ck * C
        for p in range(G_HEADS // 2):
            lg = logg[r0:r0 + C, p * LANES:(p + 1) * LANES]
            bc = jnp.dot(tril_f, lg, preferred_element_type=F32, precision=HIGHEST)
            bl = bc[C - 1:C, :]
            q = gla_ref[0, r0:r0 + C, GLA_Q[0] + p * LANES:GLA_Q[0] + (p + 1) * LANES].astype(F32) * (G_DK ** -0.5)
            k = gla_ref[0, r0:r0 + C, GLA_K[0] + p * LANES:GLA_K[0] + (p + 1) * LANES].astype(F32)
            v = gla_ref[0, r0:r0 + C, GLA_V[0] + p * 2 * G_DV:GLA_V[0] + (p + 1) * 2 * G_DV]
            q_in = (q * jnp.exp(bc)).astype(BF16)
            k_st = (k * jnp.exp(bl - bc)).astype(BF16)
            q_rel = q * jnp.exp(bc - bl)
            o_intra = []
            for sub in range(2):
                qm = jnp.where(lane_head == sub, q_rel, 0.0).astype(BF16)
                att = lax.dot_general(qm, k_st, NT, preferred_element_type=F32)
                att = jnp.where(tril, att, 0.0).astype(BF16)
                o_intra.append(jnp.dot(att, v[:, sub * G_DV:(sub + 1) * G_DV], preferred_element_type=F32))
            st = st_s[p]
            o_inter = lax.dot_general(q_in, st.astype(BF16), NT, preferred_element_type=F32)
            uT = lax.dot_general(v, k_st, TN, preferred_element_type=F32)
            st_s[p] = st * jnp.exp(bl) + jnp.where(st_diag, uT, 0.0)
            for sub in range(2):
                hd = 2 * p + sub
                o = o_intra[sub] + o_inter[:, sub * G_DV:(sub + 1) * G_DV]
                y = _ln(o) * ng_ref[:, hd * G_DV:(hd + 1) * G_DV]
                g = gla_ref[0, r0:r0 + C, GLA_R[0] + hd * G_DV:GLA_R[0] + (hd + 1) * G_DV].astype(F32)
                o_ref[0, r0:r0 + C, hd * G_DV:(hd + 1) * G_DV] = (y * (g * _sigmoid(g))).astype(BF16)


def _gla_call(gla, glr, wg, bg, ng, tg):
    B, S, _ = gla.shape
    const = lambda b, j: (0, 0)
    return pl.pallas_call(
        _gla_body,
        grid=(B, S // tg),
        in_specs=[pl.BlockSpec((1, tg, 1536), lambda b, j: (b, j, 0)),
                  pl.BlockSpec((1, tg, G_RANK), lambda b, j: (b, j, 0)),
                  pl.BlockSpec((G_RANK, G_KW), const),
                  pl.BlockSpec((1, G_KW), const),
                  pl.BlockSpec((1, G_VW), const)],
        out_specs=pl.BlockSpec((1, tg, G_VW), lambda b, j: (b, j, 0)),
        out_shape=jax.ShapeDtypeStruct((B, S, G_VW), BF16),
        scratch_shapes=[pltpu.VMEM((G_HEADS // 2, 2 * G_DV, LANES), F32)],
        compiler_params=_params("parallel", "arbitrary"),
        name="gla",
    )(gla, glr, wg, bg, ng)


ROUTER_ROWS = 40
ROUTER_E0 = 8


def _post_body(oa_ref, ob_ref, gates_ref, x_ref, gt1_ref, sh2_ref, sc2_ref, wa_ref, wb_ref, wo_ref,
               g1_ref, b1_ref, wr_ref, br_ref, x1_ref, h2_ref, gate_ref):
    tm = x_ref.shape[1]
    D = x_ref.shape[2]
    ya = jnp.dot(oa_ref[0], wa_ref[...], preferred_element_type=F32)
    yb = jnp.dot(ob_ref[0], wb_ref[...], preferred_element_type=F32)
    ga = gates_ref[0, :, 0:D].astype(F32)
    gb = gates_ref[0, :, D:2 * D].astype(F32)
    merged = _sigmoid(ga) * ya + _sigmoid(gb) * yb
    y = jnp.dot(merged.astype(BF16), wo_ref[...], preferred_element_type=F32)
    x1 = _ln(DN_ALPHA * x_ref[0] + gt1_ref[0] * y) * g1_ref[...] + b1_ref[...]
    x1_ref[0] = x1
    h2 = _ln(x1) * (1.0 + sc2_ref[0]) + sh2_ref[0]
    h2_ref[0] = h2.astype(BF16)

    lt = lax.dot_general(wr_ref[...], h2, NT, preferred_element_type=F32, precision=HIGHEST) + br_ref[...]
    gl = lt[0:N_GROUPS]
    gmax = jnp.max(gl, axis=0, keepdims=True)
    g_w = 1.0 / jnp.sum(jnp.exp(gl - gmax), axis=0, keepdims=True)
    r4 = lax.broadcasted_iota(I32, (N_GROUPS, tm), 0)
    g_idx = jnp.min(jnp.where(gl == gmax, r4, N_GROUPS), axis=0, keepdims=True)
    eg = jnp.zeros((EXPERTS_PER_GROUP, tm), F32)
    for g in range(N_GROUPS):
        lo = ROUTER_E0 + g * EXPERTS_PER_GROUP
        eg = jnp.where(g_idx == g, lt[lo:lo + EXPERTS_PER_GROUP], eg)
    r8 = lax.broadcasted_iota(I32, (EXPERTS_PER_GROUP, tm), 0)
    e1 = jnp.max(eg, axis=0, keepdims=True)
    i1 = jnp.min(jnp.where(eg == e1, r8, EXPERTS_PER_GROUP), axis=0, keepdims=True)
    eg2 = jnp.where(r8 == i1, -jnp.inf, eg)
    e2 = jnp.max(eg2, axis=0, keepdims=True)
    i2 = jnp.min(jnp.where(eg2 == e2, r8, EXPERTS_PER_GROUP), axis=0, keepdims=True)
    d = jnp.exp(e2 - e1)
    w1 = g_w / (1.0 + d)
    w2 = g_w * d / (1.0 + d)
    in_group = jnp.where(r8 == i1, w1, 0.0) + jnp.where(r8 == i2, w2, 0.0)
    blocks = [jnp.where(g_idx == g, in_group, 0.0) for g in range(N_GROUPS)]
    blocks.append(jnp.zeros((LANES - N_EXPERTS, tm), F32))
    gate_ref[...] = jnp.concatenate(blocks, axis=0).T


def _post_call(o_a, o_b, gates, x, gt1, sh2, sc2, wa, wb, wo, g1, b1, wr, br, tm):
    B, S, D = x.shape
    nt = S // tm
    const = lambda b, t: (0, 0)
    row = lambda b, t: (b, 0, 0)
    tile = lambda b, t: (b, t, 0)
    return pl.pallas_call(
        _post_body,
        grid=(B, nt),
        in_specs=[pl.BlockSpec((1, tm, A_WIDTH), tile),
                  pl.BlockSpec((1, tm, G_VW), tile),
                  pl.BlockSpec((1, tm, 2 * D), tile),
                  pl.BlockSpec((1, tm, D), tile),
                  pl.BlockSpec((1, 1, D), row),
                  pl.BlockSpec((1, 1, D), row),
                  pl.BlockSpec((1, 1, D), row),
                  pl.BlockSpec(wa.shape, const),
                  pl.BlockSpec(wb.shape, const),
                  pl.BlockSpec(wo.shape, const),
                  pl.BlockSpec((1, D), const),
                  pl.BlockSpec((1, D), const),
                  pl.BlockSpec(wr.shape, const),
                  pl.BlockSpec(br.shape, const)],
        out_specs=(pl.BlockSpec((1, tm, D), tile),
                   pl.BlockSpec((1, tm, D), tile),
                   pl.BlockSpec((tm, LANES), lambda b, t: (b * nt + t, 0))),
        out_shape=(jax.ShapeDtypeStruct((B, S, D), F32),
                   jax.ShapeDtypeStruct((B, S, D), BF16),
                   jax.ShapeDtypeStruct((B * S, LANES), F32)),
        compiler_params=_params("parallel", "parallel"),
        name="post",
    )(o_a, o_b, gates, x, gt1, sh2, sc2, wa, wb, wo, g1, b1, wr, br)


MOE_EXPERTS_PER_STEP = 4


def _moe_body(h2_ref, gate_ref, x1_ref, gt2_ref, w1_ref, w3_ref, w2_ref, g2_ref, b2_ref, o_ref, acc_s):
    c = pl.program_id(1)
    ne = MOE_EXPERTS_PER_STEP

    @pl.when(c == 0)
    def _():
        acc_s[...] = jnp.zeros_like(acc_s)

    h = h2_ref[...]
    gate = pltpu.roll(gate_ref[...], (LANES - c * ne) % LANES, axis=1)
    hid = []
    for j in range(ne):
        a = jnp.dot(h, w1_ref[j], preferred_element_type=F32)
        b = jnp.dot(h, w3_ref[j], preferred_element_type=F32)
        hid.append((a * _sigmoid(a) * b * gate[:, j:j + 1]).astype(BF16))
    acc_s[...] += jnp.dot(jnp.concatenate(hid, axis=1), w2_ref[...], preferred_element_type=F32)

    @pl.when(c == pl.num_programs(1) - 1)
    def _():
        z = DN_ALPHA * x1_ref[...] + gt2_ref[0] * acc_s[...]
        o_ref[...] = _ln(z) * g2_ref[...] + b2_ref[...]


def _moe_call(h2, gate, x1, gt2, w1, w3, w2, g2, b2, tm, S):
    T, D = h2.shape
    ne = MOE_EXPERTS_PER_STEP
    nc = N_EXPERTS // ne
    tiles_per_seq = S // tm
    tile = lambda t, c: (t, 0)
    const = lambda t, c: (0, 0)
    return pl.pallas_call(
        _moe_body,
        grid=(T // tm, nc),
        in_specs=[pl.BlockSpec((tm, D), tile),
                  pl.BlockSpec((tm, LANES), tile),
                  pl.BlockSpec((tm, D), tile),
                  pl.BlockSpec((1, 1, D), lambda t, c: (t // tiles_per_seq, 0, 0)),
                  pl.BlockSpec((ne, D, D_EXPERT), lambda t, c: (c, 0, 0)),
                  pl.BlockSpec((ne, D, D_EXPERT), lambda t, c: (c, 0, 0)),
                  pl.BlockSpec((ne * D_EXPERT, D), lambda t, c: (c, 0)),
                  pl.BlockSpec((1, D), const),
                  pl.BlockSpec((1, D), const)],
        out_specs=pl.BlockSpec((tm, D), tile),
        out_shape=jax.ShapeDtypeStruct((T, D), F32),
        scratch_shapes=[pltpu.VMEM((tm, D), F32)],
        compiler_params=_params("parallel", "arbitrary"),
        name="moe",
    )(h2, gate, x1, gt2, w1, w3, w2, g2, b2)


def _pick(n, pref):
    return pref if n % pref == 0 else n


def kernel(x, c, rel_bias, w_ada, b_ada, w_in, gla_w_gate, gla_b_gate, gla_norm_g, w_branch_a, w_branch_b, w_out, ln1_g, ln1_b, w_router_group, b_router_group, w_router_expert, b_router_expert, w_exp_gate, w_exp_up, w_exp_down, ln2_g, ln2_b):
    B, S, D = x.shape
    assert S % (2 * QBLK) == 0 and D == 1024 and w_ada.shape[0] == DEPTH == 1
    l = 0

    ada = _ada_call(c, w_ada[l], b_ada[l])
    sh1, sc1, gt1, sh2, sc2, gt2 = [ada[:, i * D:(i + 1) * D].reshape(B, 1, D) for i in range(6)]

    offs = np.concatenate([[0], np.cumsum(SPLIT_SIZES)])
    seg = lambda i: w_in[l][:, offs[i]:offs[i + 1]]
    (w_aq, w_ak, w_av, w_iq, w_ik, w_iw, w_gq, w_gk, w_gv, w_gr, w_glr, w_ga, w_gb) = [seg(i) for i in range(13)]
    pad = jnp.zeros((D, TOK_SMALL[1] - TOK_SMALL[0] - IDX_DIM - G_RANK), F32)
    w_tok = jnp.concatenate([w_ak, w_gq, w_gk, w_gv, w_gr, w_ga, w_gb, w_ik, w_glr, pad], axis=1).astype(BF16)
    w_ch = jnp.concatenate([w_aq, w_av, w_iq, w_iw], axis=1).T.astype(BF16)

    tm = _pick(S, 512)
    kk, gla, gates, ik, glr, qT, vT, iqT, iwT = _inproj_call(x, sh1, sc1, w_tok, w_ch, tm)

    o_a = _dsa_call(rel_bias, ik, kk, vT, qT, iqT, iwT)
    o_b = _gla_call(gla, glr, gla_w_gate[l], gla_b_gate[l].reshape(1, G_KW), gla_norm_g[l].reshape(1, G_VW),
                    _pick(S, 256))

    wr = jnp.zeros((ROUTER_ROWS, D), F32)
    wr = wr.at[0:N_GROUPS].set(w_router_group[l].T).at[ROUTER_E0:ROUTER_E0 + N_EXPERTS].set(w_router_expert[l].T)
    br = jnp.zeros((ROUTER_ROWS, 1), F32)
    br = br.at[0:N_GROUPS, 0].set(b_router_group[l]).at[ROUTER_E0:ROUTER_E0 + N_EXPERTS, 0].set(b_router_expert[l])
    x1, h2, gate = _post_call(o_a, o_b, gates, x, gt1, sh2, sc2,
                              w_branch_a[l].astype(BF16), w_branch_b[l].astype(BF16), w_out[l].astype(BF16),
                              ln1_g[l].reshape(1, D), ln1_b[l].reshape(1, D), wr, br, tm)

    tm5 = _pick(S, 1024)
    out = _moe_call(h2.reshape(B * S, D), gate, x1.reshape(B * S, D), gt2,
                    w_exp_gate[l].astype(BF16), w_exp_up[l].astype(BF16),
                    w_exp_down[l].astype(BF16).reshape(N_EXPERTS * D_EXPERT, D),
                    ln2_g[l].reshape(1, D), ln2_b[l].reshape(1, D), tm5, S)
    return out.reshape(B, S, D)
```

```python
import functools
import math

import numpy as np
import jax
import jax.numpy as jnp
from jax import lax
from jax.experimental import pallas as pl
from jax.experimental.pallas import tpu as pltpu

F32 = jnp.float32
BF16 = jnp.bfloat16
I32 = jnp.int32
HIGHEST = lax.Precision.HIGHEST

A_HEADS = 8
A_HEAD_DIM = 64
A_WIDTH = A_HEADS * A_HEAD_DIM
IDX_HEADS = 16
IDX_DIM = 32
TOPK_MAX = 256
QBLK = 128
REL_BUCKETS = 32
REL_MAX_DIST = 128
G_HEADS = 4
G_DK = 64
G_DV = 128
G_KW = G_HEADS * G_DK
G_VW = G_HEADS * G_DV
G_RANK = 16
G_TAU = 16.0
G_CHUNK = 64
N_GROUPS = 4
EXPERTS_PER_GROUP = 8
N_EXPERTS = N_GROUPS * EXPERTS_PER_GROUP
D_EXPERT = 256
DEPTH = 1
DN_ALPHA = (2.0 * DEPTH) ** 0.25
LN_EPS = 1e-5
SPLIT_SIZES = (A_WIDTH, A_WIDTH, A_WIDTH, IDX_HEADS * IDX_DIM, IDX_DIM, IDX_HEADS,
               G_KW, G_KW, G_VW, G_VW, G_RANK, 1024, 1024)

LANES = 128
VMEM_LIMIT_BYTES = 56 * 1024 * 1024

NEG = -1e30
LOG2E = math.log2(math.e)
INT_MIN = -2 ** 31
INT_MAX = 2 ** 31 - 1
KEY_NEG_INF = -2 ** 31 + 0x7FFFFF

NT = (((1,), (1,)), ((), ()))
TN = (((0,), (0,)), ((), ()))


def _ln(x):
    mu = jnp.mean(x, axis=-1, keepdims=True)
    xc = x - mu
    var = jnp.mean(xc * xc, axis=-1, keepdims=True)
    return xc * lax.rsqrt(var + LN_EPS)


def _sigmoid(x):
    return 1.0 / (1.0 + jnp.exp(-x))


def _params(*sem):
    return pltpu.CompilerParams(dimension_semantics=sem, vmem_limit_bytes=VMEM_LIMIT_BYTES)


def _ada_body(c_ref, w_ref, b_ref, o_ref):
    c = c_ref[...]
    cond = c * _sigmoid(c)
    o_ref[...] = jnp.dot(cond, w_ref[...], preferred_element_type=F32, precision=HIGHEST) + b_ref[...]


def _ada_call(c, w, b):
    B, D = c.shape
    N = w.shape[1]
    tn = 1536
    return pl.pallas_call(
        _ada_body,
        grid=(N // tn,),
        in_specs=[pl.BlockSpec((B, D), lambda j: (0, 0)),
                  pl.BlockSpec((D, tn), lambda j: (0, j)),
                  pl.BlockSpec((1, tn), lambda j: (0, j))],
        out_specs=pl.BlockSpec((B, tn), lambda j: (0, j)),
        out_shape=jax.ShapeDtypeStruct((B, N), F32),
        compiler_params=_params("arbitrary"),
        name="ada",
    )(c, w, b.reshape(1, N))


TOK_K = (0, 512)
TOK_GLA = (512, 2048)
TOK_GATES = (2048, 4096)
TOK_SMALL = (4096, 4224)
CH_Q = (0, 512)
CH_V = (512, 1024)
CH_IQ = (1024, 1536)
CH_IW = (1536, 1552)
IW_SCALE = IDX_HEADS ** -0.5 * IDX_DIM ** -0.5


def _inproj_body(x_ref, sh_ref, sc_ref, wtok_ref, wch_ref,
                 k_ref, gla_ref, gates_ref, ik_ref, glr_ref, qT_ref, vT_ref, iqT_ref, iwT_ref):
    tm = x_ref.shape[1]
    h = (_ln(x_ref[0]) * (1.0 + sc_ref[0]) + sh_ref[0]).astype(BF16)

    def tok(ab):
        return jnp.dot(h, wtok_ref[:, ab[0]:ab[1]], preferred_element_type=F32)

    def ch(ab):
        return lax.dot_general(wch_ref[ab[0]:ab[1], :], h, NT, preferred_element_type=F32)

    kres = tok(TOK_K)
    for p in range(A_WIDTH // LANES):
        k_ref[0, p] = kres[:, p * LANES:(p + 1) * LANES].astype(BF16)
    gla_ref[0] = tok(TOK_GLA).astype(BF16)
    gates_ref[0] = tok(TOK_GATES).astype(BF16)
    small = tok(TOK_SMALL)
    ik_ref[0] = small[:, :IDX_DIM].astype(BF16)
    glr_ref[0] = small[:, IDX_DIM:IDX_DIM + G_RANK]

    qT_ref[0] = (ch(CH_Q) * (A_HEAD_DIM ** -0.5 * LOG2E)).astype(BF16)
    vres = ch(CH_V).astype(BF16)
    for j in range(tm // LANES):
        vT_ref[0, j] = vres[:, j * LANES:(j + 1) * LANES]
    iqT_ref[0] = ch(CH_IQ).astype(BF16)
    iwT_ref[0] = ch(CH_IW) * IW_SCALE


def _inproj_call(x, sh1, sc1, w_tok, w_ch, tm):
    B, S, D = x.shape
    nt = S // tm
    const = lambda b, t: (0, 0)
    out_shape = (
        jax.ShapeDtypeStruct((B, A_WIDTH // LANES, S, LANES), BF16),
        jax.ShapeDtypeStruct((B, S, 1536), BF16),
        jax.ShapeDtypeStruct((B, S, 2048), BF16),
        jax.ShapeDtypeStruct((B, S, IDX_DIM), BF16),
        jax.ShapeDtypeStruct((B, S, G_RANK), F32),
        jax.ShapeDtypeStruct((B, A_WIDTH, S), BF16),
        jax.ShapeDtypeStruct((B, S // LANES, A_WIDTH, LANES), BF16),
        jax.ShapeDtypeStruct((B, IDX_HEADS * IDX_DIM, S), BF16),
        jax.ShapeDtypeStruct((B, IDX_HEADS, S), F32),
    )
    out_specs = (
        pl.BlockSpec((1, A_WIDTH // LANES, tm, LANES), lambda b, t: (b, 0, t, 0)),
        pl.BlockSpec((1, tm, 1536), lambda b, t: (b, t, 0)),
        pl.BlockSpec((1, tm, 2048), lambda b, t: (b, t, 0)),
        pl.BlockSpec((1, tm, IDX_DIM), lambda b, t: (b, t, 0)),
        pl.BlockSpec((1, tm, G_RANK), lambda b, t: (b, t, 0)),
        pl.BlockSpec((1, A_WIDTH, tm), lambda b, t: (b, 0, t)),
        pl.BlockSpec((1, tm // LANES, A_WIDTH, LANES), lambda b, t: (b, t, 0, 0)),
        pl.BlockSpec((1, IDX_HEADS * IDX_DIM, tm), lambda b, t: (b, 0, t)),
        pl.BlockSpec((1, IDX_HEADS, tm), lambda b, t: (b, 0, t)),
    )
    return pl.pallas_call(
        _inproj_body,
        grid=(B, nt),
        in_specs=[pl.BlockSpec((1, tm, D), lambda b, t: (b, t, 0)),
                  pl.BlockSpec((1, 1, D), lambda b, t: (b, 0, 0)),
                  pl.BlockSpec((1, 1, D), lambda b, t: (b, 0, 0)),
                  pl.BlockSpec(w_tok.shape, const),
                  pl.BlockSpec(w_ch.shape, const)],
        out_specs=out_specs,
        out_shape=out_shape,
        compiler_params=_params("parallel", "parallel"),
        name="inproj",
    )(x, sh1, sc1, w_tok, w_ch)


IDX_CHUNK = 256
CNT_CHUNK = 512
SORT_GROUP = 4
ATT_CHUNK = 256
TBL_PAD = 2 * QBLK
ACC_ROWS = A_HEAD_DIM + 16


def _rel_bucket_table():
    s = np.arange(2 * QBLK)[:, None]
    t = np.arange(QBLK)[None, :]
    dist = np.maximum(t + QBLK - s, 0)
    max_exact = REL_BUCKETS // 2
    d_f = np.maximum(dist, 1).astype(np.float32)
    large = max_exact + (np.log(d_f / max_exact) / math.log(REL_MAX_DIST / max_exact)
                         * (REL_BUCKETS - max_exact)).astype(np.int32)
    large = np.minimum(large, REL_BUCKETS - 1)
    return np.where(dist < max_exact, dist, large).astype(np.int32)


def _far_bucket():
    max_exact = REL_BUCKETS // 2
    v = max_exact + int(np.float32(np.log(np.float32(QBLK + 1) / max_exact) / math.log(REL_MAX_DIST / max_exact)
                                   * (REL_BUCKETS - max_exact)))
    assert min(v, REL_BUCKETS - 1) == REL_BUCKETS - 1
    return REL_BUCKETS - 1


def _dsa_body(rb_ref, bkt_ref, ik_ref, kk_ref, vT_ref, qT_ref, iqT_ref, iwT_ref, o_ref,
              sc_s, srt_s, thr_s, madd_s, tbl_s, oT_s, xcut_s, qm_s, sa_s, sb_s, acc_s, mall_s, mblk_s,
              *, topk, idx_bits, max_cnt):
    i = pl.program_id(1)
    nck = (i + 2) // 2
    t_idx = i * QBLK + lax.broadcasted_iota(I32, (1, QBLK), 1)

    @pl.when(i == 0)
    def _():
        bkt = bkt_ref[...]
        tbl_s[...] = jnp.zeros_like(tbl_s)
        for h in range(A_HEADS):
            t = jnp.zeros((2 * QBLK, QBLK), F32)
            for k in range(REL_BUCKETS):
                t = jnp.where(bkt == k, rb_ref[k, h], t)
            tbl_s[h, TBL_PAD:TBL_PAD + 2 * QBLK, :] = (t - rb_ref[_far_bucket(), h]) * LOG2E

    def key_to_float(key):
        key = jnp.maximum(key, KEY_NEG_INF)
        return pltpu.bitcast(jnp.where(key < 0, key ^ INT_MAX, key), F32)

    def score_chunk(c, carry):
        s0 = pl.multiple_of(c * IDX_CHUNK, IDX_CHUNK)
        kc = ik_ref[0, pl.ds(s0, IDX_CHUNK), :]
        acc = jnp.zeros((IDX_CHUNK, QBLK), F32)
        for hp in range(IDX_HEADS // 2):
            r0 = hp * 2 * IDX_DIM
            rhs = jnp.concatenate([iqT_ref[0, r0:r0 + IDX_DIM, :],
                                   iqT_ref[0, r0 + IDX_DIM:r0 + 2 * IDX_DIM, :]], axis=1)
            z = jnp.dot(kc, rhs, preferred_element_type=F32)
            acc = acc + jnp.maximum(z[:, :QBLK], 0.0) * iwT_ref[0, 2 * hp:2 * hp + 1, :]
            acc = acc + jnp.maximum(z[:, QBLK:], 0.0) * iwT_ref[0, 2 * hp + 1:2 * hp + 2, :]
        s_idx = s0 + lax.broadcasted_iota(I32, (IDX_CHUNK, QBLK), 0)
        sc_s[pl.ds(s0, IDX_CHUNK), :] = jnp.where(s_idx <= t_idx, acc, -jnp.inf)
        return carry

    lax.fori_loop(0, nck, score_chunk, 0)

    @pl.when(nck % 2 == 1)
    def _():
        pad0 = pl.multiple_of(nck * IDX_CHUNK, IDX_CHUNK)
        sc_s[pl.ds(pad0, IDX_CHUNK), :] = jnp.full((IDX_CHUNK, QBLK), -jnp.inf, F32)

    ncnt = (nck + 1) // 2

    def count(pred):
        def body(c, cnt):
            s0 = pl.multiple_of(c * CNT_CHUNK, CNT_CHUNK)
            k = sc_s[pl.ds(s0, CNT_CHUNK), :]
            s_idx = s0 + lax.broadcasted_iota(I32, (CNT_CHUNK, QBLK), 0)
            m = jnp.where(pred(k, s_idx), 1, 0)
            return cnt + jnp.sum(m.reshape(CNT_CHUNK // 8, 8, QBLK), axis=0)
        cnt = lax.fori_loop(0, ncnt, body, jnp.zeros((8, QBLK), I32))
        return jnp.sum(cnt, axis=0, keepdims=True)

    def search_block(n):
        groups = n * CNT_CHUNK // (8 * SORT_GROUP)

        for g in range(groups):
            v = [sc_s[(SORT_GROUP * g + u) * 8:(SORT_GROUP * g + u + 1) * 8, :] for u in range(SORT_GROUP)]
            for a, b in ((0, 1), (2, 3), (0, 2), (1, 3), (1, 2)):
                v[a], v[b] = jnp.maximum(v[a], v[b]), jnp.minimum(v[a], v[b])
            for u in range(SORT_GROUP):
                srt_s[(SORT_GROUP * g + u) * 8:(SORT_GROUP * g + u + 1) * 8, :] = v[u]

        def count_ge(cand_key):
            cand = key_to_float(cand_key)
            parts = []
            for g in range(groups):
                cnt = 0
                for u in range(SORT_GROUP):
                    tile = srt_s[(SORT_GROUP * g + u) * 8:(SORT_GROUP * g + u + 1) * 8, :]
                    cnt = jnp.where(tile >= cand, u + 1, cnt)
                parts.append(cnt)
            while len(parts) > 1:
                odd = parts[len(parts) & ~1:]
                parts = [parts[j] + parts[j + 1] for j in range(0, len(parts) - 1, 2)] + odd
            return jnp.sum(parts[0], axis=0, keepdims=True)

        c0 = count_ge(jnp.zeros((1, QBLK), I32))
        ok = c0 >= topk
        T = jnp.where(ok, 0, INT_MIN).astype(I32)
        cnt_T = jnp.where(ok, c0, n * CNT_CHUNK)

        def bit_body(j, carry):
            T, cnt_T = carry
            cand = T | jnp.left_shift(jnp.int32(1), 30 - j)
            c = count_ge(cand)
            ok = c >= topk
            return jnp.where(ok, cand, T), jnp.where(ok, c, cnt_T)

        T, cnt_T = lax.fori_loop(0, 31, bit_body, (T, cnt_T))
        thr_s[0:1, :] = T
        thr_s[1:2, :] = cnt_T
        T = jnp.maximum(T, KEY_NEG_INF)
        thr_s[2:3, :] = count_ge(jnp.where(T == INT_MAX, T, T + 1))

    for n in range(1, max_cnt + 1):
        pl.when(ncnt == n)(functools.partial(search_block, n))
    T_key = jnp.maximum(thr_s[0:1, :], KEY_NEG_INF)
    T = key_to_float(T_key)
    cnt_ge = thr_s[1:2, :]
    cnt_gt = thr_s[2:3, :]

    need = topk - cnt_gt
    excess = jnp.where((cnt_ge - cnt_gt > need) & (T_key > KEY_NEG_INF), 1.0, 0.0)
    xcut_s[...] = jnp.full((1, QBLK), INT_MAX, I32)

    @pl.when(jnp.max(excess) > 0.0)
    def _():
        X = jnp.zeros((1, QBLK), I32)
        for b in range(idx_bits - 1, -1, -1):
            cand = X | (1 << b)
            f = count(lambda k, s: (k == T) & (s < cand))
            X = jnp.where(f < need, cand, X)
        xcut_s[...] = X

    xcut = xcut_s[...]

    def mask_chunk(c, carry):
        s0 = pl.multiple_of(c * IDX_CHUNK, IDX_CHUNK)
        k = sc_s[pl.ds(s0, IDX_CHUNK), :]
        s_idx = s0 + lax.broadcasted_iota(I32, (IDX_CHUNK, QBLK), 0)
        sel = ((k > T) | ((k == T) & (s_idx <= xcut))) & (s_idx <= t_idx)
        madd_s[pl.ds(s0, IDX_CHUNK), :] = jnp.where(sel, 0.0, NEG)
        return carry

    lax.fori_loop(0, nck, mask_chunk, 0)

    c_last = i // 2
    even = 1 - (i - 2 * c_last)
    row_head = lax.broadcasted_iota(I32, (LANES, QBLK), 0) // A_HEAD_DIM
    for h in range(A_HEADS):
        qp = qT_ref[0, (h // 2) * LANES:(h // 2 + 1) * LANES, :]
        qm_s[h] = jnp.where(row_head == h % 2, qp, jnp.zeros_like(qp))
    acc_s[...] = jnp.zeros_like(acc_s)
    ones = jnp.ones((ACC_ROWS - A_HEAD_DIM, ATT_CHUNK), BF16)

    def logits(c, s_buf):
        s0 = pl.multiple_of(c * ATT_CHUNK, ATT_CHUNK)
        madd = madd_s[pl.ds(s0, ATT_CHUNK), :]
        off = jnp.where(c == c_last, 2 * QBLK + QBLK * even, jnp.where(c == c_last - 1, QBLK * even, 0))
        off = pl.multiple_of(off, QBLK)
        m_blk = []
        for h in range(A_HEADS):
            kc = kk_ref[0, h // 2, pl.ds(s0, ATT_CHUNK), :]
            s = jnp.dot(kc, qm_s[h], preferred_element_type=F32) + madd + tbl_s[h, pl.ds(off, ATT_CHUNK), :]
            s_buf[h] = s
            m_blk.append(jnp.max(s, axis=0, keepdims=True))
        return jnp.concatenate(m_blk, axis=0)

    def accumulate(c, s_buf, m_all, m_blk):
        m_new = jnp.maximum(m_all, m_blk)
        alpha = jnp.exp2(m_all - m_new)
        for h in range(A_HEADS):
            rows = slice(h * A_HEAD_DIM, (h + 1) * A_HEAD_DIM)
            p = jnp.exp2(s_buf[h] - m_new[h:h + 1]).astype(BF16)
            vt = jnp.concatenate([vT_ref[0, 2 * c + u, rows, :] for u in range(ATT_CHUNK // QBLK)], axis=1)
            vt = jnp.concatenate([vt, ones], axis=0)
            acc_s[h] = alpha[h:h + 1] * acc_s[h] + jnp.dot(vt, p, preferred_element_type=F32)
        return m_new

    def att_body(pair, carry):
        m_all, m_blk = carry
        c = 2 * pair
        m_b = logits(c + 1, sb_s)
        m_all = accumulate(c, sa_s, m_all, m_blk)
        m_a = logits(c + 2, sa_s)
        return accumulate(c + 1, sb_s, m_all, m_b), m_a

    n_pairs = c_last // 2
    carry = (jnp.full((A_HEADS, QBLK), NEG, F32), logits(0, sa_s))
    m_all, m_blk = lax.fori_loop(0, n_pairs, att_body, carry)
    mall_s[...] = m_all
    mblk_s[...] = m_blk

    @pl.when(c_last % 2 == 1)
    def _():
        m_b = logits(c_last, sb_s)
        m_all = accumulate(c_last - 1, sa_s, mall_s[...], mblk_s[...])
        accumulate(c_last, sb_s, m_all, m_b)

    @pl.when(c_last % 2 == 0)
    def _():
        accumulate(c_last, sa_s, mall_s[...], mblk_s[...])

    for h in range(A_HEADS):
        rows = slice(h * A_HEAD_DIM, (h + 1) * A_HEAD_DIM)
        oT_s[rows, :] = acc_s[h, 0:A_HEAD_DIM, :] / acc_s[h, A_HEAD_DIM:A_HEAD_DIM + 1, :]
    o_ref[0] = oT_s[...].T.astype(BF16)


def _dsa_call(rel_bias, ik, kk, vT, qT, iqT, iwT):
    B, S, _ = ik.shape
    nb = S // QBLK
    topk = min(TOPK_MAX, S // 4)
    bkt = jnp.asarray(_rel_bucket_table())
    body = functools.partial(_dsa_body, topk=topk, idx_bits=int(math.log2(S)), max_cnt=S // CNT_CHUNK)
    return pl.pallas_call(
        body,
        grid=(B, nb),
        in_specs=[pl.BlockSpec(memory_space=pltpu.SMEM),
                  pl.BlockSpec((2 * QBLK, QBLK), lambda b, i: (0, 0)),
                  pl.BlockSpec((1, S, IDX_DIM), lambda b, i: (b, 0, 0)),
                  pl.BlockSpec((1, A_WIDTH // LANES, S, LANES), lambda b, i: (b, 0, 0, 0)),
                  pl.BlockSpec((1, S // LANES, A_WIDTH, LANES), lambda b, i: (b, 0, 0, 0)),
                  pl.BlockSpec((1, A_WIDTH, QBLK), lambda b, i: (b, 0, i)),
                  pl.BlockSpec((1, IDX_HEADS * IDX_DIM, QBLK), lambda b, i: (b, 0, i)),
                  pl.BlockSpec((1, IDX_HEADS, QBLK), lambda b, i: (b, 0, i))],
        out_specs=pl.BlockSpec((1, QBLK, A_WIDTH), lambda b, i: (b, i, 0)),
        out_shape=jax.ShapeDtypeStruct((B, S, A_WIDTH), BF16),
        scratch_shapes=[pltpu.VMEM((S, QBLK), F32),
                        pltpu.VMEM((S, QBLK), F32),
                        pltpu.VMEM((8, QBLK), I32),
                        pltpu.VMEM((S, QBLK), F32),
                        pltpu.VMEM((A_HEADS, TBL_PAD + 3 * QBLK, QBLK), F32),
                        pltpu.VMEM((A_WIDTH, QBLK), F32),
                        pltpu.VMEM((1, QBLK), I32),
                        pltpu.VMEM((A_HEADS, LANES, QBLK), BF16),
                        pltpu.VMEM((A_HEADS, ATT_CHUNK, QBLK), F32),
                        pltpu.VMEM((A_HEADS, ATT_CHUNK, QBLK), F32),
                        pltpu.VMEM((A_HEADS, ACC_ROWS, QBLK), F32),
                        pltpu.VMEM((A_HEADS, QBLK), F32),
                        pltpu.VMEM((A_HEADS, QBLK), F32)],
        compiler_params=_params("parallel", "arbitrary"),
        name="dsa",
    )(rel_bias, bkt, ik, kk, vT, qT, iqT, iwT)


GLA_Q = (0, 256)
GLA_K = (256, 512)
GLA_V = (512, 1024)
GLA_R = (1024, 1536)


def _gla_body(gla_ref, glr_ref, wg_ref, bg_ref, ng_ref, o_ref, st_s):
    tg = gla_ref.shape[1]
    C = G_CHUNK

    @pl.when(pl.program_id(1) == 0)
    def _():
        st_s[...] = jnp.zeros_like(st_s)

    xg = jnp.dot(glr_ref[0], wg_ref[...], preferred_element_type=F32, precision=HIGHEST) + bg_ref[...]
    logg = -(jnp.maximum(-xg, 0.0) + jnp.log1p(jnp.exp(-jnp.abs(xg)))) * (1.0 / G_TAU)

    ri = lax.broadcasted_iota(I32, (C, C), 0)
    ci = lax.broadcasted_iota(I32, (C, C), 1)
    tril = ri >= ci
    tril_f = jnp.where(tril, 1.0, 0.0).astype(F32)
    lane_head = lax.broadcasted_iota(I32, (C, LANES), 1) // G_DK
    st_rows = lax.broadcasted_iota(I32, (2 * G_DV, LANES), 0) // G_DV
    st_cols = lax.broadcasted_iota(I32, (2 * G_DV, LANES), 1) // G_DK
    st_diag = st_rows == st_cols

    for ck in range(tg // C):
        r0 = ck * C
        for p in range(G_HEADS // 2):
            lg = logg[r0:r0 + C, p * LANES:(p + 1) * LANES]
            bc = jnp.dot(tril_f, lg, preferred_element_type=F32, precision=HIGHEST)
            bl = bc[C - 1:C, :]
            q = gla_ref[0, r0:r0 + C, GLA_Q[0] + p * LANES:GLA_Q[0] + (p + 1) * LANES].astype(F32) * (G_DK ** -0.5)
            k = gla_ref[0, r0:r0 + C, GLA_K[0] + p * LANES:GLA_K[0] + (p + 1) * LANES].astype(F32)
            v = gla_ref[0, r0:r0 + C, GLA_V[0] + p * 2 * G_DV:GLA_V[0] + (p + 1) * 2 * G_DV]
            q_in = (q * jnp.exp(bc)).astype(BF16)
            k_st = (k * jnp.exp(bl - bc)).astype(BF16)
            q_rel = q * jnp.exp(bc - bl)
            o_intra = []
            for sub in range(2):
                qm = jnp.where(lane_head == sub, q_rel, 0.0).astype(BF16)
                att = lax.dot_general(qm, k_st, NT, preferred_element_type=F32)
                att = jnp.where(tril, att, 0.0).astype(BF16)
                o_intra.append(jnp.dot(att, v[:, sub * G_DV:(sub + 1) * G_DV], preferred_element_type=F32))
            st = st_s[p]
            o_inter = lax.dot_general(q_in, st.astype(BF16), NT, preferred_element_type=F32)
            uT = lax.dot_general(v, k_st, TN, preferred_element_type=F32)
            st_s[p] = st * jnp.exp(bl) + jnp.where(st_diag, uT, 0.0)
            for sub in range(2):
                hd = 2 * p + sub
                o = o_intra[sub] + o_inter[:, sub * G_DV:(sub + 1) * G_DV]
                y = _ln(o) * ng_ref[:, hd * G_DV:(hd + 1) * G_DV]
                g = gla_ref[0, r0:r0 + C, GLA_R[0] + hd * G_DV:GLA_R[0] + (hd + 1) * G_DV].astype(F32)
                o_ref[0, r0:r0 + C, hd * G_DV:(hd + 1) * G_DV] = (y * (g * _sigmoid(g))).astype(BF16)


def _gla_call(gla, glr, wg, bg, ng, tg):
    B, S, _ = gla.shape
    const = lambda b, j: (0, 0)
    return pl.pallas_call(
        _gla_body,
        grid=(B, S // tg),
        in_specs=[pl.BlockSpec((1, tg, 1536), lambda b, j: (b, j, 0)),
                  pl.BlockSpec((1, tg, G_RANK), lambda b, j: (b, j, 0)),
                  pl.BlockSpec((G_RANK, G_KW), const),
                  pl.BlockSpec((1, G_KW), const),
                  pl.BlockSpec((1, G_VW), const)],
        out_specs=pl.BlockSpec((1, tg, G_VW), lambda b, j: (b, j, 0)),
        out_shape=jax.ShapeDtypeStruct((B, S, G_VW), BF16),
        scratch_shapes=[pltpu.VMEM((G_HEADS // 2, 2 * G_DV, LANES), F32)],
        compiler_params=_params("parallel", "arbitrary"),
        name="gla",
    )(gla, glr, wg, bg, ng)


ROUTER_ROWS = 40
ROUTER_E0 = 8


def _post_body(oa_ref, ob_ref, gates_ref, x_ref, gt1_ref, sh2_ref, sc2_ref, wa_ref, wb_ref, wo_ref,
               g1_ref, b1_ref, wr_ref, br_ref, x1_ref, h2_ref, gate_ref):
    tm = x_ref.shape[1]
    D = x_ref.shape[2]
    ya = jnp.dot(oa_ref[0], wa_ref[...], preferred_element_type=F32)
    yb = jnp.dot(ob_ref[0], wb_ref[...], preferred_element_type=F32)
    ga = gates_ref[0, :, 0:D].astype(F32)
    gb = gates_ref[0, :, D:2 * D].astype(F32)
    merged = _sigmoid(ga) * ya + _sigmoid(gb) * yb
    y = jnp.dot(merged.astype(BF16), wo_ref[...], preferred_element_type=F32)
    x1 = _ln(DN_ALPHA * x_ref[0] + gt1_ref[0] * y) * g1_ref[...] + b1_ref[...]
    x1_ref[0] = x1
    h2 = _ln(x1) * (1.0 + sc2_ref[0]) + sh2_ref[0]
    h2_ref[0] = h2.astype(BF16)

    lt = lax.dot_general(wr_ref[...], h2, NT, preferred_element_type=F32, precision=HIGHEST) + br_ref[...]
    gl = lt[0:N_GROUPS]
    gmax = jnp.max(gl, axis=0, keepdims=True)
    g_w = 1.0 / jnp.sum(jnp.exp(gl - gmax), axis=0, keepdims=True)
    r4 = lax.broadcasted_iota(I32, (N_GROUPS, tm), 0)
    g_idx = jnp.min(jnp.where(gl == gmax, r4, N_GROUPS), axis=0, keepdims=True)
    eg = jnp.zeros((EXPERTS_PER_GROUP, tm), F32)
    for g in range(N_GROUPS):
        lo = ROUTER_E0 + g * EXPERTS_PER_GROUP
        eg = jnp.where(g_idx == g, lt[lo:lo + EXPERTS_PER_GROUP], eg)
    r8 = lax.broadcasted_iota(I32, (EXPERTS_PER_GROUP, tm), 0)
    e1 = jnp.max(eg, axis=0, keepdims=True)
    i1 = jnp.min(jnp.where(eg == e1, r8, EXPERTS_PER_GROUP), axis=0, keepdims=True)
    eg2 = jnp.where(r8 == i1, -jnp.inf, eg)
    e2 = jnp.max(eg2, axis=0, keepdims=True)
    i2 = jnp.min(jnp.where(eg2 == e2, r8, EXPERTS_PER_GROUP), axis=0, keepdims=True)
    d = jnp.exp(e2 - e1)
    w1 = g_w / (1.0 + d)
    w2 = g_w * d / (1.0 + d)
    in_group = jnp.where(r8 == i1, w1, 0.0) + jnp.where(r8 == i2, w2, 0.0)
    blocks = [jnp.where(g_idx == g, in_group, 0.0) for g in range(N_GROUPS)]
    blocks.append(jnp.zeros((LANES - N_EXPERTS, tm), F32))
    gate_ref[...] = jnp.concatenate(blocks, axis=0).T


def _post_call(o_a, o_b, gates, x, gt1, sh2, sc2, wa, wb, wo, g1, b1, wr, br, tm):
    B, S, D = x.shape
    nt = S // tm
    const = lambda b, t: (0, 0)
    row = lambda b, t: (b, 0, 0)
    tile = lambda b, t: (b, t, 0)
    return pl.pallas_call(
        _post_body,
        grid=(B, nt),
        in_specs=[pl.BlockSpec((1, tm, A_WIDTH), tile),
                  pl.BlockSpec((1, tm, G_VW), tile),
                  pl.BlockSpec((1, tm, 2 * D), tile),
                  pl.BlockSpec((1, tm, D), tile),
                  pl.BlockSpec((1, 1, D), row),
                  pl.BlockSpec((1, 1, D), row),
                  pl.BlockSpec((1, 1, D), row),
                  pl.BlockSpec(wa.shape, const),
                  pl.BlockSpec(wb.shape, const),
                  pl.BlockSpec(wo.shape, const),
                  pl.BlockSpec((1, D), const),
                  pl.BlockSpec((1, D), const),
                  pl.BlockSpec(wr.shape, const),
                  pl.BlockSpec(br.shape, const)],
        out_specs=(pl.BlockSpec((1, tm, D), tile),
                   pl.BlockSpec((1, tm, D), tile),
                   pl.BlockSpec((tm, LANES), lambda b, t: (b * nt + t, 0))),
        out_shape=(jax.ShapeDtypeStruct((B, S, D), F32),
                   jax.ShapeDtypeStruct((B, S, D), BF16),
                   jax.ShapeDtypeStruct((B * S, LANES), F32)),
        compiler_params=_params("parallel", "parallel"),
        name="post",
    )(o_a, o_b, gates, x, gt1, sh2, sc2, wa, wb, wo, g1, b1, wr, br)


MOE_EXPERTS_PER_STEP = 4


def _moe_body(h2_ref, gate_ref, x1_ref, gt2_ref, w1_ref, w3_ref, w2_ref, g2_ref, b2_ref, o_ref, acc_s):
    c = pl.program_id(1)
    ne = MOE_EXPERTS_PER_STEP

    @pl.when(c == 0)
    def _():
        acc_s[...] = jnp.zeros_like(acc_s)

    h = h2_ref[...]
    gate = pltpu.roll(gate_ref[...], (LANES - c * ne) % LANES, axis=1)
    hid = []
    for j in range(ne):
        a = jnp.dot(h, w1_ref[j], preferred_element_type=F32)
        b = jnp.dot(h, w3_ref[j], preferred_element_type=F32)
        hid.append((a * _sigmoid(a) * b * gate[:, j:j + 1]).astype(BF16))
    acc_s[...] += jnp.dot(jnp.concatenate(hid, axis=1), w2_ref[...], preferred_element_type=F32)

    @pl.when(c == pl.num_programs(1) - 1)
    def _():
        z = DN_ALPHA * x1_ref[...] + gt2_ref[0] * acc_s[...]
        o_ref[...] = _ln(z) * g2_ref[...] + b2_ref[...]


def _moe_call(h2, gate, x1, gt2, w1, w3, w2, g2, b2, tm, S):
    T, D = h2.shape
    ne = MOE_EXPERTS_PER_STEP
    nc = N_EXPERTS // ne
    tiles_per_seq = S // tm
    tile = lambda t, c: (t, 0)
    const = lambda t, c: (0, 0)
    return pl.pallas_call(
        _moe_body,
        grid=(T // tm, nc),
        in_specs=[pl.BlockSpec((tm, D), tile),
                  pl.BlockSpec((tm, LANES), tile),
                  pl.BlockSpec((tm, D), tile),
                  pl.BlockSpec((1, 1, D), lambda t, c: (t // tiles_per_seq, 0, 0)),
                  pl.BlockSpec((ne, D, D_EXPERT), lambda t, c: (c, 0, 0)),
                  pl.BlockSpec((ne, D, D_EXPERT), lambda t, c: (c, 0, 0)),
                  pl.BlockSpec((ne * D_EXPERT, D), lambda t, c: (c, 0)),
                  pl.BlockSpec((1, D), const),
                  pl.BlockSpec((1, D), const)],
        out_specs=pl.BlockSpec((tm, D), tile),
        out_shape=jax.ShapeDtypeStruct((T, D), F32),
        scratch_shapes=[pltpu.VMEM((tm, D), F32)],
        compiler_params=_params("parallel", "arbitrary"),
        name="moe",
    )(h2, gate, x1, gt2, w1, w3, w2, g2, b2)


def _pick(n, pref):
    return pref if n % pref == 0 else n


def kernel(x, c, rel_bias, w_ada, b_ada, w_in, gla_w_gate, gla_b_gate, gla_norm_g, w_branch_a, w_branch_b, w_out, ln1_g, ln1_b, w_router_group, b_router_group, w_router_expert, b_router_expert, w_exp_gate, w_exp_up, w_exp_down, ln2_g, ln2_b):
    B, S, D = x.shape
    assert S % (2 * QBLK) == 0 and D == 1024 and w_ada.shape[0] == DEPTH == 1
    l = 0

    ada = _ada_call(c, w_ada[l], b_ada[l])
    sh1, sc1, gt1, sh2, sc2, gt2 = [ada[:, i * D:(i + 1) * D].reshape(B, 1, D) for i in range(6)]

    offs = np.concatenate([[0], np.cumsum(SPLIT_SIZES)])
    seg = lambda i: w_in[l][:, offs[i]:offs[i + 1]]
    (w_aq, w_ak, w_av, w_iq, w_ik, w_iw, w_gq, w_gk, w_gv, w_gr, w_glr, w_ga, w_gb) = [seg(i) for i in range(13)]
    pad = jnp.zeros((D, TOK_SMALL[1] - TOK_SMALL[0] - IDX_DIM - G_RANK), F32)
    w_tok = jnp.concatenate([w_ak, w_gq, w_gk, w_gv, w_gr, w_ga, w_gb, w_ik, w_glr, pad], axis=1).astype(BF16)
    w_ch = jnp.concatenate([w_aq, w_av, w_iq, w_iw], axis=1).T.astype(BF16)

    tm = _pick(S, 512)
    kk, gla, gates, ik, glr, qT, vT, iqT, iwT = _inproj_call(x, sh1, sc1, w_tok, w_ch, tm)

    o_a = _dsa_call(rel_bias, ik, kk, vT, qT, iqT, iwT)
    o_b = _gla_call(gla, glr, gla_w_gate[l], gla_b_gate[l].reshape(1, G_KW), gla_norm_g[l].reshape(1, G_VW),
                    _pick(S, 256))

    wr = jnp.zeros((ROUTER_ROWS, D), F32)
    wr = wr.at[0:N_GROUPS].set(w_router_group[l].T).at[ROUTER_E0:ROUTER_E0 + N_EXPERTS].set(w_router_expert[l].T)
    br = jnp.zeros((ROUTER_ROWS, 1), F32)
    br = br.at[0:N_GROUPS, 0].set(b_router_group[l]).at[ROUTER_E0:ROUTER_E0 + N_EXPERTS, 0].set(b_router_expert[l])
    x1, h2, gate = _post_call(o_a, o_b, gates, x, gt1, sh2, sc2,
                              w_branch_a[l].astype(BF16), w_branch_b[l].astype(BF16), w_out[l].astype(BF16),
                              ln1_g[l].reshape(1, D), ln1_b[l].reshape(1, D), wr, br, tm)

    tm5 = _pick(S, 1024)
    out = _moe_call(h2.reshape(B * S, D), gate, x1.reshape(B * S, D), gt2,
                    w_exp_gate[l].astype(BF16), w_exp_up[l].astype(BF16),
                    w_exp_down[l].astype(BF16).reshape(N_EXPERTS * D_EXPERT, D),
                    ln2_g[l].reshape(1, D), ln2_b[l].reshape(1, D), tm5, S)
    return out.reshape(B, S, D)
```

```python
import functools
import math

import numpy as np
import jax
import jax.numpy as jnp
from jax import lax
from jax.experimental import pallas as pl
from jax.experimental.pallas import tpu as pltpu

F32 = jnp.float32
BF16 = jnp.bfloat16
I32 = jnp.int32
HIGHEST = lax.Precision.HIGHEST

A_HEADS = 8
A_HEAD_DIM = 64
A_WIDTH = A_HEADS * A_HEAD_DIM
IDX_HEADS = 16
IDX_DIM = 32
TOPK_MAX = 256
QBLK = 128
REL_BUCKETS = 32
REL_MAX_DIST = 128
G_HEADS = 4
G_DK = 64
G_DV = 128
G_KW = G_HEADS * G_DK
G_VW = G_HEADS * G_DV
G_RANK = 16
G_TAU = 16.0
G_CHUNK = 64
N_GROUPS = 4
EXPERTS_PER_GROUP = 8
N_EXPERTS = N_GROUPS * EXPERTS_PER_GROUP
D_EXPERT = 256
DEPTH = 1
DN_ALPHA = (2.0 * DEPTH) ** 0.25
LN_EPS = 1e-5
SPLIT_SIZES = (A_WIDTH, A_WIDTH, A_WIDTH, IDX_HEADS * IDX_DIM, IDX_DIM, IDX_HEADS,
               G_KW, G_KW, G_VW, G_VW, G_RANK, 1024, 1024)

LANES = 128
VMEM_LIMIT_BYTES = 56 * 1024 * 1024

NEG = -1e30
LOG2E = math.log2(math.e)
INT_MIN = -2 ** 31
INT_MAX = 2 ** 31 - 1
KEY_NEG_INF = -2 ** 31 + 0x7FFFFF

NT = (((1,), (1,)), ((), ()))
TN = (((0,), (0,)), ((), ()))


def _ln(x):
    mu = jnp.mean(x, axis=-1, keepdims=True)
    xc = x - mu
    var = jnp.mean(xc * xc, axis=-1, keepdims=True)
    return xc * lax.rsqrt(var + LN_EPS)


def _sigmoid(x):
    return 1.0 / (1.0 + jnp.exp(-x))


def _params(*sem):
    return pltpu.CompilerParams(dimension_semantics=sem, vmem_limit_bytes=VMEM_LIMIT_BYTES)


def _ada_body(c_ref, w_ref, b_ref, o_ref):
    c = c_ref[...]
    cond = c * _sigmoid(c)
    o_ref[...] = jnp.dot(cond, w_ref[...], preferred_element_type=F32, precision=HIGHEST) + b_ref[...]


def _ada_call(c, w, b):
    B, D = c.shape
    N = w.shape[1]
    tn = 1536
    return pl.pallas_call(
        _ada_body,
        grid=(N // tn,),
        in_specs=[pl.BlockSpec((B, D), lambda j: (0, 0)),
                  pl.BlockSpec((D, tn), lambda j: (0, j)),
                  pl.BlockSpec((1, tn), lambda j: (0, j))],
        out_specs=pl.BlockSpec((B, tn), lambda j: (0, j)),
        out_shape=jax.ShapeDtypeStruct((B, N), F32),
        compiler_params=_params("arbitrary"),
        name="ada",
    )(c, w, b.reshape(1, N))


TOK_K = (0, 512)
TOK_GLA = (512, 2048)
TOK_GATES = (2048, 4096)
TOK_SMALL = (4096, 4224)
CH_Q = (0, 512)
CH_V = (512, 1024)
CH_IQ = (1024, 1536)
CH_IW = (1536, 1552)
IW_SCALE = IDX_HEADS ** -0.5 * IDX_DIM ** -0.5


def _inproj_body(x_ref, sh_ref, sc_ref, wtok_ref, wch_ref,
                 k_ref, gla_ref, gates_ref, ik_ref, glr_ref, qT_ref, vT_ref, iqT_ref, iwT_ref):
    tm = x_ref.shape[1]
    h = (_ln(x_ref[0]) * (1.0 + sc_ref[0]) + sh_ref[0]).astype(BF16)

    def tok(ab):
        return jnp.dot(h, wtok_ref[:, ab[0]:ab[1]], preferred_element_type=F32)

    def ch(ab):
        return lax.dot_general(wch_ref[ab[0]:ab[1], :], h, NT, preferred_element_type=F32)

    kres = tok(TOK_K)
    for p in range(A_WIDTH // LANES):
        k_ref[0, p] = kres[:, p * LANES:(p + 1) * LANES].astype(BF16)
    gla_ref[0] = tok(TOK_GLA).astype(BF16)
    gates_ref[0] = tok(TOK_GATES).astype(BF16)
    small = tok(TOK_SMALL)
    ik_ref[0] = small[:, :IDX_DIM].astype(BF16)
    glr_ref[0] = small[:, IDX_DIM:IDX_DIM + G_RANK]

    qT_ref[0] = (ch(CH_Q) * (A_HEAD_DIM ** -0.5 * LOG2E)).astype(BF16)
    vres = ch(CH_V).astype(BF16)
    for j in range(tm // LANES):
        vT_ref[0, j] = vres[:, j * LANES:(j + 1) * LANES]
    iqT_ref[0] = ch(CH_IQ).astype(BF16)
    iwT_ref[0] = ch(CH_IW) * IW_SCALE


def _inproj_call(x, sh1, sc1, w_tok, w_ch, tm):
    B, S, D = x.shape
    nt = S // tm
    const = lambda b, t: (0, 0)
    out_shape = (
        jax.ShapeDtypeStruct((B, A_WIDTH // LANES, S, LANES), BF16),
        jax.ShapeDtypeStruct((B, S, 1536), BF16),
        jax.ShapeDtypeStruct((B, S, 2048), BF16),
        jax.ShapeDtypeStruct((B, S, IDX_DIM), BF16),
        jax.ShapeDtypeStruct((B, S, G_RANK), F32),
        jax.ShapeDtypeStruct((B, A_WIDTH, S), BF16),
        jax.ShapeDtypeStruct((B, S // LANES, A_WIDTH, LANES), BF16),
        jax.ShapeDtypeStruct((B, IDX_HEADS * IDX_DIM, S), BF16),
        jax.ShapeDtypeStruct((B, IDX_HEADS, S), F32),
    )
    out_specs = (
        pl.BlockSpec((1, A_WIDTH // LANES, tm, LANES), lambda b, t: (b, 0, t, 0)),
        pl.BlockSpec((1, tm, 1536), lambda b, t: (b, t, 0)),
        pl.BlockSpec((1, tm, 2048), lambda b, t: (b, t, 0)),
        pl.BlockSpec((1, tm, IDX_DIM), lambda b, t: (b, t, 0)),
        pl.BlockSpec((1, tm, G_RANK), lambda b, t: (b, t, 0)),
        pl.BlockSpec((1, A_WIDTH, tm), lambda b, t: (b, 0, t)),
        pl.BlockSpec((1, tm // LANES, A_WIDTH, LANES), lambda b, t: (b, t, 0, 0)),
        pl.BlockSpec((1, IDX_HEADS * IDX_DIM, tm), lambda b, t: (b, 0, t)),
        pl.BlockSpec((1, IDX_HEADS, tm), lambda b, t: (b, 0, t)),
    )
    return pl.pallas_call(
        _inproj_body,
        grid=(B, nt),
        in_specs=[pl.BlockSpec((1, tm, D), lambda b, t: (b, t, 0)),
                  pl.BlockSpec((1, 1, D), lambda b, t: (b, 0, 0)),
                  pl.BlockSpec((1, 1, D), lambda b, t: (b, 0, 0)),
                  pl.BlockSpec(w_tok.shape, const),
                  pl.BlockSpec(w_ch.shape, const)],
        out_specs=out_specs,
        out_shape=out_shape,
        compiler_params=_params("parallel", "parallel"),
        name="inproj",
    )(x, sh1, sc1, w_tok, w_ch)


IDX_CHUNK = 256
CNT_CHUNK = 512
SORT_GROUP = 4
ATT_CHUNK = 256
TBL_PAD = 2 * QBLK
ACC_ROWS = A_HEAD_DIM + 16


def _rel_bucket_table():
    s = np.arange(2 * QBLK)[:, None]
    t = np.arange(QBLK)[None, :]
    dist = np.maximum(t + QBLK - s, 0)
    max_exact = REL_BUCKETS // 2
    d_f = np.maximum(dist, 1).astype(np.float32)
    large = max_exact + (np.log(d_f / max_exact) / math.log(REL_MAX_DIST / max_exact)
                         * (REL_BUCKETS - max_exact)).astype(np.int32)
    large = np.minimum(large, REL_BUCKETS - 1)
    return np.where(dist < max_exact, dist, large).astype(np.int32)


def _far_bucket():
    max_exact = REL_BUCKETS // 2
    v = max_exact + int(np.float32(np.log(np.float32(QBLK + 1) / max_exact) / math.log(REL_MAX_DIST / max_exact)
                                   * (REL_BUCKETS - max_exact)))
    assert min(v, REL_BUCKETS - 1) == REL_BUCKETS - 1
    return REL_BUCKETS - 1


def _dsa_body(rb_ref, bkt_ref, ik_ref, kk_ref, vT_ref, qT_ref, iqT_ref, iwT_ref, o_ref,
              sc_s, srt_s, thr_s, madd_s, tbl_s, oT_s, xcut_s, qm_s, sa_s, sb_s, acc_s, mall_s, mblk_s,
              *, topk, idx_bits, max_cnt):
    i = pl.program_id(1)
    nck = (i + 2) // 2
    t_idx = i * QBLK + lax.broadcasted_iota(I32, (1, QBLK), 1)

    @pl.when(i == 0)
    def _():
        bkt = bkt_ref[...]
        tbl_s[...] = jnp.zeros_like(tbl_s)
        for h in range(A_HEADS):
            t = jnp.zeros((2 * QBLK, QBLK), F32)
            for k in range(REL_BUCKETS):
                t = jnp.where(bkt == k, rb_ref[k, h], t)
            tbl_s[h, TBL_PAD:TBL_PAD + 2 * QBLK, :] = (t - rb_ref[_far_bucket(), h]) * LOG2E

    def key_to_float(key):
        key = jnp.maximum(key, KEY_NEG_INF)
        return pltpu.bitcast(jnp.where(key < 0, key ^ INT_MAX, key), F32)

    def score_chunk(c, carry):
        s0 = pl.multiple_of(c * IDX_CHUNK, IDX_CHUNK)
        kc = ik_ref[0, pl.ds(s0, IDX_CHUNK), :]
        acc = jnp.zeros((IDX_CHUNK, QBLK), F32)
        for hp in range(IDX_HEADS // 2):
            r0 = hp * 2 * IDX_DIM
            rhs = jnp.concatenate([iqT_ref[0, r0:r0 + IDX_DIM, :],
                                   iqT_ref[0, r0 + IDX_DIM:r0 + 2 * IDX_DIM, :]], axis=1)
            z = jnp.dot(kc, rhs, preferred_element_type=F32)
            acc = acc + jnp.maximum(z[:, :QBLK], 0.0) * iwT_ref[0, 2 * hp:2 * hp + 1, :]
            acc = acc + jnp.maximum(z[:, QBLK:], 0.0) * iwT_ref[0, 2 * hp + 1:2 * hp + 2, :]
        s_idx = s0 + lax.broadcasted_iota(I32, (IDX_CHUNK, QBLK), 0)
        sc_s[pl.ds(s0, IDX_CHUNK), :] = jnp.where(s_idx <= t_idx, acc, -jnp.inf)
        return carry

    ncnt = (nck + 1) // 2
    lax.fori_loop(0, ncnt, lambda c, carry: score_chunk(2 * c + 1, score_chunk(2 * c, carry)), 0)

    def count(pred):
        def body(c, cnt):
            s0 = pl.multiple_of(c * CNT_CHUNK, CNT_CHUNK)
            k = sc_s[pl.ds(s0, CNT_CHUNK), :]
            s_idx = s0 + lax.broadcasted_iota(I32, (CNT_CHUNK, QBLK), 0)
            m = jnp.where(pred(k, s_idx), 1, 0)
            return cnt + jnp.sum(m.reshape(CNT_CHUNK // 8, 8, QBLK), axis=0)
        cnt = lax.fori_loop(0, ncnt, body, jnp.zeros((8, QBLK), I32))
        return jnp.sum(cnt, axis=0, keepdims=True)

    def search_block(n):
        groups = n * CNT_CHUNK // (8 * SORT_GROUP)

        for g in range(groups):
            v = [sc_s[(SORT_GROUP * g + u) * 8:(SORT_GROUP * g + u + 1) * 8, :] for u in range(SORT_GROUP)]
            for a, b in ((0, 1), (2, 3), (0, 2), (1, 3), (1, 2)):
                v[a], v[b] = jnp.maximum(v[a], v[b]), jnp.minimum(v[a], v[b])
            for u in range(SORT_GROUP):
                srt_s[(SORT_GROUP * g + u) * 8:(SORT_GROUP * g + u + 1) * 8, :] = v[u]

        def count_ge(cand_key):
            cand = key_to_float(cand_key)
            parts = []
            for g in range(groups):
                cnt = 0
                for u in range(SORT_GROUP):
                    tile = srt_s[(SORT_GROUP * g + u) * 8:(SORT_GROUP * g + u + 1) * 8, :]
                    cnt = jnp.where(tile >= cand, u + 1, cnt)
                parts.append(cnt)
            while len(parts) > 1:
                odd = parts[len(parts) & ~1:]
                parts = [parts[j] + parts[j + 1] for j in range(0, len(parts) - 1, 2)] + odd
            return jnp.sum(parts[0], axis=0, keepdims=True)

        c0 = count_ge(jnp.zeros((1, QBLK), I32))
        ok = c0 >= topk
        T = jnp.where(ok, 0, INT_MIN).astype(I32)
        cnt_T = jnp.where(ok, c0, n * CNT_CHUNK)

        def bit_body(j, carry):
            T, cnt_T = carry
            cand = T | jnp.left_shift(jnp.int32(1), 30 - j)
            c = count_ge(cand)
            ok = c >= topk
            return jnp.where(ok, cand, T), jnp.where(ok, c, cnt_T)

        T, cnt_T = lax.fori_loop(0, 31, bit_body, (T, cnt_T))
        thr_s[0:1, :] = T
        thr_s[1:2, :] = cnt_T
        T = jnp.maximum(T, KEY_NEG_INF)
        thr_s[2:3, :] = count_ge(jnp.where(T == INT_MAX, T, T + 1))

    for n in range(1, max_cnt + 1):
        pl.when(ncnt == n)(functools.partial(search_block, n))
    T_key = jnp.maximum(thr_s[0:1, :], KEY_NEG_INF)
    T = key_to_float(T_key)
    cnt_ge = thr_s[1:2, :]
    cnt_gt = thr_s[2:3, :]

    need = topk - cnt_gt
    excess = jnp.where((cnt_ge - cnt_gt > need) & (T_key > KEY_NEG_INF), 1.0, 0.0)
    xcut_s[...] = jnp.full((1, QBLK), INT_MAX, I32)

    @pl.when(jnp.max(excess) > 0.0)
    def _():
        X = jnp.zeros((1, QBLK), I32)
        for b in range(idx_bits - 1, -1, -1):
            cand = X | (1 << b)
            f = count(lambda k, s: (k == T) & (s < cand))
            X = jnp.where(f < need, cand, X)
        xcut_s[...] = X

    xcut = xcut_s[...]

    def mask_chunk(c, carry):
        s0 = pl.multiple_of(c * IDX_CHUNK, IDX_CHUNK)
        k = sc_s[pl.ds(s0, IDX_CHUNK), :]
        s_idx = s0 + lax.broadcasted_iota(I32, (IDX_CHUNK, QBLK), 0)
        sel = ((k > T) | ((k == T) & (s_idx <= xcut))) & (s_idx <= t_idx)
        madd_s[pl.ds(s0, IDX_CHUNK), :] = jnp.where(sel, 0.0, NEG)
        return carry

    lax.fori_loop(0, nck, mask_chunk, 0)

    c_last = i // 2
    even = 1 - (i - 2 * c_last)
    row_head = lax.broadcasted_iota(I32, (LANES, QBLK), 0) // A_HEAD_DIM
    for h in range(A_HEADS):
        qp = qT_ref[0, (h // 2) * LANES:(h // 2 + 1) * LANES, :]
        qm_s[h] = jnp.where(row_head == h % 2, qp, jnp.zeros_like(qp))
    acc_s[...] = jnp.zeros_like(acc_s)
    ones = jnp.ones((ACC_ROWS - A_HEAD_DIM, ATT_CHUNK), BF16)

    def logits(c, s_buf):
        s0 = pl.multiple_of(c * ATT_CHUNK, ATT_CHUNK)
        madd = madd_s[pl.ds(s0, ATT_CHUNK), :]
        off = jnp.where(c == c_last, 2 * QBLK + QBLK * even, jnp.where(c == c_last - 1, QBLK * even, 0))
        off = pl.multiple_of(off, QBLK)
        m_blk = []
        for h in range(A_HEADS):
            kc = kk_ref[0, h // 2, pl.ds(s0, ATT_CHUNK), :]
            s = jnp.dot(kc, qm_s[h], preferred_element_type=F32) + madd + tbl_s[h, pl.ds(off, ATT_CHUNK), :]
            s_buf[h] = s
            m_blk.append(jnp.max(s, axis=0, keepdims=True))
        return jnp.concatenate(m_blk, axis=0)

    def accumulate(c, s_buf, m_all, m_blk):
        m_new = jnp.maximum(m_all, m_blk)
        alpha = jnp.exp2(m_all - m_new)
        for h in range(A_HEADS):
            rows = slice(h * A_HEAD_DIM, (h + 1) * A_HEAD_DIM)
            p = jnp.exp2(s_buf[h] - m_new[h:h + 1]).astype(BF16)
            vt = jnp.concatenate([vT_ref[0, 2 * c + u, rows, :] for u in range(ATT_CHUNK // QBLK)], axis=1)
            vt = jnp.concatenate([vt, ones], axis=0)
            acc_s[h] = alpha[h:h + 1] * acc_s[h] + jnp.dot(vt, p, preferred_element_type=F32)
        return m_new

    def att_body(pair, carry):
        m_all, m_blk = carry
        c = 2 * pair
        m_b = logits(c + 1, sb_s)
        m_all = accumulate(c, sa_s, m_all, m_blk)
        m_a = logits(c + 2, sa_s)
        return accumulate(c + 1, sb_s, m_all, m_b), m_a

    n_pairs = c_last // 2
    carry = (jnp.full((A_HEADS, QBLK), NEG, F32), logits(0, sa_s))
    m_all, m_blk = lax.fori_loop(0, n_pairs, att_body, carry)
    mall_s[...] = m_all
    mblk_s[...] = m_blk

    @pl.when(c_last % 2 == 1)
    def _():
        m_b = logits(c_last, sb_s)
        m_all = accumulate(c_last - 1, sa_s, mall_s[...], mblk_s[...])
        accumulate(c_last, sb_s, m_all, m_b)

    @pl.when(c_last % 2 == 0)
    def _():
        accumulate(c_last, sa_s, mall_s[...], mblk_s[...])

    for h in range(A_HEADS):
        rows = slice(h * A_HEAD_DIM, (h + 1) * A_HEAD_DIM)
        oT_s[rows, :] = acc_s[h, 0:A_HEAD_DIM, :] / acc_s[h, A_HEAD_DIM:A_HEAD_DIM + 1, :]
    o_ref[0] = oT_s[...].T.astype(BF16)


def _dsa_call(rel_bias, ik, kk, vT, qT, iqT, iwT):
    B, S, _ = ik.shape
    nb = S // QBLK
    topk = min(TOPK_MAX, S // 4)
    bkt = jnp.asarray(_rel_bucket_table())
    body = functools.partial(_dsa_body, topk=topk, idx_bits=int(math.log2(S)), max_cnt=S // CNT_CHUNK)
    return pl.pallas_call(
        body,
        grid=(B, nb),
        in_specs=[pl.BlockSpec(memory_space=pltpu.SMEM),
                  pl.BlockSpec((2 * QBLK, QBLK), lambda b, i: (0, 0)),
                  pl.BlockSpec((1, S, IDX_DIM), lambda b, i: (b, 0, 0)),
                  pl.BlockSpec((1, A_WIDTH // LANES, S, LANES), lambda b, i: (b, 0, 0, 0)),
                  pl.BlockSpec((1, S // LANES, A_WIDTH, LANES), lambda b, i: (b, 0, 0, 0)),
                  pl.BlockSpec((1, A_WIDTH, QBLK), lambda b, i: (b, 0, i)),
                  pl.BlockSpec((1, IDX_HEADS * IDX_DIM, QBLK), lambda b, i: (b, 0, i)),
                  pl.BlockSpec((1, IDX_HEADS, QBLK), lambda b, i: (b, 0, i))],
        out_specs=pl.BlockSpec((1, QBLK, A_WIDTH), lambda b, i: (b, i, 0)),
        out_shape=jax.ShapeDtypeStruct((B, S, A_WIDTH), BF16),
        scratch_shapes=[pltpu.VMEM((S, QBLK), F32),
                        pltpu.VMEM((S, QBLK), F32),
                        pltpu.VMEM((8, QBLK), I32),
                        pltpu.VMEM((S, QBLK), F32),
                        pltpu.VMEM((A_HEADS, TBL_PAD + 3 * QBLK, QBLK), F32),
                        pltpu.VMEM((A_WIDTH, QBLK), F32),
                        pltpu.VMEM((1, QBLK), I32),
                        pltpu.VMEM((A_HEADS, LANES, QBLK), BF16),
                        pltpu.VMEM((A_HEADS, ATT_CHUNK, QBLK), F32),
                        pltpu.VMEM((A_HEADS, ATT_CHUNK, QBLK), F32),
                        pltpu.VMEM((A_HEADS, ACC_ROWS, QBLK), F32),
                        pltpu.VMEM((A_HEADS, QBLK), F32),
                        pltpu.VMEM((A_HEADS, QBLK), F32)],
        compiler_params=_params("parallel", "arbitrary"),
        name="dsa",
    )(rel_bias, bkt, ik, kk, vT, qT, iqT, iwT)


GLA_Q = (0, 256)
GLA_K = (256, 512)
GLA_V = (512, 1024)
GLA_R = (1024, 1536)


def _gla_body(gla_ref, glr_ref, wg_ref, bg_ref, ng_ref, o_ref, st_s):
    tg = gla_ref.shape[1]
    C = G_CHUNK

    @pl.when(pl.program_id(1) == 0)
    def _():
        st_s[...] = jnp.zeros_like(st_s)

    xg = jnp.dot(glr_ref[0], wg_ref[...], preferred_element_type=F32, precision=HIGHEST) + bg_ref[...]
    logg = -(jnp.maximum(-xg, 0.0) + jnp.log1p(jnp.exp(-jnp.abs(xg)))) * (1.0 / G_TAU)

    rt = lax.broadcasted_iota(I32, (tg, tg), 0)
    ct = lax.broadcasted_iota(I32, (tg, tg), 1)
    cum_f = jnp.where((rt // C == ct // C) & (rt >= ct), 1.0, 0.0).astype(F32)
    bc_all = jnp.dot(cum_f, logg, preferred_element_type=F32, precision=HIGHEST)

    ri = lax.broadcasted_iota(I32, (C, C), 0)
    ci = lax.broadcasted_iota(I32, (C, C), 1)
    tril = ri >= ci
    lane_head = lax.broadcasted_iota(I32, (C, LANES), 1) // G_DK
    st_rows = lax.broadcasted_iota(I32, (2 * G_DV, LANES), 0) // G_DV
    st_cols = lax.broadcasted_iota(I32, (2 * G_DV, LANES), 1) // G_DK
    st_diag = st_rows == st_cols

    for ck in range(tg // C):
        r0 = ck * C
        for p in range(G_HEADS // 2):
            bc = bc_all[r0:r0 + C, p * LANES:(p + 1) * LANES]
            bl = bc[C - 1:C, :]
            q = gla_ref[0, r0:r0 + C, GLA_Q[0] + p * LANES:GLA_Q[0] + (p + 1) * LANES].astype(F32) * (G_DK ** -0.5)
            k = gla_ref[0, r0:r0 + C, GLA_K[0] + p * LANES:GLA_K[0] + (p + 1) * LANES].astype(F32)
            v = gla_ref[0, r0:r0 + C, GLA_V[0] + p * 2 * G_DV:GLA_V[0] + (p + 1) * 2 * G_DV]
            q_in = (q * jnp.exp(bc)).astype(BF16)
            k_st = (k * jnp.exp(bl - bc)).astype(BF16)
            q_rel = q * jnp.exp(bc - bl)
            o_intra = []
            for sub in range(2):
                qm = jnp.where(lane_head == sub, q_rel, 0.0).astype(BF16)
                att = lax.dot_general(qm, k_st, NT, preferred_element_type=F32)
                att = jnp.where(tril, att, 0.0).astype(BF16)
                o_intra.append(jnp.dot(att, v[:, sub * G_DV:(sub + 1) * G_DV], preferred_element_type=F32))
            st = st_s[p]
            o_inter = lax.dot_general(q_in, st.astype(BF16), NT, preferred_element_type=F32)
            uT = lax.dot_general(v, k_st, TN, preferred_element_type=F32)
            st_s[p] = st * jnp.exp(bl) + jnp.where(st_diag, uT, 0.0)
            for sub in range(2):
                hd = 2 * p + sub
                o = o_intra[sub] + o_inter[:, sub * G_DV:(sub + 1) * G_DV]
                y = _ln(o) * ng_ref[:, hd * G_DV:(hd + 1) * G_DV]
                g = gla_ref[0, r0:r0 + C, GLA_R[0] + hd * G_DV:GLA_R[0] + (hd + 1) * G_DV].astype(F32)
                o_ref[0, r0:r0 + C, hd * G_DV:(hd + 1) * G_DV] = (y * (g * _sigmoid(g))).astype(BF16)


def _gla_call(gla, glr, wg, bg, ng, tg):
    B, S, _ = gla.shape
    const = lambda b, j: (0, 0)
    return pl.pallas_call(
        _gla_body,
        grid=(B, S // tg),
        in_specs=[pl.BlockSpec((1, tg, 1536), lambda b, j: (b, j, 0)),
                  pl.BlockSpec((1, tg, G_RANK), lambda b, j: (b, j, 0)),
                  pl.BlockSpec((G_RANK, G_KW), const),
                  pl.BlockSpec((1, G_KW), const),
                  pl.BlockSpec((1, G_VW), const)],
        out_specs=pl.BlockSpec((1, tg, G_VW), lambda b, j: (b, j, 0)),
        out_shape=jax.ShapeDtypeStruct((B, S, G_VW), BF16),
        scratch_shapes=[pltpu.VMEM((G_HEADS // 2, 2 * G_DV, LANES), F32)],
        compiler_params=_params("parallel", "arbitrary"),
        name="gla",
    )(gla, glr, wg, bg, ng)


ROUTER_ROWS = 40
ROUTER_E0 = 8


def _post_body(oa_ref, ob_ref, gates_ref, x_ref, gt1_ref, sh2_ref, sc2_ref, wa_ref, wb_ref, wo_ref,
               g1_ref, b1_ref, wr_ref, br_ref, x1_ref, h2_ref, gate_ref):
    tm = x_ref.shape[1]
    D = x_ref.shape[2]
    ya = jnp.dot(oa_ref[0], wa_ref[...], preferred_element_type=F32)
    yb = jnp.dot(ob_ref[0], wb_ref[...], preferred_element_type=F32)
    ga = gates_ref[0, :, 0:D].astype(F32)
    gb = gates_ref[0, :, D:2 * D].astype(F32)
    merged = _sigmoid(ga) * ya + _sigmoid(gb) * yb
    y = jnp.dot(merged.astype(BF16), wo_ref[...], preferred_element_type=F32)
    x1 = _ln(DN_ALPHA * x_ref[0] + gt1_ref[0] * y) * g1_ref[...] + b1_ref[...]
    x1_ref[0] = x1
    h2 = _ln(x1) * (1.0 + sc2_ref[0]) + sh2_ref[0]
    h2_ref[0] = h2.astype(BF16)

    lt = lax.dot_general(wr_ref[...], h2, NT, preferred_element_type=F32, precision=HIGHEST) + br_ref[...]
    gl = lt[0:N_GROUPS]
    gmax = jnp.max(gl, axis=0, keepdims=True)
    g_w = 1.0 / jnp.sum(jnp.exp(gl - gmax), axis=0, keepdims=True)
    r4 = lax.broadcasted_iota(I32, (N_GROUPS, tm), 0)
    g_idx = jnp.min(jnp.where(gl == gmax, r4, N_GROUPS), axis=0, keepdims=True)
    eg = jnp.zeros((EXPERTS_PER_GROUP, tm), F32)
    for g in range(N_GROUPS):
        lo = ROUTER_E0 + g * EXPERTS_PER_GROUP
        eg = jnp.where(g_idx == g, lt[lo:lo + EXPERTS_PER_GROUP], eg)
    r8 = lax.broadcasted_iota(I32, (EXPERTS_PER_GROUP, tm), 0)
    e1 = jnp.max(eg, axis=0, keepdims=True)
    i1 = jnp.min(jnp.where(eg == e1, r8, EXPERTS_PER_GROUP), axis=0, keepdims=True)
    eg2 = jnp.where(r8 == i1, -jnp.inf, eg)
    e2 = jnp.max(eg2, axis=0, keepdims=True)
    i2 = jnp.min(jnp.where(eg2 == e2, r8, EXPERTS_PER_GROUP), axis=0, keepdims=True)
    d = jnp.exp(e2 - e1)
    w1 = g_w / (1.0 + d)
    w2 = g_w * d / (1.0 + d)
    in_group = jnp.where(r8 == i1, w1, 0.0) + jnp.where(r8 == i2, w2, 0.0)
    blocks = [jnp.where(g_idx == g, in_group, 0.0) for g in range(N_GROUPS)]
    blocks.append(jnp.zeros((LANES - N_EXPERTS, tm), F32))
    gate_ref[...] = jnp.concatenate(blocks, axis=0).T


def _post_call(o_a, o_b, gates, x, gt1, sh2, sc2, wa, wb, wo, g1, b1, wr, br, tm):
    B, S, D = x.shape
    nt = S // tm
    const = lambda b, t: (0, 0)
    row = lambda b, t: (b, 0, 0)
    tile = lambda b, t: (b, t, 0)
    return pl.pallas_call(
        _post_body,
        grid=(B, nt),
        in_specs=[pl.BlockSpec((1, tm, A_WIDTH), tile),
                  pl.BlockSpec((1, tm, G_VW), tile),
                  pl.BlockSpec((1, tm, 2 * D), tile),
                  pl.BlockSpec((1, tm, D), tile),
                  pl.BlockSpec((1, 1, D), row),
                  pl.BlockSpec((1, 1, D), row),
                  pl.BlockSpec((1, 1, D), row),
                  pl.BlockSpec(wa.shape, const),
                  pl.BlockSpec(wb.shape, const),
                  pl.BlockSpec(wo.shape, const),
                  pl.BlockSpec((1, D), const),
                  pl.BlockSpec((1, D), const),
                  pl.BlockSpec(wr.shape, const),
                  pl.BlockSpec(br.shape, const)],
        out_specs=(pl.BlockSpec((1, tm, D), tile),
                   pl.BlockSpec((1, tm, D), tile),
                   pl.BlockSpec((tm, LANES), lambda b, t: (b * nt + t, 0))),
        out_shape=(jax.ShapeDtypeStruct((B, S, D), F32),
                   jax.ShapeDtypeStruct((B, S, D), BF16),
                   jax.ShapeDtypeStruct((B * S, LANES), F32)),
        compiler_params=_params("parallel", "parallel"),
        name="post",
    )(o_a, o_b, gates, x, gt1, sh2, sc2, wa, wb, wo, g1, b1, wr, br)


MOE_EXPERTS_PER_STEP = 4


def _moe_body(h2_ref, gate_ref, x1_ref, gt2_ref, w1_ref, w3_ref, w2_ref, g2_ref, b2_ref, o_ref, acc_s):
    c = pl.program_id(1)
    ne = MOE_EXPERTS_PER_STEP

    @pl.when(c == 0)
    def _():
        acc_s[...] = jnp.zeros_like(acc_s)

    h = h2_ref[...]
    gate = pltpu.roll(gate_ref[...], (LANES - c * ne) % LANES, axis=1)
    hid = []
    for j in range(ne):
        a = jnp.dot(h, w1_ref[j], preferred_element_type=F32)
        b = jnp.dot(h, w3_ref[j], preferred_element_type=F32)
        hid.append((a * _sigmoid(a) * b * gate[:, j:j + 1]).astype(BF16))
    acc_s[...] += jnp.dot(jnp.concatenate(hid, axis=1), w2_ref[...], preferred_element_type=F32)

    @pl.when(c == pl.num_programs(1) - 1)
    def _():
        z = DN_ALPHA * x1_ref[...] + gt2_ref[0] * acc_s[...]
        o_ref[...] = _ln(z) * g2_ref[...] + b2_ref[...]


def _moe_call(h2, gate, x1, gt2, w1, w3, w2, g2, b2, tm, S):
    T, D = h2.shape
    ne = MOE_EXPERTS_PER_STEP
    nc = N_EXPERTS // ne
    tiles_per_seq = S // tm
    tile = lambda t, c: (t, 0)
    const = lambda t, c: (0, 0)
    return pl.pallas_call(
        _moe_body,
        grid=(T // tm, nc),
        in_specs=[pl.BlockSpec((tm, D), tile),
                  pl.BlockSpec((tm, LANES), tile),
                  pl.BlockSpec((tm, D), tile),
                  pl.BlockSpec((1, 1, D), lambda t, c: (t // tiles_per_seq, 0, 0)),
                  pl.BlockSpec((ne, D, D_EXPERT), lambda t, c: (c, 0, 0)),
                  pl.BlockSpec((ne, D, D_EXPERT), lambda t, c: (c, 0, 0)),
                  pl.BlockSpec((ne * D_EXPERT, D), lambda t, c: (c, 0)),
                  pl.BlockSpec((1, D), const),
                  pl.BlockSpec((1, D), const)],
        out_specs=pl.BlockSpec((tm, D), tile),
        out_shape=jax.ShapeDtypeStruct((T, D), F32),
        scratch_shapes=[pltpu.VMEM((tm, D), F32)],
        compiler_params=_params("parallel", "arbitrary"),
        name="moe",
    )(h2, gate, x1, gt2, w1, w3, w2, g2, b2)


def _pick(n, pref):
    return pref if n % pref == 0 else n


def kernel(x, c, rel_bias, w_ada, b_ada, w_in, gla_w_gate, gla_b_gate, gla_norm_g, w_branch_a, w_branch_b, w_out, ln1_g, ln1_b, w_router_group, b_router_group, w_router_expert, b_router_expert, w_exp_gate, w_exp_up, w_exp_down, ln2_g, ln2_b):
    B, S, D = x.shape
    assert S % (2 * QBLK) == 0 and D == 1024 and w_ada.shape[0] == DEPTH == 1
    l = 0

    ada = _ada_call(c, w_ada[l], b_ada[l])
    sh1, sc1, gt1, sh2, sc2, gt2 = [ada[:, i * D:(i + 1) * D].reshape(B, 1, D) for i in range(6)]

    offs = np.concatenate([[0], np.cumsum(SPLIT_SIZES)])
    seg = lambda i: w_in[l][:, offs[i]:offs[i + 1]]
    (w_aq, w_ak, w_av, w_iq, w_ik, w_iw, w_gq, w_gk, w_gv, w_gr, w_glr, w_ga, w_gb) = [seg(i) for i in range(13)]
    pad = jnp.zeros((D, TOK_SMALL[1] - TOK_SMALL[0] - IDX_DIM - G_RANK), F32)
    w_tok = jnp.concatenate([w_ak, w_gq, w_gk, w_gv, w_gr, w_ga, w_gb, w_ik, w_glr, pad], axis=1).astype(BF16)
    w_ch = jnp.concatenate([w_aq, w_av, w_iq, w_iw], axis=1).T.astype(BF16)

    tm = _pick(S, 512)
    kk, gla, gates, ik, glr, qT, vT, iqT, iwT = _inproj_call(x, sh1, sc1, w_tok, w_ch, tm)

    o_a = _dsa_call(rel_bias, ik, kk, vT, qT, iqT, iwT)
    o_b = _gla_call(gla, glr, gla_w_gate[l], gla_b_gate[l].reshape(1, G_KW), gla_norm_g[l].reshape(1, G_VW),
                    _pick(S, 256))

    wr = jnp.zeros((ROUTER_ROWS, D), F32)
    wr = wr.at[0:N_GROUPS].set(w_router_group[l].T).at[ROUTER_E0:ROUTER_E0 + N_EXPERTS].set(w_router_expert[l].T)
    br = jnp.zeros((ROUTER_ROWS, 1), F32)
    br = br.at[0:N_GROUPS, 0].set(b_router_group[l]).at[ROUTER_E0:ROUTER_E0 + N_EXPERTS, 0].set(b_router_expert[l])
    x1, h2, gate = _post_call(o_a, o_b, gates, x, gt1, sh2, sc2,
                              w_branch_a[l].astype(BF16), w_branch_b[l].astype(BF16), w_out[l].astype(BF16),
                              ln1_g[l].reshape(1, D), ln1_b[l].reshape(1, D), wr, br, tm)

    tm5 = _pick(S, 1024)
    out = _moe_call(h2.reshape(B * S, D), gate, x1.reshape(B * S, D), gt2,
                    w_exp_gate[l].astype(BF16), w_exp_up[l].astype(BF16),
                    w_exp_down[l].astype(BF16).reshape(N_EXPERTS * D_EXPERT, D),
                    ln2_g[l].reshape(1, D), ln2_b[l].reshape(1, D), tm5, S)
    return out.reshape(B, S, D)
```

```python
import functools
import math

import numpy as np
import jax
import jax.numpy as jnp
from jax import lax
from jax.experimental import pallas as pl
from jax.experimental.pallas import tpu as pltpu

F32 = jnp.float32
BF16 = jnp.bfloat16
I32 = jnp.int32
HIGHEST = lax.Precision.HIGHEST

A_HEADS = 8
A_HEAD_DIM = 64
A_WIDTH = A_HEADS * A_HEAD_DIM
IDX_HEADS = 16
IDX_DIM = 32
TOPK_MAX = 256
QBLK = 128
REL_BUCKETS = 32
REL_MAX_DIST = 128
G_HEADS = 4
G_DK = 64
G_DV = 128
G_KW = G_HEADS * G_DK
G_VW = G_HEADS * G_DV
G_RANK = 16
G_TAU = 16.0
G_CHUNK = 64
N_GROUPS = 4
EXPERTS_PER_GROUP = 8
N_EXPERTS = N_GROUPS * EXPERTS_PER_GROUP
D_EXPERT = 256
DEPTH = 1
DN_ALPHA = (2.0 * DEPTH) ** 0.25
LN_EPS = 1e-5
SPLIT_SIZES = (A_WIDTH, A_WIDTH, A_WIDTH, IDX_HEADS * IDX_DIM, IDX_DIM, IDX_HEADS,
               G_KW, G_KW, G_VW, G_VW, G_RANK, 1024, 1024)

LANES = 128
VMEM_LIMIT_BYTES = 56 * 1024 * 1024

NEG = -1e30
LOG2E = math.log2(math.e)
INT_MIN = -2 ** 31
INT_MAX = 2 ** 31 - 1
KEY_NEG_INF = -2 ** 31 + 0x7FFFFF

NT = (((1,), (1,)), ((), ()))
TN = (((0,), (0,)), ((), ()))


def _ln(x):
    mu = jnp.mean(x, axis=-1, keepdims=True)
    xc = x - mu
    var = jnp.mean(xc * xc, axis=-1, keepdims=True)
    return xc * lax.rsqrt(var + LN_EPS)


def _sigmoid(x):
    return 1.0 / (1.0 + jnp.exp(-x))


def _params(*sem):
    return pltpu.CompilerParams(dimension_semantics=sem, vmem_limit_bytes=VMEM_LIMIT_BYTES)


def _ada_body(c_ref, w_ref, b_ref, o_ref):
    c = c_ref[...]
    cond = c * _sigmoid(c)
    o_ref[...] = jnp.dot(cond, w_ref[...], preferred_element_type=F32, precision=HIGHEST) + b_ref[...]


def _ada_call(c, w, b):
    B, D = c.shape
    N = w.shape[1]
    tn = 1536
    return pl.pallas_call(
        _ada_body,
        grid=(N // tn,),
        in_specs=[pl.BlockSpec((B, D), lambda j: (0, 0)),
                  pl.BlockSpec((D, tn), lambda j: (0, j)),
                  pl.BlockSpec((1, tn), lambda j: (0, j))],
        out_specs=pl.BlockSpec((B, tn), lambda j: (0, j)),
        out_shape=jax.ShapeDtypeStruct((B, N), F32),
        compiler_params=_params("arbitrary"),
        name="ada",
    )(c, w, b.reshape(1, N))


TOK_K = (0, 512)
TOK_GLA = (512, 2048)
TOK_GATES = (2048, 4096)
TOK_SMALL = (4096, 4224)
CH_Q = (0, 512)
CH_V = (512, 1024)
CH_IQ = (1024, 1536)
CH_IW = (1536, 1552)
IW_SCALE = IDX_HEADS ** -0.5 * IDX_DIM ** -0.5


def _inproj_body(x_ref, sh_ref, sc_ref, wtok_ref, wch_ref,
                 k_ref, gla_ref, gates_ref, ik_ref, glr_ref, qT_ref, vT_ref, iqT_ref, iwT_ref):
    tm = x_ref.shape[1]
    h = (_ln(x_ref[0]) * (1.0 + sc_ref[0]) + sh_ref[0]).astype(BF16)

    def tok(ab):
        return jnp.dot(h, wtok_ref[:, ab[0]:ab[1]], preferred_element_type=F32)

    def ch(ab):
        return lax.dot_general(wch_ref[ab[0]:ab[1], :], h, NT, preferred_element_type=F32)

    kres = tok(TOK_K)
    for p in range(A_WIDTH // LANES):
        k_ref[0, p] = kres[:, p * LANES:(p + 1) * LANES].astype(BF16)
    gla_ref[0] = tok(TOK_GLA).astype(BF16)
    gates_ref[0] = tok(TOK_GATES).astype(BF16)
    small = tok(TOK_SMALL)
    ik_ref[0] = small[:, :IDX_DIM].astype(BF16)
    glr_ref[0] = small[:, IDX_DIM:IDX_DIM + G_RANK]

    qT_ref[0] = (ch(CH_Q) * (A_HEAD_DIM ** -0.5 * LOG2E)).astype(BF16)
    vres = ch(CH_V).astype(BF16)
    for j in range(tm // LANES):
        vT_ref[0, j] = vres[:, j * LANES:(j + 1) * LANES]
    iqT_ref[0] = ch(CH_IQ).astype(BF16)
    iwT_ref[0] = ch(CH_IW) * IW_SCALE


def _inproj_call(x, sh1, sc1, w_tok, w_ch, tm):
    B, S, D = x.shape
    nt = S // tm
    const = lambda b, t: (0, 0)
    out_shape = (
        jax.ShapeDtypeStruct((B, A_WIDTH // LANES, S, LANES), BF16),
        jax.ShapeDtypeStruct((B, S, 1536), BF16),
        jax.ShapeDtypeStruct((B, S, 2048), BF16),
        jax.ShapeDtypeStruct((B, S, IDX_DIM), BF16),
        jax.ShapeDtypeStruct((B, S, G_RANK), F32),
        jax.ShapeDtypeStruct((B, A_WIDTH, S), BF16),
        jax.ShapeDtypeStruct((B, S // LANES, A_WIDTH, LANES), BF16),
        jax.ShapeDtypeStruct((B, IDX_HEADS * IDX_DIM, S), BF16),
        jax.ShapeDtypeStruct((B, IDX_HEADS, S), F32),
    )
    out_specs = (
        pl.BlockSpec((1, A_WIDTH // LANES, tm, LANES), lambda b, t: (b, 0, t, 0)),
        pl.BlockSpec((1, tm, 1536), lambda b, t: (b, t, 0)),
        pl.BlockSpec((1, tm, 2048), lambda b, t: (b, t, 0)),
        pl.BlockSpec((1, tm, IDX_DIM), lambda b, t: (b, t, 0)),
        pl.BlockSpec((1, tm, G_RANK), lambda b, t: (b, t, 0)),
        pl.BlockSpec((1, A_WIDTH, tm), lambda b, t: (b, 0, t)),
        pl.BlockSpec((1, tm // LANES, A_WIDTH, LANES), lambda b, t: (b, t, 0, 0)),
        pl.BlockSpec((1, IDX_HEADS * IDX_DIM, tm), lambda b, t: (b, 0, t)),
        pl.BlockSpec((1, IDX_HEADS, tm), lambda b, t: (b, 0, t)),
    )
    return pl.pallas_call(
        _inproj_body,
        grid=(B, nt),
        in_specs=[pl.BlockSpec((1, tm, D), lambda b, t: (b, t, 0)),
                  pl.BlockSpec((1, 1, D), lambda b, t: (b, 0, 0)),
                  pl.BlockSpec((1, 1, D), lambda b, t: (b, 0, 0)),
                  pl.BlockSpec(w_tok.shape, const),
                  pl.BlockSpec(w_ch.shape, const)],
        out_specs=out_specs,
        out_shape=out_shape,
        compiler_params=_params("parallel", "parallel"),
        name="inproj",
    )(x, sh1, sc1, w_tok, w_ch)


IDX_CHUNK = 256
CNT_CHUNK = 512
SORT_GROUP = 4
ATT_CHUNK = 256
TBL_PAD = 2 * QBLK
ACC_ROWS = A_HEAD_DIM + 16


def _rel_bucket_table():
    s = np.arange(2 * QBLK)[:, None]
    t = np.arange(QBLK)[None, :]
    dist = np.maximum(t + QBLK - s, 0)
    max_exact = REL_BUCKETS // 2
    d_f = np.maximum(dist, 1).astype(np.float32)
    large = max_exact + (np.log(d_f / max_exact) / math.log(REL_MAX_DIST / max_exact)
                         * (REL_BUCKETS - max_exact)).astype(np.int32)
    large = np.minimum(large, REL_BUCKETS - 1)
    return np.where(dist < max_exact, dist, large).astype(np.int32)


def _far_bucket():
    max_exact = REL_BUCKETS // 2
    v = max_exact + int(np.float32(np.log(np.float32(QBLK + 1) / max_exact) / math.log(REL_MAX_DIST / max_exact)
                                   * (REL_BUCKETS - max_exact)))
    assert min(v, REL_BUCKETS - 1) == REL_BUCKETS - 1
    return REL_BUCKETS - 1


def _dsa_body(rb_ref, bkt_ref, ik_ref, kk_ref, vT_ref, qT_ref, iqT_ref, iwT_ref, o_ref,
              sc_s, srt_s, thr_s, madd_s, tbl_s, oT_s, xcut_s, qm_s, sa_s, sb_s, acc_s, mall_s, mblk_s,
              *, topk, idx_bits, max_cnt):
    i = pl.program_id(1)
    nck = (i + 2) // 2
    t_idx = i * QBLK + lax.broadcasted_iota(I32, (1, QBLK), 1)

    @pl.when(i == 0)
    def _():
        bkt = bkt_ref[...]
        tbl_s[...] = jnp.zeros_like(tbl_s)
        for h in range(A_HEADS):
            t = jnp.zeros((2 * QBLK, QBLK), F32)
            for k in range(REL_BUCKETS):
                t = jnp.where(bkt == k, rb_ref[k, h], t)
            tbl_s[h, TBL_PAD:TBL_PAD + 2 * QBLK, :] = (t - rb_ref[_far_bucket(), h]) * LOG2E

    def key_to_float(key):
        key = jnp.maximum(key, KEY_NEG_INF)
        return pltpu.bitcast(jnp.where(key < 0, key ^ INT_MAX, key), F32)

    def score_chunk(c, carry):
        s0 = pl.multiple_of(c * IDX_CHUNK, IDX_CHUNK)
        kc = ik_ref[0, pl.ds(s0, IDX_CHUNK), :]
        acc = jnp.zeros((IDX_CHUNK, QBLK), F32)
        for hp in range(IDX_HEADS // 2):
            r0 = hp * 2 * IDX_DIM
            rhs = jnp.concatenate([iqT_ref[0, r0:r0 + IDX_DIM, :],
                                   iqT_ref[0, r0 + IDX_DIM:r0 + 2 * IDX_DIM, :]], axis=1)
            z = jnp.dot(kc, rhs, preferred_element_type=F32)
            acc = acc + jnp.maximum(z[:, :QBLK], 0.0) * iwT_ref[0, 2 * hp:2 * hp + 1, :]
            acc = acc + jnp.maximum(z[:, QBLK:], 0.0) * iwT_ref[0, 2 * hp + 1:2 * hp + 2, :]
        s_idx = s0 + lax.broadcasted_iota(I32, (IDX_CHUNK, QBLK), 0)
        sc_s[pl.ds(s0, IDX_CHUNK), :] = jnp.where(s_idx <= t_idx, acc, -jnp.inf)
        return carry

    ncnt = (nck + 1) // 2
    lax.fori_loop(0, ncnt, lambda c, carry: score_chunk(2 * c + 1, score_chunk(2 * c, carry)), 0)

    def count(pred):
        def body(c, cnt):
            s0 = pl.multiple_of(c * CNT_CHUNK, CNT_CHUNK)
            k = sc_s[pl.ds(s0, CNT_CHUNK), :]
            s_idx = s0 + lax.broadcasted_iota(I32, (CNT_CHUNK, QBLK), 0)
            m = jnp.where(pred(k, s_idx), 1, 0)
            return cnt + jnp.sum(m.reshape(CNT_CHUNK // 8, 8, QBLK), axis=0)
        cnt = lax.fori_loop(0, ncnt, body, jnp.zeros((8, QBLK), I32))
        return jnp.sum(cnt, axis=0, keepdims=True)

    def search_block(n):
        groups = n * CNT_CHUNK // (8 * SORT_GROUP)

        for g in range(groups):
            v = [sc_s[(SORT_GROUP * g + u) * 8:(SORT_GROUP * g + u + 1) * 8, :] for u in range(SORT_GROUP)]
            for a, b in ((0, 1), (2, 3), (0, 2), (1, 3), (1, 2)):
                v[a], v[b] = jnp.maximum(v[a], v[b]), jnp.minimum(v[a], v[b])
            for u in range(SORT_GROUP):
                srt_s[(SORT_GROUP * g + u) * 8:(SORT_GROUP * g + u + 1) * 8, :] = v[u]

        def count_ge(cand_key):
            cand = key_to_float(cand_key)
            parts = []
            for g in range(groups):
                cnt = 0
                for u in range(SORT_GROUP):
                    tile = srt_s[(SORT_GROUP * g + u) * 8:(SORT_GROUP * g + u + 1) * 8, :]
                    cnt = jnp.where(tile >= cand, u + 1, cnt)
                parts.append(cnt)
            while len(parts) > 1:
                odd = parts[len(parts) & ~1:]
                parts = [parts[j] + parts[j + 1] for j in range(0, len(parts) - 1, 2)] + odd
            return jnp.sum(parts[0], axis=0, keepdims=True)

        c0 = count_ge(jnp.zeros((1, QBLK), I32))
        ok = c0 >= topk
        T = jnp.where(ok, 0, INT_MIN).astype(I32)
        cnt_T = jnp.where(ok, c0, n * CNT_CHUNK)

        def bit_body(j, carry):
            T, cnt_T = carry
            cand = T | jnp.left_shift(jnp.int32(1), 30 - j)
            c = count_ge(cand)
            ok = c >= topk
            return jnp.where(ok, cand, T), jnp.where(ok, c, cnt_T)

        T, cnt_T = lax.fori_loop(0, 31, bit_body, (T, cnt_T))
        thr_s[0:1, :] = T
        thr_s[1:2, :] = cnt_T
        T = jnp.maximum(T, KEY_NEG_INF)
        thr_s[2:3, :] = count_ge(jnp.where(T == INT_MAX, T, T + 1))

    for n in range(1, max_cnt + 1):
        pl.when(ncnt == n)(functools.partial(search_block, n))
    T_key = jnp.maximum(thr_s[0:1, :], KEY_NEG_INF)
    T = key_to_float(T_key)
    cnt_ge = thr_s[1:2, :]
    cnt_gt = thr_s[2:3, :]

    need = topk - cnt_gt
    excess = jnp.where((cnt_ge - cnt_gt > need) & (T_key > KEY_NEG_INF), 1.0, 0.0)
    xcut_s[...] = jnp.full((1, QBLK), INT_MAX, I32)

    @pl.when(jnp.max(excess) > 0.0)
    def _():
        X = jnp.zeros((1, QBLK), I32)
        for b in range(idx_bits - 1, -1, -1):
            cand = X | (1 << b)
            f = count(lambda k, s: (k == T) & (s < cand))
            X = jnp.where(f < need, cand, X)
        xcut_s[...] = X

    xcut = xcut_s[...]

    def mask_chunk(c, carry):
        s0 = pl.multiple_of(c * IDX_CHUNK, IDX_CHUNK)
        k = sc_s[pl.ds(s0, IDX_CHUNK), :]
        s_idx = s0 + lax.broadcasted_iota(I32, (IDX_CHUNK, QBLK), 0)
        sel = ((k > T) | ((k == T) & (s_idx <= xcut))) & (s_idx <= t_idx)
        madd_s[pl.ds(s0, IDX_CHUNK), :] = jnp.where(sel, 0.0, NEG)
        return carry

    lax.fori_loop(0, nck, mask_chunk, 0)

    c_last = i // 2
    even = 1 - (i - 2 * c_last)
    row_head = lax.broadcasted_iota(I32, (LANES, QBLK), 0) // A_HEAD_DIM
    for h in range(A_HEADS):
        qp = qT_ref[0, (h // 2) * LANES:(h // 2 + 1) * LANES, :]
        qm_s[h] = jnp.where(row_head == h % 2, qp, jnp.zeros_like(qp))
    acc_s[...] = jnp.zeros_like(acc_s)
    ones = jnp.ones((ACC_ROWS - A_HEAD_DIM, ATT_CHUNK), BF16)

    def logits(c, s_buf):
        s0 = pl.multiple_of(c * ATT_CHUNK, ATT_CHUNK)
        madd = madd_s[pl.ds(s0, ATT_CHUNK), :]
        off = jnp.where(c == c_last, 2 * QBLK + QBLK * even, jnp.where(c == c_last - 1, QBLK * even, 0))
        off = pl.multiple_of(off, QBLK)
        m_blk = []
        for h in range(A_HEADS):
            kc = kk_ref[0, h // 2, pl.ds(s0, ATT_CHUNK), :]
            s = jnp.dot(kc, qm_s[h], preferred_element_type=F32) + madd + tbl_s[h, pl.ds(off, ATT_CHUNK), :]
            s_buf[h] = s
            m_blk.append(jnp.max(s, axis=0, keepdims=True))
        return jnp.concatenate(m_blk, axis=0)

    def accumulate(c, s_buf, m_all, m_blk):
        m_new = jnp.maximum(m_all, m_blk)
        alpha = jnp.exp2(m_all - m_new)
        for h in range(A_HEADS):
            rows = slice(h * A_HEAD_DIM, (h + 1) * A_HEAD_DIM)
            p = jnp.exp2(s_buf[h] - m_new[h:h + 1]).astype(BF16)
            vt = jnp.concatenate([vT_ref[0, 2 * c + u, rows, :] for u in range(ATT_CHUNK // QBLK)], axis=1)
            vt = jnp.concatenate([vt, ones], axis=0)
            acc_s[h] = alpha[h:h + 1] * acc_s[h] + jnp.dot(vt, p, preferred_element_type=F32)
        return m_new

    def att_body(pair, carry):
        m_all, m_blk = carry
        c = 2 * pair
        m_b = logits(c + 1, sb_s)
        m_all = accumulate(c, sa_s, m_all, m_blk)
        m_a = logits(c + 2, sa_s)
        return accumulate(c + 1, sb_s, m_all, m_b), m_a

    n_pairs = c_last // 2
    carry = (jnp.full((A_HEADS, QBLK), NEG, F32), logits(0, sa_s))
    m_all, m_blk = lax.fori_loop(0, n_pairs, att_body, carry)
    mall_s[...] = m_all
    mblk_s[...] = m_blk

    @pl.when(c_last % 2 == 1)
    def _():
        m_b = logits(c_last, sb_s)
        m_all = accumulate(c_last - 1, sa_s, mall_s[...], mblk_s[...])
        accumulate(c_last, sb_s, m_all, m_b)

    @pl.when(c_last % 2 == 0)
    def _():
        accumulate(c_last, sa_s, mall_s[...], mblk_s[...])

    for h in range(A_HEADS):
        rows = slice(h * A_HEAD_DIM, (h + 1) * A_HEAD_DIM)
        oT_s[rows, :] = acc_s[h, 0:A_HEAD_DIM, :] / acc_s[h, A_HEAD_DIM:A_HEAD_DIM + 1, :]
    o_ref[0] = oT_s[...].T.astype(BF16)


def _dsa_call(rel_bias, ik, kk, vT, qT, iqT, iwT):
    B, S, _ = ik.shape
    nb = S // QBLK
    topk = min(TOPK_MAX, S // 4)
    bkt = jnp.asarray(_rel_bucket_table())
    body = functools.partial(_dsa_body, topk=topk, idx_bits=int(math.log2(S)), max_cnt=S // CNT_CHUNK)
    return pl.pallas_call(
        body,
        grid=(B, nb),
        in_specs=[pl.BlockSpec(memory_space=pltpu.SMEM),
                  pl.BlockSpec((2 * QBLK, QBLK), lambda b, i: (0, 0)),
                  pl.BlockSpec((1, S, IDX_DIM), lambda b, i: (b, 0, 0)),
                  pl.BlockSpec((1, A_WIDTH // LANES, S, LANES), lambda b, i: (b, 0, 0, 0)),
                  pl.BlockSpec((1, S // LANES, A_WIDTH, LANES), lambda b, i: (b, 0, 0, 0)),
                  pl.BlockSpec((1, A_WIDTH, QBLK), lambda b, i: (b, 0, i)),
                  pl.BlockSpec((1, IDX_HEADS * IDX_DIM, QBLK), lambda b, i: (b, 0, i)),
                  pl.BlockSpec((1, IDX_HEADS, QBLK), lambda b, i: (b, 0, i))],
        out_specs=pl.BlockSpec((1, QBLK, A_WIDTH), lambda b, i: (b, i, 0)),
        out_shape=jax.ShapeDtypeStruct((B, S, A_WIDTH), BF16),
        scratch_shapes=[pltpu.VMEM((S, QBLK), F32),
                        pltpu.VMEM((S, QBLK), F32),
                        pltpu.VMEM((8, QBLK), I32),
                        pltpu.VMEM((S, QBLK), F32),
                        pltpu.VMEM((A_HEADS, TBL_PAD + 3 * QBLK, QBLK), F32),
                        pltpu.VMEM((A_WIDTH, QBLK), F32),
                        pltpu.VMEM((1, QBLK), I32),
                        pltpu.VMEM((A_HEADS, LANES, QBLK), BF16),
                        pltpu.VMEM((A_HEADS, ATT_CHUNK, QBLK), F32),
                        pltpu.VMEM((A_HEADS, ATT_CHUNK, QBLK), F32),
                        pltpu.VMEM((A_HEADS, ACC_ROWS, QBLK), F32),
                        pltpu.VMEM((A_HEADS, QBLK), F32),
                        pltpu.VMEM((A_HEADS, QBLK), F32)],
        compiler_params=_params("parallel", "arbitrary"),
        name="dsa",
    )(rel_bias, bkt, ik, kk, vT, qT, iqT, iwT)


GLA_Q = (0, 256)
GLA_K = (256, 512)
GLA_V = (512, 1024)
GLA_R = (1024, 1536)


def _gla_body(gla_ref, glr_ref, wg_ref, bg_ref, ng_ref, o_ref, st_s):
    tg = gla_ref.shape[1]
    C = G_CHUNK

    @pl.when(pl.program_id(1) == 0)
    def _():
        st_s[...] = jnp.zeros_like(st_s)

    xg = jnp.dot(glr_ref[0], wg_ref[...], preferred_element_type=F32, precision=HIGHEST) + bg_ref[...]
    logg = -(jnp.maximum(-xg, 0.0) + jnp.log1p(jnp.exp(-jnp.abs(xg)))) * (1.0 / G_TAU)

    rt = lax.broadcasted_iota(I32, (tg, tg), 0)
    ct = lax.broadcasted_iota(I32, (tg, tg), 1)
    cum_f = jnp.where((rt // C == ct // C) & (rt >= ct), 1.0, 0.0).astype(F32)
    bc_all = jnp.dot(cum_f, logg, preferred_element_type=F32, precision=HIGHEST)

    ri = lax.broadcasted_iota(I32, (C, C), 0)
    ci = lax.broadcasted_iota(I32, (C, C), 1)
    tril = ri >= ci
    lane_head = lax.broadcasted_iota(I32, (C, LANES), 1) // G_DK
    st_rows = lax.broadcasted_iota(I32, (2 * G_DV, LANES), 0) // G_DV
    st_cols = lax.broadcasted_iota(I32, (2 * G_DV, LANES), 1) // G_DK
    st_diag = st_rows == st_cols

    for ck in range(tg // C):
        r0 = ck * C
        for p in range(G_HEADS // 2):
            bc = bc_all[r0:r0 + C, p * LANES:(p + 1) * LANES]
            bl = bc[C - 1:C, :]
            q = gla_ref[0, r0:r0 + C, GLA_Q[0] + p * LANES:GLA_Q[0] + (p + 1) * LANES].astype(F32) * (G_DK ** -0.5)
            k = gla_ref[0, r0:r0 + C, GLA_K[0] + p * LANES:GLA_K[0] + (p + 1) * LANES].astype(F32)
            v = gla_ref[0, r0:r0 + C, GLA_V[0] + p * 2 * G_DV:GLA_V[0] + (p + 1) * 2 * G_DV]
            q_in = (q * jnp.exp(bc)).astype(BF16)
            k_st = (k * jnp.exp(bl - bc)).astype(BF16)
            q_rel = q * jnp.exp(bc - bl)
            o_intra = []
            for sub in range(2):
                qm = jnp.where(lane_head == sub, q_rel, 0.0).astype(BF16)
                att = lax.dot_general(qm, k_st, NT, preferred_element_type=F32)
                att = jnp.where(tril, att, 0.0).astype(BF16)
                o_intra.append(jnp.dot(att, v[:, sub * G_DV:(sub + 1) * G_DV], preferred_element_type=F32))
            st = st_s[p]
            o_inter = lax.dot_general(q_in, st.astype(BF16), NT, preferred_element_type=F32)
            uT = lax.dot_general(v, k_st, TN, preferred_element_type=F32)
            st_s[p] = st * jnp.exp(bl) + jnp.where(st_diag, uT, 0.0)
            for sub in range(2):
                hd = 2 * p + sub
                o = o_intra[sub] + o_inter[:, sub * G_DV:(sub + 1) * G_DV]
                y = _ln(o) * ng_ref[:, hd * G_DV:(hd + 1) * G_DV]
                g = gla_ref[0, r0:r0 + C, GLA_R[0] + hd * G_DV:GLA_R[0] + (hd + 1) * G_DV].astype(F32)
                o_ref[0, r0:r0 + C, hd * G_DV:(hd + 1) * G_DV] = (y * (g * _sigmoid(g))).astype(BF16)


def _gla_call(gla, glr, wg, bg, ng, tg):
    B, S, _ = gla.shape
    const = lambda b, j: (0, 0)
    return pl.pallas_call(
        _gla_body,
        grid=(B, S // tg),
        in_specs=[pl.BlockSpec((1, tg, 1536), lambda b, j: (b, j, 0)),
                  pl.BlockSpec((1, tg, G_RANK), lambda b, j: (b, j, 0)),
                  pl.BlockSpec((G_RANK, G_KW), const),
                  pl.BlockSpec((1, G_KW), const),
                  pl.BlockSpec((1, G_VW), const)],
        out_specs=pl.BlockSpec((1, tg, G_VW), lambda b, j: (b, j, 0)),
        out_shape=jax.ShapeDtypeStruct((B, S, G_VW), BF16),
        scratch_shapes=[pltpu.VMEM((G_HEADS // 2, 2 * G_DV, LANES), F32)],
        compiler_params=_params("parallel", "arbitrary"),
        name="gla",
    )(gla, glr, wg, bg, ng)


ROUTER_ROWS = 40
ROUTER_E0 = 8


def _post_body(oa_ref, ob_ref, gates_ref, x_ref, gt1_ref, sh2_ref, sc2_ref, wa_ref, wb_ref, wo_ref,
               g1_ref, b1_ref, wr_ref, br_ref, x1_ref, h2_ref, gate_ref, grp_ref):
    tm = x_ref.shape[1]
    D = x_ref.shape[2]
    ya = jnp.dot(oa_ref[0], wa_ref[...], preferred_element_type=F32)
    yb = jnp.dot(ob_ref[0], wb_ref[...], preferred_element_type=F32)
    ga = gates_ref[0, :, 0:D].astype(F32)
    gb = gates_ref[0, :, D:2 * D].astype(F32)
    merged = _sigmoid(ga) * ya + _sigmoid(gb) * yb
    y = jnp.dot(merged.astype(BF16), wo_ref[...], preferred_element_type=F32)
    x1 = _ln(DN_ALPHA * x_ref[0] + gt1_ref[0] * y) * g1_ref[...] + b1_ref[...]
    x1_ref[0] = x1
    h2 = _ln(x1) * (1.0 + sc2_ref[0]) + sh2_ref[0]
    h2_ref[0] = h2.astype(BF16)

    lt = lax.dot_general(wr_ref[...], h2, NT, preferred_element_type=F32, precision=HIGHEST) + br_ref[...]
    gl = lt[0:N_GROUPS]
    gmax = jnp.max(gl, axis=0, keepdims=True)
    g_w = 1.0 / jnp.sum(jnp.exp(gl - gmax), axis=0, keepdims=True)
    r4 = lax.broadcasted_iota(I32, (N_GROUPS, tm), 0)
    g_idx = jnp.min(jnp.where(gl == gmax, r4, N_GROUPS), axis=0, keepdims=True)
    eg = jnp.zeros((EXPERTS_PER_GROUP, tm), F32)
    for g in range(N_GROUPS):
        lo = ROUTER_E0 + g * EXPERTS_PER_GROUP
        eg = jnp.where(g_idx == g, lt[lo:lo + EXPERTS_PER_GROUP], eg)
    r8 = lax.broadcasted_iota(I32, (EXPERTS_PER_GROUP, tm), 0)
    e1 = jnp.max(eg, axis=0, keepdims=True)
    i1 = jnp.min(jnp.where(eg == e1, r8, EXPERTS_PER_GROUP), axis=0, keepdims=True)
    eg2 = jnp.where(r8 == i1, -jnp.inf, eg)
    e2 = jnp.max(eg2, axis=0, keepdims=True)
    i2 = jnp.min(jnp.where(eg2 == e2, r8, EXPERTS_PER_GROUP), axis=0, keepdims=True)
    d = jnp.exp(e2 - e1)
    w1 = g_w / (1.0 + d)
    w2 = g_w * d / (1.0 + d)
    in_group = jnp.where(r8 == i1, w1, 0.0) + jnp.where(r8 == i2, w2, 0.0)
    blocks = [jnp.where(g_idx == g, in_group, 0.0) for g in range(N_GROUPS)]
    blocks.append(jnp.zeros((LANES - N_EXPERTS, tm), F32))
    gate_ref[...] = jnp.concatenate(blocks, axis=0).T
    r8g = lax.broadcasted_iota(I32, (8, tm), 0)
    grp_ref[...] = jnp.where(r8g == g_idx, 1.0, 0.0)


def _post_call(o_a, o_b, gates, x, gt1, sh2, sc2, wa, wb, wo, g1, b1, wr, br, tm):
    B, S, D = x.shape
    nt = S // tm
    const = lambda b, t: (0, 0)
    row = lambda b, t: (b, 0, 0)
    tile = lambda b, t: (b, t, 0)
    return pl.pallas_call(
        _post_body,
        grid=(B, nt),
        in_specs=[pl.BlockSpec((1, tm, A_WIDTH), tile),
                  pl.BlockSpec((1, tm, G_VW), tile),
                  pl.BlockSpec((1, tm, 2 * D), tile),
                  pl.BlockSpec((1, tm, D), tile),
                  pl.BlockSpec((1, 1, D), row),
                  pl.BlockSpec((1, 1, D), row),
                  pl.BlockSpec((1, 1, D), row),
                  pl.BlockSpec(wa.shape, const),
                  pl.BlockSpec(wb.shape, const),
                  pl.BlockSpec(wo.shape, const),
                  pl.BlockSpec((1, D), const),
                  pl.BlockSpec((1, D), const),
                  pl.BlockSpec(wr.shape, const),
                  pl.BlockSpec(br.shape, const)],
        out_specs=(pl.BlockSpec((1, tm, D), tile),
                   pl.BlockSpec((1, tm, D), tile),
                   pl.BlockSpec((tm, LANES), lambda b, t: (b * nt + t, 0)),
                   pl.BlockSpec((8, tm), lambda b, t: (0, b * nt + t))),
        out_shape=(jax.ShapeDtypeStruct((B, S, D), F32),
                   jax.ShapeDtypeStruct((B, S, D), BF16),
                   jax.ShapeDtypeStruct((B * S, LANES), F32),
                   jax.ShapeDtypeStruct((8, B * S), F32)),
        compiler_params=_params("parallel", "parallel"),
        name="post",
    )(o_a, o_b, gates, x, gt1, sh2, sc2, wa, wb, wo, g1, b1, wr, br)


MOE_EXPERTS_PER_STEP = 4
MOE_ROW_BLOCK = 128
MOE_PERM_ROWS = 256


def _moe_body(h2_ref, gate_ref, grp_ref, x1_ref, gt2_ref, w1_ref, w3_ref, w2_ref, g2_ref, b2_ref, o_ref,
              perm_s, hs_s, gs_s, ys_s, tri_s, seg_s):
    t = pl.program_id(0)
    s = pl.program_id(1)
    ne = MOE_EXPERTS_PER_STEP
    rb = MOE_ROW_BLOCK
    tm = h2_ref.shape[0]
    tm_pad = perm_s.shape[0]
    steps_per_group = EXPERTS_PER_GROUP // ne

    @pl.when((t == 0) & (s == 0))
    def _():
        r = lax.broadcasted_iota(I32, (tm, tm), 0)
        c = lax.broadcasted_iota(I32, (tm, tm), 1)
        tri_s[...] = jnp.where(r < c, 1.0, 0.0).astype(BF16)

    @pl.when(s == 0)
    def _():
        oh = grp_ref[...]
        rank = jnp.dot(oh.astype(BF16), tri_s[...], preferred_element_type=F32)
        cnt = jnp.sum(oh, axis=1, keepdims=True)
        blocks = jnp.floor((cnt + (rb - 1)) * (1.0 / rb))
        off = jnp.zeros((1, 1), F32)
        dest = jnp.zeros((1, tm), F32)
        for g in range(N_GROUPS):
            seg_s[2 * g] = jnp.sum(off).astype(I32)
            seg_s[2 * g + 1] = jnp.sum(blocks[g:g + 1, :]).astype(I32)
            dest = dest + oh[g:g + 1, :] * (off + rank[g:g + 1, :])
            off = off + blocks[g:g + 1, :] * rb
        dest_i = dest.astype(I32)
        gate = gate_ref[...]
        g_hi = gate.astype(BF16)
        g_lo = (gate - g_hi.astype(F32)).astype(BF16)
        h2 = h2_ref[...]
        for c in range(tm_pad // MOE_PERM_ROWS):
            rows = slice(c * MOE_PERM_ROWS, (c + 1) * MOE_PERM_ROWS)
            d_idx = c * MOE_PERM_ROWS + lax.broadcasted_iota(I32, (MOE_PERM_ROWS, tm), 0)
            perm = jnp.where(d_idx == dest_i, 1.0, 0.0).astype(BF16)
            perm_s[rows, :] = perm
            hs_s[rows, :] = jnp.dot(perm, h2, preferred_element_type=F32).astype(BF16)
            gs_s[rows, :] = (jnp.dot(perm, g_hi, preferred_element_type=F32)
                             + jnp.dot(perm, g_lo, preferred_element_type=F32))
        ys_s[...] = jnp.zeros_like(ys_s)

    g = s // steps_per_group
    row0 = seg_s[2 * g]
    nblk = seg_s[2 * g + 1]

    def block(i, carry):
        r0 = pl.multiple_of(row0 + i * rb, rb)
        x = hs_s[pl.ds(r0, rb), :]
        gsel = pltpu.roll(gs_s[pl.ds(r0, rb), :], (LANES - s * ne) % LANES, axis=1)
        hid = []
        for j in range(ne):
            a = jnp.dot(x, w1_ref[j], preferred_element_type=F32)
            b = jnp.dot(x, w3_ref[j], preferred_element_type=F32)
            hid.append((a * _sigmoid(a) * b * gsel[:, j:j + 1]).astype(BF16))
        ys_s[pl.ds(r0, rb), :] += jnp.dot(jnp.concatenate(hid, axis=1), w2_ref[...], preferred_element_type=F32)
        return carry

    lax.fori_loop(0, nblk, block, 0)

    @pl.when(s == pl.num_programs(1) - 1)
    def _():
        perm = perm_s[...]
        for c in range(o_ref.shape[1] // MOE_PERM_ROWS):
            cols = slice(c * MOE_PERM_ROWS, (c + 1) * MOE_PERM_ROWS)
            ys = ys_s[:, cols]
            y_hi = ys.astype(BF16)
            y_lo = (ys - y_hi.astype(F32)).astype(BF16)
            o_ref[:, cols] = (lax.dot_general(perm, y_hi, TN, preferred_element_type=F32)
                              + lax.dot_general(perm, y_lo, TN, preferred_element_type=F32))
        z = DN_ALPHA * x1_ref[...] + gt2_ref[0] * o_ref[...]
        o_ref[...] = _ln(z) * g2_ref[...] + b2_ref[...]


def _moe_call(h2, gate, grp, x1, gt2, w1, w3, w2, g2, b2, tm, S):
    T, D = h2.shape
    ne = MOE_EXPERTS_PER_STEP
    nc = N_EXPERTS // ne
    tm_pad = tm + N_GROUPS * MOE_ROW_BLOCK
    tiles_per_seq = S // tm
    tile = lambda t, c: (t, 0)
    const = lambda t, c: (0, 0)
    return pl.pallas_call(
        _moe_body,
        grid=(T // tm, nc),
        in_specs=[pl.BlockSpec((tm, D), tile),
                  pl.BlockSpec((tm, LANES), tile),
                  pl.BlockSpec((8, tm), lambda t, c: (0, t)),
                  pl.BlockSpec((tm, D), tile),
                  pl.BlockSpec((1, 1, D), lambda t, c: (t // tiles_per_seq, 0, 0)),
                  pl.BlockSpec((ne, D, D_EXPERT), lambda t, c: (c, 0, 0)),
                  pl.BlockSpec((ne, D, D_EXPERT), lambda t, c: (c, 0, 0)),
                  pl.BlockSpec((ne * D_EXPERT, D), lambda t, c: (c, 0)),
                  pl.BlockSpec((1, D), const),
                  pl.BlockSpec((1, D), const)],
        out_specs=pl.BlockSpec((tm, D), tile),
        out_shape=jax.ShapeDtypeStruct((T, D), F32),
        scratch_shapes=[pltpu.VMEM((tm_pad, tm), BF16),
                        pltpu.VMEM((tm_pad, D), BF16),
                        pltpu.VMEM((tm_pad, LANES), F32),
                        pltpu.VMEM((tm_pad, D), F32),
                        pltpu.VMEM((tm, tm), BF16),
                        pltpu.SMEM((2 * N_GROUPS,), I32)],
        compiler_params=_params("arbitrary", "arbitrary"),
        name="moe",
    )(h2, gate, grp, x1, gt2, w1, w3, w2, g2, b2)


def _pick(n, pref):
    return pref if n % pref == 0 else n


def kernel(x, c, rel_bias, w_ada, b_ada, w_in, gla_w_gate, gla_b_gate, gla_norm_g, w_branch_a, w_branch_b, w_out, ln1_g, ln1_b, w_router_group, b_router_group, w_router_expert, b_router_expert, w_exp_gate, w_exp_up, w_exp_down, ln2_g, ln2_b):
    B, S, D = x.shape
    assert S % (2 * QBLK) == 0 and D == 1024 and w_ada.shape[0] == DEPTH == 1
    l = 0

    ada = _ada_call(c, w_ada[l], b_ada[l])
    sh1, sc1, gt1, sh2, sc2, gt2 = [ada[:, i * D:(i + 1) * D].reshape(B, 1, D) for i in range(6)]

    offs = np.concatenate([[0], np.cumsum(SPLIT_SIZES)])
    seg = lambda i: w_in[l][:, offs[i]:offs[i + 1]]
    (w_aq, w_ak, w_av, w_iq, w_ik, w_iw, w_gq, w_gk, w_gv, w_gr, w_glr, w_ga, w_gb) = [seg(i) for i in range(13)]
    pad = jnp.zeros((D, TOK_SMALL[1] - TOK_SMALL[0] - IDX_DIM - G_RANK), F32)
    w_tok = jnp.concatenate([w_ak, w_gq, w_gk, w_gv, w_gr, w_ga, w_gb, w_ik, w_glr, pad], axis=1).astype(BF16)
    w_ch = jnp.concatenate([w_aq, w_av, w_iq, w_iw], axis=1).T.astype(BF16)

    tm = _pick(S, 512)
    kk, gla, gates, ik, glr, qT, vT, iqT, iwT = _inproj_call(x, sh1, sc1, w_tok, w_ch, tm)

    o_a = _dsa_call(rel_bias, ik, kk, vT, qT, iqT, iwT)
    o_b = _gla_call(gla, glr, gla_w_gate[l], gla_b_gate[l].reshape(1, G_KW), gla_norm_g[l].reshape(1, G_VW),
                    _pick(S, 256))

    wr = jnp.zeros((ROUTER_ROWS, D), F32)
    wr = wr.at[0:N_GROUPS].set(w_router_group[l].T).at[ROUTER_E0:ROUTER_E0 + N_EXPERTS].set(w_router_expert[l].T)
    br = jnp.zeros((ROUTER_ROWS, 1), F32)
    br = br.at[0:N_GROUPS, 0].set(b_router_group[l]).at[ROUTER_E0:ROUTER_E0 + N_EXPERTS, 0].set(b_router_expert[l])
    x1, h2, gate, grp = _post_call(o_a, o_b, gates, x, gt1, sh2, sc2,
                              w_branch_a[l].astype(BF16), w_branch_b[l].astype(BF16), w_out[l].astype(BF16),
                              ln1_g[l].reshape(1, D), ln1_b[l].reshape(1, D), wr, br, tm)

    tm5 = _pick(S, 1024)
    out = _moe_call(h2.reshape(B * S, D), gate, grp, x1.reshape(B * S, D), gt2,
                    w_exp_gate[l].astype(BF16), w_exp_up[l].astype(BF16),
                    w_exp_down[l].astype(BF16).reshape(N_EXPERTS * D_EXPERT, D),
                    ln2_g[l].reshape(1, D), ln2_b[l].reshape(1, D), tm5, S)
    return out.reshape(B, S, D)
```

```python
import functools
import math

import numpy as np
import jax
import jax.numpy as jnp
from jax import lax
from jax.experimental import pallas as pl
from jax.experimental.pallas import tpu as pltpu

F32 = jnp.float32
BF16 = jnp.bfloat16
I32 = jnp.int32
HIGHEST = lax.Precision.HIGHEST

A_HEADS = 8
A_HEAD_DIM = 64
A_WIDTH = A_HEADS * A_HEAD_DIM
IDX_HEADS = 16
IDX_DIM = 32
TOPK_MAX = 256
QBLK = 128
REL_BUCKETS = 32
REL_MAX_DIST = 128
G_HEADS = 4
G_DK = 64
G_DV = 128
G_KW = G_HEADS * G_DK
G_VW = G_HEADS * G_DV
G_RANK = 16
G_TAU = 16.0
G_CHUNK = 64
N_GROUPS = 4
EXPERTS_PER_GROUP = 8
N_EXPERTS = N_GROUPS * EXPERTS_PER_GROUP
D_EXPERT = 256
DEPTH = 1
DN_ALPHA = (2.0 * DEPTH) ** 0.25
LN_EPS = 1e-5
SPLIT_SIZES = (A_WIDTH, A_WIDTH, A_WIDTH, IDX_HEADS * IDX_DIM, IDX_DIM, IDX_HEADS,
               G_KW, G_KW, G_VW, G_VW, G_RANK, 1024, 1024)

LANES = 128
VMEM_LIMIT_BYTES = 56 * 1024 * 1024

NEG = -1e30
LOG2E = math.log2(math.e)
INT_MIN = -2 ** 31
INT_MAX = 2 ** 31 - 1
KEY_NEG_INF = -2 ** 31 + 0x7FFFFF

NT = (((1,), (1,)), ((), ()))
TN = (((0,), (0,)), ((), ()))


def _ln(x):
    mu = jnp.mean(x, axis=-1, keepdims=True)
    xc = x - mu
    var = jnp.mean(xc * xc, axis=-1, keepdims=True)
    return xc * lax.rsqrt(var + LN_EPS)


def _sigmoid(x):
    return 1.0 / (1.0 + jnp.exp(-x))


def _params(*sem):
    return pltpu.CompilerParams(dimension_semantics=sem, vmem_limit_bytes=VMEM_LIMIT_BYTES)


def _ada_body(c_ref, w_ref, b_ref, o_ref):
    c = c_ref[...]
    cond = c * _sigmoid(c)
    o_ref[...] = jnp.dot(cond, w_ref[...], preferred_element_type=F32, precision=HIGHEST) + b_ref[...]


def _ada_call(c, w, b):
    B, D = c.shape
    N = w.shape[1]
    tn = 1536
    return pl.pallas_call(
        _ada_body,
        grid=(N // tn,),
        in_specs=[pl.BlockSpec((B, D), lambda j: (0, 0)),
                  pl.BlockSpec((D, tn), lambda j: (0, j)),
                  pl.BlockSpec((1, tn), lambda j: (0, j))],
        out_specs=pl.BlockSpec((B, tn), lambda j: (0, j)),
        out_shape=jax.ShapeDtypeStruct((B, N), F32),
        compiler_params=_params("arbitrary"),
        name="ada",
    )(c, w, b.reshape(1, N))


TOK_K = (0, 512)
TOK_GLA = (512, 2048)
TOK_GATES = (2048, 4096)
TOK_SMALL = (4096, 4224)
CH_Q = (0, 512)
CH_V = (512, 1024)
CH_IQ = (1024, 1536)
CH_IW = (1536, 1552)
IW_SCALE = IDX_HEADS ** -0.5 * IDX_DIM ** -0.5


def _inproj_body(x_ref, sh_ref, sc_ref, wtok_ref, wch_ref,
                 k_ref, gla_ref, gates_ref, ik_ref, glr_ref, qT_ref, vT_ref, iqT_ref, iwT_ref):
    tm = x_ref.shape[1]
    h = (_ln(x_ref[0]) * (1.0 + sc_ref[0]) + sh_ref[0]).astype(BF16)

    def tok(ab):
        return jnp.dot(h, wtok_ref[:, ab[0]:ab[1]], preferred_element_type=F32)

    def ch(ab):
        return lax.dot_general(wch_ref[ab[0]:ab[1], :], h, NT, preferred_element_type=F32)

    kres = tok(TOK_K)
    for p in range(A_WIDTH // LANES):
        k_ref[0, p] = kres[:, p * LANES:(p + 1) * LANES].astype(BF16)
    gla_ref[0] = tok(TOK_GLA).astype(BF16)
    gates_ref[0] = tok(TOK_GATES).astype(BF16)
    small = tok(TOK_SMALL)
    ik_ref[0] = small[:, :IDX_DIM].astype(BF16)
    glr_ref[0] = small[:, IDX_DIM:IDX_DIM + G_RANK]

    qT_ref[0] = (ch(CH_Q) * (A_HEAD_DIM ** -0.5 * LOG2E)).astype(BF16)
    vres = ch(CH_V).astype(BF16)
    for j in range(tm // LANES):
        vT_ref[0, j] = vres[:, j * LANES:(j + 1) * LANES]
    iqT_ref[0] = ch(CH_IQ).astype(BF16)
    iwT_ref[0] = ch(CH_IW) * IW_SCALE


def _inproj_call(x, sh1, sc1, w_tok, w_ch, tm):
    B, S, D = x.shape
    nt = S // tm
    const = lambda b, t: (0, 0)
    out_shape = (
        jax.ShapeDtypeStruct((B, A_WIDTH // LANES, S, LANES), BF16),
        jax.ShapeDtypeStruct((B, S, 1536), BF16),
        jax.ShapeDtypeStruct((B, S, 2048), BF16),
        jax.ShapeDtypeStruct((B, S, IDX_DIM), BF16),
        jax.ShapeDtypeStruct((B, S, G_RANK), F32),
        jax.ShapeDtypeStruct((B, A_WIDTH, S), BF16),
        jax.ShapeDtypeStruct((B, S // LANES, A_WIDTH, LANES), BF16),
        jax.ShapeDtypeStruct((B, IDX_HEADS * IDX_DIM, S), BF16),
        jax.ShapeDtypeStruct((B, IDX_HEADS, S), F32),
    )
    out_specs = (
        pl.BlockSpec((1, A_WIDTH // LANES, tm, LANES), lambda b, t: (b, 0, t, 0)),
        pl.BlockSpec((1, tm, 1536), lambda b, t: (b, t, 0)),
        pl.BlockSpec((1, tm, 2048), lambda b, t: (b, t, 0)),
        pl.BlockSpec((1, tm, IDX_DIM), lambda b, t: (b, t, 0)),
        pl.BlockSpec((1, tm, G_RANK), lambda b, t: (b, t, 0)),
        pl.BlockSpec((1, A_WIDTH, tm), lambda b, t: (b, 0, t)),
        pl.BlockSpec((1, tm // LANES, A_WIDTH, LANES), lambda b, t: (b, t, 0, 0)),
        pl.BlockSpec((1, IDX_HEADS * IDX_DIM, tm), lambda b, t: (b, 0, t)),
        pl.BlockSpec((1, IDX_HEADS, tm), lambda b, t: (b, 0, t)),
    )
    return pl.pallas_call(
        _inproj_body,
        grid=(B, nt),
        in_specs=[pl.BlockSpec((1, tm, D), lambda b, t: (b, t, 0)),
                  pl.BlockSpec((1, 1, D), lambda b, t: (b, 0, 0)),
                  pl.BlockSpec((1, 1, D), lambda b, t: (b, 0, 0)),
                  pl.BlockSpec(w_tok.shape, const),
                  pl.BlockSpec(w_ch.shape, const)],
        out_specs=out_specs,
        out_shape=out_shape,
        compiler_params=_params("parallel", "parallel"),
        name="inproj",
    )(x, sh1, sc1, w_tok, w_ch)


IDX_CHUNK = 256
CNT_CHUNK = 512
SORT_GROUP = 4
ATT_CHUNK = 256
TBL_PAD = 2 * QBLK
ACC_ROWS = A_HEAD_DIM + 16


def _rel_bucket_table():
    s = np.arange(2 * QBLK)[:, None]
    t = np.arange(QBLK)[None, :]
    dist = np.maximum(t + QBLK - s, 0)
    max_exact = REL_BUCKETS // 2
    d_f = np.maximum(dist, 1).astype(np.float32)
    large = max_exact + (np.log(d_f / max_exact) / math.log(REL_MAX_DIST / max_exact)
                         * (REL_BUCKETS - max_exact)).astype(np.int32)
    large = np.minimum(large, REL_BUCKETS - 1)
    return np.where(dist < max_exact, dist, large).astype(np.int32)


def _far_bucket():
    max_exact = REL_BUCKETS // 2
    v = max_exact + int(np.float32(np.log(np.float32(QBLK + 1) / max_exact) / math.log(REL_MAX_DIST / max_exact)
                                   * (REL_BUCKETS - max_exact)))
    assert min(v, REL_BUCKETS - 1) == REL_BUCKETS - 1
    return REL_BUCKETS - 1


def _dsa_body(rb_ref, bkt_ref, ik_ref, kk_ref, vT_ref, qT_ref, iqT_ref, iwT_ref, o_ref,
              sc_s, srt_s, thr_s, madd_s, tbl_s, oT_s, xcut_s, qm_s, sa_s, sb_s, acc_s, mall_s, mblk_s,
              *, topk, idx_bits, max_cnt):
    i = pl.program_id(1)
    nck = (i + 2) // 2
    t_idx = i * QBLK + lax.broadcasted_iota(I32, (1, QBLK), 1)

    @pl.when(i == 0)
    def _():
        bkt = bkt_ref[...]
        tbl_s[...] = jnp.zeros_like(tbl_s)
        for h in range(A_HEADS):
            t = jnp.zeros((2 * QBLK, QBLK), F32)
            for k in range(REL_BUCKETS):
                t = jnp.where(bkt == k, rb_ref[k, h], t)
            tbl_s[h, TBL_PAD:TBL_PAD + 2 * QBLK, :] = (t - rb_ref[_far_bucket(), h]) * LOG2E

    def key_to_float(key):
        key = jnp.maximum(key, KEY_NEG_INF)
        return pltpu.bitcast(jnp.where(key < 0, key ^ INT_MAX, key), F32)

    def score_chunk(c, carry):
        s0 = pl.multiple_of(c * IDX_CHUNK, IDX_CHUNK)
        kc = ik_ref[0, pl.ds(s0, IDX_CHUNK), :]
        acc = jnp.zeros((IDX_CHUNK, QBLK), F32)
        for hp in range(IDX_HEADS // 2):
            r0 = hp * 2 * IDX_DIM
            rhs = jnp.concatenate([iqT_ref[0, r0:r0 + IDX_DIM, :],
                                   iqT_ref[0, r0 + IDX_DIM:r0 + 2 * IDX_DIM, :]], axis=1)
            z = jnp.dot(kc, rhs, preferred_element_type=F32)
            acc = acc + jnp.maximum(z[:, :QBLK], 0.0) * iwT_ref[0, 2 * hp:2 * hp + 1, :]
            acc = acc + jnp.maximum(z[:, QBLK:], 0.0) * iwT_ref[0, 2 * hp + 1:2 * hp + 2, :]
        s_idx = s0 + lax.broadcasted_iota(I32, (IDX_CHUNK, QBLK), 0)
        sc_s[pl.ds(s0, IDX_CHUNK), :] = jnp.where(s_idx <= t_idx, acc, -jnp.inf)
        return carry

    ncnt = (nck + 1) // 2
    lax.fori_loop(0, ncnt, lambda c, carry: score_chunk(2 * c + 1, score_chunk(2 * c, carry)), 0)

    def count(pred):
        def body(c, cnt):
            s0 = pl.multiple_of(c * CNT_CHUNK, CNT_CHUNK)
            k = sc_s[pl.ds(s0, CNT_CHUNK), :]
            s_idx = s0 + lax.broadcasted_iota(I32, (CNT_CHUNK, QBLK), 0)
            m = jnp.where(pred(k, s_idx), 1, 0)
            return cnt + jnp.sum(m.reshape(CNT_CHUNK // 8, 8, QBLK), axis=0)
        cnt = lax.fori_loop(0, ncnt, body, jnp.zeros((8, QBLK), I32))
        return jnp.sum(cnt, axis=0, keepdims=True)

    def search_block(n):
        groups = n * CNT_CHUNK // (8 * SORT_GROUP)

        for g in range(groups):
            v = [sc_s[(SORT_GROUP * g + u) * 8:(SORT_GROUP * g + u + 1) * 8, :] for u in range(SORT_GROUP)]
            for a, b in ((0, 1), (2, 3), (0, 2), (1, 3), (1, 2)):
                v[a], v[b] = jnp.maximum(v[a], v[b]), jnp.minimum(v[a], v[b])
            for u in range(SORT_GROUP):
                srt_s[(SORT_GROUP * g + u) * 8:(SORT_GROUP * g + u + 1) * 8, :] = v[u]

        def count_ge(cand_key):
            cand = key_to_float(cand_key)
            parts = []
            for g in range(groups):
                cnt = 0
                for u in range(SORT_GROUP):
                    tile = srt_s[(SORT_GROUP * g + u) * 8:(SORT_GROUP * g + u + 1) * 8, :]
                    cnt = jnp.where(tile >= cand, u + 1, cnt)
                parts.append(cnt)
            while len(parts) > 1:
                odd = parts[len(parts) & ~1:]
                parts = [parts[j] + parts[j + 1] for j in range(0, len(parts) - 1, 2)] + odd
            return jnp.sum(parts[0], axis=0, keepdims=True)

        c0 = count_ge(jnp.zeros((1, QBLK), I32))
        ok = c0 >= topk
        T = jnp.where(ok, 0, INT_MIN).astype(I32)
        cnt_T = jnp.where(ok, c0, n * CNT_CHUNK)

        def bit_body(j, carry):
            T, cnt_T = carry
            cand = T | jnp.left_shift(jnp.int32(1), 30 - j)
            c = count_ge(cand)
            ok = c >= topk
            return jnp.where(ok, cand, T), jnp.where(ok, c, cnt_T)

        T, cnt_T = lax.fori_loop(0, 31, bit_body, (T, cnt_T))
        thr_s[0:1, :] = T
        thr_s[1:2, :] = cnt_T
        T = jnp.maximum(T, KEY_NEG_INF)
        thr_s[2:3, :] = count_ge(jnp.where(T == INT_MAX, T, T + 1))

    for n in range(1, max_cnt + 1):
        pl.when(ncnt == n)(functools.partial(search_block, n))
    T_key = jnp.maximum(thr_s[0:1, :], KEY_NEG_INF)
    T = key_to_float(T_key)
    cnt_ge = thr_s[1:2, :]
    cnt_gt = thr_s[2:3, :]

    need = topk - cnt_gt
    excess = jnp.where((cnt_ge - cnt_gt > need) & (T_key > KEY_NEG_INF), 1.0, 0.0)
    xcut_s[...] = jnp.full((1, QBLK), INT_MAX, I32)

    @pl.when(jnp.max(excess) > 0.0)
    def _():
        X = jnp.zeros((1, QBLK), I32)
        for b in range(idx_bits - 1, -1, -1):
            cand = X | (1 << b)
            f = count(lambda k, s: (k == T) & (s < cand))
            X = jnp.where(f < need, cand, X)
        xcut_s[...] = X

    xcut = xcut_s[...]

    def mask_chunk(c, carry):
        s0 = pl.multiple_of(c * IDX_CHUNK, IDX_CHUNK)
        k = sc_s[pl.ds(s0, IDX_CHUNK), :]
        s_idx = s0 + lax.broadcasted_iota(I32, (IDX_CHUNK, QBLK), 0)
        sel = ((k > T) | ((k == T) & (s_idx <= xcut))) & (s_idx <= t_idx)
        madd_s[pl.ds(s0, IDX_CHUNK), :] = jnp.where(sel, 0.0, NEG)
        return carry

    lax.fori_loop(0, nck, mask_chunk, 0)

    c_last = i // 2
    even = 1 - (i - 2 * c_last)
    row_head = lax.broadcasted_iota(I32, (LANES, QBLK), 0) // A_HEAD_DIM
    for h in range(A_HEADS):
        qp = qT_ref[0, (h // 2) * LANES:(h // 2 + 1) * LANES, :]
        qm_s[h] = jnp.where(row_head == h % 2, qp, jnp.zeros_like(qp))
    acc_s[...] = jnp.zeros_like(acc_s)
    ones = jnp.ones((ACC_ROWS - A_HEAD_DIM, ATT_CHUNK), BF16)

    def logits(c, s_buf):
        s0 = pl.multiple_of(c * ATT_CHUNK, ATT_CHUNK)
        madd = madd_s[pl.ds(s0, ATT_CHUNK), :]
        off = jnp.where(c == c_last, 2 * QBLK + QBLK * even, jnp.where(c == c_last - 1, QBLK * even, 0))
        off = pl.multiple_of(off, QBLK)
        m_blk = []
        for h in range(A_HEADS):
            kc = kk_ref[0, h // 2, pl.ds(s0, ATT_CHUNK), :]
            s = jnp.dot(kc, qm_s[h], preferred_element_type=F32) + madd + tbl_s[h, pl.ds(off, ATT_CHUNK), :]
            s_buf[h] = s
            m_blk.append(jnp.max(s, axis=0, keepdims=True))
        return jnp.concatenate(m_blk, axis=0)

    def accumulate(c, s_buf, m_all, m_blk):
        m_new = jnp.maximum(m_all, m_blk)
        alpha = jnp.exp2(m_all - m_new)
        for h in range(A_HEADS):
            rows = slice(h * A_HEAD_DIM, (h + 1) * A_HEAD_DIM)
            p = jnp.exp2(s_buf[h] - m_new[h:h + 1]).astype(BF16)
            vt = jnp.concatenate([vT_ref[0, 2 * c + u, rows, :] for u in range(ATT_CHUNK // QBLK)], axis=1)
            vt = jnp.concatenate([vt, ones], axis=0)
            acc_s[h] = alpha[h:h + 1] * acc_s[h] + jnp.dot(vt, p, preferred_element_type=F32)
        return m_new

    def att_body(pair, carry):
        m_all, m_blk = carry
        c = 2 * pair
        m_b = logits(c + 1, sb_s)
        m_all = accumulate(c, sa_s, m_all, m_blk)
        m_a = logits(c + 2, sa_s)
        return accumulate(c + 1, sb_s, m_all, m_b), m_a

    n_pairs = c_last // 2
    carry = (jnp.full((A_HEADS, QBLK), NEG, F32), logits(0, sa_s))
    m_all, m_blk = lax.fori_loop(0, n_pairs, att_body, carry)
    mall_s[...] = m_all
    mblk_s[...] = m_blk

    @pl.when(c_last % 2 == 1)
    def _():
        m_b = logits(c_last, sb_s)
        m_all = accumulate(c_last - 1, sa_s, mall_s[...], mblk_s[...])
        accumulate(c_last, sb_s, m_all, m_b)

    @pl.when(c_last % 2 == 0)
    def _():
        accumulate(c_last, sa_s, mall_s[...], mblk_s[...])

    for h in range(A_HEADS):
        rows = slice(h * A_HEAD_DIM, (h + 1) * A_HEAD_DIM)
        oT_s[rows, :] = acc_s[h, 0:A_HEAD_DIM, :] / acc_s[h, A_HEAD_DIM:A_HEAD_DIM + 1, :]
    o_ref[0] = oT_s[...].T.astype(BF16)


def _dsa_call(rel_bias, ik, kk, vT, qT, iqT, iwT):
    B, S, _ = ik.shape
    nb = S // QBLK
    topk = min(TOPK_MAX, S // 4)
    bkt = jnp.asarray(_rel_bucket_table())
    body = functools.partial(_dsa_body, topk=topk, idx_bits=int(math.log2(S)), max_cnt=S // CNT_CHUNK)
    return pl.pallas_call(
        body,
        grid=(B, nb),
        in_specs=[pl.BlockSpec(memory_space=pltpu.SMEM),
                  pl.BlockSpec((2 * QBLK, QBLK), lambda b, i: (0, 0)),
                  pl.BlockSpec((1, S, IDX_DIM), lambda b, i: (b, 0, 0)),
                  pl.BlockSpec((1, A_WIDTH // LANES, S, LANES), lambda b, i: (b, 0, 0, 0)),
                  pl.BlockSpec((1, S // LANES, A_WIDTH, LANES), lambda b, i: (b, 0, 0, 0)),
                  pl.BlockSpec((1, A_WIDTH, QBLK), lambda b, i: (b, 0, i)),
                  pl.BlockSpec((1, IDX_HEADS * IDX_DIM, QBLK), lambda b, i: (b, 0, i)),
                  pl.BlockSpec((1, IDX_HEADS, QBLK), lambda b, i: (b, 0, i))],
        out_specs=pl.BlockSpec((1, QBLK, A_WIDTH), lambda b, i: (b, i, 0)),
        out_shape=jax.ShapeDtypeStruct((B, S, A_WIDTH), BF16),
        scratch_shapes=[pltpu.VMEM((S, QBLK), F32),
                        pltpu.VMEM((S, QBLK), F32),
                        pltpu.VMEM((8, QBLK), I32),
                        pltpu.VMEM((S, QBLK), F32),
                        pltpu.VMEM((A_HEADS, TBL_PAD + 3 * QBLK, QBLK), F32),
                        pltpu.VMEM((A_WIDTH, QBLK), F32),
                        pltpu.VMEM((1, QBLK), I32),
                        pltpu.VMEM((A_HEADS, LANES, QBLK), BF16),
                        pltpu.VMEM((A_HEADS, ATT_CHUNK, QBLK), F32),
                        pltpu.VMEM((A_HEADS, ATT_CHUNK, QBLK), F32),
                        pltpu.VMEM((A_HEADS, ACC_ROWS, QBLK), F32),
                        pltpu.VMEM((A_HEADS, QBLK), F32),
                        pltpu.VMEM((A_HEADS, QBLK), F32)],
        compiler_params=_params("parallel", "arbitrary"),
        name="dsa",
    )(rel_bias, bkt, ik, kk, vT, qT, iqT, iwT)


GLA_Q = (0, 256)
GLA_K = (256, 512)
GLA_V = (512, 1024)
GLA_R = (1024, 1536)


def _gla_body(gla_ref, glr_ref, wg_ref, bg_ref, ng_ref, o_ref, st_s):
    tg = gla_ref.shape[1]
    C = G_CHUNK

    @pl.when(pl.program_id(1) == 0)
    def _():
        st_s[...] = jnp.zeros_like(st_s)

    xg = jnp.dot(glr_ref[0], wg_ref[...], preferred_element_type=F32, precision=HIGHEST) + bg_ref[...]
    logg = -(jnp.maximum(-xg, 0.0) + jnp.log1p(jnp.exp(-jnp.abs(xg)))) * (1.0 / G_TAU)

    rt = lax.broadcasted_iota(I32, (tg, tg), 0)
    ct = lax.broadcasted_iota(I32, (tg, tg), 1)
    cum_f = jnp.where((rt // C == ct // C) & (rt >= ct), 1.0, 0.0).astype(F32)
    bc_all = jnp.dot(cum_f, logg, preferred_element_type=F32, precision=HIGHEST)

    ri = lax.broadcasted_iota(I32, (C, C), 0)
    ci = lax.broadcasted_iota(I32, (C, C), 1)
    tril = ri >= ci
    lane_head = lax.broadcasted_iota(I32, (C, LANES), 1) // G_DK
    st_rows = lax.broadcasted_iota(I32, (2 * G_DV, LANES), 0) // G_DV
    st_cols = lax.broadcasted_iota(I32, (2 * G_DV, LANES), 1) // G_DK
    st_diag = st_rows == st_cols

    for ck in range(tg // C):
        r0 = ck * C
        for p in range(G_HEADS // 2):
            bc = bc_all[r0:r0 + C, p * LANES:(p + 1) * LANES]
            bl = bc[C - 1:C, :]
            q = gla_ref[0, r0:r0 + C, GLA_Q[0] + p * LANES:GLA_Q[0] + (p + 1) * LANES].astype(F32) * (G_DK ** -0.5)
            k = gla_ref[0, r0:r0 + C, GLA_K[0] + p * LANES:GLA_K[0] + (p + 1) * LANES].astype(F32)
            v = gla_ref[0, r0:r0 + C, GLA_V[0] + p * 2 * G_DV:GLA_V[0] + (p + 1) * 2 * G_DV]
            q_in = (q * jnp.exp(bc)).astype(BF16)
            k_st = (k * jnp.exp(bl - bc)).astype(BF16)
            q_rel = q * jnp.exp(bc - bl)
            o_intra = []
            for sub in range(2):
                qm = jnp.where(lane_head == sub, q_rel, 0.0).astype(BF16)
                att = lax.dot_general(qm, k_st, NT, preferred_element_type=F32)
                att = jnp.where(tril, att, 0.0).astype(BF16)
                o_intra.append(jnp.dot(att, v[:, sub * G_DV:(sub + 1) * G_DV], preferred_element_type=F32))
            st = st_s[p]
            o_inter = lax.dot_general(q_in, st.astype(BF16), NT, preferred_element_type=F32)
            uT = lax.dot_general(v, k_st, TN, preferred_element_type=F32)
            st_s[p] = st * jnp.exp(bl) + jnp.where(st_diag, uT, 0.0)
            for sub in range(2):
                hd = 2 * p + sub
                o = o_intra[sub] + o_inter[:, sub * G_DV:(sub + 1) * G_DV]
                y = _ln(o) * ng_ref[:, hd * G_DV:(hd + 1) * G_DV]
                g = gla_ref[0, r0:r0 + C, GLA_R[0] + hd * G_DV:GLA_R[0] + (hd + 1) * G_DV].astype(F32)
                o_ref[0, r0:r0 + C, hd * G_DV:(hd + 1) * G_DV] = (y * (g * _sigmoid(g))).astype(BF16)


def _gla_call(gla, glr, wg, bg, ng, tg):
    B, S, _ = gla.shape
    const = lambda b, j: (0, 0)
    return pl.pallas_call(
        _gla_body,
        grid=(B, S // tg),
        in_specs=[pl.BlockSpec((1, tg, 1536), lambda b, j: (b, j, 0)),
                  pl.BlockSpec((1, tg, G_RANK), lambda b, j: (b, j, 0)),
                  pl.BlockSpec((G_RANK, G_KW), const),
                  pl.BlockSpec((1, G_KW), const),
                  pl.BlockSpec((1, G_VW), const)],
        out_specs=pl.BlockSpec((1, tg, G_VW), lambda b, j: (b, j, 0)),
        out_shape=jax.ShapeDtypeStruct((B, S, G_VW), BF16),
        scratch_shapes=[pltpu.VMEM((G_HEADS // 2, 2 * G_DV, LANES), F32)],
        compiler_params=_params("parallel", "arbitrary"),
        name="gla",
    )(gla, glr, wg, bg, ng)


ROUTER_ROWS = 40
ROUTER_E0 = 8


def _post_body(oa_ref, ob_ref, gates_ref, x_ref, gt1_ref, sh2_ref, sc2_ref, wa_ref, wb_ref, wo_ref,
               g1_ref, b1_ref, wr_ref, br_ref, x1_ref, h2_ref, gate_ref, grp_ref):
    tm = x_ref.shape[1]
    D = x_ref.shape[2]
    ya = jnp.dot(oa_ref[0], wa_ref[...], preferred_element_type=F32)
    yb = jnp.dot(ob_ref[0], wb_ref[...], preferred_element_type=F32)
    ga = gates_ref[0, :, 0:D].astype(F32)
    gb = gates_ref[0, :, D:2 * D].astype(F32)
    merged = _sigmoid(ga) * ya + _sigmoid(gb) * yb
    y = jnp.dot(merged.astype(BF16), wo_ref[...], preferred_element_type=F32)
    x1 = _ln(DN_ALPHA * x_ref[0] + gt1_ref[0] * y) * g1_ref[...] + b1_ref[...]
    x1_ref[0] = x1
    h2 = _ln(x1) * (1.0 + sc2_ref[0]) + sh2_ref[0]
    h2_ref[0] = h2.astype(BF16)

    lt = lax.dot_general(wr_ref[...], h2, NT, preferred_element_type=F32, precision=HIGHEST) + br_ref[...]
    gl = lt[0:N_GROUPS]
    gmax = jnp.max(gl, axis=0, keepdims=True)
    g_w = 1.0 / jnp.sum(jnp.exp(gl - gmax), axis=0, keepdims=True)
    r4 = lax.broadcasted_iota(I32, (N_GROUPS, tm), 0)
    g_idx = jnp.min(jnp.where(gl == gmax, r4, N_GROUPS), axis=0, keepdims=True)
    eg = jnp.zeros((EXPERTS_PER_GROUP, tm), F32)
    for g in range(N_GROUPS):
        lo = ROUTER_E0 + g * EXPERTS_PER_GROUP
        eg = jnp.where(g_idx == g, lt[lo:lo + EXPERTS_PER_GROUP], eg)
    r8 = lax.broadcasted_iota(I32, (EXPERTS_PER_GROUP, tm), 0)
    e1 = jnp.max(eg, axis=0, keepdims=True)
    i1 = jnp.min(jnp.where(eg == e1, r8, EXPERTS_PER_GROUP), axis=0, keepdims=True)
    eg2 = jnp.where(r8 == i1, -jnp.inf, eg)
    e2 = jnp.max(eg2, axis=0, keepdims=True)
    i2 = jnp.min(jnp.where(eg2 == e2, r8, EXPERTS_PER_GROUP), axis=0, keepdims=True)
    d = jnp.exp(e2 - e1)
    w1 = g_w / (1.0 + d)
    w2 = g_w * d / (1.0 + d)
    in_group = jnp.where(r8 == i1, w1, 0.0) + jnp.where(r8 == i2, w2, 0.0)
    blocks = [jnp.where(g_idx == g, in_group, 0.0) for g in range(N_GROUPS)]
    blocks.append(jnp.zeros((LANES - N_EXPERTS, tm), F32))
    gate_ref[...] = jnp.concatenate(blocks, axis=0).T
    r8g = lax.broadcasted_iota(I32, (8, tm), 0)
    grp_ref[...] = jnp.where(r8g == g_idx, 1.0, 0.0)


def _post_call(o_a, o_b, gates, x, gt1, sh2, sc2, wa, wb, wo, g1, b1, wr, br, tm):
    B, S, D = x.shape
    nt = S // tm
    const = lambda b, t: (0, 0)
    row = lambda b, t: (b, 0, 0)
    tile = lambda b, t: (b, t, 0)
    return pl.pallas_call(
        _post_body,
        grid=(B, nt),
        in_specs=[pl.BlockSpec((1, tm, A_WIDTH), tile),
                  pl.BlockSpec((1, tm, G_VW), tile),
                  pl.BlockSpec((1, tm, 2 * D), tile),
                  pl.BlockSpec((1, tm, D), tile),
                  pl.BlockSpec((1, 1, D), row),
                  pl.BlockSpec((1, 1, D), row),
                  pl.BlockSpec((1, 1, D), row),
                  pl.BlockSpec(wa.shape, const),
                  pl.BlockSpec(wb.shape, const),
                  pl.BlockSpec(wo.shape, const),
                  pl.BlockSpec((1, D), const),
                  pl.BlockSpec((1, D), const),
                  pl.BlockSpec(wr.shape, const),
                  pl.BlockSpec(br.shape, const)],
        out_specs=(pl.BlockSpec((1, tm, D), tile),
                   pl.BlockSpec((1, tm, D), tile),
                   pl.BlockSpec((tm, LANES), lambda b, t: (b * nt + t, 0)),
                   pl.BlockSpec((8, tm), lambda b, t: (0, b * nt + t))),
        out_shape=(jax.ShapeDtypeStruct((B, S, D), F32),
                   jax.ShapeDtypeStruct((B, S, D), BF16),
                   jax.ShapeDtypeStruct((B * S, LANES), F32),
                   jax.ShapeDtypeStruct((8, B * S), F32)),
        compiler_params=_params("parallel", "parallel"),
        name="post",
    )(o_a, o_b, gates, x, gt1, sh2, sc2, wa, wb, wo, g1, b1, wr, br)


MOE_EXPERTS_PER_STEP = 4
MOE_ROW_BLOCK = 128
MOE_PERM_ROWS = 256


def _moe_body(h2_ref, gate_ref, grp_ref, x1_ref, gt2_ref, w1_ref, w3_ref, w2_ref, g2_ref, b2_ref, o_ref,
              perm_s, hs_s, gs_s, ys_s, tri_s, seg_s):
    t = pl.program_id(0)
    s = pl.program_id(1)
    ne = MOE_EXPERTS_PER_STEP
    rb = MOE_ROW_BLOCK
    tm = h2_ref.shape[0]
    tm_pad = perm_s.shape[0]
    steps_per_group = EXPERTS_PER_GROUP // ne

    @pl.when((t == 0) & (s == 0))
    def _():
        r = lax.broadcasted_iota(I32, (tm, tm), 0)
        c = lax.broadcasted_iota(I32, (tm, tm), 1)
        tri_s[...] = jnp.where(r < c, 1.0, 0.0).astype(BF16)

    @pl.when(s == 0)
    def _():
        oh = grp_ref[...]
        rank = jnp.dot(oh.astype(BF16), tri_s[...], preferred_element_type=F32)
        cnt = jnp.sum(oh, axis=1, keepdims=True)
        blocks = jnp.floor((cnt + (rb - 1)) * (1.0 / rb))
        off = jnp.zeros((1, 1), F32)
        dest = jnp.zeros((1, tm), F32)
        for g in range(N_GROUPS):
            seg_s[2 * g] = jnp.sum(off).astype(I32)
            seg_s[2 * g + 1] = jnp.sum(blocks[g:g + 1, :]).astype(I32)
            dest = dest + oh[g:g + 1, :] * (off + rank[g:g + 1, :])
            off = off + blocks[g:g + 1, :] * rb
        dest_i = dest.astype(I32)
        gate = gate_ref[...]
        g_hi = gate.astype(BF16)
        g_lo = (gate - g_hi.astype(F32)).astype(BF16)
        h2 = h2_ref[...]
        for c in range(tm_pad // MOE_PERM_ROWS):
            rows = slice(c * MOE_PERM_ROWS, (c + 1) * MOE_PERM_ROWS)
            d_idx = c * MOE_PERM_ROWS + lax.broadcasted_iota(I32, (MOE_PERM_ROWS, tm), 0)
            perm = jnp.where(d_idx == dest_i, 1.0, 0.0).astype(BF16)
            perm_s[rows, :] = perm
            hs_s[rows, :] = jnp.dot(perm, h2, preferred_element_type=F32).astype(BF16)
            gs_s[rows, :] = (jnp.dot(perm, g_hi, preferred_element_type=F32)
                             + jnp.dot(perm, g_lo, preferred_element_type=F32))
        ys_s[...] = jnp.zeros_like(ys_s)

    g = s // steps_per_group
    row0 = seg_s[2 * g]
    nblk = seg_s[2 * g + 1]

    def block(i, carry):
        r0 = pl.multiple_of(row0 + i * rb, rb)
        x = hs_s[pl.ds(r0, rb), :]
        gsel = pltpu.roll(gs_s[pl.ds(r0, rb), :], (LANES - s * ne) % LANES, axis=1)
        hid = []
        for j in range(ne):
            a = jnp.dot(x, w1_ref[j], preferred_element_type=F32)
            b = jnp.dot(x, w3_ref[j], preferred_element_type=F32)
            hid.append((a * _sigmoid(a) * b * gsel[:, j:j + 1]).astype(BF16))
        ys_s[pl.ds(r0, rb), :] += jnp.dot(jnp.concatenate(hid, axis=1), w2_ref[...], preferred_element_type=F32)
        return carry

    lax.fori_loop(0, nblk, block, 0)

    @pl.when(s == pl.num_programs(1) - 1)
    def _():
        perm = perm_s[...]
        for c in range(o_ref.shape[1] // MOE_PERM_ROWS):
            cols = slice(c * MOE_PERM_ROWS, (c + 1) * MOE_PERM_ROWS)
            o_ref[:, cols] = lax.dot_general(perm, ys_s[:, cols].astype(BF16), TN, preferred_element_type=F32)
        z = DN_ALPHA * x1_ref[...] + gt2_ref[0] * o_ref[...]
        o_ref[...] = _ln(z) * g2_ref[...] + b2_ref[...]


def _moe_call(h2, gate, grp, x1, gt2, w1, w3, w2, g2, b2, tm, S):
    T, D = h2.shape
    ne = MOE_EXPERTS_PER_STEP
    nc = N_EXPERTS // ne
    tm_pad = tm + N_GROUPS * MOE_ROW_BLOCK
    tiles_per_seq = S // tm
    tile = lambda t, c: (t, 0)
    const = lambda t, c: (0, 0)
    return pl.pallas_call(
        _moe_body,
        grid=(T // tm, nc),
        in_specs=[pl.BlockSpec((tm, D), tile),
                  pl.BlockSpec((tm, LANES), tile),
                  pl.BlockSpec((8, tm), lambda t, c: (0, t)),
                  pl.BlockSpec((tm, D), tile),
                  pl.BlockSpec((1, 1, D), lambda t, c: (t // tiles_per_seq, 0, 0)),
                  pl.BlockSpec((ne, D, D_EXPERT), lambda t, c: (c, 0, 0)),
                  pl.BlockSpec((ne, D, D_EXPERT), lambda t, c: (c, 0, 0)),
                  pl.BlockSpec((ne * D_EXPERT, D), lambda t, c: (c, 0)),
                  pl.BlockSpec((1, D), const),
                  pl.BlockSpec((1, D), const)],
        out_specs=pl.BlockSpec((tm, D), tile),
        out_shape=jax.ShapeDtypeStruct((T, D), F32),
        scratch_shapes=[pltpu.VMEM((tm_pad, tm), BF16),
                        pltpu.VMEM((tm_pad, D), BF16),
                        pltpu.VMEM((tm_pad, LANES), F32),
                        pltpu.VMEM((tm_pad, D), F32),
                        pltpu.VMEM((tm, tm), BF16),
                        pltpu.SMEM((2 * N_GROUPS,), I32)],
        compiler_params=_params("arbitrary", "arbitrary"),
        name="moe",
    )(h2, gate, grp, x1, gt2, w1, w3, w2, g2, b2)


def _pick(n, pref):
    return pref if n % pref == 0 else n


def kernel(x, c, rel_bias, w_ada, b_ada, w_in, gla_w_gate, gla_b_gate, gla_norm_g, w_branch_a, w_branch_b, w_out, ln1_g, ln1_b, w_router_group, b_router_group, w_router_expert, b_router_expert, w_exp_gate, w_exp_up, w_exp_down, ln2_g, ln2_b):
    B, S, D = x.shape
    assert S % (2 * QBLK) == 0 and D == 1024 and w_ada.shape[0] == DEPTH == 1
    l = 0

    ada = _ada_call(c, w_ada[l], b_ada[l])
    sh1, sc1, gt1, sh2, sc2, gt2 = [ada[:, i * D:(i + 1) * D].reshape(B, 1, D) for i in range(6)]

    offs = np.concatenate([[0], np.cumsum(SPLIT_SIZES)])
    seg = lambda i: w_in[l][:, offs[i]:offs[i + 1]]
    (w_aq, w_ak, w_av, w_iq, w_ik, w_iw, w_gq, w_gk, w_gv, w_gr, w_glr, w_ga, w_gb) = [seg(i) for i in range(13)]
    pad = jnp.zeros((D, TOK_SMALL[1] - TOK_SMALL[0] - IDX_DIM - G_RANK), F32)
    w_tok = jnp.concatenate([w_ak, w_gq, w_gk, w_gv, w_gr, w_ga, w_gb, w_ik, w_glr, pad], axis=1).astype(BF16)
    w_ch = jnp.concatenate([w_aq, w_av, w_iq, w_iw], axis=1).T.astype(BF16)

    tm = _pick(S, 512)
    kk, gla, gates, ik, glr, qT, vT, iqT, iwT = _inproj_call(x, sh1, sc1, w_tok, w_ch, tm)

    o_a = _dsa_call(rel_bias, ik, kk, vT, qT, iqT, iwT)
    o_b = _gla_call(gla, glr, gla_w_gate[l], gla_b_gate[l].reshape(1, G_KW), gla_norm_g[l].reshape(1, G_VW),
                    _pick(S, 256))

    wr = jnp.zeros((ROUTER_ROWS, D), F32)
    wr = wr.at[0:N_GROUPS].set(w_router_group[l].T).at[ROUTER_E0:ROUTER_E0 + N_EXPERTS].set(w_router_expert[l].T)
    br = jnp.zeros((ROUTER_ROWS, 1), F32)
    br = br.at[0:N_GROUPS, 0].set(b_router_group[l]).at[ROUTER_E0:ROUTER_E0 + N_EXPERTS, 0].set(b_router_expert[l])
    x1, h2, gate, grp = _post_call(o_a, o_b, gates, x, gt1, sh2, sc2,
                              w_branch_a[l].astype(BF16), w_branch_b[l].astype(BF16), w_out[l].astype(BF16),
                              ln1_g[l].reshape(1, D), ln1_b[l].reshape(1, D), wr, br, tm)

    tm5 = _pick(S, 1024)
    out = _moe_call(h2.reshape(B * S, D), gate, grp, x1.reshape(B * S, D), gt2,
                    w_exp_gate[l].astype(BF16), w_exp_up[l].astype(BF16),
                    w_exp_down[l].astype(BF16).reshape(N_EXPERTS * D_EXPERT, D),
                    ln2_g[l].reshape(1, D), ln2_b[l].reshape(1, D), tm5, S)
    return out.reshape(B, S, D)
```

```python
import functools
import math

import numpy as np
import jax
import jax.numpy as jnp
from jax import lax
from jax.experimental import pallas as pl
from jax.experimental.pallas import tpu as pltpu

F32 = jnp.float32
BF16 = jnp.bfloat16
I32 = jnp.int32
HIGHEST = lax.Precision.HIGHEST

A_HEADS = 8
A_HEAD_DIM = 64
A_WIDTH = A_HEADS * A_HEAD_DIM
IDX_HEADS = 16
IDX_DIM = 32
TOPK_MAX = 256
QBLK = 128
REL_BUCKETS = 32
REL_MAX_DIST = 128
G_HEADS = 4
G_DK = 64
G_DV = 128
G_KW = G_HEADS * G_DK
G_VW = G_HEADS * G_DV
G_RANK = 16
G_TAU = 16.0
G_CHUNK = 64
N_GROUPS = 4
EXPERTS_PER_GROUP = 8
N_EXPERTS = N_GROUPS * EXPERTS_PER_GROUP
D_EXPERT = 256
DEPTH = 1
DN_ALPHA = (2.0 * DEPTH) ** 0.25
LN_EPS = 1e-5
SPLIT_SIZES = (A_WIDTH, A_WIDTH, A_WIDTH, IDX_HEADS * IDX_DIM, IDX_DIM, IDX_HEADS,
               G_KW, G_KW, G_VW, G_VW, G_RANK, 1024, 1024)

LANES = 128
VMEM_LIMIT_BYTES = 56 * 1024 * 1024

NEG = -1e30
LOG2E = math.log2(math.e)
INT_MIN = -2 ** 31
INT_MAX = 2 ** 31 - 1
KEY_NEG_INF = -2 ** 31 + 0x7FFFFF

NT = (((1,), (1,)), ((), ()))
TN = (((0,), (0,)), ((), ()))


def _ln(x):
    mu = jnp.mean(x, axis=-1, keepdims=True)
    xc = x - mu
    var = jnp.mean(xc * xc, axis=-1, keepdims=True)
    return xc * lax.rsqrt(var + LN_EPS)


def _sigmoid(x):
    return 0.5 * jnp.tanh(0.5 * x) + 0.5


def _params(*sem):
    return pltpu.CompilerParams(dimension_semantics=sem, vmem_limit_bytes=VMEM_LIMIT_BYTES)


def _ada_body(c_ref, w_ref, b_ref, o_ref):
    c = c_ref[...]
    cond = c * _sigmoid(c)
    o_ref[...] = jnp.dot(cond, w_ref[...], preferred_element_type=F32, precision=HIGHEST) + b_ref[...]


def _ada_call(c, w, b):
    B, D = c.shape
    N = w.shape[1]
    tn = 1536
    return pl.pallas_call(
        _ada_body,
        grid=(N // tn,),
        in_specs=[pl.BlockSpec((B, D), lambda j: (0, 0)),
                  pl.BlockSpec((D, tn), lambda j: (0, j)),
                  pl.BlockSpec((1, tn), lambda j: (0, j))],
        out_specs=pl.BlockSpec((B, tn), lambda j: (0, j)),
        out_shape=jax.ShapeDtypeStruct((B, N), F32),
        compiler_params=_params("arbitrary"),
        name="ada",
    )(c, w, b.reshape(1, N))


TOK_K = (0, 512)
TOK_GLA = (512, 2048)
TOK_GATES = (2048, 4096)
TOK_SMALL = (4096, 4224)
CH_Q = (0, 512)
CH_V = (512, 1024)
CH_IQ = (1024, 1536)
CH_IW = (1536, 1552)
IW_SCALE = IDX_HEADS ** -0.5 * IDX_DIM ** -0.5


def _inproj_body(x_ref, sh_ref, sc_ref, wtok_ref, wch_ref,
                 k_ref, gla_ref, gates_ref, ik_ref, glr_ref, qT_ref, vT_ref, iqT_ref, iwT_ref):
    tm = x_ref.shape[1]
    h = (_ln(x_ref[0]) * (1.0 + sc_ref[0]) + sh_ref[0]).astype(BF16)

    def tok(ab):
        return jnp.dot(h, wtok_ref[:, ab[0]:ab[1]], preferred_element_type=F32)

    def ch(ab):
        return lax.dot_general(wch_ref[ab[0]:ab[1], :], h, NT, preferred_element_type=F32)

    kres = tok(TOK_K)
    for p in range(A_WIDTH // LANES):
        k_ref[0, p] = kres[:, p * LANES:(p + 1) * LANES].astype(BF16)
    gla_ref[0] = tok(TOK_GLA).astype(BF16)
    gates_ref[0] = tok(TOK_GATES).astype(BF16)
    small = tok(TOK_SMALL)
    ik_ref[0] = small[:, :IDX_DIM].astype(BF16)
    glr_ref[0] = small[:, IDX_DIM:IDX_DIM + G_RANK]

    qT_ref[0] = (ch(CH_Q) * (A_HEAD_DIM ** -0.5 * LOG2E)).astype(BF16)
    vres = ch(CH_V).astype(BF16)
    for j in range(tm // LANES):
        vT_ref[0, j] = vres[:, j * LANES:(j + 1) * LANES]
    iqT_ref[0] = ch(CH_IQ).astype(BF16)
    iwT_ref[0] = ch(CH_IW) * IW_SCALE


def _inproj_call(x, sh1, sc1, w_tok, w_ch, tm):
    B, S, D = x.shape
    nt = S // tm
    const = lambda b, t: (0, 0)
    out_shape = (
        jax.ShapeDtypeStruct((B, A_WIDTH // LANES, S, LANES), BF16),
        jax.ShapeDtypeStruct((B, S, 1536), BF16),
        jax.ShapeDtypeStruct((B, S, 2048), BF16),
        jax.ShapeDtypeStruct((B, S, IDX_DIM), BF16),
        jax.ShapeDtypeStruct((B, S, G_RANK), F32),
        jax.ShapeDtypeStruct((B, A_WIDTH, S), BF16),
        jax.ShapeDtypeStruct((B, S // LANES, A_WIDTH, LANES), BF16),
        jax.ShapeDtypeStruct((B, IDX_HEADS * IDX_DIM, S), BF16),
        jax.ShapeDtypeStruct((B, IDX_HEADS, S), F32),
    )
    out_specs = (
        pl.BlockSpec((1, A_WIDTH // LANES, tm, LANES), lambda b, t: (b, 0, t, 0)),
        pl.BlockSpec((1, tm, 1536), lambda b, t: (b, t, 0)),
        pl.BlockSpec((1, tm, 2048), lambda b, t: (b, t, 0)),
        pl.BlockSpec((1, tm, IDX_DIM), lambda b, t: (b, t, 0)),
        pl.BlockSpec((1, tm, G_RANK), lambda b, t: (b, t, 0)),
        pl.BlockSpec((1, A_WIDTH, tm), lambda b, t: (b, 0, t)),
        pl.BlockSpec((1, tm // LANES, A_WIDTH, LANES), lambda b, t: (b, t, 0, 0)),
        pl.BlockSpec((1, IDX_HEADS * IDX_DIM, tm), lambda b, t: (b, 0, t)),
        pl.BlockSpec((1, IDX_HEADS, tm), lambda b, t: (b, 0, t)),
    )
    return pl.pallas_call(
        _inproj_body,
        grid=(B, nt),
        in_specs=[pl.BlockSpec((1, tm, D), lambda b, t: (b, t, 0)),
                  pl.BlockSpec((1, 1, D), lambda b, t: (b, 0, 0)),
                  pl.BlockSpec((1, 1, D), lambda b, t: (b, 0, 0)),
                  pl.BlockSpec(w_tok.shape, const),
                  pl.BlockSpec(w_ch.shape, const)],
        out_specs=out_specs,
        out_shape=out_shape,
        compiler_params=_params("parallel", "parallel"),
        name="inproj",
    )(x, sh1, sc1, w_tok, w_ch)


IDX_CHUNK = 256
CNT_CHUNK = 512
SORT_GROUP = 4
ATT_CHUNK = 256
TBL_PAD = 2 * QBLK
ACC_ROWS = A_HEAD_DIM + 16


def _rel_bucket_table():
    s = np.arange(2 * QBLK)[:, None]
    t = np.arange(QBLK)[None, :]
    dist = np.maximum(t + QBLK - s, 0)
    max_exact = REL_BUCKETS // 2
    d_f = np.maximum(dist, 1).astype(np.float32)
    large = max_exact + (np.log(d_f / max_exact) / math.log(REL_MAX_DIST / max_exact)
                         * (REL_BUCKETS - max_exact)).astype(np.int32)
    large = np.minimum(large, REL_BUCKETS - 1)
    return np.where(dist < max_exact, dist, large).astype(np.int32)


def _far_bucket():
    max_exact = REL_BUCKETS // 2
    v = max_exact + int(np.float32(np.log(np.float32(QBLK + 1) / max_exact) / math.log(REL_MAX_DIST / max_exact)
                                   * (REL_BUCKETS - max_exact)))
    assert min(v, REL_BUCKETS - 1) == REL_BUCKETS - 1
    return REL_BUCKETS - 1


def _dsa_body(rb_ref, bkt_ref, ik_ref, kk_ref, vT_ref, qT_ref, iqT_ref, iwT_ref, o_ref,
              sc_s, srt_s, thr_s, madd_s, tbl_s, oT_s, xcut_s, qm_s, sa_s, sb_s, acc_s, mall_s, mblk_s,
              *, topk, idx_bits, max_cnt):
    i = pl.program_id(1)
    nck = (i + 2) // 2
    t_idx = i * QBLK + lax.broadcasted_iota(I32, (1, QBLK), 1)

    @pl.when(i == 0)
    def _():
        bkt = bkt_ref[...]
        tbl_s[...] = jnp.zeros_like(tbl_s)
        for h in range(A_HEADS):
            t = jnp.zeros((2 * QBLK, QBLK), F32)
            for k in range(REL_BUCKETS):
                t = jnp.where(bkt == k, rb_ref[k, h], t)
            tbl_s[h, TBL_PAD:TBL_PAD + 2 * QBLK, :] = (t - rb_ref[_far_bucket(), h]) * LOG2E

    def key_to_float(key):
        key = jnp.maximum(key, KEY_NEG_INF)
        return pltpu.bitcast(jnp.where(key < 0, key ^ INT_MAX, key), F32)

    def score_chunk(c, carry):
        s0 = pl.multiple_of(c * IDX_CHUNK, IDX_CHUNK)
        kc = ik_ref[0, pl.ds(s0, IDX_CHUNK), :]
        acc = jnp.zeros((IDX_CHUNK, QBLK), F32)
        for hp in range(IDX_HEADS // 2):
            r0 = hp * 2 * IDX_DIM
            rhs = jnp.concatenate([iqT_ref[0, r0:r0 + IDX_DIM, :],
                                   iqT_ref[0, r0 + IDX_DIM:r0 + 2 * IDX_DIM, :]], axis=1)
            z = jnp.dot(kc, rhs, preferred_element_type=F32)
            acc = acc + jnp.maximum(z[:, :QBLK], 0.0) * iwT_ref[0, 2 * hp:2 * hp + 1, :]
            acc = acc + jnp.maximum(z[:, QBLK:], 0.0) * iwT_ref[0, 2 * hp + 1:2 * hp + 2, :]
        s_idx = s0 + lax.broadcasted_iota(I32, (IDX_CHUNK, QBLK), 0)
        sc_s[pl.ds(s0, IDX_CHUNK), :] = jnp.where(s_idx <= t_idx, acc, -jnp.inf)
        return carry

    ncnt = (nck + 1) // 2
    lax.fori_loop(0, ncnt, lambda c, carry: score_chunk(2 * c + 1, score_chunk(2 * c, carry)), 0)

    def count(pred):
        def body(c, cnt):
            s0 = pl.multiple_of(c * CNT_CHUNK, CNT_CHUNK)
            k = sc_s[pl.ds(s0, CNT_CHUNK), :]
            s_idx = s0 + lax.broadcasted_iota(I32, (CNT_CHUNK, QBLK), 0)
            m = jnp.where(pred(k, s_idx), 1, 0)
            return cnt + jnp.sum(m.reshape(CNT_CHUNK // 8, 8, QBLK), axis=0)
        cnt = lax.fori_loop(0, ncnt, body, jnp.zeros((8, QBLK), I32))
        return jnp.sum(cnt, axis=0, keepdims=True)

    def search_block(n):
        groups = n * CNT_CHUNK // (8 * SORT_GROUP)

        for g in range(groups):
            v = [sc_s[(SORT_GROUP * g + u) * 8:(SORT_GROUP * g + u + 1) * 8, :] for u in range(SORT_GROUP)]
            for a, b in ((0, 1), (2, 3), (0, 2), (1, 3), (1, 2)):
                v[a], v[b] = jnp.maximum(v[a], v[b]), jnp.minimum(v[a], v[b])
            for u in range(SORT_GROUP):
                srt_s[(SORT_GROUP * g + u) * 8:(SORT_GROUP * g + u + 1) * 8, :] = v[u]

        def count_ge(cand_key):
            cand = key_to_float(cand_key)
            parts = []
            for g in range(groups):
                cnt = 0
                for u in range(SORT_GROUP):
                    tile = srt_s[(SORT_GROUP * g + u) * 8:(SORT_GROUP * g + u + 1) * 8, :]
                    cnt = jnp.where(tile >= cand, u + 1, cnt)
                parts.append(cnt)
            while len(parts) > 1:
                odd = parts[len(parts) & ~1:]
                parts = [parts[j] + parts[j + 1] for j in range(0, len(parts) - 1, 2)] + odd
            return jnp.sum(parts[0], axis=0, keepdims=True)

        c0 = count_ge(jnp.zeros((1, QBLK), I32))
        ok = c0 >= topk
        T = jnp.where(ok, 0, INT_MIN).astype(I32)
        cnt_T = jnp.where(ok, c0, n * CNT_CHUNK)

        def bit_body(j, carry):
            T, cnt_T = carry
            cand = T | jnp.left_shift(jnp.int32(1), 30 - j)
            c = count_ge(cand)
            ok = c >= topk
            return jnp.where(ok, cand, T), jnp.where(ok, c, cnt_T)

        T, cnt_T = lax.fori_loop(0, 31, bit_body, (T, cnt_T))
        thr_s[0:1, :] = T
        thr_s[1:2, :] = cnt_T
        T = jnp.maximum(T, KEY_NEG_INF)
        thr_s[2:3, :] = count_ge(jnp.where(T == INT_MAX, T, T + 1))

    for n in range(1, max_cnt + 1):
        pl.when(ncnt == n)(functools.partial(search_block, n))
    T_key = jnp.maximum(thr_s[0:1, :], KEY_NEG_INF)
    T = key_to_float(T_key)
    cnt_ge = thr_s[1:2, :]
    cnt_gt = thr_s[2:3, :]

    need = topk - cnt_gt
    excess = jnp.where((cnt_ge - cnt_gt > need) & (T_key > KEY_NEG_INF), 1.0, 0.0)
    xcut_s[...] = jnp.full((1, QBLK), INT_MAX, I32)

    @pl.when(jnp.max(excess) > 0.0)
    def _():
        X = jnp.zeros((1, QBLK), I32)
        for b in range(idx_bits - 1, -1, -1):
            cand = X | (1 << b)
            f = count(lambda k, s: (k == T) & (s < cand))
            X = jnp.where(f < need, cand, X)
        xcut_s[...] = X

    xcut = xcut_s[...]

    def mask_chunk(c, carry):
        s0 = pl.multiple_of(c * IDX_CHUNK, IDX_CHUNK)
        k = sc_s[pl.ds(s0, IDX_CHUNK), :]
        s_idx = s0 + lax.broadcasted_iota(I32, (IDX_CHUNK, QBLK), 0)
        sel = ((k > T) | ((k == T) & (s_idx <= xcut))) & (s_idx <= t_idx)
        madd_s[pl.ds(s0, IDX_CHUNK), :] = jnp.where(sel, 0.0, NEG)
        return carry

    lax.fori_loop(0, nck, mask_chunk, 0)

    c_last = i // 2
    even = 1 - (i - 2 * c_last)
    row_head = lax.broadcasted_iota(I32, (LANES, QBLK), 0) // A_HEAD_DIM
    for h in range(A_HEADS):
        qp = qT_ref[0, (h // 2) * LANES:(h // 2 + 1) * LANES, :]
        qm_s[h] = jnp.where(row_head == h % 2, qp, jnp.zeros_like(qp))
    acc_s[...] = jnp.zeros_like(acc_s)
    ones = jnp.ones((ACC_ROWS - A_HEAD_DIM, ATT_CHUNK), BF16)

    def logits(c, s_buf):
        s0 = pl.multiple_of(c * ATT_CHUNK, ATT_CHUNK)
        madd = madd_s[pl.ds(s0, ATT_CHUNK), :]
        off = jnp.where(c == c_last, 2 * QBLK + QBLK * even, jnp.where(c == c_last - 1, QBLK * even, 0))
        off = pl.multiple_of(off, QBLK)
        m_blk = []
        for h in range(A_HEADS):
            kc = kk_ref[0, h // 2, pl.ds(s0, ATT_CHUNK), :]
            s = jnp.dot(kc, qm_s[h], preferred_element_type=F32) + madd + tbl_s[h, pl.ds(off, ATT_CHUNK), :]
            s_buf[h] = s
            m_blk.append(jnp.max(s, axis=0, keepdims=True))
        return jnp.concatenate(m_blk, axis=0)

    def accumulate(c, s_buf, m_all, m_blk):
        m_new = jnp.maximum(m_all, m_blk)
        alpha = jnp.exp2(m_all - m_new)
        for h in range(A_HEADS):
            rows = slice(h * A_HEAD_DIM, (h + 1) * A_HEAD_DIM)
            p = jnp.exp2(s_buf[h] - m_new[h:h + 1]).astype(BF16)
            vt = jnp.concatenate([vT_ref[0, 2 * c + u, rows, :] for u in range(ATT_CHUNK // QBLK)], axis=1)
            vt = jnp.concatenate([vt, ones], axis=0)
            acc_s[h] = alpha[h:h + 1] * acc_s[h] + jnp.dot(vt, p, preferred_element_type=F32)
        return m_new

    def att_body(pair, carry):
        m_all, m_blk = carry
        c = 2 * pair
        m_b = logits(c + 1, sb_s)
        m_all = accumulate(c, sa_s, m_all, m_blk)
        m_a = logits(c + 2, sa_s)
        return accumulate(c + 1, sb_s, m_all, m_b), m_a

    n_pairs = c_last // 2
    carry = (jnp.full((A_HEADS, QBLK), NEG, F32), logits(0, sa_s))
    m_all, m_blk = lax.fori_loop(0, n_pairs, att_body, carry)
    mall_s[...] = m_all
    mblk_s[...] = m_blk

    @pl.when(c_last % 2 == 1)
    def _():
        m_b = logits(c_last, sb_s)
        m_all = accumulate(c_last - 1, sa_s, mall_s[...], mblk_s[...])
        accumulate(c_last, sb_s, m_all, m_b)

    @pl.when(c_last % 2 == 0)
    def _():
        accumulate(c_last, sa_s, mall_s[...], mblk_s[...])

    for h in range(A_HEADS):
        rows = slice(h * A_HEAD_DIM, (h + 1) * A_HEAD_DIM)
        oT_s[rows, :] = acc_s[h, 0:A_HEAD_DIM, :] / acc_s[h, A_HEAD_DIM:A_HEAD_DIM + 1, :]
    o_ref[0] = oT_s[...].T.astype(BF16)


def _dsa_call(rel_bias, ik, kk, vT, qT, iqT, iwT):
    B, S, _ = ik.shape
    nb = S // QBLK
    topk = min(TOPK_MAX, S // 4)
    bkt = jnp.asarray(_rel_bucket_table())
    body = functools.partial(_dsa_body, topk=topk, idx_bits=int(math.log2(S)), max_cnt=S // CNT_CHUNK)
    return pl.pallas_call(
        body,
        grid=(B, nb),
        in_specs=[pl.BlockSpec(memory_space=pltpu.SMEM),
                  pl.BlockSpec((2 * QBLK, QBLK), lambda b, i: (0, 0)),
                  pl.BlockSpec((1, S, IDX_DIM), lambda b, i: (b, 0, 0)),
                  pl.BlockSpec((1, A_WIDTH // LANES, S, LANES), lambda b, i: (b, 0, 0, 0)),
                  pl.BlockSpec((1, S // LANES, A_WIDTH, LANES), lambda b, i: (b, 0, 0, 0)),
                  pl.BlockSpec((1, A_WIDTH, QBLK), lambda b, i: (b, 0, i)),
                  pl.BlockSpec((1, IDX_HEADS * IDX_DIM, QBLK), lambda b, i: (b, 0, i)),
                  pl.BlockSpec((1, IDX_HEADS, QBLK), lambda b, i: (b, 0, i))],
        out_specs=pl.BlockSpec((1, QBLK, A_WIDTH), lambda b, i: (b, i, 0)),
        out_shape=jax.ShapeDtypeStruct((B, S, A_WIDTH), BF16),
        scratch_shapes=[pltpu.VMEM((S, QBLK), F32),
                        pltpu.VMEM((S, QBLK), F32),
                        pltpu.VMEM((8, QBLK), I32),
                        pltpu.VMEM((S, QBLK), F32),
                        pltpu.VMEM((A_HEADS, TBL_PAD + 3 * QBLK, QBLK), F32),
                        pltpu.VMEM((A_WIDTH, QBLK), F32),
                        pltpu.VMEM((1, QBLK), I32),
                        pltpu.VMEM((A_HEADS, LANES, QBLK), BF16),
                        pltpu.VMEM((A_HEADS, ATT_CHUNK, QBLK), F32),
                        pltpu.VMEM((A_HEADS, ATT_CHUNK, QBLK), F32),
                        pltpu.VMEM((A_HEADS, ACC_ROWS, QBLK), F32),
                        pltpu.VMEM((A_HEADS, QBLK), F32),
                        pltpu.VMEM((A_HEADS, QBLK), F32)],
        compiler_params=_params("parallel", "arbitrary"),
        name="dsa",
    )(rel_bias, bkt, ik, kk, vT, qT, iqT, iwT)


GLA_Q = (0, 256)
GLA_K = (256, 512)
GLA_V = (512, 1024)
GLA_R = (1024, 1536)


def _gla_body(gla_ref, glr_ref, wg_ref, bg_ref, ng_ref, o_ref, st_s):
    tg = gla_ref.shape[1]
    C = G_CHUNK

    @pl.when(pl.program_id(1) == 0)
    def _():
        st_s[...] = jnp.zeros_like(st_s)

    glr = glr_ref[0]
    wg = wg_ref[...]
    glr_hi, wg_hi = glr.astype(BF16), wg.astype(BF16)
    glr_lo = (glr - glr_hi.astype(F32)).astype(BF16)
    wg_lo = (wg - wg_hi.astype(F32)).astype(BF16)
    xg = (jnp.dot(glr_hi, wg_hi, preferred_element_type=F32) + jnp.dot(glr_hi, wg_lo, preferred_element_type=F32)
          + jnp.dot(glr_lo, wg_hi, preferred_element_type=F32)) + bg_ref[...]
    logg = -(jnp.maximum(-xg, 0.0) + jnp.log1p(jnp.exp(-jnp.abs(xg)))) * (1.0 / G_TAU)

    rt = lax.broadcasted_iota(I32, (tg, tg), 0)
    ct = lax.broadcasted_iota(I32, (tg, tg), 1)
    cum = jnp.where((rt // C == ct // C) & (rt >= ct), 1.0, 0.0).astype(BF16)
    logg_hi = logg.astype(BF16)
    logg_lo = (logg - logg_hi.astype(F32)).astype(BF16)
    bc_all = (jnp.dot(cum, logg_hi, preferred_element_type=F32)
              + jnp.dot(cum, logg_lo, preferred_element_type=F32))

    bl_all = jnp.concatenate([jnp.broadcast_to(bc_all[(ck + 1) * C - 1:(ck + 1) * C, :], (C, G_KW))
                              for ck in range(tg // C)], axis=0)

    q_all = gla_ref[0, :, GLA_Q[0]:GLA_Q[1]].astype(F32) * (G_DK ** -0.5)
    k_all = gla_ref[0, :, GLA_K[0]:GLA_K[1]].astype(F32)
    q_in_all = (q_all * jnp.exp(bc_all)).astype(BF16)
    k_st_all = (k_all * jnp.exp(bl_all - bc_all)).astype(BF16)
    q_rel_all = q_all * jnp.exp(bc_all - bl_all)
    decay_all = jnp.exp(bl_all)

    ri = lax.broadcasted_iota(I32, (C, C), 0)
    ci = lax.broadcasted_iota(I32, (C, C), 1)
    tril = ri >= ci
    lane_head = lax.broadcasted_iota(I32, (C, LANES), 1) // G_DK
    st_rows = lax.broadcasted_iota(I32, (2 * G_DV, LANES), 0) // G_DV
    st_cols = lax.broadcasted_iota(I32, (2 * G_DV, LANES), 1) // G_DK
    st_diag = st_rows == st_cols
    n_ck = tg // C
    n_p = G_HEADS // 2
    units = [(ck, p) for ck in range(n_ck) for p in range(n_p)]
    rows = lambda ck: slice(ck * C, (ck + 1) * C)
    lanes = lambda p: slice(p * LANES, (p + 1) * LANES)
    v_of = lambda ck, p: gla_ref[0, rows(ck), GLA_V[0] + p * 2 * G_DV:GLA_V[0] + (p + 1) * 2 * G_DV]

    att = {}
    for ck, p in units:
        k_st = k_st_all[rows(ck), lanes(p)]
        for sub in range(2):
            qm = jnp.where(lane_head == sub, q_rel_all[rows(ck), lanes(p)], 0.0).astype(BF16)
            a = lax.dot_general(qm, k_st, NT, preferred_element_type=F32)
            att[ck, p, sub] = jnp.where(tril, a, 0.0).astype(BF16)
    o_intra = {}
    uT = {}
    for ck, p in units:
        v = v_of(ck, p)
        for sub in range(2):
            o_intra[ck, p, sub] = jnp.dot(att[ck, p, sub], v[:, sub * G_DV:(sub + 1) * G_DV],
                                          preferred_element_type=F32)
        uT[ck, p] = lax.dot_general(v, k_st_all[rows(ck), lanes(p)], TN, preferred_element_type=F32)
    o_inter = {}
    for p in range(n_p):
        st = st_s[p]
        for ck in range(n_ck):
            o_inter[ck, p] = lax.dot_general(q_in_all[rows(ck), lanes(p)], st.astype(BF16), NT,
                                             preferred_element_type=F32)
            st = st * decay_all[ck * C:ck * C + 1, lanes(p)] + jnp.where(st_diag, uT[ck, p], 0.0)
        st_s[p] = st
    for ck, p in units:
        for sub in range(2):
            hd = 2 * p + sub
            o = o_intra[ck, p, sub] + o_inter[ck, p][:, sub * G_DV:(sub + 1) * G_DV]
            y = _ln(o) * ng_ref[:, hd * G_DV:(hd + 1) * G_DV]
            g = gla_ref[0, rows(ck), GLA_R[0] + hd * G_DV:GLA_R[0] + (hd + 1) * G_DV].astype(F32)
            o_ref[0, rows(ck), hd * G_DV:(hd + 1) * G_DV] = (y * (g * _sigmoid(g))).astype(BF16)


def _gla_call(gla, glr, wg, bg, ng, tg):
    B, S, _ = gla.shape
    const = lambda b, j: (0, 0)
    return pl.pallas_call(
        _gla_body,
        grid=(B, S // tg),
        in_specs=[pl.BlockSpec((1, tg, 1536), lambda b, j: (b, j, 0)),
                  pl.BlockSpec((1, tg, G_RANK), lambda b, j: (b, j, 0)),
                  pl.BlockSpec((G_RANK, G_KW), const),
                  pl.BlockSpec((1, G_KW), const),
                  pl.BlockSpec((1, G_VW), const)],
        out_specs=pl.BlockSpec((1, tg, G_VW), lambda b, j: (b, j, 0)),
        out_shape=jax.ShapeDtypeStruct((B, S, G_VW), BF16),
        scratch_shapes=[pltpu.VMEM((G_HEADS // 2, 2 * G_DV, LANES), F32)],
        compiler_params=_params("parallel", "arbitrary"),
        name="gla",
    )(gla, glr, wg, bg, ng)


ROUTER_ROWS = 40
ROUTER_E0 = 8


def _post_body(oa_ref, ob_ref, gates_ref, x_ref, gt1_ref, sh2_ref, sc2_ref, wa_ref, wb_ref, wo_ref,
               g1_ref, b1_ref, wr_ref, br_ref, x1_ref, h2_ref, gate_ref, grp_ref):
    tm = x_ref.shape[1]
    D = x_ref.shape[2]
    ya = jnp.dot(oa_ref[0], wa_ref[...], preferred_element_type=F32)
    yb = jnp.dot(ob_ref[0], wb_ref[...], preferred_element_type=F32)
    ga = gates_ref[0, :, 0:D].astype(F32)
    gb = gates_ref[0, :, D:2 * D].astype(F32)
    merged = _sigmoid(ga) * ya + _sigmoid(gb) * yb
    y = jnp.dot(merged.astype(BF16), wo_ref[...], preferred_element_type=F32)
    x1 = _ln(DN_ALPHA * x_ref[0] + gt1_ref[0] * y) * g1_ref[...] + b1_ref[...]
    x1_ref[0] = x1
    h2 = _ln(x1) * (1.0 + sc2_ref[0]) + sh2_ref[0]
    h2_hi = h2.astype(BF16)
    h2_ref[0] = h2_hi

    h2_lo = (h2 - h2_hi.astype(F32)).astype(BF16)
    wr = wr_ref[...]
    wr_hi = wr.astype(BF16)
    wr_lo = (wr - wr_hi.astype(F32)).astype(BF16)
    lt = (lax.dot_general(wr_hi, h2_hi, NT, preferred_element_type=F32)
          + lax.dot_general(wr_hi, h2_lo, NT, preferred_element_type=F32)
          + lax.dot_general(wr_lo, h2_hi, NT, preferred_element_type=F32)) + br_ref[...]
    gl = lt[0:N_GROUPS]
    gmax = jnp.max(gl, axis=0, keepdims=True)
    g_w = 1.0 / jnp.sum(jnp.exp(gl - gmax), axis=0, keepdims=True)
    r4 = lax.broadcasted_iota(I32, (N_GROUPS, tm), 0)
    g_idx = jnp.min(jnp.where(gl == gmax, r4, N_GROUPS), axis=0, keepdims=True)
    eg = jnp.zeros((EXPERTS_PER_GROUP, tm), F32)
    for g in range(N_GROUPS):
        lo = ROUTER_E0 + g * EXPERTS_PER_GROUP
        eg = jnp.where(g_idx == g, lt[lo:lo + EXPERTS_PER_GROUP], eg)
    r8 = lax.broadcasted_iota(I32, (EXPERTS_PER_GROUP, tm), 0)
    e1 = jnp.max(eg, axis=0, keepdims=True)
    i1 = jnp.min(jnp.where(eg == e1, r8, EXPERTS_PER_GROUP), axis=0, keepdims=True)
    eg2 = jnp.where(r8 == i1, -jnp.inf, eg)
    e2 = jnp.max(eg2, axis=0, keepdims=True)
    i2 = jnp.min(jnp.where(eg2 == e2, r8, EXPERTS_PER_GROUP), axis=0, keepdims=True)
    d = jnp.exp(e2 - e1)
    w1 = g_w / (1.0 + d)
    w2 = g_w * d / (1.0 + d)
    in_group = jnp.where(r8 == i1, w1, 0.0) + jnp.where(r8 == i2, w2, 0.0)
    blocks = [jnp.where(g_idx == g, in_group, 0.0) for g in range(N_GROUPS)]
    blocks.append(jnp.zeros((LANES - N_EXPERTS, tm), F32))
    gate_ref[...] = jnp.concatenate(blocks, axis=0).T
    r8g = lax.broadcasted_iota(I32, (8, tm), 0)
    grp_ref[...] = jnp.where(r8g == g_idx, 1.0, 0.0)


def _post_call(o_a, o_b, gates, x, gt1, sh2, sc2, wa, wb, wo, g1, b1, wr, br, tm):
    B, S, D = x.shape
    nt = S // tm
    const = lambda b, t: (0, 0)
    row = lambda b, t: (b, 0, 0)
    tile = lambda b, t: (b, t, 0)
    return pl.pallas_call(
        _post_body,
        grid=(B, nt),
        in_specs=[pl.BlockSpec((1, tm, A_WIDTH), tile),
                  pl.BlockSpec((1, tm, G_VW), tile),
                  pl.BlockSpec((1, tm, 2 * D), tile),
                  pl.BlockSpec((1, tm, D), tile),
                  pl.BlockSpec((1, 1, D), row),
                  pl.BlockSpec((1, 1, D), row),
                  pl.BlockSpec((1, 1, D), row),
                  pl.BlockSpec(wa.shape, const),
                  pl.BlockSpec(wb.shape, const),
                  pl.BlockSpec(wo.shape, const),
                  pl.BlockSpec((1, D), const),
                  pl.BlockSpec((1, D), const),
                  pl.BlockSpec(wr.shape, const),
                  pl.BlockSpec(br.shape, const)],
        out_specs=(pl.BlockSpec((1, tm, D), tile),
                   pl.BlockSpec((1, tm, D), tile),
                   pl.BlockSpec((tm, LANES), lambda b, t: (b * nt + t, 0)),
                   pl.BlockSpec((8, tm), lambda b, t: (0, b * nt + t))),
        out_shape=(jax.ShapeDtypeStruct((B, S, D), F32),
                   jax.ShapeDtypeStruct((B, S, D), BF16),
                   jax.ShapeDtypeStruct((B * S, LANES), F32),
                   jax.ShapeDtypeStruct((8, B * S), F32)),
        compiler_params=_params("parallel", "parallel"),
        name="post",
    )(o_a, o_b, gates, x, gt1, sh2, sc2, wa, wb, wo, g1, b1, wr, br)


MOE_EXPERTS_PER_STEP = 4
MOE_ROW_BLOCK = 128
MOE_PERM_ROWS = 256


def _moe_body(h2_ref, gate_ref, grp_ref, x1_ref, gt2_ref, w1_ref, w3_ref, w2_ref, g2_ref, b2_ref, o_ref,
              perm_s, hs_s, gs_s, ys_s, tri_s, seg_s):
    t = pl.program_id(0)
    s = pl.program_id(1)
    ne = MOE_EXPERTS_PER_STEP
    rb = MOE_ROW_BLOCK
    tm = h2_ref.shape[0]
    tm_pad = perm_s.shape[0]
    steps_per_group = EXPERTS_PER_GROUP // ne

    @pl.when((t == 0) & (s == 0))
    def _():
        r = lax.broadcasted_iota(I32, (tm, tm), 0)
        c = lax.broadcasted_iota(I32, (tm, tm), 1)
        tri_s[...] = jnp.where(r < c, 1.0, 0.0).astype(BF16)

    @pl.when(s == 0)
    def _():
        oh = grp_ref[...]
        rank = jnp.dot(oh.astype(BF16), tri_s[...], preferred_element_type=F32)
        cnt = jnp.sum(oh, axis=1, keepdims=True)
        blocks = jnp.floor((cnt + (rb - 1)) * (1.0 / rb))
        off = jnp.zeros((1, 1), F32)
        dest = jnp.zeros((1, tm), F32)
        for g in range(N_GROUPS):
            seg_s[2 * g] = jnp.sum(off).astype(I32)
            seg_s[2 * g + 1] = jnp.sum(blocks[g:g + 1, :]).astype(I32)
            dest = dest + oh[g:g + 1, :] * (off + rank[g:g + 1, :])
            off = off + blocks[g:g + 1, :] * rb
        dest_i = dest.astype(I32)
        gate = gate_ref[...]
        g_hi = gate.astype(BF16)
        g_lo = (gate - g_hi.astype(F32)).astype(BF16)
        h2 = h2_ref[...]
        for c in range(tm_pad // MOE_PERM_ROWS):
            rows = slice(c * MOE_PERM_ROWS, (c + 1) * MOE_PERM_ROWS)
            d_idx = c * MOE_PERM_ROWS + lax.broadcasted_iota(I32, (MOE_PERM_ROWS, tm), 0)
            perm = jnp.where(d_idx == dest_i, 1.0, 0.0).astype(BF16)
            perm_s[rows, :] = perm
            hs_s[rows, :] = jnp.dot(perm, h2, preferred_element_type=F32).astype(BF16)
            gs_s[rows, :] = (jnp.dot(perm, g_hi, preferred_element_type=F32)
                             + jnp.dot(perm, g_lo, preferred_element_type=F32))
        ys_s[...] = jnp.zeros_like(ys_s)

    g = s // steps_per_group
    row0 = seg_s[2 * g]
    nblk = seg_s[2 * g + 1]

    def block(i, carry):
        r0 = pl.multiple_of(row0 + i * rb, rb)
        x = hs_s[pl.ds(r0, rb), :]
        gsel = pltpu.roll(gs_s[pl.ds(r0, rb), :], (LANES - s * ne) % LANES, axis=1)
        hid = []
        for j in range(ne):
            a = jnp.dot(x, w1_ref[j], preferred_element_type=F32)
            b = jnp.dot(x, w3_ref[j], preferred_element_type=F32)
            hid.append((a * _sigmoid(a) * b * gsel[:, j:j + 1]).astype(BF16))
        ys_s[pl.ds(r0, rb), :] += jnp.dot(jnp.concatenate(hid, axis=1), w2_ref[...], preferred_element_type=F32)
        return carry

    lax.fori_loop(0, nblk, block, 0)

    @pl.when(s == pl.num_programs(1) - 1)
    def _():
        perm = perm_s[...]
        for c in range(o_ref.shape[1] // MOE_PERM_ROWS):
            cols = slice(c * MOE_PERM_ROWS, (c + 1) * MOE_PERM_ROWS)
            o_ref[:, cols] = lax.dot_general(perm, ys_s[:, cols].astype(BF16), TN, preferred_element_type=F32)
        z = DN_ALPHA * x1_ref[...] + gt2_ref[0] * o_ref[...]
        o_ref[...] = _ln(z) * g2_ref[...] + b2_ref[...]


def _moe_call(h2, gate, grp, x1, gt2, w1, w3, w2, g2, b2, tm, S):
    T, D = h2.shape
    ne = MOE_EXPERTS_PER_STEP
    nc = N_EXPERTS // ne
    tm_pad = tm + N_GROUPS * MOE_ROW_BLOCK
    tiles_per_seq = S // tm
    tile = lambda t, c: (t, 0)
    const = lambda t, c: (0, 0)
    return pl.pallas_call(
        _moe_body,
        grid=(T // tm, nc),
        in_specs=[pl.BlockSpec((tm, D), tile),
                  pl.BlockSpec((tm, LANES), tile),
                  pl.BlockSpec((8, tm), lambda t, c: (0, t)),
                  pl.BlockSpec((tm, D), tile),
                  pl.BlockSpec((1, 1, D), lambda t, c: (t // tiles_per_seq, 0, 0)),
                  pl.BlockSpec((ne, D, D_EXPERT), lambda t, c: (c, 0, 0)),
                  pl.BlockSpec((ne, D, D_EXPERT), lambda t, c: (c, 0, 0)),
                  pl.BlockSpec((ne * D_EXPERT, D), lambda t, c: (c, 0)),
                  pl.BlockSpec((1, D), const),
                  pl.BlockSpec((1, D), const)],
        out_specs=pl.BlockSpec((tm, D), tile),
        out_shape=jax.ShapeDtypeStruct((T, D), F32),
        scratch_shapes=[pltpu.VMEM((tm_pad, tm), BF16),
                        pltpu.VMEM((tm_pad, D), BF16),
                        pltpu.VMEM((tm_pad, LANES), F32),
                        pltpu.VMEM((tm_pad, D), F32),
                        pltpu.VMEM((tm, tm), BF16),
                        pltpu.SMEM((2 * N_GROUPS,), I32)],
        compiler_params=_params("arbitrary", "arbitrary"),
        name="moe",
    )(h2, gate, grp, x1, gt2, w1, w3, w2, g2, b2)


def _pick(n, pref):
    return pref if n % pref == 0 else n


def kernel(x, c, rel_bias, w_ada, b_ada, w_in, gla_w_gate, gla_b_gate, gla_norm_g, w_branch_a, w_branch_b, w_out, ln1_g, ln1_b, w_router_group, b_router_group, w_router_expert, b_router_expert, w_exp_gate, w_exp_up, w_exp_down, ln2_g, ln2_b):
    B, S, D = x.shape
    assert S % (2 * QBLK) == 0 and D == 1024 and w_ada.shape[0] == DEPTH == 1
    l = 0

    ada = _ada_call(c, w_ada[l], b_ada[l])
    sh1, sc1, gt1, sh2, sc2, gt2 = [ada[:, i * D:(i + 1) * D].reshape(B, 1, D) for i in range(6)]

    offs = np.concatenate([[0], np.cumsum(SPLIT_SIZES)])
    seg = lambda i: w_in[l][:, offs[i]:offs[i + 1]]
    (w_aq, w_ak, w_av, w_iq, w_ik, w_iw, w_gq, w_gk, w_gv, w_gr, w_glr, w_ga, w_gb) = [seg(i) for i in range(13)]
    pad = jnp.zeros((D, TOK_SMALL[1] - TOK_SMALL[0] - IDX_DIM - G_RANK), F32)
    w_tok = jnp.concatenate([w_ak, w_gq, w_gk, w_gv, w_gr, w_ga, w_gb, w_ik, w_glr, pad], axis=1).astype(BF16)
    w_ch = jnp.concatenate([w_aq, w_av, w_iq, w_iw], axis=1).T.astype(BF16)

    tm = _pick(S, 512)
    kk, gla, gates, ik, glr, qT, vT, iqT, iwT = _inproj_call(x, sh1, sc1, w_tok, w_ch, tm)

    o_a = _dsa_call(rel_bias, ik, kk, vT, qT, iqT, iwT)
    o_b = _gla_call(gla, glr, gla_w_gate[l], gla_b_gate[l].reshape(1, G_KW), gla_norm_g[l].reshape(1, G_VW),
                    _pick(S, 256))

    wr = jnp.zeros((ROUTER_ROWS, D), F32)
    wr = wr.at[0:N_GROUPS].set(w_router_group[l].T).at[ROUTER_E0:ROUTER_E0 + N_EXPERTS].set(w_router_expert[l].T)
    br = jnp.zeros((ROUTER_ROWS, 1), F32)
    br = br.at[0:N_GROUPS, 0].set(b_router_group[l]).at[ROUTER_E0:ROUTER_E0 + N_EXPERTS, 0].set(b_router_expert[l])
    x1, h2, gate, grp = _post_call(o_a, o_b, gates, x, gt1, sh2, sc2,
                              w_branch_a[l].astype(BF16), w_branch_b[l].astype(BF16), w_out[l].astype(BF16),
                              ln1_g[l].reshape(1, D), ln1_b[l].reshape(1, D), wr, br, tm)

    tm5 = _pick(S, 1024)
    out = _moe_call(h2.reshape(B * S, D), gate, grp, x1.reshape(B * S, D), gt2,
                    w_exp_gate[l].astype(BF16), w_exp_up[l].astype(BF16),
                    w_exp_down[l].astype(BF16).reshape(N_EXPERTS * D_EXPERT, D),
                    ln2_g[l].reshape(1, D), ln2_b[l].reshape(1, D), tm5, S)
    return out.reshape(B, S, D)
```

```python
import functools
import math

import numpy as np
import jax
import jax.numpy as jnp
from jax import lax
from jax.experimental import pallas as pl
from jax.experimental.pallas import tpu as pltpu

F32 = jnp.float32
BF16 = jnp.bfloat16
I32 = jnp.int32
HIGHEST = lax.Precision.HIGHEST

A_HEADS = 8
A_HEAD_DIM = 64
A_WIDTH = A_HEADS * A_HEAD_DIM
IDX_HEADS = 16
IDX_DIM = 32
TOPK_MAX = 256
QBLK = 128
REL_BUCKETS = 32
REL_MAX_DIST = 128
G_HEADS = 4
G_DK = 64
G_DV = 128
G_KW = G_HEADS * G_DK
G_VW = G_HEADS * G_DV
G_RANK = 16
G_TAU = 16.0
G_CHUNK = 64
N_GROUPS = 4
EXPERTS_PER_GROUP = 8
N_EXPERTS = N_GROUPS * EXPERTS_PER_GROUP
D_EXPERT = 256
DEPTH = 1
DN_ALPHA = (2.0 * DEPTH) ** 0.25
LN_EPS = 1e-5
SPLIT_SIZES = (A_WIDTH, A_WIDTH, A_WIDTH, IDX_HEADS * IDX_DIM, IDX_DIM, IDX_HEADS,
               G_KW, G_KW, G_VW, G_VW, G_RANK, 1024, 1024)

LANES = 128
VMEM_LIMIT_BYTES = 56 * 1024 * 1024

NEG = -1e30
LOG2E = math.log2(math.e)
INT_MIN = -2 ** 31
INT_MAX = 2 ** 31 - 1
KEY_NEG_INF = -2 ** 31 + 0x7FFFFF

NT = (((1,), (1,)), ((), ()))
TN = (((0,), (0,)), ((), ()))


def _ln(x):
    mu = jnp.mean(x, axis=-1, keepdims=True)
    xc = x - mu
    var = jnp.mean(xc * xc, axis=-1, keepdims=True)
    return xc * lax.rsqrt(var + LN_EPS)


def _sigmoid(x):
    return 0.5 * jnp.tanh(0.5 * x) + 0.5


def _params(*sem):
    return pltpu.CompilerParams(dimension_semantics=sem, vmem_limit_bytes=VMEM_LIMIT_BYTES)


def _ada_body(c_ref, w_ref, b_ref, o_ref):
    c = c_ref[...]
    cond = c * _sigmoid(c)
    o_ref[...] = jnp.dot(cond, w_ref[...], preferred_element_type=F32, precision=HIGHEST) + b_ref[...]


def _ada_call(c, w, b):
    B, D = c.shape
    N = w.shape[1]
    tn = 1536
    return pl.pallas_call(
        _ada_body,
        grid=(N // tn,),
        in_specs=[pl.BlockSpec((B, D), lambda j: (0, 0)),
                  pl.BlockSpec((D, tn), lambda j: (0, j)),
                  pl.BlockSpec((1, tn), lambda j: (0, j))],
        out_specs=pl.BlockSpec((B, tn), lambda j: (0, j)),
        out_shape=jax.ShapeDtypeStruct((B, N), F32),
        compiler_params=_params("arbitrary"),
        name="ada",
    )(c, w, b.reshape(1, N))


TOK_K = (0, 512)
TOK_GLA = (512, 2048)
TOK_GATES = (2048, 4096)
TOK_SMALL = (4096, 4224)
CH_Q = (0, 512)
CH_V = (512, 1024)
CH_IQ = (1024, 1536)
CH_IW = (1536, 1552)
IW_SCALE = IDX_HEADS ** -0.5 * IDX_DIM ** -0.5


def _inproj_body(x_ref, sh_ref, sc_ref, wtok_ref, wch_ref,
                 k_ref, gla_ref, gates_ref, ik_ref, glr_ref, qT_ref, vT_ref, iqT_ref, iwT_ref):
    tm = x_ref.shape[1]
    h = (_ln(x_ref[0]) * (1.0 + sc_ref[0]) + sh_ref[0]).astype(BF16)

    def tok(ab):
        return jnp.dot(h, wtok_ref[:, ab[0]:ab[1]], preferred_element_type=F32)

    def ch(ab):
        return lax.dot_general(wch_ref[ab[0]:ab[1], :], h, NT, preferred_element_type=F32)

    kres = tok(TOK_K)
    for p in range(A_WIDTH // LANES):
        k_ref[0, p] = kres[:, p * LANES:(p + 1) * LANES].astype(BF16)
    gla_ref[0] = tok(TOK_GLA).astype(BF16)
    gates_ref[0] = tok(TOK_GATES).astype(BF16)
    small = tok(TOK_SMALL)
    ik_ref[0] = small[:, :IDX_DIM].astype(BF16)
    glr_ref[0] = small[:, IDX_DIM:IDX_DIM + G_RANK]

    qT_ref[0] = (ch(CH_Q) * (A_HEAD_DIM ** -0.5 * LOG2E)).astype(BF16)
    vres = ch(CH_V).astype(BF16)
    for j in range(tm // LANES):
        vT_ref[0, j] = vres[:, j * LANES:(j + 1) * LANES]
    iqT_ref[0] = ch(CH_IQ).astype(BF16)
    iwT_ref[0] = ch(CH_IW) * IW_SCALE


def _inproj_call(x, sh1, sc1, w_tok, w_ch, tm):
    B, S, D = x.shape
    nt = S // tm
    const = lambda b, t: (0, 0)
    out_shape = (
        jax.ShapeDtypeStruct((B, A_WIDTH // LANES, S, LANES), BF16),
        jax.ShapeDtypeStruct((B, S, 1536), BF16),
        jax.ShapeDtypeStruct((B, S, 2048), BF16),
        jax.ShapeDtypeStruct((B, S, IDX_DIM), BF16),
        jax.ShapeDtypeStruct((B, S, G_RANK), F32),
        jax.ShapeDtypeStruct((B, A_WIDTH, S), BF16),
        jax.ShapeDtypeStruct((B, S // LANES, A_WIDTH, LANES), BF16),
        jax.ShapeDtypeStruct((B, IDX_HEADS * IDX_DIM, S), BF16),
        jax.ShapeDtypeStruct((B, IDX_HEADS, S), F32),
    )
    out_specs = (
        pl.BlockSpec((1, A_WIDTH // LANES, tm, LANES), lambda b, t: (b, 0, t, 0)),
        pl.BlockSpec((1, tm, 1536), lambda b, t: (b, t, 0)),
        pl.BlockSpec((1, tm, 2048), lambda b, t: (b, t, 0)),
        pl.BlockSpec((1, tm, IDX_DIM), lambda b, t: (b, t, 0)),
        pl.BlockSpec((1, tm, G_RANK), lambda b, t: (b, t, 0)),
        pl.BlockSpec((1, A_WIDTH, tm), lambda b, t: (b, 0, t)),
        pl.BlockSpec((1, tm // LANES, A_WIDTH, LANES), lambda b, t: (b, t, 0, 0)),
        pl.BlockSpec((1, IDX_HEADS * IDX_DIM, tm), lambda b, t: (b, 0, t)),
        pl.BlockSpec((1, IDX_HEADS, tm), lambda b, t: (b, 0, t)),
    )
    return pl.pallas_call(
        _inproj_body,
        grid=(B, nt),
        in_specs=[pl.BlockSpec((1, tm, D), lambda b, t: (b, t, 0)),
                  pl.BlockSpec((1, 1, D), lambda b, t: (b, 0, 0)),
                  pl.BlockSpec((1, 1, D), lambda b, t: (b, 0, 0)),
                  pl.BlockSpec(w_tok.shape, const),
                  pl.BlockSpec(w_ch.shape, const)],
        out_specs=out_specs,
        out_shape=out_shape,
        compiler_params=_params("parallel", "parallel"),
        name="inproj",
    )(x, sh1, sc1, w_tok, w_ch)


IDX_CHUNK = 256
CNT_CHUNK = 256
SORT_GROUP = 4
ATT_CHUNK = 256
TBL_PAD = 2 * QBLK
ACC_ROWS = A_HEAD_DIM + 16


def _rel_bucket_table():
    s = np.arange(2 * QBLK)[:, None]
    t = np.arange(QBLK)[None, :]
    dist = np.maximum(t + QBLK - s, 0)
    max_exact = REL_BUCKETS // 2
    d_f = np.maximum(dist, 1).astype(np.float32)
    large = max_exact + (np.log(d_f / max_exact) / math.log(REL_MAX_DIST / max_exact)
                         * (REL_BUCKETS - max_exact)).astype(np.int32)
    large = np.minimum(large, REL_BUCKETS - 1)
    return np.where(dist < max_exact, dist, large).astype(np.int32)


def _far_bucket():
    max_exact = REL_BUCKETS // 2
    v = max_exact + int(np.float32(np.log(np.float32(QBLK + 1) / max_exact) / math.log(REL_MAX_DIST / max_exact)
                                   * (REL_BUCKETS - max_exact)))
    assert min(v, REL_BUCKETS - 1) == REL_BUCKETS - 1
    return REL_BUCKETS - 1


def _dsa_body(rb_ref, bkt_ref, ik_ref, kk_ref, vT_ref, qT_ref, iqT_ref, iwT_ref, o_ref,
              sc_s, srt_s, thr_s, madd_s, tbl_s, oT_s, xcut_s, qm_s, sa_s, sb_s, acc_s, mall_s, mblk_s,
              *, topk, idx_bits, max_cnt):
    i = pl.program_id(1)
    nck = (i + 2) // 2
    t_idx = i * QBLK + lax.broadcasted_iota(I32, (1, QBLK), 1)

    @pl.when(i == 0)
    def _():
        bkt = bkt_ref[...]
        tbl_s[...] = jnp.zeros_like(tbl_s)
        for h in range(A_HEADS):
            t = jnp.zeros((2 * QBLK, QBLK), F32)
            for k in range(REL_BUCKETS):
                t = jnp.where(bkt == k, rb_ref[k, h], t)
            tbl_s[h, TBL_PAD:TBL_PAD + 2 * QBLK, :] = (t - rb_ref[_far_bucket(), h]) * LOG2E

    def key_to_float(key):
        key = jnp.maximum(key, KEY_NEG_INF)
        return pltpu.bitcast(jnp.where(key < 0, key ^ INT_MAX, key), F32)

    def score_chunk(c, carry):
        s0 = pl.multiple_of(c * IDX_CHUNK, IDX_CHUNK)
        kc = ik_ref[0, pl.ds(s0, IDX_CHUNK), :]
        acc = jnp.zeros((IDX_CHUNK, QBLK), F32)
        for hp in range(IDX_HEADS // 2):
            r0 = hp * 2 * IDX_DIM
            rhs = jnp.concatenate([iqT_ref[0, r0:r0 + IDX_DIM, :],
                                   iqT_ref[0, r0 + IDX_DIM:r0 + 2 * IDX_DIM, :]], axis=1)
            z = jnp.dot(kc, rhs, preferred_element_type=F32)
            acc = acc + jnp.maximum(z[:, :QBLK], 0.0) * iwT_ref[0, 2 * hp:2 * hp + 1, :]
            acc = acc + jnp.maximum(z[:, QBLK:], 0.0) * iwT_ref[0, 2 * hp + 1:2 * hp + 2, :]
        s_idx = s0 + lax.broadcasted_iota(I32, (IDX_CHUNK, QBLK), 0)
        sc_s[pl.ds(s0, IDX_CHUNK), :] = jnp.where(s_idx <= t_idx, acc, -jnp.inf)
        return carry

    ncnt = nck
    lax.fori_loop(0, nck // 2, lambda c, carry: score_chunk(2 * c + 1, score_chunk(2 * c, carry)), 0)

    @pl.when(nck % 2 == 1)
    def _():
        score_chunk(nck - 1, 0)

    def count(pred):
        def body(c, cnt):
            s0 = pl.multiple_of(c * CNT_CHUNK, CNT_CHUNK)
            k = sc_s[pl.ds(s0, CNT_CHUNK), :]
            s_idx = s0 + lax.broadcasted_iota(I32, (CNT_CHUNK, QBLK), 0)
            m = jnp.where(pred(k, s_idx), 1, 0)
            return cnt + jnp.sum(m.reshape(CNT_CHUNK // 8, 8, QBLK), axis=0)
        cnt = lax.fori_loop(0, ncnt, body, jnp.zeros((8, QBLK), I32))
        return jnp.sum(cnt, axis=0, keepdims=True)

    def search_block(n):
        groups = n * CNT_CHUNK // (8 * SORT_GROUP)

        for g in range(groups):
            v = [sc_s[(SORT_GROUP * g + u) * 8:(SORT_GROUP * g + u + 1) * 8, :] for u in range(SORT_GROUP)]
            for a, b in ((0, 1), (2, 3), (0, 2), (1, 3), (1, 2)):
                v[a], v[b] = jnp.maximum(v[a], v[b]), jnp.minimum(v[a], v[b])
            for u in range(SORT_GROUP):
                srt_s[(SORT_GROUP * g + u) * 8:(SORT_GROUP * g + u + 1) * 8, :] = v[u]

        def count_ge(cand_key):
            cand = key_to_float(cand_key)
            parts = []
            for g in range(groups):
                cnt = 0
                for u in range(SORT_GROUP):
                    tile = srt_s[(SORT_GROUP * g + u) * 8:(SORT_GROUP * g + u + 1) * 8, :]
                    cnt = jnp.where(tile >= cand, u + 1, cnt)
                parts.append(cnt)
            while len(parts) > 1:
                odd = parts[len(parts) & ~1:]
                parts = [parts[j] + parts[j + 1] for j in range(0, len(parts) - 1, 2)] + odd
            return jnp.sum(parts[0], axis=0, keepdims=True)

        c0 = count_ge(jnp.zeros((1, QBLK), I32))
        ok = c0 >= topk
        T = jnp.where(ok, 0, INT_MIN).astype(I32)
        cnt_T = jnp.where(ok, c0, n * CNT_CHUNK)

        def bit_body(j, carry):
            T, cnt_T = carry
            cand = T | jnp.left_shift(jnp.int32(1), 30 - j)
            c = count_ge(cand)
            ok = c >= topk
            return jnp.where(ok, cand, T), jnp.where(ok, c, cnt_T)

        T, cnt_T = lax.fori_loop(0, 31, bit_body, (T, cnt_T))
        thr_s[0:1, :] = T
        thr_s[1:2, :] = cnt_T
        T = jnp.maximum(T, KEY_NEG_INF)
        thr_s[2:3, :] = count_ge(jnp.where(T == INT_MAX, T, T + 1))

    for n in range(1, max_cnt + 1):
        pl.when(ncnt == n)(functools.partial(search_block, n))
    T_key = jnp.maximum(thr_s[0:1, :], KEY_NEG_INF)
    T = key_to_float(T_key)
    cnt_ge = thr_s[1:2, :]
    cnt_gt = thr_s[2:3, :]

    need = topk - cnt_gt
    excess = jnp.where((cnt_ge - cnt_gt > need) & (T_key > KEY_NEG_INF), 1.0, 0.0)
    xcut_s[...] = jnp.full((1, QBLK), INT_MAX, I32)

    @pl.when(jnp.max(excess) > 0.0)
    def _():
        X = jnp.zeros((1, QBLK), I32)
        for b in range(idx_bits - 1, -1, -1):
            cand = X | (1 << b)
            f = count(lambda k, s: (k == T) & (s < cand))
            X = jnp.where(f < need, cand, X)
        xcut_s[...] = X

    xcut = xcut_s[...]

    def mask_chunk(c, carry):
        s0 = pl.multiple_of(c * IDX_CHUNK, IDX_CHUNK)
        k = sc_s[pl.ds(s0, IDX_CHUNK), :]
        s_idx = s0 + lax.broadcasted_iota(I32, (IDX_CHUNK, QBLK), 0)
        sel = ((k > T) | ((k == T) & (s_idx <= xcut))) & (s_idx <= t_idx)
        madd_s[pl.ds(s0, IDX_CHUNK), :] = jnp.where(sel, 0.0, NEG)
        return carry

    lax.fori_loop(0, nck, mask_chunk, 0)

    c_last = i // 2
    even = 1 - (i - 2 * c_last)
    row_head = lax.broadcasted_iota(I32, (LANES, QBLK), 0) // A_HEAD_DIM
    for h in range(A_HEADS):
        qp = qT_ref[0, (h // 2) * LANES:(h // 2 + 1) * LANES, :]
        qm_s[h] = jnp.where(row_head == h % 2, qp, jnp.zeros_like(qp))
    acc_s[...] = jnp.zeros_like(acc_s)
    ones = jnp.ones((ACC_ROWS - A_HEAD_DIM, ATT_CHUNK), BF16)

    def logits(c, s_buf):
        s0 = pl.multiple_of(c * ATT_CHUNK, ATT_CHUNK)
        madd = madd_s[pl.ds(s0, ATT_CHUNK), :]
        off = jnp.where(c == c_last, 2 * QBLK + QBLK * even, jnp.where(c == c_last - 1, QBLK * even, 0))
        off = pl.multiple_of(off, QBLK)
        m_blk = []
        for h in range(A_HEADS):
            kc = kk_ref[0, h // 2, pl.ds(s0, ATT_CHUNK), :]
            s = jnp.dot(kc, qm_s[h], preferred_element_type=F32) + madd + tbl_s[h, pl.ds(off, ATT_CHUNK), :]
            s_buf[h] = s
            m_blk.append(jnp.max(s, axis=0, keepdims=True))
        return jnp.concatenate(m_blk, axis=0)

    def accumulate(c, s_buf, m_all, m_blk):
        m_new = jnp.maximum(m_all, m_blk)
        alpha = jnp.exp2(m_all - m_new)
        for h in range(A_HEADS):
            rows = slice(h * A_HEAD_DIM, (h + 1) * A_HEAD_DIM)
            p = jnp.exp2(s_buf[h] - m_new[h:h + 1]).astype(BF16)
            vt = jnp.concatenate([vT_ref[0, 2 * c + u, rows, :] for u in range(ATT_CHUNK // QBLK)], axis=1)
            vt = jnp.concatenate([vt, ones], axis=0)
            acc_s[h] = alpha[h:h + 1] * acc_s[h] + jnp.dot(vt, p, preferred_element_type=F32)
        return m_new

    def att_body(pair, carry):
        m_all, m_blk = carry
        c = 2 * pair
        m_b = logits(c + 1, sb_s)
        m_all = accumulate(c, sa_s, m_all, m_blk)
        m_a = logits(c + 2, sa_s)
        return accumulate(c + 1, sb_s, m_all, m_b), m_a

    n_pairs = c_last // 2
    carry = (jnp.full((A_HEADS, QBLK), NEG, F32), logits(0, sa_s))
    m_all, m_blk = lax.fori_loop(0, n_pairs, att_body, carry)
    mall_s[...] = m_all
    mblk_s[...] = m_blk

    @pl.when(c_last % 2 == 1)
    def _():
        m_b = logits(c_last, sb_s)
        m_all = accumulate(c_last - 1, sa_s, mall_s[...], mblk_s[...])
        accumulate(c_last, sb_s, m_all, m_b)

    @pl.when(c_last % 2 == 0)
    def _():
        accumulate(c_last, sa_s, mall_s[...], mblk_s[...])

    for h in range(A_HEADS):
        rows = slice(h * A_HEAD_DIM, (h + 1) * A_HEAD_DIM)
        oT_s[rows, :] = acc_s[h, 0:A_HEAD_DIM, :] / acc_s[h, A_HEAD_DIM:A_HEAD_DIM + 1, :]
    o_ref[0] = oT_s[...].T.astype(BF16)


def _dsa_call(rel_bias, ik, kk, vT, qT, iqT, iwT):
    B, S, _ = ik.shape
    nb = S // QBLK
    topk = min(TOPK_MAX, S // 4)
    bkt = jnp.asarray(_rel_bucket_table())
    body = functools.partial(_dsa_body, topk=topk, idx_bits=int(math.log2(S)), max_cnt=S // CNT_CHUNK)
    return pl.pallas_call(
        body,
        grid=(B, nb),
        in_specs=[pl.BlockSpec(memory_space=pltpu.SMEM),
                  pl.BlockSpec((2 * QBLK, QBLK), lambda b, i: (0, 0)),
                  pl.BlockSpec((1, S, IDX_DIM), lambda b, i: (b, 0, 0)),
                  pl.BlockSpec((1, A_WIDTH // LANES, S, LANES), lambda b, i: (b, 0, 0, 0)),
                  pl.BlockSpec((1, S // LANES, A_WIDTH, LANES), lambda b, i: (b, 0, 0, 0)),
                  pl.BlockSpec((1, A_WIDTH, QBLK), lambda b, i: (b, 0, i)),
                  pl.BlockSpec((1, IDX_HEADS * IDX_DIM, QBLK), lambda b, i: (b, 0, i)),
                  pl.BlockSpec((1, IDX_HEADS, QBLK), lambda b, i: (b, 0, i))],
        out_specs=pl.BlockSpec((1, QBLK, A_WIDTH), lambda b, i: (b, i, 0)),
        out_shape=jax.ShapeDtypeStruct((B, S, A_WIDTH), BF16),
        scratch_shapes=[pltpu.VMEM((S, QBLK), F32),
                        pltpu.VMEM((S, QBLK), F32),
                        pltpu.VMEM((8, QBLK), I32),
                        pltpu.VMEM((S, QBLK), F32),
                        pltpu.VMEM((A_HEADS, TBL_PAD + 3 * QBLK, QBLK), F32),
                        pltpu.VMEM((A_WIDTH, QBLK), F32),
                        pltpu.VMEM((1, QBLK), I32),
                        pltpu.VMEM((A_HEADS, LANES, QBLK), BF16),
                        pltpu.VMEM((A_HEADS, ATT_CHUNK, QBLK), F32),
                        pltpu.VMEM((A_HEADS, ATT_CHUNK, QBLK), F32),
                        pltpu.VMEM((A_HEADS, ACC_ROWS, QBLK), F32),
                        pltpu.VMEM((A_HEADS, QBLK), F32),
                        pltpu.VMEM((A_HEADS, QBLK), F32)],
        compiler_params=_params("parallel", "arbitrary"),
        name="dsa",
    )(rel_bias, bkt, ik, kk, vT, qT, iqT, iwT)


GLA_Q = (0, 256)
GLA_K = (256, 512)
GLA_V = (512, 1024)
GLA_R = (1024, 1536)


def _gla_body(gla_ref, glr_ref, wg_ref, bg_ref, ng_ref, o_ref, st_s):
    tg = gla_ref.shape[1]
    C = G_CHUNK

    @pl.when(pl.program_id(1) == 0)
    def _():
        st_s[...] = jnp.zeros_like(st_s)

    glr = glr_ref[0]
    wg = wg_ref[...]
    glr_hi, wg_hi = glr.astype(BF16), wg.astype(BF16)
    glr_lo = (glr - glr_hi.astype(F32)).astype(BF16)
    wg_lo = (wg - wg_hi.astype(F32)).astype(BF16)
    xg = (jnp.dot(glr_hi, wg_hi, preferred_element_type=F32) + jnp.dot(glr_hi, wg_lo, preferred_element_type=F32)
          + jnp.dot(glr_lo, wg_hi, preferred_element_type=F32)) + bg_ref[...]
    logg = -(jnp.maximum(-xg, 0.0) + jnp.log1p(jnp.exp(-jnp.abs(xg)))) * (1.0 / G_TAU)

    rt = lax.broadcasted_iota(I32, (tg, tg), 0)
    ct = lax.broadcasted_iota(I32, (tg, tg), 1)
    cum = jnp.where((rt // C == ct // C) & (rt >= ct), 1.0, 0.0).astype(BF16)
    logg_hi = logg.astype(BF16)
    logg_lo = (logg - logg_hi.astype(F32)).astype(BF16)
    bc_all = (jnp.dot(cum, logg_hi, preferred_element_type=F32)
              + jnp.dot(cum, logg_lo, preferred_element_type=F32))

    bl_all = jnp.concatenate([jnp.broadcast_to(bc_all[(ck + 1) * C - 1:(ck + 1) * C, :], (C, G_KW))
                              for ck in range(tg // C)], axis=0)

    q_all = gla_ref[0, :, GLA_Q[0]:GLA_Q[1]].astype(F32) * (G_DK ** -0.5)
    k_all = gla_ref[0, :, GLA_K[0]:GLA_K[1]].astype(F32)
    q_in_all = (q_all * jnp.exp(bc_all)).astype(BF16)
    k_st_all = (k_all * jnp.exp(bl_all - bc_all)).astype(BF16)
    q_rel_all = q_all * jnp.exp(bc_all - bl_all)
    decay_all = jnp.exp(bl_all)

    ri = lax.broadcasted_iota(I32, (C, C), 0)
    ci = lax.broadcasted_iota(I32, (C, C), 1)
    tril = ri >= ci
    lane_head = lax.broadcasted_iota(I32, (C, LANES), 1) // G_DK
    st_rows = lax.broadcasted_iota(I32, (2 * G_DV, LANES), 0) // G_DV
    st_cols = lax.broadcasted_iota(I32, (2 * G_DV, LANES), 1) // G_DK
    st_diag = st_rows == st_cols
    n_ck = tg // C
    n_p = G_HEADS // 2
    units = [(ck, p) for ck in range(n_ck) for p in range(n_p)]
    rows = lambda ck: slice(ck * C, (ck + 1) * C)
    lanes = lambda p: slice(p * LANES, (p + 1) * LANES)
    v_of = lambda ck, p: gla_ref[0, rows(ck), GLA_V[0] + p * 2 * G_DV:GLA_V[0] + (p + 1) * 2 * G_DV]

    att = {}
    for ck, p in units:
        k_st = k_st_all[rows(ck), lanes(p)]
        for sub in range(2):
            qm = jnp.where(lane_head == sub, q_rel_all[rows(ck), lanes(p)], 0.0).astype(BF16)
            a = lax.dot_general(qm, k_st, NT, preferred_element_type=F32)
            att[ck, p, sub] = jnp.where(tril, a, 0.0).astype(BF16)
    o_intra = {}
    uT = {}
    for ck, p in units:
        v = v_of(ck, p)
        for sub in range(2):
            o_intra[ck, p, sub] = jnp.dot(att[ck, p, sub], v[:, sub * G_DV:(sub + 1) * G_DV],
                                          preferred_element_type=F32)
        uT[ck, p] = lax.dot_general(v, k_st_all[rows(ck), lanes(p)], TN, preferred_element_type=F32)
    o_inter = {}
    for p in range(n_p):
        st = st_s[p]
        for ck in range(n_ck):
            o_inter[ck, p] = lax.dot_general(q_in_all[rows(ck), lanes(p)], st.astype(BF16), NT,
                                             preferred_element_type=F32)
            st = st * decay_all[ck * C:ck * C + 1, lanes(p)] + jnp.where(st_diag, uT[ck, p], 0.0)
        st_s[p] = st
    for ck, p in units:
        for sub in range(2):
            hd = 2 * p + sub
            o = o_intra[ck, p, sub] + o_inter[ck, p][:, sub * G_DV:(sub + 1) * G_DV]
            y = _ln(o) * ng_ref[:, hd * G_DV:(hd + 1) * G_DV]
            g = gla_ref[0, rows(ck), GLA_R[0] + hd * G_DV:GLA_R[0] + (hd + 1) * G_DV].astype(F32)
            o_ref[0, rows(ck), hd * G_DV:(hd + 1) * G_DV] = (y * (g * _sigmoid(g))).astype(BF16)


def _gla_call(gla, glr, wg, bg, ng, tg):
    B, S, _ = gla.shape
    const = lambda b, j: (0, 0)
    return pl.pallas_call(
        _gla_body,
        grid=(B, S // tg),
        in_specs=[pl.BlockSpec((1, tg, 1536), lambda b, j: (b, j, 0)),
                  pl.BlockSpec((1, tg, G_RANK), lambda b, j: (b, j, 0)),
                  pl.BlockSpec((G_RANK, G_KW), const),
                  pl.BlockSpec((1, G_KW), const),
                  pl.BlockSpec((1, G_VW), const)],
        out_specs=pl.BlockSpec((1, tg, G_VW), lambda b, j: (b, j, 0)),
        out_shape=jax.ShapeDtypeStruct((B, S, G_VW), BF16),
        scratch_shapes=[pltpu.VMEM((G_HEADS // 2, 2 * G_DV, LANES), F32)],
        compiler_params=_params("parallel", "arbitrary"),
        name="gla",
    )(gla, glr, wg, bg, ng)


ROUTER_ROWS = 40
ROUTER_E0 = 8


def _post_body(oa_ref, ob_ref, gates_ref, x_ref, gt1_ref, sh2_ref, sc2_ref, wa_ref, wb_ref, wo_ref,
               g1_ref, b1_ref, wr_ref, br_ref, x1_ref, h2_ref, gate_ref, grp_ref):
    tm = x_ref.shape[1]
    D = x_ref.shape[2]
    ya = jnp.dot(oa_ref[0], wa_ref[...], preferred_element_type=F32)
    yb = jnp.dot(ob_ref[0], wb_ref[...], preferred_element_type=F32)
    ga = gates_ref[0, :, 0:D].astype(F32)
    gb = gates_ref[0, :, D:2 * D].astype(F32)
    merged = _sigmoid(ga) * ya + _sigmoid(gb) * yb
    y = jnp.dot(merged.astype(BF16), wo_ref[...], preferred_element_type=F32)
    x1 = _ln(DN_ALPHA * x_ref[0] + gt1_ref[0] * y) * g1_ref[...] + b1_ref[...]
    x1_ref[0] = x1
    h2 = _ln(x1) * (1.0 + sc2_ref[0]) + sh2_ref[0]
    h2_hi = h2.astype(BF16)
    h2_ref[0] = h2_hi

    h2_lo = (h2 - h2_hi.astype(F32)).astype(BF16)
    wr = wr_ref[...]
    wr_hi = wr.astype(BF16)
    wr_lo = (wr - wr_hi.astype(F32)).astype(BF16)
    lt = (lax.dot_general(wr_hi, h2_hi, NT, preferred_element_type=F32)
          + lax.dot_general(wr_hi, h2_lo, NT, preferred_element_type=F32)
          + lax.dot_general(wr_lo, h2_hi, NT, preferred_element_type=F32)) + br_ref[...]
    gl = lt[0:N_GROUPS]
    gmax = jnp.max(gl, axis=0, keepdims=True)
    g_w = 1.0 / jnp.sum(jnp.exp(gl - gmax), axis=0, keepdims=True)
    r4 = lax.broadcasted_iota(I32, (N_GROUPS, tm), 0)
    g_idx = jnp.min(jnp.where(gl == gmax, r4, N_GROUPS), axis=0, keepdims=True)
    eg = jnp.zeros((EXPERTS_PER_GROUP, tm), F32)
    for g in range(N_GROUPS):
        lo = ROUTER_E0 + g * EXPERTS_PER_GROUP
        eg = jnp.where(g_idx == g, lt[lo:lo + EXPERTS_PER_GROUP], eg)
    r8 = lax.broadcasted_iota(I32, (EXPERTS_PER_GROUP, tm), 0)
    e1 = jnp.max(eg, axis=0, keepdims=True)
    i1 = jnp.min(jnp.where(eg == e1, r8, EXPERTS_PER_GROUP), axis=0, keepdims=True)
    eg2 = jnp.where(r8 == i1, -jnp.inf, eg)
    e2 = jnp.max(eg2, axis=0, keepdims=True)
    i2 = jnp.min(jnp.where(eg2 == e2, r8, EXPERTS_PER_GROUP), axis=0, keepdims=True)
    d = jnp.exp(e2 - e1)
    w1 = g_w / (1.0 + d)
    w2 = g_w * d / (1.0 + d)
    in_group = jnp.where(r8 == i1, w1, 0.0) + jnp.where(r8 == i2, w2, 0.0)
    blocks = [jnp.where(g_idx == g, in_group, 0.0) for g in range(N_GROUPS)]
    blocks.append(jnp.zeros((LANES - N_EXPERTS, tm), F32))
    gate_ref[...] = jnp.concatenate(blocks, axis=0).T
    r8g = lax.broadcasted_iota(I32, (8, tm), 0)
    grp_ref[...] = jnp.where(r8g == g_idx, 1.0, 0.0)


def _post_call(o_a, o_b, gates, x, gt1, sh2, sc2, wa, wb, wo, g1, b1, wr, br, tm):
    B, S, D = x.shape
    nt = S // tm
    const = lambda b, t: (0, 0)
    row = lambda b, t: (b, 0, 0)
    tile = lambda b, t: (b, t, 0)
    return pl.pallas_call(
        _post_body,
        grid=(B, nt),
        in_specs=[pl.BlockSpec((1, tm, A_WIDTH), tile),
                  pl.BlockSpec((1, tm, G_VW), tile),
                  pl.BlockSpec((1, tm, 2 * D), tile),
                  pl.BlockSpec((1, tm, D), tile),
                  pl.BlockSpec((1, 1, D), row),
                  pl.BlockSpec((1, 1, D), row),
                  pl.BlockSpec((1, 1, D), row),
                  pl.BlockSpec(wa.shape, const),
                  pl.BlockSpec(wb.shape, const),
                  pl.BlockSpec(wo.shape, const),
                  pl.BlockSpec((1, D), const),
                  pl.BlockSpec((1, D), const),
                  pl.BlockSpec(wr.shape, const),
                  pl.BlockSpec(br.shape, const)],
        out_specs=(pl.BlockSpec((1, tm, D), tile),
                   pl.BlockSpec((1, tm, D), tile),
                   pl.BlockSpec((tm, LANES), lambda b, t: (b * nt + t, 0)),
                   pl.BlockSpec((8, tm), lambda b, t: (0, b * nt + t))),
        out_shape=(jax.ShapeDtypeStruct((B, S, D), F32),
                   jax.ShapeDtypeStruct((B, S, D), BF16),
                   jax.ShapeDtypeStruct((B * S, LANES), F32),
                   jax.ShapeDtypeStruct((8, B * S), F32)),
        compiler_params=_params("parallel", "parallel"),
        name="post",
    )(o_a, o_b, gates, x, gt1, sh2, sc2, wa, wb, wo, g1, b1, wr, br)


MOE_EXPERTS_PER_STEP = 4
MOE_ROW_BLOCK = 128
MOE_PERM_ROWS = 256


def _moe_body(h2_ref, gate_ref, grp_ref, x1_ref, gt2_ref, w1_ref, w3_ref, w2_ref, g2_ref, b2_ref, o_ref,
              perm_s, hs_s, gs_s, ys_s, tri_s, seg_s):
    t = pl.program_id(0)
    s = pl.program_id(1)
    ne = MOE_EXPERTS_PER_STEP
    rb = MOE_ROW_BLOCK
    tm = h2_ref.shape[0]
    tm_pad = perm_s.shape[0]
    steps_per_group = EXPERTS_PER_GROUP // ne

    @pl.when((t == 0) & (s == 0))
    def _():
        r = lax.broadcasted_iota(I32, (tm, tm), 0)
        c = lax.broadcasted_iota(I32, (tm, tm), 1)
        tri_s[...] = jnp.where(r < c, 1.0, 0.0).astype(BF16)

    @pl.when(s == 0)
    def _():
        oh = grp_ref[...]
        rank = jnp.dot(oh.astype(BF16), tri_s[...], preferred_element_type=F32)
        cnt = jnp.sum(oh, axis=1, keepdims=True)
        blocks = jnp.floor((cnt + (rb - 1)) * (1.0 / rb))
        off = jnp.zeros((1, 1), F32)
        dest = jnp.zeros((1, tm), F32)
        for g in range(N_GROUPS):
            seg_s[2 * g] = jnp.sum(off).astype(I32)
            seg_s[2 * g + 1] = jnp.sum(blocks[g:g + 1, :]).astype(I32)
            dest = dest + oh[g:g + 1, :] * (off + rank[g:g + 1, :])
            off = off + blocks[g:g + 1, :] * rb
        dest_i = dest.astype(I32)
        gate = gate_ref[...]
        g_hi = gate.astype(BF16)
        g_lo = (gate - g_hi.astype(F32)).astype(BF16)
        h2 = h2_ref[...]
        for c in range(tm_pad // MOE_PERM_ROWS):
            rows = slice(c * MOE_PERM_ROWS, (c + 1) * MOE_PERM_ROWS)
            d_idx = c * MOE_PERM_ROWS + lax.broadcasted_iota(I32, (MOE_PERM_ROWS, tm), 0)
            perm = jnp.where(d_idx == dest_i, 1.0, 0.0).astype(BF16)
            perm_s[rows, :] = perm
            hs_s[rows, :] = jnp.dot(perm, h2, preferred_element_type=F32).astype(BF16)
            gs_s[rows, :] = (jnp.dot(perm, g_hi, preferred_element_type=F32)
                             + jnp.dot(perm, g_lo, preferred_element_type=F32))
        ys_s[...] = jnp.zeros_like(ys_s)

    g = s // steps_per_group
    row0 = seg_s[2 * g]
    nblk = seg_s[2 * g + 1]

    def block(i, carry):
        r0 = pl.multiple_of(row0 + i * rb, rb)
        x = hs_s[pl.ds(r0, rb), :]
        gsel = pltpu.roll(gs_s[pl.ds(r0, rb), :], (LANES - s * ne) % LANES, axis=1)
        hid = []
        for j in range(ne):
            a = jnp.dot(x, w1_ref[j], preferred_element_type=F32)
            b = jnp.dot(x, w3_ref[j], preferred_element_type=F32)
            hid.append((a * _sigmoid(a) * b * gsel[:, j:j + 1]).astype(BF16))
        ys_s[pl.ds(r0, rb), :] += jnp.dot(jnp.concatenate(hid, axis=1), w2_ref[...], preferred_element_type=F32)
        return carry

    lax.fori_loop(0, nblk, block, 0)

    @pl.when(s == pl.num_programs(1) - 1)
    def _():
        perm = perm_s[...]
        for c in range(o_ref.shape[1] // MOE_PERM_ROWS):
            cols = slice(c * MOE_PERM_ROWS, (c + 1) * MOE_PERM_ROWS)
            o_ref[:, cols] = lax.dot_general(perm, ys_s[:, cols].astype(BF16), TN, preferred_element_type=F32)
        z = DN_ALPHA * x1_ref[...] + gt2_ref[0] * o_ref[...]
        o_ref[...] = _ln(z) * g2_ref[...] + b2_ref[...]


def _moe_call(h2, gate, grp, x1, gt2, w1, w3, w2, g2, b2, tm, S):
    T, D = h2.shape
    ne = MOE_EXPERTS_PER_STEP
    nc = N_EXPERTS // ne
    tm_pad = tm + N_GROUPS * MOE_ROW_BLOCK
    tiles_per_seq = S // tm
    tile = lambda t, c: (t, 0)
    const = lambda t, c: (0, 0)
    return pl.pallas_call(
        _moe_body,
        grid=(T // tm, nc),
        in_specs=[pl.BlockSpec((tm, D), tile),
                  pl.BlockSpec((tm, LANES), tile),
                  pl.BlockSpec((8, tm), lambda t, c: (0, t)),
                  pl.BlockSpec((tm, D), tile),
                  pl.BlockSpec((1, 1, D), lambda t, c: (t // tiles_per_seq, 0, 0)),
                  pl.BlockSpec((ne, D, D_EXPERT), lambda t, c: (c, 0, 0)),
                  pl.BlockSpec((ne, D, D_EXPERT), lambda t, c: (c, 0, 0)),
                  pl.BlockSpec((ne * D_EXPERT, D), lambda t, c: (c, 0)),
                  pl.BlockSpec((1, D), const),
                  pl.BlockSpec((1, D), const)],
        out_specs=pl.BlockSpec((tm, D), tile),
        out_shape=jax.ShapeDtypeStruct((T, D), F32),
        scratch_shapes=[pltpu.VMEM((tm_pad, tm), BF16),
                        pltpu.VMEM((tm_pad, D), BF16),
                        pltpu.VMEM((tm_pad, LANES), F32),
                        pltpu.VMEM((tm_pad, D), F32),
                        pltpu.VMEM((tm, tm), BF16),
                        pltpu.SMEM((2 * N_GROUPS,), I32)],
        compiler_params=_params("arbitrary", "arbitrary"),
        name="moe",
    )(h2, gate, grp, x1, gt2, w1, w3, w2, g2, b2)


def _pick(n, pref):
    return pref if n % pref == 0 else n


def kernel(x, c, rel_bias, w_ada, b_ada, w_in, gla_w_gate, gla_b_gate, gla_norm_g, w_branch_a, w_branch_b, w_out, ln1_g, ln1_b, w_router_group, b_router_group, w_router_expert, b_router_expert, w_exp_gate, w_exp_up, w_exp_down, ln2_g, ln2_b):
    B, S, D = x.shape
    assert S % (2 * QBLK) == 0 and D == 1024 and w_ada.shape[0] == DEPTH == 1
    l = 0

    ada = _ada_call(c, w_ada[l], b_ada[l])
    sh1, sc1, gt1, sh2, sc2, gt2 = [ada[:, i * D:(i + 1) * D].reshape(B, 1, D) for i in range(6)]

    offs = np.concatenate([[0], np.cumsum(SPLIT_SIZES)])
    seg = lambda i: w_in[l][:, offs[i]:offs[i + 1]]
    (w_aq, w_ak, w_av, w_iq, w_ik, w_iw, w_gq, w_gk, w_gv, w_gr, w_glr, w_ga, w_gb) = [seg(i) for i in range(13)]
    pad = jnp.zeros((D, TOK_SMALL[1] - TOK_SMALL[0] - IDX_DIM - G_RANK), F32)
    w_tok = jnp.concatenate([w_ak, w_gq, w_gk, w_gv, w_gr, w_ga, w_gb, w_ik, w_glr, pad], axis=1).astype(BF16)
    w_ch = jnp.concatenate([w_aq, w_av, w_iq, w_iw], axis=1).T.astype(BF16)

    tm = _pick(S, 512)
    kk, gla, gates, ik, glr, qT, vT, iqT, iwT = _inproj_call(x, sh1, sc1, w_tok, w_ch, tm)

    o_a = _dsa_call(rel_bias, ik, kk, vT, qT, iqT, iwT)
    o_b = _gla_call(gla, glr, gla_w_gate[l], gla_b_gate[l].reshape(1, G_KW), gla_norm_g[l].reshape(1, G_VW),
                    _pick(S, 256))

    wr = jnp.zeros((ROUTER_ROWS, D), F32)
    wr = wr.at[0:N_GROUPS].set(w_router_group[l].T).at[ROUTER_E0:ROUTER_E0 + N_EXPERTS].set(w_router_expert[l].T)
    br = jnp.zeros((ROUTER_ROWS, 1), F32)
    br = br.at[0:N_GROUPS, 0].set(b_router_group[l]).at[ROUTER_E0:ROUTER_E0 + N_EXPERTS, 0].set(b_router_expert[l])
    x1, h2, gate, grp = _post_call(o_a, o_b, gates, x, gt1, sh2, sc2,
                              w_branch_a[l].astype(BF16), w_branch_b[l].astype(BF16), w_out[l].astype(BF16),
                              ln1_g[l].reshape(1, D), ln1_b[l].reshape(1, D), wr, br, tm)

    tm5 = _pick(S, 1024)
    out = _moe_call(h2.reshape(B * S, D), gate, grp, x1.reshape(B * S, D), gt2,
                    w_exp_gate[l].astype(BF16), w_exp_up[l].astype(BF16),
                    w_exp_down[l].astype(BF16).reshape(N_EXPERTS * D_EXPERT, D),
                    ln2_g[l].reshape(1, D), ln2_b[l].reshape(1, D), tm5, S)
    return out.reshape(B, S, D)
```

```python
import functools
import math

import numpy as np
import jax
import jax.numpy as jnp
from jax import lax
from jax.experimental import pallas as pl
from jax.experimental.pallas import tpu as pltpu

F32 = jnp.float32
BF16 = jnp.bfloat16
I32 = jnp.int32
HIGHEST = lax.Precision.HIGHEST

A_HEADS = 8
A_HEAD_DIM = 64
A_WIDTH = A_HEADS * A_HEAD_DIM
IDX_HEADS = 16
IDX_DIM = 32
TOPK_MAX = 256
QBLK = 128
REL_BUCKETS = 32
REL_MAX_DIST = 128
G_HEADS = 4
G_DK = 64
G_DV = 128
G_KW = G_HEADS * G_DK
G_VW = G_HEADS * G_DV
G_RANK = 16
G_TAU = 16.0
G_CHUNK = 64
N_GROUPS = 4
EXPERTS_PER_GROUP = 8
N_EXPERTS = N_GROUPS * EXPERTS_PER_GROUP
D_EXPERT = 256
DEPTH = 1
DN_ALPHA = (2.0 * DEPTH) ** 0.25
LN_EPS = 1e-5
SPLIT_SIZES = (A_WIDTH, A_WIDTH, A_WIDTH, IDX_HEADS * IDX_DIM, IDX_DIM, IDX_HEADS,
               G_KW, G_KW, G_VW, G_VW, G_RANK, 1024, 1024)

LANES = 128
VMEM_LIMIT_BYTES = 56 * 1024 * 1024

NEG = -1e30
LOG2E = math.log2(math.e)
INT_MIN = -2 ** 31
INT_MAX = 2 ** 31 - 1
KEY_NEG_INF = -2 ** 31 + 0x7FFFFF

NT = (((1,), (1,)), ((), ()))
TN = (((0,), (0,)), ((), ()))


def _ln(x):
    mu = jnp.mean(x, axis=-1, keepdims=True)
    xc = x - mu
    var = jnp.mean(xc * xc, axis=-1, keepdims=True)
    return xc * lax.rsqrt(var + LN_EPS)


def _sigmoid(x):
    return 0.5 * jnp.tanh(0.5 * x) + 0.5


def _params(*sem):
    return pltpu.CompilerParams(dimension_semantics=sem, vmem_limit_bytes=VMEM_LIMIT_BYTES)


def _ada_body(c_ref, w_ref, b_ref, o_ref):
    c = c_ref[...]
    cond = c * _sigmoid(c)
    o_ref[...] = jnp.dot(cond, w_ref[...], preferred_element_type=F32, precision=HIGHEST) + b_ref[...]


def _ada_call(c, w, b):
    B, D = c.shape
    N = w.shape[1]
    tn = 1536
    return pl.pallas_call(
        _ada_body,
        grid=(N // tn,),
        in_specs=[pl.BlockSpec((B, D), lambda j: (0, 0)),
                  pl.BlockSpec((D, tn), lambda j: (0, j)),
                  pl.BlockSpec((1, tn), lambda j: (0, j))],
        out_specs=pl.BlockSpec((B, tn), lambda j: (0, j)),
        out_shape=jax.ShapeDtypeStruct((B, N), F32),
        compiler_params=_params("arbitrary"),
        name="ada",
    )(c, w, b.reshape(1, N))


TOK_K = (0, 512)
TOK_GLA = (512, 2048)
TOK_GATES = (2048, 4096)
TOK_SMALL = (4096, 4224)
CH_Q = (0, 512)
CH_V = (512, 1024)
CH_IQ = (1024, 1536)
CH_IW = (1536, 1552)
IW_SCALE = IDX_HEADS ** -0.5 * IDX_DIM ** -0.5


def _inproj_body(x_ref, sh_ref, sc_ref, wtok_ref, wch_ref,
                 k_ref, gla_ref, gates_ref, ik_ref, glr_ref, qT_ref, vT_ref, iqT_ref, iwT_ref):
    tm = x_ref.shape[1]
    h = (_ln(x_ref[0]) * (1.0 + sc_ref[0]) + sh_ref[0]).astype(BF16)

    def tok(ab):
        return jnp.dot(h, wtok_ref[:, ab[0]:ab[1]], preferred_element_type=F32)

    def ch(ab):
        return lax.dot_general(wch_ref[ab[0]:ab[1], :], h, NT, preferred_element_type=F32)

    kres = tok(TOK_K)
    for p in range(A_WIDTH // LANES):
        k_ref[0, p] = kres[:, p * LANES:(p + 1) * LANES].astype(BF16)
    gla_ref[0] = tok(TOK_GLA).astype(BF16)
    gates_ref[0] = tok(TOK_GATES).astype(BF16)
    small = tok(TOK_SMALL)
    ik_ref[0] = small[:, :IDX_DIM].astype(BF16)
    glr_ref[0] = small[:, IDX_DIM:IDX_DIM + G_RANK]

    qT_ref[0] = (ch(CH_Q) * (A_HEAD_DIM ** -0.5 * LOG2E)).astype(BF16)
    vres = ch(CH_V).astype(BF16)
    for j in range(tm // LANES):
        vT_ref[0, j] = vres[:, j * LANES:(j + 1) * LANES]
    iqT_ref[0] = ch(CH_IQ).astype(BF16)
    iwT_ref[0] = ch(CH_IW) * IW_SCALE


def _inproj_call(x, sh1, sc1, w_tok, w_ch, tm):
    B, S, D = x.shape
    nt = S // tm
    const = lambda b, t: (0, 0)
    out_shape = (
        jax.ShapeDtypeStruct((B, A_WIDTH // LANES, S, LANES), BF16),
        jax.ShapeDtypeStruct((B, S, 1536), BF16),
        jax.ShapeDtypeStruct((B, S, 2048), BF16),
        jax.ShapeDtypeStruct((B, S, IDX_DIM), BF16),
        jax.ShapeDtypeStruct((B, S, G_RANK), F32),
        jax.ShapeDtypeStruct((B, A_WIDTH, S), BF16),
        jax.ShapeDtypeStruct((B, S // LANES, A_WIDTH, LANES), BF16),
        jax.ShapeDtypeStruct((B, IDX_HEADS * IDX_DIM, S), BF16),
        jax.ShapeDtypeStruct((B, IDX_HEADS, S), F32),
    )
    out_specs = (
        pl.BlockSpec((1, A_WIDTH // LANES, tm, LANES), lambda b, t: (b, 0, t, 0)),
        pl.BlockSpec((1, tm, 1536), lambda b, t: (b, t, 0)),
        pl.BlockSpec((1, tm, 2048), lambda b, t: (b, t, 0)),
        pl.BlockSpec((1, tm, IDX_DIM), lambda b, t: (b, t, 0)),
        pl.BlockSpec((1, tm, G_RANK), lambda b, t: (b, t, 0)),
        pl.BlockSpec((1, A_WIDTH, tm), lambda b, t: (b, 0, t)),
        pl.BlockSpec((1, tm // LANES, A_WIDTH, LANES), lambda b, t: (b, t, 0, 0)),
        pl.BlockSpec((1, IDX_HEADS * IDX_DIM, tm), lambda b, t: (b, 0, t)),
        pl.BlockSpec((1, IDX_HEADS, tm), lambda b, t: (b, 0, t)),
    )
    return pl.pallas_call(
        _inproj_body,
        grid=(B, nt),
        in_specs=[pl.BlockSpec((1, tm, D), lambda b, t: (b, t, 0)),
                  pl.BlockSpec((1, 1, D), lambda b, t: (b, 0, 0)),
                  pl.BlockSpec((1, 1, D), lambda b, t: (b, 0, 0)),
                  pl.BlockSpec(w_tok.shape, const),
                  pl.BlockSpec(w_ch.shape, const)],
        out_specs=out_specs,
        out_shape=out_shape,
        compiler_params=_params("parallel", "parallel"),
        name="inproj",
    )(x, sh1, sc1, w_tok, w_ch)


IDX_CHUNK = 256
CNT_CHUNK = 256
SORT_GROUP = 4
ATT_CHUNK = 256
TBL_PAD = 2 * QBLK
ACC_ROWS = A_HEAD_DIM + 16


def _rel_bucket_table():
    s = np.arange(2 * QBLK)[:, None]
    t = np.arange(QBLK)[None, :]
    dist = np.maximum(t + QBLK - s, 0)
    max_exact = REL_BUCKETS // 2
    d_f = np.maximum(dist, 1).astype(np.float32)
    large = max_exact + (np.log(d_f / max_exact) / math.log(REL_MAX_DIST / max_exact)
                         * (REL_BUCKETS - max_exact)).astype(np.int32)
    large = np.minimum(large, REL_BUCKETS - 1)
    return np.where(dist < max_exact, dist, large).astype(np.int32)


def _far_bucket():
    max_exact = REL_BUCKETS // 2
    v = max_exact + int(np.float32(np.log(np.float32(QBLK + 1) / max_exact) / math.log(REL_MAX_DIST / max_exact)
                                   * (REL_BUCKETS - max_exact)))
    assert min(v, REL_BUCKETS - 1) == REL_BUCKETS - 1
    return REL_BUCKETS - 1


def _dsa_body(rb_ref, bkt_ref, ik_ref, kk_ref, vT_ref, qT_ref, iqT_ref, iwT_ref, o_ref,
              sc_s, srt_s, thr_s, madd_s, tbl_s, oT_s, xcut_s, qm_s, sa_s, sb_s, acc_s, mall_s, mblk_s,
              *, topk, idx_bits, max_cnt):
    i = pl.program_id(1)
    nck = (i + 2) // 2
    t_idx = i * QBLK + lax.broadcasted_iota(I32, (1, QBLK), 1)

    @pl.when(i == 0)
    def _():
        bkt = bkt_ref[...]
        tbl_s[...] = jnp.zeros_like(tbl_s)
        for h in range(A_HEADS):
            t = jnp.zeros((2 * QBLK, QBLK), F32)
            for k in range(REL_BUCKETS):
                t = jnp.where(bkt == k, rb_ref[k, h], t)
            tbl_s[h, TBL_PAD:TBL_PAD + 2 * QBLK, :] = (t - rb_ref[_far_bucket(), h]) * LOG2E

    def key_to_float(key):
        key = jnp.maximum(key, KEY_NEG_INF)
        return pltpu.bitcast(jnp.where(key < 0, key ^ INT_MAX, key), F32)

    def score_chunk(c, carry):
        s0 = pl.multiple_of(c * IDX_CHUNK, IDX_CHUNK)
        kc = ik_ref[0, pl.ds(s0, IDX_CHUNK), :]
        acc = jnp.zeros((IDX_CHUNK, QBLK), F32)
        for hp in range(IDX_HEADS // 2):
            r0 = hp * 2 * IDX_DIM
            rhs = jnp.concatenate([iqT_ref[0, r0:r0 + IDX_DIM, :],
                                   iqT_ref[0, r0 + IDX_DIM:r0 + 2 * IDX_DIM, :]], axis=1)
            z = jnp.dot(kc, rhs, preferred_element_type=F32)
            acc = acc + jnp.maximum(z[:, :QBLK], 0.0) * iwT_ref[0, 2 * hp:2 * hp + 1, :]
            acc = acc + jnp.maximum(z[:, QBLK:], 0.0) * iwT_ref[0, 2 * hp + 1:2 * hp + 2, :]
        s_idx = s0 + lax.broadcasted_iota(I32, (IDX_CHUNK, QBLK), 0)
        sc_s[pl.ds(s0, IDX_CHUNK), :] = jnp.where(s_idx <= t_idx, acc, -jnp.inf)
        return carry

    ncnt = nck
    lax.fori_loop(0, nck // 2, lambda c, carry: score_chunk(2 * c + 1, score_chunk(2 * c, carry)), 0)

    @pl.when(nck % 2 == 1)
    def _():
        score_chunk(nck - 1, 0)

    def count(pred):
        def body(c, cnt):
            s0 = pl.multiple_of(c * CNT_CHUNK, CNT_CHUNK)
            k = sc_s[pl.ds(s0, CNT_CHUNK), :]
            s_idx = s0 + lax.broadcasted_iota(I32, (CNT_CHUNK, QBLK), 0)
            m = jnp.where(pred(k, s_idx), 1, 0)
            return cnt + jnp.sum(m.reshape(CNT_CHUNK // 8, 8, QBLK), axis=0)
        cnt = lax.fori_loop(0, ncnt, body, jnp.zeros((8, QBLK), I32))
        return jnp.sum(cnt, axis=0, keepdims=True)

    def search_block(n):
        groups = n * CNT_CHUNK // (8 * SORT_GROUP)

        for g in range(groups):
            v = [sc_s[(SORT_GROUP * g + u) * 8:(SORT_GROUP * g + u + 1) * 8, :] for u in range(SORT_GROUP)]
            for a, b in ((0, 1), (2, 3), (0, 2), (1, 3), (1, 2)):
                v[a], v[b] = jnp.maximum(v[a], v[b]), jnp.minimum(v[a], v[b])
            for u in range(SORT_GROUP):
                srt_s[(SORT_GROUP * g + u) * 8:(SORT_GROUP * g + u + 1) * 8, :] = v[u]

        def count_ge(cand_key):
            cand = key_to_float(cand_key)
            parts = []
            for g in range(groups):
                cnt = 0
                for u in range(SORT_GROUP):
                    tile = srt_s[(SORT_GROUP * g + u) * 8:(SORT_GROUP * g + u + 1) * 8, :]
                    cnt = jnp.where(tile >= cand, u + 1, cnt)
                parts.append(cnt)
            while len(parts) > 1:
                odd = parts[len(parts) & ~1:]
                parts = [parts[j] + parts[j + 1] for j in range(0, len(parts) - 1, 2)] + odd
            return jnp.sum(parts[0], axis=0, keepdims=True)

        c0 = count_ge(jnp.zeros((1, QBLK), I32))
        ok = c0 >= topk
        T = jnp.where(ok, 0, INT_MIN).astype(I32)
        cnt_T = jnp.where(ok, c0, n * CNT_CHUNK)

        def bit_body(j, carry):
            T, cnt_T = carry
            cand = T | jnp.left_shift(jnp.int32(1), 30 - j)
            c = count_ge(cand)
            ok = c >= topk
            return jnp.where(ok, cand, T), jnp.where(ok, c, cnt_T)

        T, cnt_T = lax.fori_loop(0, 31, bit_body, (T, cnt_T))
        thr_s[0:1, :] = T
        thr_s[1:2, :] = cnt_T
        T = jnp.maximum(T, KEY_NEG_INF)
        thr_s[2:3, :] = count_ge(jnp.where(T == INT_MAX, T, T + 1))

    for n in range(1, max_cnt + 1):
        pl.when(ncnt == n)(functools.partial(search_block, n))
    T_key = jnp.maximum(thr_s[0:1, :], KEY_NEG_INF)
    T = key_to_float(T_key)
    cnt_ge = thr_s[1:2, :]
    cnt_gt = thr_s[2:3, :]

    need = topk - cnt_gt
    excess = jnp.where((cnt_ge - cnt_gt > need) & (T_key > KEY_NEG_INF), 1.0, 0.0)
    xcut_s[...] = jnp.full((1, QBLK), INT_MAX, I32)

    @pl.when(jnp.max(excess) > 0.0)
    def _():
        X = jnp.zeros((1, QBLK), I32)
        for b in range(idx_bits - 1, -1, -1):
            cand = X | (1 << b)
            f = count(lambda k, s: (k == T) & (s < cand))
            X = jnp.where(f < need, cand, X)
        xcut_s[...] = X

    xcut = xcut_s[...]

    def mask_chunk(c, carry):
        s0 = pl.multiple_of(c * IDX_CHUNK, IDX_CHUNK)
        k = sc_s[pl.ds(s0, IDX_CHUNK), :]
        s_idx = s0 + lax.broadcasted_iota(I32, (IDX_CHUNK, QBLK), 0)
        sel = ((k > T) | ((k == T) & (s_idx <= xcut))) & (s_idx <= t_idx)
        madd_s[pl.ds(s0, IDX_CHUNK), :] = jnp.where(sel, 0.0, NEG)
        return carry

    lax.fori_loop(0, nck, mask_chunk, 0)

    c_last = i // 2
    even = 1 - (i - 2 * c_last)
    row_head = lax.broadcasted_iota(I32, (LANES, QBLK), 0) // A_HEAD_DIM
    for h in range(A_HEADS):
        qp = qT_ref[0, (h // 2) * LANES:(h // 2 + 1) * LANES, :]
        qm_s[h] = jnp.where(row_head == h % 2, qp, jnp.zeros_like(qp))
    acc_s[...] = jnp.zeros_like(acc_s)
    ones = jnp.ones((ACC_ROWS - A_HEAD_DIM, ATT_CHUNK), BF16)

    def logits(c, s_buf, biased=True):
        s0 = pl.multiple_of(c * ATT_CHUNK, ATT_CHUNK)
        madd = madd_s[pl.ds(s0, ATT_CHUNK), :]
        off = jnp.where(c == c_last, 2 * QBLK + QBLK * even, jnp.where(c == c_last - 1, QBLK * even, 0))
        off = pl.multiple_of(off, QBLK)
        m_blk = []
        for h in range(A_HEADS):
            kc = kk_ref[0, h // 2, pl.ds(s0, ATT_CHUNK), :]
            s = jnp.dot(kc, qm_s[h], preferred_element_type=F32) + madd
            if biased:
                s = s + tbl_s[h, pl.ds(off, ATT_CHUNK), :]
            s_buf[h] = s
            m_blk.append(jnp.max(s, axis=0, keepdims=True))
        return jnp.concatenate(m_blk, axis=0)

    def accumulate(c, s_buf, m_all, m_blk):
        m_new = jnp.maximum(m_all, m_blk)
        alpha = jnp.exp2(m_all - m_new)
        for h in range(A_HEADS):
            rows = slice(h * A_HEAD_DIM, (h + 1) * A_HEAD_DIM)
            p = jnp.exp2(s_buf[h] - m_new[h:h + 1]).astype(BF16)
            vt = jnp.concatenate([vT_ref[0, 2 * c + u, rows, :] for u in range(ATT_CHUNK // QBLK)], axis=1)
            vt = jnp.concatenate([vt, ones], axis=0)
            acc_s[h] = alpha[h:h + 1] * acc_s[h] + jnp.dot(vt, p, preferred_element_type=F32)
        return m_new

    def att_body(biased, pair, carry):
        m_all, m_blk = carry
        c = 2 * pair
        m_b = logits(c + 1, sb_s, biased)
        m_all = accumulate(c, sa_s, m_all, m_blk)
        m_a = logits(c + 2, sa_s, biased)
        return accumulate(c + 1, sb_s, m_all, m_b), m_a

    n_pairs = c_last // 2
    n_far_pairs = jnp.maximum(n_pairs - 1, 0)
    carry = (jnp.full((A_HEADS, QBLK), NEG, F32), logits(0, sa_s))
    carry = lax.fori_loop(0, n_far_pairs, functools.partial(att_body, False), carry)
    m_all, m_blk = lax.fori_loop(n_far_pairs, n_pairs, functools.partial(att_body, True), carry)
    mall_s[...] = m_all
    mblk_s[...] = m_blk

    @pl.when(c_last % 2 == 1)
    def _():
        m_b = logits(c_last, sb_s)
        m_all = accumulate(c_last - 1, sa_s, mall_s[...], mblk_s[...])
        accumulate(c_last, sb_s, m_all, m_b)

    @pl.when(c_last % 2 == 0)
    def _():
        accumulate(c_last, sa_s, mall_s[...], mblk_s[...])

    for h in range(A_HEADS):
        rows = slice(h * A_HEAD_DIM, (h + 1) * A_HEAD_DIM)
        oT_s[rows, :] = acc_s[h, 0:A_HEAD_DIM, :] / acc_s[h, A_HEAD_DIM:A_HEAD_DIM + 1, :]
    o_ref[0] = oT_s[...].T.astype(BF16)


def _dsa_call(rel_bias, ik, kk, vT, qT, iqT, iwT):
    B, S, _ = ik.shape
    nb = S // QBLK
    topk = min(TOPK_MAX, S // 4)
    bkt = jnp.asarray(_rel_bucket_table())
    body = functools.partial(_dsa_body, topk=topk, idx_bits=int(math.log2(S)), max_cnt=S // CNT_CHUNK)
    return pl.pallas_call(
        body,
        grid=(B, nb),
        in_specs=[pl.BlockSpec(memory_space=pltpu.SMEM),
                  pl.BlockSpec((2 * QBLK, QBLK), lambda b, i: (0, 0)),
                  pl.BlockSpec((1, S, IDX_DIM), lambda b, i: (b, 0, 0)),
                  pl.BlockSpec((1, A_WIDTH // LANES, S, LANES), lambda b, i: (b, 0, 0, 0)),
                  pl.BlockSpec((1, S // LANES, A_WIDTH, LANES), lambda b, i: (b, 0, 0, 0)),
                  pl.BlockSpec((1, A_WIDTH, QBLK), lambda b, i: (b, 0, i)),
                  pl.BlockSpec((1, IDX_HEADS * IDX_DIM, QBLK), lambda b, i: (b, 0, i)),
                  pl.BlockSpec((1, IDX_HEADS, QBLK), lambda b, i: (b, 0, i))],
        out_specs=pl.BlockSpec((1, QBLK, A_WIDTH), lambda b, i: (b, i, 0)),
        out_shape=jax.ShapeDtypeStruct((B, S, A_WIDTH), BF16),
        scratch_shapes=[pltpu.VMEM((S, QBLK), F32),
                        pltpu.VMEM((S, QBLK), F32),
                        pltpu.VMEM((8, QBLK), I32),
                        pltpu.VMEM((S, QBLK), F32),
                        pltpu.VMEM((A_HEADS, TBL_PAD + 3 * QBLK, QBLK), F32),
                        pltpu.VMEM((A_WIDTH, QBLK), F32),
                        pltpu.VMEM((1, QBLK), I32),
                        pltpu.VMEM((A_HEADS, LANES, QBLK), BF16),
                        pltpu.VMEM((A_HEADS, ATT_CHUNK, QBLK), F32),
                        pltpu.VMEM((A_HEADS, ATT_CHUNK, QBLK), F32),
                        pltpu.VMEM((A_HEADS, ACC_ROWS, QBLK), F32),
                        pltpu.VMEM((A_HEADS, QBLK), F32),
                        pltpu.VMEM((A_HEADS, QBLK), F32)],
        compiler_params=_params("parallel", "arbitrary"),
        name="dsa",
    )(rel_bias, bkt, ik, kk, vT, qT, iqT, iwT)


GLA_Q = (0, 256)
GLA_K = (256, 512)
GLA_V = (512, 1024)
GLA_R = (1024, 1536)


def _gla_body(gla_ref, glr_ref, wg_ref, bg_ref, ng_ref, o_ref, st_s):
    tg = gla_ref.shape[1]
    C = G_CHUNK

    @pl.when(pl.program_id(1) == 0)
    def _():
        st_s[...] = jnp.zeros_like(st_s)

    glr = glr_ref[0]
    wg = wg_ref[...]
    glr_hi, wg_hi = glr.astype(BF16), wg.astype(BF16)
    glr_lo = (glr - glr_hi.astype(F32)).astype(BF16)
    wg_lo = (wg - wg_hi.astype(F32)).astype(BF16)
    xg = (jnp.dot(glr_hi, wg_hi, preferred_element_type=F32) + jnp.dot(glr_hi, wg_lo, preferred_element_type=F32)
          + jnp.dot(glr_lo, wg_hi, preferred_element_type=F32)) + bg_ref[...]
    logg = -(jnp.maximum(-xg, 0.0) + jnp.log1p(jnp.exp(-jnp.abs(xg)))) * (1.0 / G_TAU)

    rt = lax.broadcasted_iota(I32, (tg, tg), 0)
    ct = lax.broadcasted_iota(I32, (tg, tg), 1)
    cum = jnp.where((rt // C == ct // C) & (rt >= ct), 1.0, 0.0).astype(BF16)
    logg_hi = logg.astype(BF16)
    logg_lo = (logg - logg_hi.astype(F32)).astype(BF16)
    bc_all = (jnp.dot(cum, logg_hi, preferred_element_type=F32)
              + jnp.dot(cum, logg_lo, preferred_element_type=F32))

    bl_all = jnp.concatenate([jnp.broadcast_to(bc_all[(ck + 1) * C - 1:(ck + 1) * C, :], (C, G_KW))
                              for ck in range(tg // C)], axis=0)

    q_all = gla_ref[0, :, GLA_Q[0]:GLA_Q[1]].astype(F32) * (G_DK ** -0.5)
    k_all = gla_ref[0, :, GLA_K[0]:GLA_K[1]].astype(F32)
    q_in_all = (q_all * jnp.exp(bc_all)).astype(BF16)
    k_st_all = (k_all * jnp.exp(bl_all - bc_all)).astype(BF16)
    q_rel_all = q_all * jnp.exp(bc_all - bl_all)
    decay_all = jnp.exp(bl_all)

    ri = lax.broadcasted_iota(I32, (C, C), 0)
    ci = lax.broadcasted_iota(I32, (C, C), 1)
    tril = ri >= ci
    lane_head = lax.broadcasted_iota(I32, (C, LANES), 1) // G_DK
    st_rows = lax.broadcasted_iota(I32, (2 * G_DV, LANES), 0) // G_DV
    st_cols = lax.broadcasted_iota(I32, (2 * G_DV, LANES), 1) // G_DK
    st_diag = st_rows == st_cols
    n_ck = tg // C
    n_p = G_HEADS // 2
    units = [(ck, p) for ck in range(n_ck) for p in range(n_p)]
    rows = lambda ck: slice(ck * C, (ck + 1) * C)
    lanes = lambda p: slice(p * LANES, (p + 1) * LANES)
    v_of = lambda ck, p: gla_ref[0, rows(ck), GLA_V[0] + p * 2 * G_DV:GLA_V[0] + (p + 1) * 2 * G_DV]

    att = {}
    for ck, p in units:
        k_st = k_st_all[rows(ck), lanes(p)]
        for sub in range(2):
            qm = jnp.where(lane_head == sub, q_rel_all[rows(ck), lanes(p)], 0.0).astype(BF16)
            a = lax.dot_general(qm, k_st, NT, preferred_element_type=F32)
            att[ck, p, sub] = jnp.where(tril, a, 0.0).astype(BF16)
    o_intra = {}
    uT = {}
    for ck, p in units:
        v = v_of(ck, p)
        for sub in range(2):
            o_intra[ck, p, sub] = jnp.dot(att[ck, p, sub], v[:, sub * G_DV:(sub + 1) * G_DV],
                                          preferred_element_type=F32)
        uT[ck, p] = lax.dot_general(v, k_st_all[rows(ck), lanes(p)], TN, preferred_element_type=F32)
    o_inter = {}
    for p in range(n_p):
        st = st_s[p]
        for ck in range(n_ck):
            o_inter[ck, p] = lax.dot_general(q_in_all[rows(ck), lanes(p)], st.astype(BF16), NT,
                                             preferred_element_type=F32)
            st = st * decay_all[ck * C:ck * C + 1, lanes(p)] + jnp.where(st_diag, uT[ck, p], 0.0)
        st_s[p] = st
    for ck, p in units:
        for sub in range(2):
            hd = 2 * p + sub
            o = o_intra[ck, p, sub] + o_inter[ck, p][:, sub * G_DV:(sub + 1) * G_DV]
            y = _ln(o) * ng_ref[:, hd * G_DV:(hd + 1) * G_DV]
            g = gla_ref[0, rows(ck), GLA_R[0] + hd * G_DV:GLA_R[0] + (hd + 1) * G_DV].astype(F32)
            o_ref[0, rows(ck), hd * G_DV:(hd + 1) * G_DV] = (y * (g * _sigmoid(g))).astype(BF16)


def _gla_call(gla, glr, wg, bg, ng, tg):
    B, S, _ = gla.shape
    const = lambda b, j: (0, 0)
    return pl.pallas_call(
        _gla_body,
        grid=(B, S // tg),
        in_specs=[pl.BlockSpec((1, tg, 1536), lambda b, j: (b, j, 0)),
                  pl.BlockSpec((1, tg, G_RANK), lambda b, j: (b, j, 0)),
                  pl.BlockSpec((G_RANK, G_KW), const),
                  pl.BlockSpec((1, G_KW), const),
                  pl.BlockSpec((1, G_VW), const)],
        out_specs=pl.BlockSpec((1, tg, G_VW), lambda b, j: (b, j, 0)),
        out_shape=jax.ShapeDtypeStruct((B, S, G_VW), BF16),
        scratch_shapes=[pltpu.VMEM((G_HEADS // 2, 2 * G_DV, LANES), F32)],
        compiler_params=_params("parallel", "arbitrary"),
        name="gla",
    )(gla, glr, wg, bg, ng)


ROUTER_ROWS = 40
ROUTER_E0 = 8


def _post_body(oa_ref, ob_ref, gates_ref, x_ref, gt1_ref, sh2_ref, sc2_ref, wa_ref, wb_ref, wo_ref,
               g1_ref, b1_ref, wr_ref, br_ref, x1_ref, h2_ref, gate_ref, grp_ref):
    tm = x_ref.shape[1]
    D = x_ref.shape[2]
    ya = jnp.dot(oa_ref[0], wa_ref[...], preferred_element_type=F32)
    yb = jnp.dot(ob_ref[0], wb_ref[...], preferred_element_type=F32)
    ga = gates_ref[0, :, 0:D].astype(F32)
    gb = gates_ref[0, :, D:2 * D].astype(F32)
    merged = _sigmoid(ga) * ya + _sigmoid(gb) * yb
    y = jnp.dot(merged.astype(BF16), wo_ref[...], preferred_element_type=F32)
    x1 = _ln(DN_ALPHA * x_ref[0] + gt1_ref[0] * y) * g1_ref[...] + b1_ref[...]
    x1_ref[0] = x1
    h2 = _ln(x1) * (1.0 + sc2_ref[0]) + sh2_ref[0]
    h2_hi = h2.astype(BF16)
    h2_ref[0] = h2_hi

    h2_lo = (h2 - h2_hi.astype(F32)).astype(BF16)
    wr = wr_ref[...]
    wr_hi = wr.astype(BF16)
    wr_lo = (wr - wr_hi.astype(F32)).astype(BF16)
    lt = (lax.dot_general(wr_hi, h2_hi, NT, preferred_element_type=F32)
          + lax.dot_general(wr_hi, h2_lo, NT, preferred_element_type=F32)
          + lax.dot_general(wr_lo, h2_hi, NT, preferred_element_type=F32)) + br_ref[...]
    gl = lt[0:N_GROUPS]
    gmax = jnp.max(gl, axis=0, keepdims=True)
    g_w = 1.0 / jnp.sum(jnp.exp(gl - gmax), axis=0, keepdims=True)
    r4 = lax.broadcasted_iota(I32, (N_GROUPS, tm), 0)
    g_idx = jnp.min(jnp.where(gl == gmax, r4, N_GROUPS), axis=0, keepdims=True)
    eg = jnp.zeros((EXPERTS_PER_GROUP, tm), F32)
    for g in range(N_GROUPS):
        lo = ROUTER_E0 + g * EXPERTS_PER_GROUP
        eg = jnp.where(g_idx == g, lt[lo:lo + EXPERTS_PER_GROUP], eg)
    r8 = lax.broadcasted_iota(I32, (EXPERTS_PER_GROUP, tm), 0)
    e1 = jnp.max(eg, axis=0, keepdims=True)
    i1 = jnp.min(jnp.where(eg == e1, r8, EXPERTS_PER_GROUP), axis=0, keepdims=True)
    eg2 = jnp.where(r8 == i1, -jnp.inf, eg)
    e2 = jnp.max(eg2, axis=0, keepdims=True)
    i2 = jnp.min(jnp.where(eg2 == e2, r8, EXPERTS_PER_GROUP), axis=0, keepdims=True)
    d = jnp.exp(e2 - e1)
    w1 = g_w / (1.0 + d)
    w2 = g_w * d / (1.0 + d)
    in_group = jnp.where(r8 == i1, w1, 0.0) + jnp.where(r8 == i2, w2, 0.0)
    blocks = [jnp.where(g_idx == g, in_group, 0.0) for g in range(N_GROUPS)]
    blocks.append(jnp.zeros((LANES - N_EXPERTS, tm), F32))
    gate_ref[...] = jnp.concatenate(blocks, axis=0).T
    r8g = lax.broadcasted_iota(I32, (8, tm), 0)
    grp_ref[...] = jnp.where(r8g == g_idx, 1.0, 0.0)


def _post_call(o_a, o_b, gates, x, gt1, sh2, sc2, wa, wb, wo, g1, b1, wr, br, tm):
    B, S, D = x.shape
    nt = S // tm
    const = lambda b, t: (0, 0)
    row = lambda b, t: (b, 0, 0)
    tile = lambda b, t: (b, t, 0)
    return pl.pallas_call(
        _post_body,
        grid=(B, nt),
        in_specs=[pl.BlockSpec((1, tm, A_WIDTH), tile),
                  pl.BlockSpec((1, tm, G_VW), tile),
                  pl.BlockSpec((1, tm, 2 * D), tile),
                  pl.BlockSpec((1, tm, D), tile),
                  pl.BlockSpec((1, 1, D), row),
                  pl.BlockSpec((1, 1, D), row),
                  pl.BlockSpec((1, 1, D), row),
                  pl.BlockSpec(wa.shape, const),
                  pl.BlockSpec(wb.shape, const),
                  pl.BlockSpec(wo.shape, const),
                  pl.BlockSpec((1, D), const),
                  pl.BlockSpec((1, D), const),
                  pl.BlockSpec(wr.shape, const),
                  pl.BlockSpec(br.shape, const)],
        out_specs=(pl.BlockSpec((1, tm, D), tile),
                   pl.BlockSpec((1, tm, D), tile),
                   pl.BlockSpec((tm, LANES), lambda b, t: (b * nt + t, 0)),
                   pl.BlockSpec((8, tm), lambda b, t: (0, b * nt + t))),
        out_shape=(jax.ShapeDtypeStruct((B, S, D), F32),
                   jax.ShapeDtypeStruct((B, S, D), BF16),
                   jax.ShapeDtypeStruct((B * S, LANES), F32),
                   jax.ShapeDtypeStruct((8, B * S), F32)),
        compiler_params=_params("parallel", "parallel"),
        name="post",
    )(o_a, o_b, gates, x, gt1, sh2, sc2, wa, wb, wo, g1, b1, wr, br)


MOE_EXPERTS_PER_STEP = 4
MOE_ROW_BLOCK = 128
MOE_PERM_ROWS = 256


def _moe_body(h2_ref, gate_ref, grp_ref, x1_ref, gt2_ref, w1_ref, w3_ref, w2_ref, g2_ref, b2_ref, o_ref,
              perm_s, hs_s, gs_s, ys_s, tri_s, seg_s):
    t = pl.program_id(0)
    s = pl.program_id(1)
    ne = MOE_EXPERTS_PER_STEP
    rb = MOE_ROW_BLOCK
    tm = h2_ref.shape[0]
    tm_pad = perm_s.shape[0]
    steps_per_group = EXPERTS_PER_GROUP // ne

    @pl.when((t == 0) & (s == 0))
    def _():
        r = lax.broadcasted_iota(I32, (tm, tm), 0)
        c = lax.broadcasted_iota(I32, (tm, tm), 1)
        tri_s[...] = jnp.where(r < c, 1.0, 0.0).astype(BF16)

    @pl.when(s == 0)
    def _():
        oh = grp_ref[...]
        rank = jnp.dot(oh.astype(BF16), tri_s[...], preferred_element_type=F32)
        cnt = jnp.sum(oh, axis=1, keepdims=True)
        blocks = jnp.floor((cnt + (rb - 1)) * (1.0 / rb))
        off = jnp.zeros((1, 1), F32)
        dest = jnp.zeros((1, tm), F32)
        for g in range(N_GROUPS):
            seg_s[2 * g] = jnp.sum(off).astype(I32)
            seg_s[2 * g + 1] = jnp.sum(blocks[g:g + 1, :]).astype(I32)
            dest = dest + oh[g:g + 1, :] * (off + rank[g:g + 1, :])
            off = off + blocks[g:g + 1, :] * rb
        dest_i = dest.astype(I32)
        gate = gate_ref[...]
        g_hi = gate.astype(BF16)
        g_lo = (gate - g_hi.astype(F32)).astype(BF16)
        h2 = h2_ref[...]
        for c in range(tm_pad // MOE_PERM_ROWS):
            rows = slice(c * MOE_PERM_ROWS, (c + 1) * MOE_PERM_ROWS)
            d_idx = c * MOE_PERM_ROWS + lax.broadcasted_iota(I32, (MOE_PERM_ROWS, tm), 0)
            perm = jnp.where(d_idx == dest_i, 1.0, 0.0).astype(BF16)
            perm_s[rows, :] = perm
            hs_s[rows, :] = jnp.dot(perm, h2, preferred_element_type=F32).astype(BF16)
            gs_s[rows, :] = (jnp.dot(perm, g_hi, preferred_element_type=F32)
                             + jnp.dot(perm, g_lo, preferred_element_type=F32))
        ys_s[...] = jnp.zeros_like(ys_s)

    g = s // steps_per_group
    row0 = seg_s[2 * g]
    nblk = seg_s[2 * g + 1]

    def block(i, carry):
        r0 = pl.multiple_of(row0 + i * rb, rb)
        x = hs_s[pl.ds(r0, rb), :]
        gsel = pltpu.roll(gs_s[pl.ds(r0, rb), :], (LANES - s * ne) % LANES, axis=1)
        hid = []
        for j in range(ne):
            a = jnp.dot(x, w1_ref[j], preferred_element_type=F32)
            b = jnp.dot(x, w3_ref[j], preferred_element_type=F32)
            hid.append((a * _sigmoid(a) * b * gsel[:, j:j + 1]).astype(BF16))
        ys_s[pl.ds(r0, rb), :] += jnp.dot(jnp.concatenate(hid, axis=1), w2_ref[...], preferred_element_type=F32)
        return carry

    lax.fori_loop(0, nblk, block, 0)

    @pl.when(s == pl.num_programs(1) - 1)
    def _():
        perm = perm_s[...]
        for c in range(o_ref.shape[1] // MOE_PERM_ROWS):
            cols = slice(c * MOE_PERM_ROWS, (c + 1) * MOE_PERM_ROWS)
            o_ref[:, cols] = lax.dot_general(perm, ys_s[:, cols].astype(BF16), TN, preferred_element_type=F32)
        z = DN_ALPHA * x1_ref[...] + gt2_ref[0] * o_ref[...]
        o_ref[...] = _ln(z) * g2_ref[...] + b2_ref[...]


def _moe_call(h2, gate, grp, x1, gt2, w1, w3, w2, g2, b2, tm, S):
    T, D = h2.shape
    ne = MOE_EXPERTS_PER_STEP
    nc = N_EXPERTS // ne
    tm_pad = tm + N_GROUPS * MOE_ROW_BLOCK
    tiles_per_seq = S // tm
    tile = lambda t, c: (t, 0)
    const = lambda t, c: (0, 0)
    return pl.pallas_call(
        _moe_body,
        grid=(T // tm, nc),
        in_specs=[pl.BlockSpec((tm, D), tile),
                  pl.BlockSpec((tm, LANES), tile),
                  pl.BlockSpec((8, tm), lambda t, c: (0, t)),
                  pl.BlockSpec((tm, D), tile),
                  pl.BlockSpec((1, 1, D), lambda t, c: (t // tiles_per_seq, 0, 0)),
                  pl.BlockSpec((ne, D, D_EXPERT), lambda t, c: (c, 0, 0)),
                  pl.BlockSpec((ne, D, D_EXPERT), lambda t, c: (c, 0, 0)),
                  pl.BlockSpec((ne * D_EXPERT, D), lambda t, c: (c, 0)),
                  pl.BlockSpec((1, D), const),
                  pl.BlockSpec((1, D), const)],
        out_specs=pl.BlockSpec((tm, D), tile),
        out_shape=jax.ShapeDtypeStruct((T, D), F32),
        scratch_shapes=[pltpu.VMEM((tm_pad, tm), BF16),
                        pltpu.VMEM((tm_pad, D), BF16),
                        pltpu.VMEM((tm_pad, LANES), F32),
                        pltpu.VMEM((tm_pad, D), F32),
                        pltpu.VMEM((tm, tm), BF16),
                        pltpu.SMEM((2 * N_GROUPS,), I32)],
        compiler_params=_params("arbitrary", "arbitrary"),
        name="moe",
    )(h2, gate, grp, x1, gt2, w1, w3, w2, g2, b2)


def _pick(n, pref):
    return pref if n % pref == 0 else n


def kernel(x, c, rel_bias, w_ada, b_ada, w_in, gla_w_gate, gla_b_gate, gla_norm_g, w_branch_a, w_branch_b, w_out, ln1_g, ln1_b, w_router_group, b_router_group, w_router_expert, b_router_expert, w_exp_gate, w_exp_up, w_exp_down, ln2_g, ln2_b):
    B, S, D = x.shape
    assert S % (2 * QBLK) == 0 and D == 1024 and w_ada.shape[0] == DEPTH == 1
    l = 0

    ada = _ada_call(c, w_ada[l], b_ada[l])
    sh1, sc1, gt1, sh2, sc2, gt2 = [ada[:, i * D:(i + 1) * D].reshape(B, 1, D) for i in range(6)]

    offs = np.concatenate([[0], np.cumsum(SPLIT_SIZES)])
    seg = lambda i: w_in[l][:, offs[i]:offs[i + 1]]
    (w_aq, w_ak, w_av, w_iq, w_ik, w_iw, w_gq, w_gk, w_gv, w_gr, w_glr, w_ga, w_gb) = [seg(i) for i in range(13)]
    pad = jnp.zeros((D, TOK_SMALL[1] - TOK_SMALL[0] - IDX_DIM - G_RANK), F32)
    w_tok = jnp.concatenate([w_ak, w_gq, w_gk, w_gv, w_gr, w_ga, w_gb, w_ik, w_glr, pad], axis=1).astype(BF16)
    w_ch = jnp.concatenate([w_aq, w_av, w_iq, w_iw], axis=1).T.astype(BF16)

    tm = _pick(S, 512)
    kk, gla, gates, ik, glr, qT, vT, iqT, iwT = _inproj_call(x, sh1, sc1, w_tok, w_ch, tm)

    o_a = _dsa_call(rel_bias, ik, kk, vT, qT, iqT, iwT)
    o_b = _gla_call(gla, glr, gla_w_gate[l], gla_b_gate[l].reshape(1, G_KW), gla_norm_g[l].reshape(1, G_VW),
                    _pick(S, 256))

    wr = jnp.zeros((ROUTER_ROWS, D), F32)
    wr = wr.at[0:N_GROUPS].set(w_router_group[l].T).at[ROUTER_E0:ROUTER_E0 + N_EXPERTS].set(w_router_expert[l].T)
    br = jnp.zeros((ROUTER_ROWS, 1), F32)
    br = br.at[0:N_GROUPS, 0].set(b_router_group[l]).at[ROUTER_E0:ROUTER_E0 + N_EXPERTS, 0].set(b_router_expert[l])
    x1, h2, gate, grp = _post_call(o_a, o_b, gates, x, gt1, sh2, sc2,
                              w_branch_a[l].astype(BF16), w_branch_b[l].astype(BF16), w_out[l].astype(BF16),
                              ln1_g[l].reshape(1, D), ln1_b[l].reshape(1, D), wr, br, tm)

    tm5 = _pick(S, 1024)
    out = _moe_call(h2.reshape(B * S, D), gate, grp, x1.reshape(B * S, D), gt2,
                    w_exp_gate[l].astype(BF16), w_exp_up[l].astype(BF16),
                    w_exp_down[l].astype(BF16).reshape(N_EXPERTS * D_EXPERT, D),
                    ln2_g[l].reshape(1, D), ln2_b[l].reshape(1, D), tm5, S)
    return out.reshape(B, S, D)
```

```python
import functools
import math

import numpy as np
import jax
import jax.numpy as jnp
from jax import lax
from jax.experimental import pallas as pl
from jax.experimental.pallas import tpu as pltpu

F32 = jnp.float32
BF16 = jnp.bfloat16
I32 = jnp.int32
HIGHEST = lax.Precision.HIGHEST

A_HEADS = 8
A_HEAD_DIM = 64
A_WIDTH = A_HEADS * A_HEAD_DIM
IDX_HEADS = 16
IDX_DIM = 32
TOPK_MAX = 256
QBLK = 128
REL_BUCKETS = 32
REL_MAX_DIST = 128
G_HEADS = 4
G_DK = 64
G_DV = 128
G_KW = G_HEADS * G_DK
G_VW = G_HEADS * G_DV
G_RANK = 16
G_TAU = 16.0
G_CHUNK = 64
N_GROUPS = 4
EXPERTS_PER_GROUP = 8
N_EXPERTS = N_GROUPS * EXPERTS_PER_GROUP
D_EXPERT = 256
DEPTH = 1
DN_ALPHA = (2.0 * DEPTH) ** 0.25
LN_EPS = 1e-5
SPLIT_SIZES = (A_WIDTH, A_WIDTH, A_WIDTH, IDX_HEADS * IDX_DIM, IDX_DIM, IDX_HEADS,
               G_KW, G_KW, G_VW, G_VW, G_RANK, 1024, 1024)

LANES = 128
VMEM_LIMIT_BYTES = 56 * 1024 * 1024

NEG = -1e30
LOG2E = math.log2(math.e)
INT_MIN = -2 ** 31
INT_MAX = 2 ** 31 - 1
KEY_NEG_INF = -2 ** 31 + 0x7FFFFF

NT = (((1,), (1,)), ((), ()))
TN = (((0,), (0,)), ((), ()))


def _ln(x):
    mu = jnp.mean(x, axis=-1, keepdims=True)
    xc = x - mu
    var = jnp.mean(xc * xc, axis=-1, keepdims=True)
    return xc * lax.rsqrt(var + LN_EPS)


def _sigmoid(x):
    return 0.5 * jnp.tanh(0.5 * x) + 0.5


def _params(*sem):
    return pltpu.CompilerParams(dimension_semantics=sem, vmem_limit_bytes=VMEM_LIMIT_BYTES)


def _ada_body(c_ref, w_ref, b_ref, o_ref):
    c = c_ref[...]
    cond = c * _sigmoid(c)
    o_ref[...] = jnp.dot(cond, w_ref[...], preferred_element_type=F32, precision=HIGHEST) + b_ref[...]


def _ada_call(c, w, b):
    B, D = c.shape
    N = w.shape[1]
    tn = 1536
    return pl.pallas_call(
        _ada_body,
        grid=(N // tn,),
        in_specs=[pl.BlockSpec((B, D), lambda j: (0, 0)),
                  pl.BlockSpec((D, tn), lambda j: (0, j)),
                  pl.BlockSpec((1, tn), lambda j: (0, j))],
        out_specs=pl.BlockSpec((B, tn), lambda j: (0, j)),
        out_shape=jax.ShapeDtypeStruct((B, N), F32),
        compiler_params=_params("arbitrary"),
        name="ada",
    )(c, w, b.reshape(1, N))


TOK_K = (0, 512)
TOK_GLA = (512, 2048)
TOK_GATES = (2048, 4096)
TOK_SMALL = (4096, 4224)
CH_Q = (0, 512)
CH_V = (512, 1024)
CH_IQ = (1024, 1536)
CH_IW = (1536, 1552)
IW_SCALE = IDX_HEADS ** -0.5 * IDX_DIM ** -0.5


def _inproj_body(x_ref, sh_ref, sc_ref, wtok_ref, wch_ref,
                 k_ref, gla_ref, gates_ref, ik_ref, glr_ref, qT_ref, vT_ref, iqT_ref, iwT_ref):
    tm = x_ref.shape[1]
    h = (_ln(x_ref[0]) * (1.0 + sc_ref[0]) + sh_ref[0]).astype(BF16)

    def tok(ab):
        return jnp.dot(h, wtok_ref[:, ab[0]:ab[1]], preferred_element_type=F32)

    def ch(ab):
        return lax.dot_general(wch_ref[ab[0]:ab[1], :], h, NT, preferred_element_type=F32)

    kres = tok(TOK_K)
    for p in range(A_WIDTH // LANES):
        k_ref[0, p] = kres[:, p * LANES:(p + 1) * LANES].astype(BF16)
    gla_ref[0] = tok(TOK_GLA).astype(BF16)
    gates_ref[0] = tok(TOK_GATES).astype(BF16)
    small = tok(TOK_SMALL)
    ik_ref[0] = small[:, :IDX_DIM].astype(BF16)
    glr_ref[0] = small[:, IDX_DIM:IDX_DIM + G_RANK]

    qT_ref[0] = (ch(CH_Q) * (A_HEAD_DIM ** -0.5 * LOG2E)).astype(BF16)
    vres = ch(CH_V).astype(BF16)
    for j in range(tm // LANES):
        vT_ref[0, j] = vres[:, j * LANES:(j + 1) * LANES]
    iqT_ref[0] = ch(CH_IQ).astype(BF16)
    iwT_ref[0] = ch(CH_IW) * IW_SCALE


def _inproj_call(x, sh1, sc1, w_tok, w_ch, tm):
    B, S, D = x.shape
    nt = S // tm
    const = lambda b, t: (0, 0)
    out_shape = (
        jax.ShapeDtypeStruct((B, A_WIDTH // LANES, S, LANES), BF16),
        jax.ShapeDtypeStruct((B, S, 1536), BF16),
        jax.ShapeDtypeStruct((B, S, 2048), BF16),
        jax.ShapeDtypeStruct((B, S, IDX_DIM), BF16),
        jax.ShapeDtypeStruct((B, S, G_RANK), F32),
        jax.ShapeDtypeStruct((B, A_WIDTH, S), BF16),
        jax.ShapeDtypeStruct((B, S // LANES, A_WIDTH, LANES), BF16),
        jax.ShapeDtypeStruct((B, IDX_HEADS * IDX_DIM, S), BF16),
        jax.ShapeDtypeStruct((B, IDX_HEADS, S), F32),
    )
    out_specs = (
        pl.BlockSpec((1, A_WIDTH // LANES, tm, LANES), lambda b, t: (b, 0, t, 0)),
        pl.BlockSpec((1, tm, 1536), lambda b, t: (b, t, 0)),
        pl.BlockSpec((1, tm, 2048), lambda b, t: (b, t, 0)),
        pl.BlockSpec((1, tm, IDX_DIM), lambda b, t: (b, t, 0)),
        pl.BlockSpec((1, tm, G_RANK), lambda b, t: (b, t, 0)),
        pl.BlockSpec((1, A_WIDTH, tm), lambda b, t: (b, 0, t)),
        pl.BlockSpec((1, tm // LANES, A_WIDTH, LANES), lambda b, t: (b, t, 0, 0)),
        pl.BlockSpec((1, IDX_HEADS * IDX_DIM, tm), lambda b, t: (b, 0, t)),
        pl.BlockSpec((1, IDX_HEADS, tm), lambda b, t: (b, 0, t)),
    )
    return pl.pallas_call(
        _inproj_body,
        grid=(B, nt),
        in_specs=[pl.BlockSpec((1, tm, D), lambda b, t: (b, t, 0)),
                  pl.BlockSpec((1, 1, D), lambda b, t: (b, 0, 0)),
                  pl.BlockSpec((1, 1, D), lambda b, t: (b, 0, 0)),
                  pl.BlockSpec(w_tok.shape, const),
                  pl.BlockSpec(w_ch.shape, const)],
        out_specs=out_specs,
        out_shape=out_shape,
        compiler_params=_params("parallel", "parallel"),
        name="inproj",
    )(x, sh1, sc1, w_tok, w_ch)


IDX_CHUNK = 256
CNT_CHUNK = 256
SORT_GROUP = 4
ATT_CHUNK = 256
TBL_PAD = 2 * QBLK
ACC_ROWS = A_HEAD_DIM + 16


def _rel_bucket_table():
    s = np.arange(2 * QBLK)[:, None]
    t = np.arange(QBLK)[None, :]
    dist = np.maximum(t + QBLK - s, 0)
    max_exact = REL_BUCKETS // 2
    d_f = np.maximum(dist, 1).astype(np.float32)
    large = max_exact + (np.log(d_f / max_exact) / math.log(REL_MAX_DIST / max_exact)
                         * (REL_BUCKETS - max_exact)).astype(np.int32)
    large = np.minimum(large, REL_BUCKETS - 1)
    return np.where(dist < max_exact, dist, large).astype(np.int32)


def _far_bucket():
    max_exact = REL_BUCKETS // 2
    v = max_exact + int(np.float32(np.log(np.float32(QBLK + 1) / max_exact) / math.log(REL_MAX_DIST / max_exact)
                                   * (REL_BUCKETS - max_exact)))
    assert min(v, REL_BUCKETS - 1) == REL_BUCKETS - 1
    return REL_BUCKETS - 1


def _dsa_body(rb_ref, bkt_ref, ik_ref, kk_ref, vT_ref, qT_ref, iqT_ref, iwT_ref, o_ref,
              sc_s, srt_s, thr_s, madd_s, tbl_s, oT_s, xcut_s, qm_s, sa_s, sb_s, acc_s, mall_s, mblk_s,
              *, topk, idx_bits, max_cnt):
    i = pl.program_id(1)
    nck = (i + 2) // 2
    t_idx = i * QBLK + lax.broadcasted_iota(I32, (1, QBLK), 1)

    @pl.when(i == 0)
    def _():
        bkt = bkt_ref[...]
        tbl_s[...] = jnp.zeros_like(tbl_s)
        for h in range(A_HEADS):
            t = jnp.zeros((2 * QBLK, QBLK), F32)
            for k in range(REL_BUCKETS):
                t = jnp.where(bkt == k, rb_ref[k, h], t)
            tbl_s[h, TBL_PAD:TBL_PAD + 2 * QBLK, :] = (t - rb_ref[_far_bucket(), h]) * LOG2E

    def key_to_float(key):
        key = jnp.maximum(key, KEY_NEG_INF)
        return pltpu.bitcast(jnp.where(key < 0, key ^ INT_MAX, key), F32)

    def score_chunk(c, carry):
        s0 = pl.multiple_of(c * IDX_CHUNK, IDX_CHUNK)
        kc = ik_ref[0, pl.ds(s0, IDX_CHUNK), :]
        acc = jnp.zeros((IDX_CHUNK, QBLK), F32)
        for hp in range(IDX_HEADS // 2):
            r0 = hp * 2 * IDX_DIM
            rhs = jnp.concatenate([iqT_ref[0, r0:r0 + IDX_DIM, :],
                                   iqT_ref[0, r0 + IDX_DIM:r0 + 2 * IDX_DIM, :]], axis=1)
            z = jnp.dot(kc, rhs, preferred_element_type=F32)
            acc = acc + jnp.maximum(z[:, :QBLK], 0.0) * iwT_ref[0, 2 * hp:2 * hp + 1, :]
            acc = acc + jnp.maximum(z[:, QBLK:], 0.0) * iwT_ref[0, 2 * hp + 1:2 * hp + 2, :]
        s_idx = s0 + lax.broadcasted_iota(I32, (IDX_CHUNK, QBLK), 0)
        sc_s[pl.ds(s0, IDX_CHUNK), :] = jnp.where(s_idx <= t_idx, acc, -jnp.inf)
        return carry

    ncnt = nck
    lax.fori_loop(0, nck // 2, lambda c, carry: score_chunk(2 * c + 1, score_chunk(2 * c, carry)), 0)

    @pl.when(nck % 2 == 1)
    def _():
        score_chunk(nck - 1, 0)

    def count(pred):
        def body(c, cnt):
            s0 = pl.multiple_of(c * CNT_CHUNK, CNT_CHUNK)
            k = sc_s[pl.ds(s0, CNT_CHUNK), :]
            s_idx = s0 + lax.broadcasted_iota(I32, (CNT_CHUNK, QBLK), 0)
            m = jnp.where(pred(k, s_idx), 1, 0)
            return cnt + jnp.sum(m.reshape(CNT_CHUNK // 8, 8, QBLK), axis=0)
        cnt = lax.fori_loop(0, ncnt, body, jnp.zeros((8, QBLK), I32))
        return jnp.sum(cnt, axis=0, keepdims=True)

    def search_block(n):
        groups = n * CNT_CHUNK // (8 * SORT_GROUP)

        for g in range(groups):
            v = [sc_s[(SORT_GROUP * g + u) * 8:(SORT_GROUP * g + u + 1) * 8, :] for u in range(SORT_GROUP)]
            for a, b in ((0, 1), (2, 3), (0, 2), (1, 3), (1, 2)):
                v[a], v[b] = jnp.maximum(v[a], v[b]), jnp.minimum(v[a], v[b])
            for u in range(SORT_GROUP):
                srt_s[(SORT_GROUP * g + u) * 8:(SORT_GROUP * g + u + 1) * 8, :] = v[u]

        def count_ge(cand_key):
            cand = key_to_float(cand_key)
            parts = []
            for g in range(groups):
                cnt = 0
                for u in range(SORT_GROUP):
                    tile = srt_s[(SORT_GROUP * g + u) * 8:(SORT_GROUP * g + u + 1) * 8, :]
                    cnt = jnp.where(tile >= cand, u + 1, cnt)
                parts.append(cnt)
            while len(parts) > 1:
                odd = parts[len(parts) & ~1:]
                parts = [parts[j] + parts[j + 1] for j in range(0, len(parts) - 1, 2)] + odd
            return jnp.sum(parts[0], axis=0, keepdims=True)

        c0 = count_ge(jnp.zeros((1, QBLK), I32))
        ok = c0 >= topk
        T = jnp.where(ok, 0, INT_MIN).astype(I32)
        cnt_T = jnp.where(ok, c0, n * CNT_CHUNK)

        def bit_body(j, carry):
            T, cnt_T = carry
            cand = T | jnp.left_shift(jnp.int32(1), 30 - j)
            c = count_ge(cand)
            ok = c >= topk
            return jnp.where(ok, cand, T), jnp.where(ok, c, cnt_T)

        T, cnt_T = lax.fori_loop(0, 31, bit_body, (T, cnt_T))
        thr_s[0:1, :] = T
        thr_s[1:2, :] = cnt_T
        T = jnp.maximum(T, KEY_NEG_INF)
        thr_s[2:3, :] = count_ge(jnp.where(T == INT_MAX, T, T + 1))

    for n in range(1, max_cnt + 1):
        pl.when(ncnt == n)(functools.partial(search_block, n))
    T_key = jnp.maximum(thr_s[0:1, :], KEY_NEG_INF)
    T = key_to_float(T_key)
    cnt_ge = thr_s[1:2, :]
    cnt_gt = thr_s[2:3, :]

    need = topk - cnt_gt
    excess = jnp.where((cnt_ge - cnt_gt > need) & (T_key > KEY_NEG_INF), 1.0, 0.0)
    xcut_s[...] = jnp.full((1, QBLK), INT_MAX, I32)

    @pl.when(jnp.max(excess) > 0.0)
    def _():
        X = jnp.zeros((1, QBLK), I32)
        for b in range(idx_bits - 1, -1, -1):
            cand = X | (1 << b)
            f = count(lambda k, s: (k == T) & (s < cand))
            X = jnp.where(f < need, cand, X)
        xcut_s[...] = X

    xcut = xcut_s[...]

    def mask_chunk(c, carry):
        s0 = pl.multiple_of(c * IDX_CHUNK, IDX_CHUNK)
        k = sc_s[pl.ds(s0, IDX_CHUNK), :]
        s_idx = s0 + lax.broadcasted_iota(I32, (IDX_CHUNK, QBLK), 0)
        sel = ((k > T) | ((k == T) & (s_idx <= xcut))) & (s_idx <= t_idx)
        madd_s[pl.ds(s0, IDX_CHUNK), :] = jnp.where(sel, 0.0, NEG)
        return carry

    lax.fori_loop(0, nck, mask_chunk, 0)

    c_last = i // 2
    even = 1 - (i - 2 * c_last)
    row_head = lax.broadcasted_iota(I32, (LANES, QBLK), 0) // A_HEAD_DIM
    for h in range(A_HEADS):
        qp = qT_ref[0, (h // 2) * LANES:(h // 2 + 1) * LANES, :]
        qm_s[h] = jnp.where(row_head == h % 2, qp, jnp.zeros_like(qp))
    acc_s[...] = jnp.zeros_like(acc_s)
    ones = jnp.ones((ACC_ROWS - A_HEAD_DIM, ATT_CHUNK), BF16)

    def logits(c, s_buf, biased=True):
        s0 = pl.multiple_of(c * ATT_CHUNK, ATT_CHUNK)
        madd = madd_s[pl.ds(s0, ATT_CHUNK), :]
        off = jnp.where(c == c_last, 2 * QBLK + QBLK * even, jnp.where(c == c_last - 1, QBLK * even, 0))
        off = pl.multiple_of(off, QBLK)
        m_blk = []
        for h in range(A_HEADS):
            kc = kk_ref[0, h // 2, pl.ds(s0, ATT_CHUNK), :]
            s = jnp.dot(kc, qm_s[h], preferred_element_type=F32) + madd
            if biased:
                s = s + tbl_s[h, pl.ds(off, ATT_CHUNK), :]
            s_buf[h] = s
            m_blk.append(jnp.max(s, axis=0, keepdims=True))
        return jnp.concatenate(m_blk, axis=0)

    def accumulate(c, s_buf, m_all, m_blk):
        m_new = jnp.maximum(m_all, m_blk)
        alpha = jnp.exp2(m_all - m_new)
        for h in range(A_HEADS):
            rows = slice(h * A_HEAD_DIM, (h + 1) * A_HEAD_DIM)
            p = jnp.exp2(s_buf[h] - m_new[h:h + 1]).astype(BF16)
            vt = jnp.concatenate([vT_ref[0, 2 * c + u, rows, :] for u in range(ATT_CHUNK // QBLK)], axis=1)
            vt = jnp.concatenate([vt, ones], axis=0)
            acc_s[h] = alpha[h:h + 1] * acc_s[h] + jnp.dot(vt, p, preferred_element_type=F32)
        return m_new

    def att_body(biased, pair, carry):
        m_all, m_blk = carry
        c = 2 * pair
        m_b = logits(c + 1, sb_s, biased)
        m_all = accumulate(c, sa_s, m_all, m_blk)
        m_a = logits(c + 2, sa_s, biased)
        return accumulate(c + 1, sb_s, m_all, m_b), m_a

    n_pairs = c_last // 2
    n_far_pairs = jnp.maximum(n_pairs - 1, 0)
    carry = (jnp.full((A_HEADS, QBLK), NEG, F32), logits(0, sa_s))
    carry = lax.fori_loop(0, n_far_pairs, functools.partial(att_body, False), carry)
    m_all, m_blk = lax.fori_loop(n_far_pairs, n_pairs, functools.partial(att_body, True), carry)
    mall_s[...] = m_all
    mblk_s[...] = m_blk

    @pl.when(c_last % 2 == 1)
    def _():
        m_b = logits(c_last, sb_s)
        m_all = accumulate(c_last - 1, sa_s, mall_s[...], mblk_s[...])
        accumulate(c_last, sb_s, m_all, m_b)

    @pl.when(c_last % 2 == 0)
    def _():
        accumulate(c_last, sa_s, mall_s[...], mblk_s[...])

    for h in range(A_HEADS):
        rows = slice(h * A_HEAD_DIM, (h + 1) * A_HEAD_DIM)
        oT_s[rows, :] = acc_s[h, 0:A_HEAD_DIM, :] / acc_s[h, A_HEAD_DIM:A_HEAD_DIM + 1, :]
    o_ref[0] = oT_s[...].T.astype(BF16)


def _dsa_call(rel_bias, ik, kk, vT, qT, iqT, iwT):
    B, S, _ = ik.shape
    nb = S // QBLK
    topk = min(TOPK_MAX, S // 4)
    bkt = jnp.asarray(_rel_bucket_table())
    body = functools.partial(_dsa_body, topk=topk, idx_bits=int(math.log2(S)), max_cnt=S // CNT_CHUNK)
    return pl.pallas_call(
        body,
        grid=(B, nb),
        in_specs=[pl.BlockSpec(memory_space=pltpu.SMEM),
                  pl.BlockSpec((2 * QBLK, QBLK), lambda b, i: (0, 0)),
                  pl.BlockSpec((1, S, IDX_DIM), lambda b, i: (b, 0, 0)),
                  pl.BlockSpec((1, A_WIDTH // LANES, S, LANES), lambda b, i: (b, 0, 0, 0)),
                  pl.BlockSpec((1, S // LANES, A_WIDTH, LANES), lambda b, i: (b, 0, 0, 0)),
                  pl.BlockSpec((1, A_WIDTH, QBLK), lambda b, i: (b, 0, i)),
                  pl.BlockSpec((1, IDX_HEADS * IDX_DIM, QBLK), lambda b, i: (b, 0, i)),
                  pl.BlockSpec((1, IDX_HEADS, QBLK), lambda b, i: (b, 0, i))],
        out_specs=pl.BlockSpec((1, QBLK, A_WIDTH), lambda b, i: (b, i, 0)),
        out_shape=jax.ShapeDtypeStruct((B, S, A_WIDTH), BF16),
        scratch_shapes=[pltpu.VMEM((S, QBLK), F32),
                        pltpu.VMEM((S, QBLK), F32),
                        pltpu.VMEM((8, QBLK), I32),
                        pltpu.VMEM((S, QBLK), F32),
                        pltpu.VMEM((A_HEADS, TBL_PAD + 3 * QBLK, QBLK), F32),
                        pltpu.VMEM((A_WIDTH, QBLK), F32),
                        pltpu.VMEM((1, QBLK), I32),
                        pltpu.VMEM((A_HEADS, LANES, QBLK), BF16),
                        pltpu.VMEM((A_HEADS, ATT_CHUNK, QBLK), F32),
                        pltpu.VMEM((A_HEADS, ATT_CHUNK, QBLK), F32),
                        pltpu.VMEM((A_HEADS, ACC_ROWS, QBLK), F32),
                        pltpu.VMEM((A_HEADS, QBLK), F32),
                        pltpu.VMEM((A_HEADS, QBLK), F32)],
        compiler_params=_params("parallel", "arbitrary"),
        name="dsa",
    )(rel_bias, bkt, ik, kk, vT, qT, iqT, iwT)


GLA_Q = (0, 256)
GLA_K = (256, 512)
GLA_V = (512, 1024)
GLA_R = (1024, 1536)


def _gla_body(gla_ref, glr_ref, wg_ref, bg_ref, ng_ref, o_ref, st_s):
    tg = gla_ref.shape[1]
    C = G_CHUNK

    @pl.when(pl.program_id(1) == 0)
    def _():
        st_s[...] = jnp.zeros_like(st_s)

    glr = glr_ref[0]
    wg = wg_ref[...]
    glr_hi, wg_hi = glr.astype(BF16), wg.astype(BF16)
    glr_lo = (glr - glr_hi.astype(F32)).astype(BF16)
    wg_lo = (wg - wg_hi.astype(F32)).astype(BF16)
    xg = (jnp.dot(glr_hi, wg_hi, preferred_element_type=F32) + jnp.dot(glr_hi, wg_lo, preferred_element_type=F32)
          + jnp.dot(glr_lo, wg_hi, preferred_element_type=F32)) + bg_ref[...]
    logg = -(jnp.maximum(-xg, 0.0) + jnp.log1p(jnp.exp(-jnp.abs(xg)))) * (1.0 / G_TAU)

    rt = lax.broadcasted_iota(I32, (tg, tg), 0)
    ct = lax.broadcasted_iota(I32, (tg, tg), 1)
    cum = jnp.where((rt // C == ct // C) & (rt >= ct), 1.0, 0.0).astype(BF16)
    logg_hi = logg.astype(BF16)
    logg_lo = (logg - logg_hi.astype(F32)).astype(BF16)
    bc_all = (jnp.dot(cum, logg_hi, preferred_element_type=F32)
              + jnp.dot(cum, logg_lo, preferred_element_type=F32))

    bl_all = jnp.concatenate([jnp.broadcast_to(bc_all[(ck + 1) * C - 1:(ck + 1) * C, :], (C, G_KW))
                              for ck in range(tg // C)], axis=0)

    q_all = gla_ref[0, :, GLA_Q[0]:GLA_Q[1]].astype(F32) * (G_DK ** -0.5)
    k_all = gla_ref[0, :, GLA_K[0]:GLA_K[1]].astype(F32)
    q_in_all = (q_all * jnp.exp(bc_all)).astype(BF16)
    k_st_all = (k_all * jnp.exp(bl_all - bc_all)).astype(BF16)
    q_rel_all = q_all * jnp.exp(bc_all - bl_all)
    decay_all = jnp.exp(bl_all)

    ri = lax.broadcasted_iota(I32, (C, C), 0)
    ci = lax.broadcasted_iota(I32, (C, C), 1)
    tril = ri >= ci
    lane_head = lax.broadcasted_iota(I32, (C, LANES), 1) // G_DK
    st_rows = lax.broadcasted_iota(I32, (2 * G_DV, LANES), 0) // G_DV
    st_cols = lax.broadcasted_iota(I32, (2 * G_DV, LANES), 1) // G_DK
    st_diag = st_rows == st_cols
    n_ck = tg // C
    n_p = G_HEADS // 2
    units = [(ck, p) for ck in range(n_ck) for p in range(n_p)]
    rows = lambda ck: slice(ck * C, (ck + 1) * C)
    lanes = lambda p: slice(p * LANES, (p + 1) * LANES)
    v_of = lambda ck, p: gla_ref[0, rows(ck), GLA_V[0] + p * 2 * G_DV:GLA_V[0] + (p + 1) * 2 * G_DV]

    att = {}
    for ck, p in units:
        k_st = k_st_all[rows(ck), lanes(p)]
        for sub in range(2):
            qm = jnp.where(lane_head == sub, q_rel_all[rows(ck), lanes(p)], 0.0).astype(BF16)
            a = lax.dot_general(qm, k_st, NT, preferred_element_type=F32)
            att[ck, p, sub] = jnp.where(tril, a, 0.0).astype(BF16)
    o_intra = {}
    uT = {}
    for ck, p in units:
        v = v_of(ck, p)
        for sub in range(2):
            o_intra[ck, p, sub] = jnp.dot(att[ck, p, sub], v[:, sub * G_DV:(sub + 1) * G_DV],
                                          preferred_element_type=F32)
        uT[ck, p] = lax.dot_general(v, k_st_all[rows(ck), lanes(p)], TN, preferred_element_type=F32)
    o_inter = {}
    for p in range(n_p):
        st = st_s[p]
        for ck in range(n_ck):
            o_inter[ck, p] = lax.dot_general(q_in_all[rows(ck), lanes(p)], st.astype(BF16), NT,
                                             preferred_element_type=F32)
            st = st * decay_all[ck * C:ck * C + 1, lanes(p)] + jnp.where(st_diag, uT[ck, p], 0.0)
        st_s[p] = st
    for ck, p in units:
        for sub in range(2):
            hd = 2 * p + sub
            o = o_intra[ck, p, sub] + o_inter[ck, p][:, sub * G_DV:(sub + 1) * G_DV]
            y = _ln(o) * ng_ref[:, hd * G_DV:(hd + 1) * G_DV]
            g = gla_ref[0, rows(ck), GLA_R[0] + hd * G_DV:GLA_R[0] + (hd + 1) * G_DV].astype(F32)
            o_ref[0, rows(ck), hd * G_DV:(hd + 1) * G_DV] = (y * (g * _sigmoid(g))).astype(BF16)


def _gla_call(gla, glr, wg, bg, ng, tg):
    B, S, _ = gla.shape
    const = lambda b, j: (0, 0)
    return pl.pallas_call(
        _gla_body,
        grid=(B, S // tg),
        in_specs=[pl.BlockSpec((1, tg, 1536), lambda b, j: (b, j, 0)),
                  pl.BlockSpec((1, tg, G_RANK), lambda b, j: (b, j, 0)),
                  pl.BlockSpec((G_RANK, G_KW), const),
                  pl.BlockSpec((1, G_KW), const),
                  pl.BlockSpec((1, G_VW), const)],
        out_specs=pl.BlockSpec((1, tg, G_VW), lambda b, j: (b, j, 0)),
        out_shape=jax.ShapeDtypeStruct((B, S, G_VW), BF16),
        scratch_shapes=[pltpu.VMEM((G_HEADS // 2, 2 * G_DV, LANES), F32)],
        compiler_params=_params("parallel", "arbitrary"),
        name="gla",
    )(gla, glr, wg, bg, ng)


ROUTER_ROWS = 40
ROUTER_E0 = 8


def _post_body(oa_ref, ob_ref, gates_ref, x_ref, gt1_ref, sh2_ref, sc2_ref, wa_ref, wb_ref, wo_ref,
               g1_ref, b1_ref, wr_ref, br_ref, x1_ref, h2_ref, gate_ref, grp_ref):
    tm = x_ref.shape[1]
    D = x_ref.shape[2]
    ya = jnp.dot(oa_ref[0], wa_ref[...], preferred_element_type=F32)
    yb = jnp.dot(ob_ref[0], wb_ref[...], preferred_element_type=F32)
    ga = gates_ref[0, :, 0:D].astype(F32)
    gb = gates_ref[0, :, D:2 * D].astype(F32)
    merged = _sigmoid(ga) * ya + _sigmoid(gb) * yb
    y = jnp.dot(merged.astype(BF16), wo_ref[...], preferred_element_type=F32)
    x1 = _ln(DN_ALPHA * x_ref[0] + gt1_ref[0] * y) * g1_ref[...] + b1_ref[...]
    x1_ref[0] = x1
    h2 = _ln(x1) * (1.0 + sc2_ref[0]) + sh2_ref[0]
    h2_hi = h2.astype(BF16)
    h2_ref[0] = h2_hi

    h2_lo = (h2 - h2_hi.astype(F32)).astype(BF16)
    wr = wr_ref[...]
    wr_hi = wr.astype(BF16)
    wr_lo = (wr - wr_hi.astype(F32)).astype(BF16)
    lt = (lax.dot_general(wr_hi, h2_hi, NT, preferred_element_type=F32)
          + lax.dot_general(wr_hi, h2_lo, NT, preferred_element_type=F32)
          + lax.dot_general(wr_lo, h2_hi, NT, preferred_element_type=F32)) + br_ref[...]
    gl = lt[0:N_GROUPS]
    gmax = jnp.max(gl, axis=0, keepdims=True)
    g_w = 1.0 / jnp.sum(jnp.exp(gl - gmax), axis=0, keepdims=True)
    r4 = lax.broadcasted_iota(I32, (N_GROUPS, tm), 0)
    g_idx = jnp.min(jnp.where(gl == gmax, r4, N_GROUPS), axis=0, keepdims=True)
    eg = jnp.zeros((EXPERTS_PER_GROUP, tm), F32)
    for g in range(N_GROUPS):
        lo = ROUTER_E0 + g * EXPERTS_PER_GROUP
        eg = jnp.where(g_idx == g, lt[lo:lo + EXPERTS_PER_GROUP], eg)
    r8 = lax.broadcasted_iota(I32, (EXPERTS_PER_GROUP, tm), 0)
    e1 = jnp.max(eg, axis=0, keepdims=True)
    i1 = jnp.min(jnp.where(eg == e1, r8, EXPERTS_PER_GROUP), axis=0, keepdims=True)
    eg2 = jnp.where(r8 == i1, -jnp.inf, eg)
    e2 = jnp.max(eg2, axis=0, keepdims=True)
    i2 = jnp.min(jnp.where(eg2 == e2, r8, EXPERTS_PER_GROUP), axis=0, keepdims=True)
    d = jnp.exp(e2 - e1)
    w1 = g_w / (1.0 + d)
    w2 = g_w * d / (1.0 + d)
    in_group = jnp.where(r8 == i1, w1, 0.0) + jnp.where(r8 == i2, w2, 0.0)
    blocks = [jnp.where(g_idx == g, in_group, 0.0) for g in range(N_GROUPS)]
    blocks.append(jnp.zeros((LANES - N_EXPERTS, tm), F32))
    gate_ref[...] = jnp.concatenate(blocks, axis=0).T
    r8g = lax.broadcasted_iota(I32, (8, tm), 0)
    grp_ref[...] = jnp.where(r8g == g_idx, 1.0, 0.0)


def _post_call(o_a, o_b, gates, x, gt1, sh2, sc2, wa, wb, wo, g1, b1, wr, br, tm):
    B, S, D = x.shape
    nt = S // tm
    const = lambda b, t: (0, 0)
    row = lambda b, t: (b, 0, 0)
    tile = lambda b, t: (b, t, 0)
    return pl.pallas_call(
        _post_body,
        grid=(B, nt),
        in_specs=[pl.BlockSpec((1, tm, A_WIDTH), tile),
                  pl.BlockSpec((1, tm, G_VW), tile),
                  pl.BlockSpec((1, tm, 2 * D), tile),
                  pl.BlockSpec((1, tm, D), tile),
                  pl.BlockSpec((1, 1, D), row),
                  pl.BlockSpec((1, 1, D), row),
                  pl.BlockSpec((1, 1, D), row),
                  pl.BlockSpec(wa.shape, const),
                  pl.BlockSpec(wb.shape, const),
                  pl.BlockSpec(wo.shape, const),
                  pl.BlockSpec((1, D), const),
                  pl.BlockSpec((1, D), const),
                  pl.BlockSpec(wr.shape, const),
                  pl.BlockSpec(br.shape, const)],
        out_specs=(pl.BlockSpec((1, tm, D), tile),
                   pl.BlockSpec((1, tm, D), tile),
                   pl.BlockSpec((tm, LANES), lambda b, t: (b * nt + t, 0)),
                   pl.BlockSpec((8, tm), lambda b, t: (0, b * nt + t))),
        out_shape=(jax.ShapeDtypeStruct((B, S, D), F32),
                   jax.ShapeDtypeStruct((B, S, D), BF16),
                   jax.ShapeDtypeStruct((B * S, LANES), F32),
                   jax.ShapeDtypeStruct((8, B * S), F32)),
        compiler_params=_params("parallel", "parallel"),
        name="post",
    )(o_a, o_b, gates, x, gt1, sh2, sc2, wa, wb, wo, g1, b1, wr, br)


MOE_EXPERTS_PER_STEP = 4
MOE_ROW_BLOCK = 128
MOE_PERM_ROWS = 256


def _moe_body(h2_ref, gate_ref, grp_ref, x1_ref, gt2_ref, w1_ref, w3_ref, w2_ref, g2_ref, b2_ref, o_ref,
              perm_s, hs_s, gs_s, ys_s, tri_s, seg_s):
    t = pl.program_id(0)
    s = pl.program_id(1)
    ne = MOE_EXPERTS_PER_STEP
    rb = MOE_ROW_BLOCK
    tm = h2_ref.shape[0]
    tm_pad = perm_s.shape[0]
    steps_per_group = EXPERTS_PER_GROUP // ne

    @pl.when((t == 0) & (s == 0))
    def _():
        r = lax.broadcasted_iota(I32, (tm, tm), 0)
        c = lax.broadcasted_iota(I32, (tm, tm), 1)
        tri_s[...] = jnp.where(r < c, 1.0, 0.0).astype(BF16)

    @pl.when(s == 0)
    def _():
        oh = grp_ref[...]
        rank = jnp.dot(oh.astype(BF16), tri_s[...], preferred_element_type=F32)
        cnt = jnp.sum(oh, axis=1, keepdims=True)
        blocks = jnp.floor((cnt + (rb - 1)) * (1.0 / rb))
        off = jnp.zeros((1, 1), F32)
        dest = jnp.zeros((1, tm), F32)
        for g in range(N_GROUPS):
            seg_s[2 * g] = jnp.sum(off).astype(I32)
            seg_s[2 * g + 1] = jnp.sum(blocks[g:g + 1, :]).astype(I32)
            dest = dest + oh[g:g + 1, :] * (off + rank[g:g + 1, :])
            off = off + blocks[g:g + 1, :] * rb
        dest_i = dest.astype(I32)
        for c in range(tm_pad // MOE_PERM_ROWS):
            rows = slice(c * MOE_PERM_ROWS, (c + 1) * MOE_PERM_ROWS)
            d_idx = c * MOE_PERM_ROWS + lax.broadcasted_iota(I32, (MOE_PERM_ROWS, tm), 0)
            perm_s[rows, :] = jnp.where(d_idx == dest_i, 1.0, 0.0).astype(BF16)
        ys_s[...] = jnp.zeros_like(ys_s)

    g = s // steps_per_group
    row0 = seg_s[2 * g]
    nblk = seg_s[2 * g + 1]

    @pl.when(s % steps_per_group == 0)
    def _():
        gate = gate_ref[...]
        g_hi = gate.astype(BF16)
        g_lo = (gate - g_hi.astype(F32)).astype(BF16)
        h2 = h2_ref[...]

        def place(i, carry):
            r0 = pl.multiple_of(row0 + i * rb, rb)
            perm = perm_s[pl.ds(r0, rb), :]
            hs_s[pl.ds(r0, rb), :] = jnp.dot(perm, h2, preferred_element_type=F32).astype(BF16)
            gs_s[pl.ds(r0, rb), :] = (jnp.dot(perm, g_hi, preferred_element_type=F32)
                                      + jnp.dot(perm, g_lo, preferred_element_type=F32))
            return carry

        lax.fori_loop(0, nblk, place, 0)

    def block(i, carry):
        r0 = pl.multiple_of(row0 + i * rb, rb)
        x = hs_s[pl.ds(r0, rb), :]
        gsel = pltpu.roll(gs_s[pl.ds(r0, rb), :], (LANES - s * ne) % LANES, axis=1)
        hid = []
        for j in range(ne):
            a = jnp.dot(x, w1_ref[j], preferred_element_type=F32)
            b = jnp.dot(x, w3_ref[j], preferred_element_type=F32)
            hid.append((a * _sigmoid(a) * b * gsel[:, j:j + 1]).astype(BF16))
        ys_s[pl.ds(r0, rb), :] += jnp.dot(jnp.concatenate(hid, axis=1), w2_ref[...], preferred_element_type=F32)
        return carry

    lax.fori_loop(0, nblk, block, 0)

    @pl.when(s == pl.num_programs(1) - 1)
    def _():
        perm = perm_s[...]
        for c in range(o_ref.shape[1] // MOE_PERM_ROWS):
            cols = slice(c * MOE_PERM_ROWS, (c + 1) * MOE_PERM_ROWS)
            o_ref[:, cols] = lax.dot_general(perm, ys_s[:, cols].astype(BF16), TN, preferred_element_type=F32)
        z = DN_ALPHA * x1_ref[...] + gt2_ref[0] * o_ref[...]
        o_ref[...] = _ln(z) * g2_ref[...] + b2_ref[...]


def _moe_call(h2, gate, grp, x1, gt2, w1, w3, w2, g2, b2, tm, S):
    T, D = h2.shape
    ne = MOE_EXPERTS_PER_STEP
    nc = N_EXPERTS // ne
    tm_pad = tm + N_GROUPS * MOE_ROW_BLOCK
    tiles_per_seq = S // tm
    tile = lambda t, c: (t, 0)
    const = lambda t, c: (0, 0)
    return pl.pallas_call(
        _moe_body,
        grid=(T // tm, nc),
        in_specs=[pl.BlockSpec((tm, D), tile),
                  pl.BlockSpec((tm, LANES), tile),
                  pl.BlockSpec((8, tm), lambda t, c: (0, t)),
                  pl.BlockSpec((tm, D), tile),
                  pl.BlockSpec((1, 1, D), lambda t, c: (t // tiles_per_seq, 0, 0)),
                  pl.BlockSpec((ne, D, D_EXPERT), lambda t, c: (c, 0, 0)),
                  pl.BlockSpec((ne, D, D_EXPERT), lambda t, c: (c, 0, 0)),
                  pl.BlockSpec((ne * D_EXPERT, D), lambda t, c: (c, 0)),
                  pl.BlockSpec((1, D), const),
                  pl.BlockSpec((1, D), const)],
        out_specs=pl.BlockSpec((tm, D), tile),
        out_shape=jax.ShapeDtypeStruct((T, D), F32),
        scratch_shapes=[pltpu.VMEM((tm_pad, tm), BF16),
                        pltpu.VMEM((tm_pad, D), BF16),
                        pltpu.VMEM((tm_pad, LANES), F32),
                        pltpu.VMEM((tm_pad, D), F32),
                        pltpu.VMEM((tm, tm), BF16),
                        pltpu.SMEM((2 * N_GROUPS,), I32)],
        compiler_params=_params("arbitrary", "arbitrary"),
        name="moe",
    )(h2, gate, grp, x1, gt2, w1, w3, w2, g2, b2)


def _pick(n, pref):
    return pref if n % pref == 0 else n


def kernel(x, c, rel_bias, w_ada, b_ada, w_in, gla_w_gate, gla_b_gate, gla_norm_g, w_branch_a, w_branch_b, w_out, ln1_g, ln1_b, w_router_group, b_router_group, w_router_expert, b_router_expert, w_exp_gate, w_exp_up, w_exp_down, ln2_g, ln2_b):
    B, S, D = x.shape
    assert S % (2 * QBLK) == 0 and D == 1024 and w_ada.shape[0] == DEPTH == 1
    l = 0

    ada = _ada_call(c, w_ada[l], b_ada[l])
    sh1, sc1, gt1, sh2, sc2, gt2 = [ada[:, i * D:(i + 1) * D].reshape(B, 1, D) for i in range(6)]

    offs = np.concatenate([[0], np.cumsum(SPLIT_SIZES)])
    seg = lambda i: w_in[l][:, offs[i]:offs[i + 1]]
    (w_aq, w_ak, w_av, w_iq, w_ik, w_iw, w_gq, w_gk, w_gv, w_gr, w_glr, w_ga, w_gb) = [seg(i) for i in range(13)]
    pad = jnp.zeros((D, TOK_SMALL[1] - TOK_SMALL[0] - IDX_DIM - G_RANK), F32)
    w_tok = jnp.concatenate([w_ak, w_gq, w_gk, w_gv, w_gr, w_ga, w_gb, w_ik, w_glr, pad], axis=1).astype(BF16)
    w_ch = jnp.concatenate([w_aq, w_av, w_iq, w_iw], axis=1).T.astype(BF16)

    tm = _pick(S, 512)
    kk, gla, gates, ik, glr, qT, vT, iqT, iwT = _inproj_call(x, sh1, sc1, w_tok, w_ch, tm)

    o_a = _dsa_call(rel_bias, ik, kk, vT, qT, iqT, iwT)
    o_b = _gla_call(gla, glr, gla_w_gate[l], gla_b_gate[l].reshape(1, G_KW), gla_norm_g[l].reshape(1, G_VW),
                    _pick(S, 256))

    wr = jnp.zeros((ROUTER_ROWS, D), F32)
    wr = wr.at[0:N_GROUPS].set(w_router_group[l].T).at[ROUTER_E0:ROUTER_E0 + N_EXPERTS].set(w_router_expert[l].T)
    br = jnp.zeros((ROUTER_ROWS, 1), F32)
    br = br.at[0:N_GROUPS, 0].set(b_router_group[l]).at[ROUTER_E0:ROUTER_E0 + N_EXPERTS, 0].set(b_router_expert[l])
    x1, h2, gate, grp = _post_call(o_a, o_b, gates, x, gt1, sh2, sc2,
                              w_branch_a[l].astype(BF16), w_branch_b[l].astype(BF16), w_out[l].astype(BF16),
                              ln1_g[l].reshape(1, D), ln1_b[l].reshape(1, D), wr, br, tm)

    tm5 = _pick(S, 1024)
    out = _moe_call(h2.reshape(B * S, D), gate, grp, x1.reshape(B * S, D), gt2,
                    w_exp_gate[l].astype(BF16), w_exp_up[l].astype(BF16),
                    w_exp_down[l].astype(BF16).reshape(N_EXPERTS * D_EXPERT, D),
                    ln2_g[l].reshape(1, D), ln2_b[l].reshape(1, D), tm5, S)
    return out.reshape(B, S, D)
```

```python
import functools
import math

import numpy as np
import jax
import jax.numpy as jnp
from jax import lax
from jax.experimental import pallas as pl
from jax.experimental.pallas import tpu as pltpu

F32 = jnp.float32
BF16 = jnp.bfloat16
I32 = jnp.int32
HIGHEST = lax.Precision.HIGHEST

A_HEADS = 8
A_HEAD_DIM = 64
A_WIDTH = A_HEADS * A_HEAD_DIM
IDX_HEADS = 16
IDX_DIM = 32
TOPK_MAX = 256
QBLK = 128
REL_BUCKETS = 32
REL_MAX_DIST = 128
G_HEADS = 4
G_DK = 64
G_DV = 128
G_KW = G_HEADS * G_DK
G_VW = G_HEADS * G_DV
G_RANK = 16
G_TAU = 16.0
G_CHUNK = 64
N_GROUPS = 4
EXPERTS_PER_GROUP = 8
N_EXPERTS = N_GROUPS * EXPERTS_PER_GROUP
D_EXPERT = 256
DEPTH = 1
DN_ALPHA = (2.0 * DEPTH) ** 0.25
LN_EPS = 1e-5
SPLIT_SIZES = (A_WIDTH, A_WIDTH, A_WIDTH, IDX_HEADS * IDX_DIM, IDX_DIM, IDX_HEADS,
               G_KW, G_KW, G_VW, G_VW, G_RANK, 1024, 1024)

LANES = 128
VMEM_LIMIT_BYTES = 56 * 1024 * 1024

NEG = -1e30
LOG2E = math.log2(math.e)
INT_MIN = -2 ** 31
INT_MAX = 2 ** 31 - 1
KEY_NEG_INF = -2 ** 31 + 0x7FFFFF

NT = (((1,), (1,)), ((), ()))
TN = (((0,), (0,)), ((), ()))


def _ln(x):
    mu = jnp.mean(x, axis=-1, keepdims=True)
    xc = x - mu
    var = jnp.mean(xc * xc, axis=-1, keepdims=True)
    return xc * lax.rsqrt(var + LN_EPS)


def _sigmoid(x):
    return 0.5 * jnp.tanh(0.5 * x) + 0.5


def _params(*sem):
    return pltpu.CompilerParams(dimension_semantics=sem, vmem_limit_bytes=VMEM_LIMIT_BYTES)


def _ada_body(c_ref, w_ref, b_ref, o_ref):
    c = c_ref[...]
    cond = c * _sigmoid(c)
    o_ref[...] = jnp.dot(cond, w_ref[...], preferred_element_type=F32, precision=HIGHEST) + b_ref[...]


def _ada_call(c, w, b):
    B, D = c.shape
    N = w.shape[1]
    tn = 1536
    return pl.pallas_call(
        _ada_body,
        grid=(N // tn,),
        in_specs=[pl.BlockSpec((B, D), lambda j: (0, 0)),
                  pl.BlockSpec((D, tn), lambda j: (0, j)),
                  pl.BlockSpec((1, tn), lambda j: (0, j))],
        out_specs=pl.BlockSpec((B, tn), lambda j: (0, j)),
        out_shape=jax.ShapeDtypeStruct((B, N), F32),
        compiler_params=_params("arbitrary"),
        name="ada",
    )(c, w, b.reshape(1, N))


TOK_K = (0, 512)
TOK_GLA = (512, 2048)
TOK_GATES = (2048, 4096)
TOK_SMALL = (4096, 4224)
CH_Q = (0, 512)
CH_V = (512, 1024)
CH_IQ = (1024, 1536)
CH_IW = (1536, 1552)
IW_SCALE = IDX_HEADS ** -0.5 * IDX_DIM ** -0.5


def _inproj_body(x_ref, sh_ref, sc_ref, wtok_ref, wch_ref,
                 k_ref, gla_ref, gates_ref, ik_ref, glr_ref, qT_ref, vT_ref, iqT_ref, iwT_ref):
    tm = x_ref.shape[1]
    h = (_ln(x_ref[0]) * (1.0 + sc_ref[0]) + sh_ref[0]).astype(BF16)

    def tok(ab):
        return jnp.dot(h, wtok_ref[:, ab[0]:ab[1]], preferred_element_type=F32)

    def ch(ab):
        return lax.dot_general(wch_ref[ab[0]:ab[1], :], h, NT, preferred_element_type=F32)

    kres = tok(TOK_K)
    for p in range(A_WIDTH // LANES):
        k_ref[0, p] = kres[:, p * LANES:(p + 1) * LANES].astype(BF16)
    gla_ref[0] = tok(TOK_GLA).astype(BF16)
    gates_ref[0] = tok(TOK_GATES).astype(BF16)
    small = tok(TOK_SMALL)
    ik_ref[0] = small[:, :IDX_DIM].astype(BF16)
    glr_ref[0] = small[:, IDX_DIM:IDX_DIM + G_RANK]

    qT_ref[0] = (ch(CH_Q) * (A_HEAD_DIM ** -0.5 * LOG2E)).astype(BF16)
    vres = ch(CH_V).astype(BF16)
    for j in range(tm // LANES):
        vT_ref[0, j] = vres[:, j * LANES:(j + 1) * LANES]
    iqT_ref[0] = ch(CH_IQ).astype(BF16)
    iwT_ref[0] = ch(CH_IW) * IW_SCALE


def _inproj_call(x, sh1, sc1, w_tok, w_ch, tm):
    B, S, D = x.shape
    nt = S // tm
    const = lambda b, t: (0, 0)
    out_shape = (
        jax.ShapeDtypeStruct((B, A_WIDTH // LANES, S, LANES), BF16),
        jax.ShapeDtypeStruct((B, S, 1536), BF16),
        jax.ShapeDtypeStruct((B, S, 2048), BF16),
        jax.ShapeDtypeStruct((B, S, IDX_DIM), BF16),
        jax.ShapeDtypeStruct((B, S, G_RANK), F32),
        jax.ShapeDtypeStruct((B, A_WIDTH, S), BF16),
        jax.ShapeDtypeStruct((B, S // LANES, A_WIDTH, LANES), BF16),
        jax.ShapeDtypeStruct((B, IDX_HEADS * IDX_DIM, S), BF16),
        jax.ShapeDtypeStruct((B, IDX_HEADS, S), F32),
    )
    out_specs = (
        pl.BlockSpec((1, A_WIDTH // LANES, tm, LANES), lambda b, t: (b, 0, t, 0)),
        pl.BlockSpec((1, tm, 1536), lambda b, t: (b, t, 0)),
        pl.BlockSpec((1, tm, 2048), lambda b, t: (b, t, 0)),
        pl.BlockSpec((1, tm, IDX_DIM), lambda b, t: (b, t, 0)),
        pl.BlockSpec((1, tm, G_RANK), lambda b, t: (b, t, 0)),
        pl.BlockSpec((1, A_WIDTH, tm), lambda b, t: (b, 0, t)),
        pl.BlockSpec((1, tm // LANES, A_WIDTH, LANES), lambda b, t: (b, t, 0, 0)),
        pl.BlockSpec((1, IDX_HEADS * IDX_DIM, tm), lambda b, t: (b, 0, t)),
        pl.BlockSpec((1, IDX_HEADS, tm), lambda b, t: (b, 0, t)),
    )
    return pl.pallas_call(
        _inproj_body,
        grid=(B, nt),
        in_specs=[pl.BlockSpec((1, tm, D), lambda b, t: (b, t, 0)),
                  pl.BlockSpec((1, 1, D), lambda b, t: (b, 0, 0)),
                  pl.BlockSpec((1, 1, D), lambda b, t: (b, 0, 0)),
                  pl.BlockSpec(w_tok.shape, const),
                  pl.BlockSpec(w_ch.shape, const)],
        out_specs=out_specs,
        out_shape=out_shape,
        compiler_params=_params("parallel", "parallel"),
        name="inproj",
    )(x, sh1, sc1, w_tok, w_ch)


IDX_CHUNK = 256
CNT_CHUNK = 256
SORT_GROUP = 4
ATT_CHUNK = 256
TBL_PAD = 2 * QBLK
ACC_ROWS = A_HEAD_DIM + 16


def _rel_bucket_table():
    s = np.arange(2 * QBLK)[:, None]
    t = np.arange(QBLK)[None, :]
    dist = np.maximum(t + QBLK - s, 0)
    max_exact = REL_BUCKETS // 2
    d_f = np.maximum(dist, 1).astype(np.float32)
    large = max_exact + (np.log(d_f / max_exact) / math.log(REL_MAX_DIST / max_exact)
                         * (REL_BUCKETS - max_exact)).astype(np.int32)
    large = np.minimum(large, REL_BUCKETS - 1)
    return np.where(dist < max_exact, dist, large).astype(np.int32)


def _far_bucket():
    max_exact = REL_BUCKETS // 2
    v = max_exact + int(np.float32(np.log(np.float32(QBLK + 1) / max_exact) / math.log(REL_MAX_DIST / max_exact)
                                   * (REL_BUCKETS - max_exact)))
    assert min(v, REL_BUCKETS - 1) == REL_BUCKETS - 1
    return REL_BUCKETS - 1


def _dsa_body(rb_ref, bkt_ref, ik_ref, kk_ref, vT_ref, qT_ref, iqT_ref, iwT_ref, o_ref,
              sc_s, srt_s, thr_s, madd_s, tbl_s, oT_s, xcut_s, qm_s, sa_s, sb_s, acc_s, mall_s, mblk_s,
              *, topk, idx_bits, max_cnt):
    i = pl.program_id(1)
    nck = (i + 2) // 2
    t_idx = i * QBLK + lax.broadcasted_iota(I32, (1, QBLK), 1)

    @pl.when(i == 0)
    def _():
        bkt = bkt_ref[...]
        tbl_s[...] = jnp.zeros_like(tbl_s)
        for h in range(A_HEADS):
            t = jnp.zeros((2 * QBLK, QBLK), F32)
            for k in range(REL_BUCKETS):
                t = jnp.where(bkt == k, rb_ref[k, h], t)
            tbl_s[h, TBL_PAD:TBL_PAD + 2 * QBLK, :] = (t - rb_ref[_far_bucket(), h]) * LOG2E

    def key_to_float(key):
        key = jnp.maximum(key, KEY_NEG_INF)
        return pltpu.bitcast(jnp.where(key < 0, key ^ INT_MAX, key), F32)

    def score_chunk(c, carry):
        s0 = pl.multiple_of(c * IDX_CHUNK, IDX_CHUNK)
        kc = ik_ref[0, pl.ds(s0, IDX_CHUNK), :]
        acc = jnp.zeros((IDX_CHUNK, QBLK), F32)
        for hp in range(IDX_HEADS // 2):
            r0 = hp * 2 * IDX_DIM
            rhs = jnp.concatenate([iqT_ref[0, r0:r0 + IDX_DIM, :],
                                   iqT_ref[0, r0 + IDX_DIM:r0 + 2 * IDX_DIM, :]], axis=1)
            z = jnp.dot(kc, rhs, preferred_element_type=F32)
            acc = acc + jnp.maximum(z[:, :QBLK], 0.0) * iwT_ref[0, 2 * hp:2 * hp + 1, :]
            acc = acc + jnp.maximum(z[:, QBLK:], 0.0) * iwT_ref[0, 2 * hp + 1:2 * hp + 2, :]
        s_idx = s0 + lax.broadcasted_iota(I32, (IDX_CHUNK, QBLK), 0)
        sc_s[pl.ds(s0, IDX_CHUNK), :] = jnp.where(s_idx <= t_idx, acc, -jnp.inf)
        return carry

    ncnt = nck
    lax.fori_loop(0, nck // 2, lambda c, carry: score_chunk(2 * c + 1, score_chunk(2 * c, carry)), 0)

    @pl.when(nck % 2 == 1)
    def _():
        score_chunk(nck - 1, 0)

    def count(pred):
        def body(c, cnt):
            s0 = pl.multiple_of(c * CNT_CHUNK, CNT_CHUNK)
            k = sc_s[pl.ds(s0, CNT_CHUNK), :]
            s_idx = s0 + lax.broadcasted_iota(I32, (CNT_CHUNK, QBLK), 0)
            m = jnp.where(pred(k, s_idx), 1, 0)
            return cnt + jnp.sum(m.reshape(CNT_CHUNK // 8, 8, QBLK), axis=0)
        cnt = lax.fori_loop(0, ncnt, body, jnp.zeros((8, QBLK), I32))
        return jnp.sum(cnt, axis=0, keepdims=True)

    def search_block(n):
        groups = n * CNT_CHUNK // (8 * SORT_GROUP)

        for g in range(groups):
            v = [sc_s[(SORT_GROUP * g + u) * 8:(SORT_GROUP * g + u + 1) * 8, :] for u in range(SORT_GROUP)]
            for a, b in ((0, 1), (2, 3), (0, 2), (1, 3), (1, 2)):
                v[a], v[b] = jnp.maximum(v[a], v[b]), jnp.minimum(v[a], v[b])
            for u in range(SORT_GROUP):
                srt_s[(SORT_GROUP * g + u) * 8:(SORT_GROUP * g + u + 1) * 8, :] = v[u]

        def count_ge(cand_key):
            cand = key_to_float(cand_key)
            parts = []
            for g in range(groups):
                cnt = 0
                for u in range(SORT_GROUP):
                    tile = srt_s[(SORT_GROUP * g + u) * 8:(SORT_GROUP * g + u + 1) * 8, :]
                    cnt = jnp.where(tile >= cand, u + 1, cnt)
                parts.append(cnt)
            while len(parts) > 1:
                odd = parts[len(parts) & ~1:]
                parts = [parts[j] + parts[j + 1] for j in range(0, len(parts) - 1, 2)] + odd
            return jnp.sum(parts[0], axis=0, keepdims=True)

        c0 = count_ge(jnp.zeros((1, QBLK), I32))
        ok = c0 >= topk
        T = jnp.where(ok, 0, INT_MIN).astype(I32)
        cnt_T = jnp.where(ok, c0, n * CNT_CHUNK)

        def bit_body(j, carry):
            T, cnt_T = carry
            cand = T | jnp.left_shift(jnp.int32(1), 30 - j)
            c = count_ge(cand)
            ok = c >= topk
            return jnp.where(ok, cand, T), jnp.where(ok, c, cnt_T)

        T, cnt_T = lax.fori_loop(0, 31, bit_body, (T, cnt_T))
        thr_s[0:1, :] = T
        thr_s[1:2, :] = cnt_T
        T = jnp.maximum(T, KEY_NEG_INF)
        thr_s[2:3, :] = count_ge(jnp.where(T == INT_MAX, T, T + 1))

    for n in range(1, max_cnt + 1):
        pl.when(ncnt == n)(functools.partial(search_block, n))
    T_key = jnp.maximum(thr_s[0:1, :], KEY_NEG_INF)
    T = key_to_float(T_key)
    cnt_ge = thr_s[1:2, :]
    cnt_gt = thr_s[2:3, :]

    need = topk - cnt_gt
    excess = jnp.where((cnt_ge - cnt_gt > need) & (T_key > KEY_NEG_INF), 1.0, 0.0)
    plain = (jnp.max(excess) == 0.0) & (jnp.min(T) > -jnp.inf)

    @pl.when(plain)
    def _():
        def mask_chunk(c, carry):
            s0 = pl.multiple_of(c * IDX_CHUNK, IDX_CHUNK)
            madd_s[pl.ds(s0, IDX_CHUNK), :] = jnp.where(sc_s[pl.ds(s0, IDX_CHUNK), :] >= T, 0.0, NEG)
            return carry

        lax.fori_loop(0, nck, mask_chunk, 0)

    @pl.when(jnp.logical_not(plain))
    def _():
        xcut_s[...] = jnp.full((1, QBLK), INT_MAX, I32)

        @pl.when(jnp.max(excess) > 0.0)
        def _():
            X = jnp.zeros((1, QBLK), I32)
            for b in range(idx_bits - 1, -1, -1):
                cand = X | (1 << b)
                f = count(lambda k, s: (k == T) & (s < cand))
                X = jnp.where(f < need, cand, X)
            xcut_s[...] = X

        xcut = xcut_s[...]

        def mask_chunk(c, carry):
            s0 = pl.multiple_of(c * IDX_CHUNK, IDX_CHUNK)
            k = sc_s[pl.ds(s0, IDX_CHUNK), :]
            s_idx = s0 + lax.broadcasted_iota(I32, (IDX_CHUNK, QBLK), 0)
            sel = ((k > T) | ((k == T) & (s_idx <= xcut))) & (s_idx <= t_idx)
            madd_s[pl.ds(s0, IDX_CHUNK), :] = jnp.where(sel, 0.0, NEG)
            return carry

        lax.fori_loop(0, nck, mask_chunk, 0)

    c_last = i // 2
    even = 1 - (i - 2 * c_last)
    row_head = lax.broadcasted_iota(I32, (LANES, QBLK), 0) // A_HEAD_DIM
    for h in range(A_HEADS):
        qp = qT_ref[0, (h // 2) * LANES:(h // 2 + 1) * LANES, :]
        qm_s[h] = jnp.where(row_head == h % 2, qp, jnp.zeros_like(qp))
    acc_s[...] = jnp.zeros_like(acc_s)
    ones = jnp.ones((ACC_ROWS - A_HEAD_DIM, ATT_CHUNK), BF16)

    def logits(c, s_buf, biased=True):
        s0 = pl.multiple_of(c * ATT_CHUNK, ATT_CHUNK)
        madd = madd_s[pl.ds(s0, ATT_CHUNK), :]
        off = jnp.where(c == c_last, 2 * QBLK + QBLK * even, jnp.where(c == c_last - 1, QBLK * even, 0))
        off = pl.multiple_of(off, QBLK)
        m_blk = []
        for h in range(A_HEADS):
            kc = kk_ref[0, h // 2, pl.ds(s0, ATT_CHUNK), :]
            s = jnp.dot(kc, qm_s[h], preferred_element_type=F32) + madd
            if biased:
                s = s + tbl_s[h, pl.ds(off, ATT_CHUNK), :]
            s_buf[h] = s
            m_blk.append(jnp.max(s, axis=0, keepdims=True))
        return jnp.concatenate(m_blk, axis=0)

    def accumulate(c, s_buf, m_all, m_blk):
        m_new = jnp.maximum(m_all, m_blk)
        alpha = jnp.exp2(m_all - m_new)
        for h in range(A_HEADS):
            rows = slice(h * A_HEAD_DIM, (h + 1) * A_HEAD_DIM)
            p = jnp.exp2(s_buf[h] - m_new[h:h + 1]).astype(BF16)
            vt = jnp.concatenate([vT_ref[0, 2 * c + u, rows, :] for u in range(ATT_CHUNK // QBLK)], axis=1)
            vt = jnp.concatenate([vt, ones], axis=0)
            acc_s[h] = alpha[h:h + 1] * acc_s[h] + jnp.dot(vt, p, preferred_element_type=F32)
        return m_new

    def att_body(biased, pair, carry):
        m_all, m_blk = carry
        c = 2 * pair
        m_b = logits(c + 1, sb_s, biased)
        m_all = accumulate(c, sa_s, m_all, m_blk)
        m_a = logits(c + 2, sa_s, biased)
        return accumulate(c + 1, sb_s, m_all, m_b), m_a

    n_pairs = c_last // 2
    n_far_pairs = jnp.maximum(n_pairs - 1, 0)
    carry = (jnp.full((A_HEADS, QBLK), NEG, F32), logits(0, sa_s))
    carry = lax.fori_loop(0, n_far_pairs, functools.partial(att_body, False), carry)
    m_all, m_blk = lax.fori_loop(n_far_pairs, n_pairs, functools.partial(att_body, True), carry)
    mall_s[...] = m_all
    mblk_s[...] = m_blk

    @pl.when(c_last % 2 == 1)
    def _():
        m_b = logits(c_last, sb_s)
        m_all = accumulate(c_last - 1, sa_s, mall_s[...], mblk_s[...])
        accumulate(c_last, sb_s, m_all, m_b)

    @pl.when(c_last % 2 == 0)
    def _():
        accumulate(c_last, sa_s, mall_s[...], mblk_s[...])

    for h in range(A_HEADS):
        rows = slice(h * A_HEAD_DIM, (h + 1) * A_HEAD_DIM)
        oT_s[rows, :] = acc_s[h, 0:A_HEAD_DIM, :] * (1.0 / acc_s[h, A_HEAD_DIM:A_HEAD_DIM + 1, :])
    o_ref[0] = oT_s[...].T.astype(BF16)


def _dsa_call(rel_bias, ik, kk, vT, qT, iqT, iwT):
    B, S, _ = ik.shape
    nb = S // QBLK
    topk = min(TOPK_MAX, S // 4)
    bkt = jnp.asarray(_rel_bucket_table())
    body = functools.partial(_dsa_body, topk=topk, idx_bits=int(math.log2(S)), max_cnt=S // CNT_CHUNK)
    return pl.pallas_call(
        body,
        grid=(B, nb),
        in_specs=[pl.BlockSpec(memory_space=pltpu.SMEM),
                  pl.BlockSpec((2 * QBLK, QBLK), lambda b, i: (0, 0)),
                  pl.BlockSpec((1, S, IDX_DIM), lambda b, i: (b, 0, 0)),
                  pl.BlockSpec((1, A_WIDTH // LANES, S, LANES), lambda b, i: (b, 0, 0, 0)),
                  pl.BlockSpec((1, S // LANES, A_WIDTH, LANES), lambda b, i: (b, 0, 0, 0)),
                  pl.BlockSpec((1, A_WIDTH, QBLK), lambda b, i: (b, 0, i)),
                  pl.BlockSpec((1, IDX_HEADS * IDX_DIM, QBLK), lambda b, i: (b, 0, i)),
                  pl.BlockSpec((1, IDX_HEADS, QBLK), lambda b, i: (b, 0, i))],
        out_specs=pl.BlockSpec((1, QBLK, A_WIDTH), lambda b, i: (b, i, 0)),
        out_shape=jax.ShapeDtypeStruct((B, S, A_WIDTH), BF16),
        scratch_shapes=[pltpu.VMEM((S, QBLK), F32),
                        pltpu.VMEM((S, QBLK), F32),
                        pltpu.VMEM((8, QBLK), I32),
                        pltpu.VMEM((S, QBLK), F32),
                        pltpu.VMEM((A_HEADS, TBL_PAD + 3 * QBLK, QBLK), F32),
                        pltpu.VMEM((A_WIDTH, QBLK), F32),
                        pltpu.VMEM((1, QBLK), I32),
                        pltpu.VMEM((A_HEADS, LANES, QBLK), BF16),
                        pltpu.VMEM((A_HEADS, ATT_CHUNK, QBLK), F32),
                        pltpu.VMEM((A_HEADS, ATT_CHUNK, QBLK), F32),
                        pltpu.VMEM((A_HEADS, ACC_ROWS, QBLK), F32),
                        pltpu.VMEM((A_HEADS, QBLK), F32),
                        pltpu.VMEM((A_HEADS, QBLK), F32)],
        compiler_params=_params("parallel", "arbitrary"),
        name="dsa",
    )(rel_bias, bkt, ik, kk, vT, qT, iqT, iwT)


GLA_Q = (0, 256)
GLA_K = (256, 512)
GLA_V = (512, 1024)
GLA_R = (1024, 1536)


def _gla_body(gla_ref, glr_ref, wg_ref, bg_ref, ng_ref, o_ref, st_s):
    tg = gla_ref.shape[1]
    C = G_CHUNK

    @pl.when(pl.program_id(1) == 0)
    def _():
        st_s[...] = jnp.zeros_like(st_s)

    glr = glr_ref[0]
    wg = wg_ref[...]
    glr_hi, wg_hi = glr.astype(BF16), wg.astype(BF16)
    glr_lo = (glr - glr_hi.astype(F32)).astype(BF16)
    wg_lo = (wg - wg_hi.astype(F32)).astype(BF16)
    xg = (jnp.dot(glr_hi, wg_hi, preferred_element_type=F32) + jnp.dot(glr_hi, wg_lo, preferred_element_type=F32)
          + jnp.dot(glr_lo, wg_hi, preferred_element_type=F32)) + bg_ref[...]
    logg = -(jnp.maximum(-xg, 0.0) + jnp.log1p(jnp.exp(-jnp.abs(xg)))) * (1.0 / G_TAU)

    rt = lax.broadcasted_iota(I32, (tg, tg), 0)
    ct = lax.broadcasted_iota(I32, (tg, tg), 1)
    cum = jnp.where((rt // C == ct // C) & (rt >= ct), 1.0, 0.0).astype(BF16)
    logg_hi = logg.astype(BF16)
    logg_lo = (logg - logg_hi.astype(F32)).astype(BF16)
    bc_all = (jnp.dot(cum, logg_hi, preferred_element_type=F32)
              + jnp.dot(cum, logg_lo, preferred_element_type=F32))

    bl_all = jnp.concatenate([jnp.broadcast_to(bc_all[(ck + 1) * C - 1:(ck + 1) * C, :], (C, G_KW))
                              for ck in range(tg // C)], axis=0)

    q_all = gla_ref[0, :, GLA_Q[0]:GLA_Q[1]].astype(F32) * (G_DK ** -0.5)
    k_all = gla_ref[0, :, GLA_K[0]:GLA_K[1]].astype(F32)
    q_in_all = (q_all * jnp.exp(bc_all)).astype(BF16)
    k_st_all = (k_all * jnp.exp(bl_all - bc_all)).astype(BF16)
    q_rel_all = q_all * jnp.exp(bc_all - bl_all)
    decay_all = jnp.exp(bl_all)

    ri = lax.broadcasted_iota(I32, (C, C), 0)
    ci = lax.broadcasted_iota(I32, (C, C), 1)
    tril = ri >= ci
    lane_head = lax.broadcasted_iota(I32, (C, LANES), 1) // G_DK
    st_rows = lax.broadcasted_iota(I32, (2 * G_DV, LANES), 0) // G_DV
    st_cols = lax.broadcasted_iota(I32, (2 * G_DV, LANES), 1) // G_DK
    st_diag = st_rows == st_cols
    n_ck = tg // C
    n_p = G_HEADS // 2
    units = [(ck, p) for ck in range(n_ck) for p in range(n_p)]
    rows = lambda ck: slice(ck * C, (ck + 1) * C)
    lanes = lambda p: slice(p * LANES, (p + 1) * LANES)
    v_of = lambda ck, p: gla_ref[0, rows(ck), GLA_V[0] + p * 2 * G_DV:GLA_V[0] + (p + 1) * 2 * G_DV]

    att = {}
    for ck, p in units:
        k_st = k_st_all[rows(ck), lanes(p)]
        for sub in range(2):
            qm = jnp.where(lane_head == sub, q_rel_all[rows(ck), lanes(p)], 0.0).astype(BF16)
            a = lax.dot_general(qm, k_st, NT, preferred_element_type=F32)
            att[ck, p, sub] = jnp.where(tril, a, 0.0).astype(BF16)
    o_intra = {}
    uT = {}
    for ck, p in units:
        v = v_of(ck, p)
        for sub in range(2):
            o_intra[ck, p, sub] = jnp.dot(att[ck, p, sub], v[:, sub * G_DV:(sub + 1) * G_DV],
                                          preferred_element_type=F32)
        uT[ck, p] = lax.dot_general(v, k_st_all[rows(ck), lanes(p)], TN, preferred_element_type=F32)
    o_inter = {}
    for p in range(n_p):
        st = st_s[p]
        for ck in range(n_ck):
            o_inter[ck, p] = lax.dot_general(q_in_all[rows(ck), lanes(p)], st.astype(BF16), NT,
                                             preferred_element_type=F32)
            st = st * decay_all[ck * C:ck * C + 1, lanes(p)] + jnp.where(st_diag, uT[ck, p], 0.0)
        st_s[p] = st
    for ck, p in units:
        for sub in range(2):
            hd = 2 * p + sub
            o = o_intra[ck, p, sub] + o_inter[ck, p][:, sub * G_DV:(sub + 1) * G_DV]
            y = _ln(o) * ng_ref[:, hd * G_DV:(hd + 1) * G_DV]
            g = gla_ref[0, rows(ck), GLA_R[0] + hd * G_DV:GLA_R[0] + (hd + 1) * G_DV].astype(F32)
            o_ref[0, rows(ck), hd * G_DV:(hd + 1) * G_DV] = (y * (g * _sigmoid(g))).astype(BF16)


def _gla_call(gla, glr, wg, bg, ng, tg):
    B, S, _ = gla.shape
    const = lambda b, j: (0, 0)
    return pl.pallas_call(
        _gla_body,
        grid=(B, S // tg),
        in_specs=[pl.BlockSpec((1, tg, 1536), lambda b, j: (b, j, 0)),
                  pl.BlockSpec((1, tg, G_RANK), lambda b, j: (b, j, 0)),
                  pl.BlockSpec((G_RANK, G_KW), const),
                  pl.BlockSpec((1, G_KW), const),
                  pl.BlockSpec((1, G_VW), const)],
        out_specs=pl.BlockSpec((1, tg, G_VW), lambda b, j: (b, j, 0)),
        out_shape=jax.ShapeDtypeStruct((B, S, G_VW), BF16),
        scratch_shapes=[pltpu.VMEM((G_HEADS // 2, 2 * G_DV, LANES), F32)],
        compiler_params=_params("parallel", "arbitrary"),
        name="gla",
    )(gla, glr, wg, bg, ng)


ROUTER_ROWS = 40
ROUTER_E0 = 8


def _post_body(oa_ref, ob_ref, gates_ref, x_ref, gt1_ref, sh2_ref, sc2_ref, wa_ref, wb_ref, wo_ref,
               g1_ref, b1_ref, wr_ref, br_ref, x1_ref, h2_ref, gate_ref, grp_ref):
    tm = x_ref.shape[1]
    D = x_ref.shape[2]
    ya = jnp.dot(oa_ref[0], wa_ref[...], preferred_element_type=F32)
    yb = jnp.dot(ob_ref[0], wb_ref[...], preferred_element_type=F32)
    ga = gates_ref[0, :, 0:D].astype(F32)
    gb = gates_ref[0, :, D:2 * D].astype(F32)
    merged = _sigmoid(ga) * ya + _sigmoid(gb) * yb
    y = jnp.dot(merged.astype(BF16), wo_ref[...], preferred_element_type=F32)
    x1 = _ln(DN_ALPHA * x_ref[0] + gt1_ref[0] * y) * g1_ref[...] + b1_ref[...]
    x1_ref[0] = x1
    h2 = _ln(x1) * (1.0 + sc2_ref[0]) + sh2_ref[0]
    h2_hi = h2.astype(BF16)
    h2_ref[0] = h2_hi

    h2_lo = (h2 - h2_hi.astype(F32)).astype(BF16)
    wr = wr_ref[...]
    wr_hi = wr.astype(BF16)
    wr_lo = (wr - wr_hi.astype(F32)).astype(BF16)
    lt = (lax.dot_general(wr_hi, h2_hi, NT, preferred_element_type=F32)
          + lax.dot_general(wr_hi, h2_lo, NT, preferred_element_type=F32)
          + lax.dot_general(wr_lo, h2_hi, NT, preferred_element_type=F32)) + br_ref[...]
    gl = lt[0:N_GROUPS]
    gmax = jnp.max(gl, axis=0, keepdims=True)
    g_w = 1.0 / jnp.sum(jnp.exp(gl - gmax), axis=0, keepdims=True)
    r4 = lax.broadcasted_iota(I32, (N_GROUPS, tm), 0)
    g_idx = jnp.min(jnp.where(gl == gmax, r4, N_GROUPS), axis=0, keepdims=True)
    eg = jnp.zeros((EXPERTS_PER_GROUP, tm), F32)
    for g in range(N_GROUPS):
        lo = ROUTER_E0 + g * EXPERTS_PER_GROUP
        eg = jnp.where(g_idx == g, lt[lo:lo + EXPERTS_PER_GROUP], eg)
    r8 = lax.broadcasted_iota(I32, (EXPERTS_PER_GROUP, tm), 0)
    e1 = jnp.max(eg, axis=0, keepdims=True)
    i1 = jnp.min(jnp.where(eg == e1, r8, EXPERTS_PER_GROUP), axis=0, keepdims=True)
    eg2 = jnp.where(r8 == i1, -jnp.inf, eg)
    e2 = jnp.max(eg2, axis=0, keepdims=True)
    i2 = jnp.min(jnp.where(eg2 == e2, r8, EXPERTS_PER_GROUP), axis=0, keepdims=True)
    d = jnp.exp(e2 - e1)
    w1 = g_w / (1.0 + d)
    w2 = g_w * d / (1.0 + d)
    in_group = jnp.where(r8 == i1, w1, 0.0) + jnp.where(r8 == i2, w2, 0.0)
    blocks = [jnp.where(g_idx == g, in_group, 0.0) for g in range(N_GROUPS)]
    blocks.append(jnp.zeros((LANES - N_EXPERTS, tm), F32))
    gate_ref[...] = jnp.concatenate(blocks, axis=0).T
    r8g = lax.broadcasted_iota(I32, (8, tm), 0)
    grp_ref[...] = jnp.where(r8g == g_idx, 1.0, 0.0)


def _post_call(o_a, o_b, gates, x, gt1, sh2, sc2, wa, wb, wo, g1, b1, wr, br, tm):
    B, S, D = x.shape
    nt = S // tm
    const = lambda b, t: (0, 0)
    row = lambda b, t: (b, 0, 0)
    tile = lambda b, t: (b, t, 0)
    return pl.pallas_call(
        _post_body,
        grid=(B, nt),
        in_specs=[pl.BlockSpec((1, tm, A_WIDTH), tile),
                  pl.BlockSpec((1, tm, G_VW), tile),
                  pl.BlockSpec((1, tm, 2 * D), tile),
                  pl.BlockSpec((1, tm, D), tile),
                  pl.BlockSpec((1, 1, D), row),
                  pl.BlockSpec((1, 1, D), row),
                  pl.BlockSpec((1, 1, D), row),
                  pl.BlockSpec(wa.shape, const),
                  pl.BlockSpec(wb.shape, const),
                  pl.BlockSpec(wo.shape, const),
                  pl.BlockSpec((1, D), const),
                  pl.BlockSpec((1, D), const),
                  pl.BlockSpec(wr.shape, const),
                  pl.BlockSpec(br.shape, const)],
        out_specs=(pl.BlockSpec((1, tm, D), tile),
                   pl.BlockSpec((1, tm, D), tile),
                   pl.BlockSpec((tm, LANES), lambda b, t: (b * nt + t, 0)),
                   pl.BlockSpec((8, tm), lambda b, t: (0, b * nt + t))),
        out_shape=(jax.ShapeDtypeStruct((B, S, D), F32),
                   jax.ShapeDtypeStruct((B, S, D), BF16),
                   jax.ShapeDtypeStruct((B * S, LANES), F32),
                   jax.ShapeDtypeStruct((8, B * S), F32)),
        compiler_params=_params("parallel", "parallel"),
        name="post",
    )(o_a, o_b, gates, x, gt1, sh2, sc2, wa, wb, wo, g1, b1, wr, br)


MOE_EXPERTS_PER_STEP = 4
MOE_ROW_BLOCK = 128
MOE_PERM_ROWS = 256


def _moe_body(h2_ref, gate_ref, grp_ref, x1_ref, gt2_ref, w1_ref, w3_ref, w2_ref, g2_ref, b2_ref, o_ref,
              perm_s, hs_s, gs_s, ys_s, tri_s, seg_s):
    t = pl.program_id(0)
    s = pl.program_id(1)
    ne = MOE_EXPERTS_PER_STEP
    rb = MOE_ROW_BLOCK
    tm = h2_ref.shape[0]
    tm_pad = perm_s.shape[0]
    steps_per_group = EXPERTS_PER_GROUP // ne

    @pl.when((t == 0) & (s == 0))
    def _():
        r = lax.broadcasted_iota(I32, (tm, tm), 0)
        c = lax.broadcasted_iota(I32, (tm, tm), 1)
        tri_s[...] = jnp.where(r < c, 1.0, 0.0).astype(BF16)

    @pl.when(s == 0)
    def _():
        oh = grp_ref[...]
        rank = jnp.dot(oh.astype(BF16), tri_s[...], preferred_element_type=F32)
        cnt = jnp.sum(oh, axis=1, keepdims=True)
        blocks = jnp.floor((cnt + (rb - 1)) * (1.0 / rb))
        off = jnp.zeros((1, 1), F32)
        dest = jnp.zeros((1, tm), F32)
        for g in range(N_GROUPS):
            seg_s[2 * g] = jnp.sum(off).astype(I32)
            seg_s[2 * g + 1] = jnp.sum(blocks[g:g + 1, :]).astype(I32)
            dest = dest + oh[g:g + 1, :] * (off + rank[g:g + 1, :])
            off = off + blocks[g:g + 1, :] * rb
        dest_i = dest.astype(I32)
        gate = gate_ref[...]
        g_hi = gate.astype(BF16)
        g_lo = (gate - g_hi.astype(F32)).astype(BF16)
        h2 = h2_ref[...]
        for c in range(tm_pad // MOE_PERM_ROWS):
            rows = slice(c * MOE_PERM_ROWS, (c + 1) * MOE_PERM_ROWS)
            d_idx = c * MOE_PERM_ROWS + lax.broadcasted_iota(I32, (MOE_PERM_ROWS, tm), 0)
            perm = jnp.where(d_idx == dest_i, 1.0, 0.0).astype(BF16)
            perm_s[rows, :] = perm
            hs_s[rows, :] = jnp.dot(perm, h2, preferred_element_type=F32).astype(BF16)
            gs_s[rows, :] = (jnp.dot(perm, g_hi, preferred_element_type=F32)
                             + jnp.dot(perm, g_lo, preferred_element_type=F32))
        ys_s[...] = jnp.zeros_like(ys_s)

    g = s // steps_per_group
    row0 = seg_s[2 * g]
    nblk = seg_s[2 * g + 1]

    def block(i, carry):
        r0 = pl.multiple_of(row0 + i * rb, rb)
        x = hs_s[pl.ds(r0, rb), :]
        gsel = pltpu.roll(gs_s[pl.ds(r0, rb), :], (LANES - s * ne) % LANES, axis=1)
        hid = []
        for j in range(ne):
            a = jnp.dot(x, w1_ref[j], preferred_element_type=F32)
            b = jnp.dot(x, w3_ref[j], preferred_element_type=F32)
            hid.append((a * _sigmoid(a) * b * gsel[:, j:j + 1]).astype(BF16))
        ys_s[pl.ds(r0, rb), :] += jnp.dot(jnp.concatenate(hid, axis=1), w2_ref[...], preferred_element_type=F32)
        return carry

    lax.fori_loop(0, nblk, block, 0)

    @pl.when(s == pl.num_programs(1) - 1)
    def _():
        perm = perm_s[...]
        for c in range(o_ref.shape[1] // MOE_PERM_ROWS):
            cols = slice(c * MOE_PERM_ROWS, (c + 1) * MOE_PERM_ROWS)
            o_ref[:, cols] = lax.dot_general(perm, ys_s[:, cols].astype(BF16), TN, preferred_element_type=F32)
        z = DN_ALPHA * x1_ref[...] + gt2_ref[0] * o_ref[...]
        o_ref[...] = _ln(z) * g2_ref[...] + b2_ref[...]


def _moe_call(h2, gate, grp, x1, gt2, w1, w3, w2, g2, b2, tm, S):
    T, D = h2.shape
    ne = MOE_EXPERTS_PER_STEP
    nc = N_EXPERTS // ne
    tm_pad = tm + N_GROUPS * MOE_ROW_BLOCK
    tiles_per_seq = S // tm
    tile = lambda t, c: (t, 0)
    const = lambda t, c: (0, 0)
    return pl.pallas_call(
        _moe_body,
        grid=(T // tm, nc),
        in_specs=[pl.BlockSpec((tm, D), tile),
                  pl.BlockSpec((tm, LANES), tile),
                  pl.BlockSpec((8, tm), lambda t, c: (0, t)),
                  pl.BlockSpec((tm, D), tile),
                  pl.BlockSpec((1, 1, D), lambda t, c: (t // tiles_per_seq, 0, 0)),
                  pl.BlockSpec((ne, D, D_EXPERT), lambda t, c: (c, 0, 0)),
                  pl.BlockSpec((ne, D, D_EXPERT), lambda t, c: (c, 0, 0)),
                  pl.BlockSpec((ne * D_EXPERT, D), lambda t, c: (c, 0)),
                  pl.BlockSpec((1, D), const),
                  pl.BlockSpec((1, D), const)],
        out_specs=pl.BlockSpec((tm, D), tile),
        out_shape=jax.ShapeDtypeStruct((T, D), F32),
        scratch_shapes=[pltpu.VMEM((tm_pad, tm), BF16),
                        pltpu.VMEM((tm_pad, D), BF16),
                        pltpu.VMEM((tm_pad, LANES), F32),
                        pltpu.VMEM((tm_pad, D), F32),
                        pltpu.VMEM((tm, tm), BF16),
                        pltpu.SMEM((2 * N_GROUPS,), I32)],
        compiler_params=_params("arbitrary", "arbitrary"),
        name="moe",
    )(h2, gate, grp, x1, gt2, w1, w3, w2, g2, b2)


def _pick(n, pref):
    return pref if n % pref == 0 else n


def kernel(x, c, rel_bias, w_ada, b_ada, w_in, gla_w_gate, gla_b_gate, gla_norm_g, w_branch_a, w_branch_b, w_out, ln1_g, ln1_b, w_router_group, b_router_group, w_router_expert, b_router_expert, w_exp_gate, w_exp_up, w_exp_down, ln2_g, ln2_b):
    B, S, D = x.shape
    assert S % (2 * QBLK) == 0 and D == 1024 and w_ada.shape[0] == DEPTH == 1
    l = 0

    ada = _ada_call(c, w_ada[l], b_ada[l])
    sh1, sc1, gt1, sh2, sc2, gt2 = [ada[:, i * D:(i + 1) * D].reshape(B, 1, D) for i in range(6)]

    offs = np.concatenate([[0], np.cumsum(SPLIT_SIZES)])
    seg = lambda i: w_in[l][:, offs[i]:offs[i + 1]]
    (w_aq, w_ak, w_av, w_iq, w_ik, w_iw, w_gq, w_gk, w_gv, w_gr, w_glr, w_ga, w_gb) = [seg(i) for i in range(13)]
    pad = jnp.zeros((D, TOK_SMALL[1] - TOK_SMALL[0] - IDX_DIM - G_RANK), F32)
    w_tok = jnp.concatenate([w_ak, w_gq, w_gk, w_gv, w_gr, w_ga, w_gb, w_ik, w_glr, pad], axis=1).astype(BF16)
    w_ch = jnp.concatenate([w_aq, w_av, w_iq, w_iw], axis=1).T.astype(BF16)

    tm = _pick(S, 512)
    kk, gla, gates, ik, glr, qT, vT, iqT, iwT = _inproj_call(x, sh1, sc1, w_tok, w_ch, tm)

    o_a = _dsa_call(rel_bias, ik, kk, vT, qT, iqT, iwT)
    o_b = _gla_call(gla, glr, gla_w_gate[l], gla_b_gate[l].reshape(1, G_KW), gla_norm_g[l].reshape(1, G_VW),
                    _pick(S, 256))

    wr = jnp.zeros((ROUTER_ROWS, D), F32)
    wr = wr.at[0:N_GROUPS].set(w_router_group[l].T).at[ROUTER_E0:ROUTER_E0 + N_EXPERTS].set(w_router_expert[l].T)
    br = jnp.zeros((ROUTER_ROWS, 1), F32)
    br = br.at[0:N_GROUPS, 0].set(b_router_group[l]).at[ROUTER_E0:ROUTER_E0 + N_EXPERTS, 0].set(b_router_expert[l])
    x1, h2, gate, grp = _post_call(o_a, o_b, gates, x, gt1, sh2, sc2,
                              w_branch_a[l].astype(BF16), w_branch_b[l].astype(BF16), w_out[l].astype(BF16),
                              ln1_g[l].reshape(1, D), ln1_b[l].reshape(1, D), wr, br, tm)

    tm5 = _pick(S, 1024)
    out = _moe_call(h2.reshape(B * S, D), gate, grp, x1.reshape(B * S, D), gt2,
                    w_exp_gate[l].astype(BF16), w_exp_up[l].astype(BF16),
                    w_exp_down[l].astype(BF16).reshape(N_EXPERTS * D_EXPERT, D),
                    ln2_g[l].reshape(1, D), ln2_b[l].reshape(1, D), tm5, S)
    return out.reshape(B, S, D)
```

```python
import functools
import math

import numpy as np
import jax
import jax.numpy as jnp
from jax import lax
from jax.experimental import pallas as pl
from jax.experimental.pallas import tpu as pltpu

F32 = jnp.float32
BF16 = jnp.bfloat16
I32 = jnp.int32
HIGHEST = lax.Precision.HIGHEST

A_HEADS = 8
A_HEAD_DIM = 64
A_WIDTH = A_HEADS * A_HEAD_DIM
IDX_HEADS = 16
IDX_DIM = 32
TOPK_MAX = 256
QBLK = 128
REL_BUCKETS = 32
REL_MAX_DIST = 128
G_HEADS = 4
G_DK = 64
G_DV = 128
G_KW = G_HEADS * G_DK
G_VW = G_HEADS * G_DV
G_RANK = 16
G_TAU = 16.0
G_CHUNK = 64
N_GROUPS = 4
EXPERTS_PER_GROUP = 8
N_EXPERTS = N_GROUPS * EXPERTS_PER_GROUP
D_EXPERT = 256
DEPTH = 1
DN_ALPHA = (2.0 * DEPTH) ** 0.25
LN_EPS = 1e-5
SPLIT_SIZES = (A_WIDTH, A_WIDTH, A_WIDTH, IDX_HEADS * IDX_DIM, IDX_DIM, IDX_HEADS,
               G_KW, G_KW, G_VW, G_VW, G_RANK, 1024, 1024)

LANES = 128
VMEM_LIMIT_BYTES = 56 * 1024 * 1024

NEG = -1e30
LOG2E = math.log2(math.e)
INT_MIN = -2 ** 31
INT_MAX = 2 ** 31 - 1
KEY_NEG_INF = -2 ** 31 + 0x7FFFFF

NT = (((1,), (1,)), ((), ()))
TN = (((0,), (0,)), ((), ()))


def _ln(x):
    mu = jnp.mean(x, axis=-1, keepdims=True)
    xc = x - mu
    var = jnp.mean(xc * xc, axis=-1, keepdims=True)
    return xc * lax.rsqrt(var + LN_EPS)


def _sigmoid(x):
    return 0.5 * jnp.tanh(0.5 * x) + 0.5


def _params(*sem):
    return pltpu.CompilerParams(dimension_semantics=sem, vmem_limit_bytes=VMEM_LIMIT_BYTES)


def _ada_body(c_ref, w_ref, b_ref, o_ref):
    c = c_ref[...]
    cond = c * _sigmoid(c)
    o_ref[...] = jnp.dot(cond, w_ref[...], preferred_element_type=F32, precision=HIGHEST) + b_ref[...]


def _ada_call(c, w, b):
    B, D = c.shape
    N = w.shape[1]
    tn = 1536
    return pl.pallas_call(
        _ada_body,
        grid=(N // tn,),
        in_specs=[pl.BlockSpec((B, D), lambda j: (0, 0)),
                  pl.BlockSpec((D, tn), lambda j: (0, j)),
                  pl.BlockSpec((1, tn), lambda j: (0, j))],
        out_specs=pl.BlockSpec((B, tn), lambda j: (0, j)),
        out_shape=jax.ShapeDtypeStruct((B, N), F32),
        compiler_params=_params("arbitrary"),
        name="ada",
    )(c, w, b.reshape(1, N))


TOK_K = (0, 512)
TOK_GLA = (512, 2048)
TOK_GATES = (2048, 4096)
TOK_SMALL = (4096, 4224)
CH_Q = (0, 512)
CH_V = (512, 1024)
CH_IQ = (1024, 1536)
CH_IW = (1536, 1552)
IW_SCALE = IDX_HEADS ** -0.5 * IDX_DIM ** -0.5


def _inproj_body(x_ref, sh_ref, sc_ref, wtok_ref, wch_ref,
                 k_ref, gla_ref, gates_ref, ik_ref, glr_ref, qT_ref, vT_ref, iqT_ref, iwT_ref):
    tm = x_ref.shape[1]
    h = (_ln(x_ref[0]) * (1.0 + sc_ref[0]) + sh_ref[0]).astype(BF16)

    def tok(ab):
        return jnp.dot(h, wtok_ref[:, ab[0]:ab[1]], preferred_element_type=F32)

    def ch(ab):
        return lax.dot_general(wch_ref[ab[0]:ab[1], :], h, NT, preferred_element_type=F32)

    kres = tok(TOK_K)
    for p in range(A_WIDTH // LANES):
        k_ref[0, p] = kres[:, p * LANES:(p + 1) * LANES].astype(BF16)
    gla_ref[0] = tok(TOK_GLA).astype(BF16)
    gates_ref[0] = tok(TOK_GATES).astype(BF16)
    small = tok(TOK_SMALL)
    ik_ref[0] = small[:, :IDX_DIM].astype(BF16)
    glr_ref[0] = small[:, IDX_DIM:IDX_DIM + G_RANK]

    qT_ref[0] = (ch(CH_Q) * (A_HEAD_DIM ** -0.5 * LOG2E)).astype(BF16)
    vres = ch(CH_V).astype(BF16)
    for j in range(tm // LANES):
        vT_ref[0, j] = vres[:, j * LANES:(j + 1) * LANES]
    iqT_ref[0] = ch(CH_IQ).astype(BF16)
    iwT_ref[0] = ch(CH_IW) * IW_SCALE


def _inproj_call(x, sh1, sc1, w_tok, w_ch, tm):
    B, S, D = x.shape
    nt = S // tm
    const = lambda b, t: (0, 0)
    out_shape = (
        jax.ShapeDtypeStruct((B, A_WIDTH // LANES, S, LANES), BF16),
        jax.ShapeDtypeStruct((B, S, 1536), BF16),
        jax.ShapeDtypeStruct((B, S, 2048), BF16),
        jax.ShapeDtypeStruct((B, S, IDX_DIM), BF16),
        jax.ShapeDtypeStruct((B, S, G_RANK), F32),
        jax.ShapeDtypeStruct((B, A_WIDTH, S), BF16),
        jax.ShapeDtypeStruct((B, S // LANES, A_WIDTH, LANES), BF16),
        jax.ShapeDtypeStruct((B, IDX_HEADS * IDX_DIM, S), BF16),
        jax.ShapeDtypeStruct((B, IDX_HEADS, S), F32),
    )
    out_specs = (
        pl.BlockSpec((1, A_WIDTH // LANES, tm, LANES), lambda b, t: (b, 0, t, 0)),
        pl.BlockSpec((1, tm, 1536), lambda b, t: (b, t, 0)),
        pl.BlockSpec((1, tm, 2048), lambda b, t: (b, t, 0)),
        pl.BlockSpec((1, tm, IDX_DIM), lambda b, t: (b, t, 0)),
        pl.BlockSpec((1, tm, G_RANK), lambda b, t: (b, t, 0)),
        pl.BlockSpec((1, A_WIDTH, tm), lambda b, t: (b, 0, t)),
        pl.BlockSpec((1, tm // LANES, A_WIDTH, LANES), lambda b, t: (b, t, 0, 0)),
        pl.BlockSpec((1, IDX_HEADS * IDX_DIM, tm), lambda b, t: (b, 0, t)),
        pl.BlockSpec((1, IDX_HEADS, tm), lambda b, t: (b, 0, t)),
    )
    return pl.pallas_call(
        _inproj_body,
        grid=(B, nt),
        in_specs=[pl.BlockSpec((1, tm, D), lambda b, t: (b, t, 0)),
                  pl.BlockSpec((1, 1, D), lambda b, t: (b, 0, 0)),
                  pl.BlockSpec((1, 1, D), lambda b, t: (b, 0, 0)),
                  pl.BlockSpec(w_tok.shape, const),
                  pl.BlockSpec(w_ch.shape, const)],
        out_specs=out_specs,
        out_shape=out_shape,
        compiler_params=_params("parallel", "parallel"),
        name="inproj",
    )(x, sh1, sc1, w_tok, w_ch)


IDX_CHUNK = 256
CNT_CHUNK = 256
SORT_GROUP = 4
ATT_CHUNK = 256
TBL_PAD = 2 * QBLK
ACC_ROWS = A_HEAD_DIM + 16


def _rel_bucket_table():
    s = np.arange(2 * QBLK)[:, None]
    t = np.arange(QBLK)[None, :]
    dist = np.maximum(t + QBLK - s, 0)
    max_exact = REL_BUCKETS // 2
    d_f = np.maximum(dist, 1).astype(np.float32)
    large = max_exact + (np.log(d_f / max_exact) / math.log(REL_MAX_DIST / max_exact)
                         * (REL_BUCKETS - max_exact)).astype(np.int32)
    large = np.minimum(large, REL_BUCKETS - 1)
    return np.where(dist < max_exact, dist, large).astype(np.int32)


def _far_bucket():
    max_exact = REL_BUCKETS // 2
    v = max_exact + int(np.float32(np.log(np.float32(QBLK + 1) / max_exact) / math.log(REL_MAX_DIST / max_exact)
                                   * (REL_BUCKETS - max_exact)))
    assert min(v, REL_BUCKETS - 1) == REL_BUCKETS - 1
    return REL_BUCKETS - 1


def _dsa_body(rb_ref, bkt_ref, ik_ref, kk_ref, vT_ref, qT_ref, iqT_ref, iwT_ref, o_ref,
              sc_s, srt_s, thr_s, madd_s, tbl_s, oT_s, xcut_s, qm_s, sa_s, sb_s, acc_s, mall_s, mblk_s,
              *, topk, idx_bits, max_cnt):
    i = pl.program_id(1)
    nck = (i + 2) // 2
    t_idx = i * QBLK + lax.broadcasted_iota(I32, (1, QBLK), 1)

    @pl.when(i == 0)
    def _():
        bkt = bkt_ref[...]
        tbl_s[...] = jnp.zeros_like(tbl_s)
        for h in range(A_HEADS):
            t = jnp.zeros((2 * QBLK, QBLK), F32)
            for k in range(REL_BUCKETS):
                t = jnp.where(bkt == k, rb_ref[k, h], t)
            tbl_s[h, TBL_PAD:TBL_PAD + 2 * QBLK, :] = (t - rb_ref[_far_bucket(), h]) * LOG2E

    def key_to_float(key):
        key = jnp.maximum(key, KEY_NEG_INF)
        return pltpu.bitcast(jnp.where(key < 0, key ^ INT_MAX, key), F32)

    def score_chunk(c, carry):
        s0 = pl.multiple_of(c * IDX_CHUNK, IDX_CHUNK)
        kc = ik_ref[0, pl.ds(s0, IDX_CHUNK), :]
        acc = jnp.zeros((IDX_CHUNK, QBLK), F32)
        for hp in range(IDX_HEADS // 2):
            r0 = hp * 2 * IDX_DIM
            rhs = jnp.concatenate([iqT_ref[0, r0:r0 + IDX_DIM, :],
                                   iqT_ref[0, r0 + IDX_DIM:r0 + 2 * IDX_DIM, :]], axis=1)
            z = jnp.dot(kc, rhs, preferred_element_type=F32)
            acc = acc + jnp.maximum(z[:, :QBLK], 0.0) * iwT_ref[0, 2 * hp:2 * hp + 1, :]
            acc = acc + jnp.maximum(z[:, QBLK:], 0.0) * iwT_ref[0, 2 * hp + 1:2 * hp + 2, :]
        s_idx = s0 + lax.broadcasted_iota(I32, (IDX_CHUNK, QBLK), 0)
        sc_s[pl.ds(s0, IDX_CHUNK), :] = jnp.where(s_idx <= t_idx, acc, -jnp.inf)
        return carry

    ncnt = nck
    lax.fori_loop(0, nck // 2, lambda c, carry: score_chunk(2 * c + 1, score_chunk(2 * c, carry)), 0)

    @pl.when(nck % 2 == 1)
    def _():
        score_chunk(nck - 1, 0)

    def count(pred):
        def body(c, cnt):
            s0 = pl.multiple_of(c * CNT_CHUNK, CNT_CHUNK)
            k = sc_s[pl.ds(s0, CNT_CHUNK), :]
            s_idx = s0 + lax.broadcasted_iota(I32, (CNT_CHUNK, QBLK), 0)
            m = jnp.where(pred(k, s_idx), 1, 0)
            return cnt + jnp.sum(m.reshape(CNT_CHUNK // 8, 8, QBLK), axis=0)
        cnt = lax.fori_loop(0, ncnt, body, jnp.zeros((8, QBLK), I32))
        return jnp.sum(cnt, axis=0, keepdims=True)

    def search_block(n):
        groups = n * CNT_CHUNK // (8 * SORT_GROUP)

        for g in range(groups):
            v = [sc_s[(SORT_GROUP * g + u) * 8:(SORT_GROUP * g + u + 1) * 8, :] for u in range(SORT_GROUP)]
            for a, b in ((0, 1), (2, 3), (0, 2), (1, 3), (1, 2)):
                v[a], v[b] = jnp.maximum(v[a], v[b]), jnp.minimum(v[a], v[b])
            for u in range(SORT_GROUP):
                srt_s[(SORT_GROUP * g + u) * 8:(SORT_GROUP * g + u + 1) * 8, :] = v[u]

        def count_ge(cand_key):
            cand = key_to_float(cand_key)
            parts = []
            for g in range(groups):
                cnt = 0
                for u in range(SORT_GROUP):
                    tile = srt_s[(SORT_GROUP * g + u) * 8:(SORT_GROUP * g + u + 1) * 8, :]
                    cnt = jnp.where(tile >= cand, u + 1, cnt)
                parts.append(cnt)
            while len(parts) > 1:
                odd = parts[len(parts) & ~1:]
                parts = [parts[j] + parts[j + 1] for j in range(0, len(parts) - 1, 2)] + odd
            return jnp.sum(parts[0], axis=0, keepdims=True)

        c0 = count_ge(jnp.zeros((1, QBLK), I32))
        ok = c0 >= topk
        T = jnp.where(ok, 0, INT_MIN).astype(I32)
        cnt_T = jnp.where(ok, c0, n * CNT_CHUNK)

        def bit_body(j, carry):
            T, cnt_T = carry
            cand = T | jnp.left_shift(jnp.int32(1), 30 - j)
            c = count_ge(cand)
            ok = c >= topk
            return jnp.where(ok, cand, T), jnp.where(ok, c, cnt_T)

        T, cnt_T = lax.fori_loop(0, 31, bit_body, (T, cnt_T))
        thr_s[0:1, :] = T
        thr_s[1:2, :] = cnt_T

    for n in range(1, max_cnt + 1):
        pl.when(ncnt == n)(functools.partial(search_block, n))
    T_key = jnp.maximum(thr_s[0:1, :], KEY_NEG_INF)
    T = key_to_float(T_key)
    cnt_ge = thr_s[1:2, :]

    excess = jnp.where((cnt_ge > topk) & (T_key > KEY_NEG_INF), 1.0, 0.0)
    plain = (jnp.max(excess) == 0.0) & (jnp.min(T) > -jnp.inf)

    @pl.when(plain)
    def _():
        def mask_chunk(c, carry):
            s0 = pl.multiple_of(c * IDX_CHUNK, IDX_CHUNK)
            madd_s[pl.ds(s0, IDX_CHUNK), :] = jnp.where(sc_s[pl.ds(s0, IDX_CHUNK), :] >= T, 0.0, NEG)
            return carry

        lax.fori_loop(0, nck, mask_chunk, 0)

    @pl.when(jnp.logical_not(plain))
    def _():
        xcut_s[...] = jnp.full((1, QBLK), INT_MAX, I32)

        @pl.when(jnp.max(excess) > 0.0)
        def _():
            need = topk - count(lambda k, s: k > T)
            X = jnp.zeros((1, QBLK), I32)
            for b in range(idx_bits - 1, -1, -1):
                cand = X | (1 << b)
                f = count(lambda k, s: (k == T) & (s < cand))
                X = jnp.where(f < need, cand, X)
            xcut_s[...] = X

        xcut = xcut_s[...]

        def mask_chunk(c, carry):
            s0 = pl.multiple_of(c * IDX_CHUNK, IDX_CHUNK)
            k = sc_s[pl.ds(s0, IDX_CHUNK), :]
            s_idx = s0 + lax.broadcasted_iota(I32, (IDX_CHUNK, QBLK), 0)
            sel = ((k > T) | ((k == T) & (s_idx <= xcut))) & (s_idx <= t_idx)
            madd_s[pl.ds(s0, IDX_CHUNK), :] = jnp.where(sel, 0.0, NEG)
            return carry

        lax.fori_loop(0, nck, mask_chunk, 0)

    c_last = i // 2
    even = 1 - (i - 2 * c_last)
    row_head = lax.broadcasted_iota(I32, (LANES, QBLK), 0) // A_HEAD_DIM
    for h in range(A_HEADS):
        qp = qT_ref[0, (h // 2) * LANES:(h // 2 + 1) * LANES, :]
        qm_s[h] = jnp.where(row_head == h % 2, qp, jnp.zeros_like(qp))
    acc_s[...] = jnp.zeros_like(acc_s)
    ones = jnp.ones((ACC_ROWS - A_HEAD_DIM, ATT_CHUNK), BF16)

    def logits(c, s_buf, biased=True):
        s0 = pl.multiple_of(c * ATT_CHUNK, ATT_CHUNK)
        madd = madd_s[pl.ds(s0, ATT_CHUNK), :]
        off = jnp.where(c == c_last, 2 * QBLK + QBLK * even, jnp.where(c == c_last - 1, QBLK * even, 0))
        off = pl.multiple_of(off, QBLK)
        m_blk = []
        for h in range(A_HEADS):
            kc = kk_ref[0, h // 2, pl.ds(s0, ATT_CHUNK), :]
            s = jnp.dot(kc, qm_s[h], preferred_element_type=F32) + madd
            if biased:
                s = s + tbl_s[h, pl.ds(off, ATT_CHUNK), :]
            s_buf[h] = s
            m_blk.append(jnp.max(s, axis=0, keepdims=True))
        return jnp.concatenate(m_blk, axis=0)

    def accumulate(c, s_buf, m_all, m_blk):
        m_new = jnp.maximum(m_all, m_blk)
        alpha = jnp.exp2(m_all - m_new)
        for h in range(A_HEADS):
            rows = slice(h * A_HEAD_DIM, (h + 1) * A_HEAD_DIM)
            p = jnp.exp2(s_buf[h] - m_new[h:h + 1]).astype(BF16)
            vt = jnp.concatenate([vT_ref[0, 2 * c + u, rows, :] for u in range(ATT_CHUNK // QBLK)], axis=1)
            vt = jnp.concatenate([vt, ones], axis=0)
            acc_s[h] = alpha[h:h + 1] * acc_s[h] + jnp.dot(vt, p, preferred_element_type=F32)
        return m_new

    def att_body(biased, pair, carry):
        m_all, m_blk = carry
        c = 2 * pair
        m_b = logits(c + 1, sb_s, biased)
        m_all = accumulate(c, sa_s, m_all, m_blk)
        m_a = logits(c + 2, sa_s, biased)
        return accumulate(c + 1, sb_s, m_all, m_b), m_a

    n_pairs = c_last // 2
    n_far_pairs = jnp.maximum(n_pairs - 1, 0)
    carry = (jnp.full((A_HEADS, QBLK), NEG, F32), logits(0, sa_s))
    carry = lax.fori_loop(0, n_far_pairs, functools.partial(att_body, False), carry)
    m_all, m_blk = lax.fori_loop(n_far_pairs, n_pairs, functools.partial(att_body, True), carry)
    mall_s[...] = m_all
    mblk_s[...] = m_blk

    @pl.when(c_last % 2 == 1)
    def _():
        m_b = logits(c_last, sb_s)
        m_all = accumulate(c_last - 1, sa_s, mall_s[...], mblk_s[...])
        accumulate(c_last, sb_s, m_all, m_b)

    @pl.when(c_last % 2 == 0)
    def _():
        accumulate(c_last, sa_s, mall_s[...], mblk_s[...])

    for h in range(A_HEADS):
        rows = slice(h * A_HEAD_DIM, (h + 1) * A_HEAD_DIM)
        oT_s[rows, :] = acc_s[h, 0:A_HEAD_DIM, :] * (1.0 / acc_s[h, A_HEAD_DIM:A_HEAD_DIM + 1, :])
    o_ref[0] = oT_s[...].T.astype(BF16)


def _dsa_call(rel_bias, ik, kk, vT, qT, iqT, iwT):
    B, S, _ = ik.shape
    nb = S // QBLK
    topk = min(TOPK_MAX, S // 4)
    bkt = jnp.asarray(_rel_bucket_table())
    body = functools.partial(_dsa_body, topk=topk, idx_bits=int(math.log2(S)), max_cnt=S // CNT_CHUNK)
    return pl.pallas_call(
        body,
        grid=(B, nb),
        in_specs=[pl.BlockSpec(memory_space=pltpu.SMEM),
                  pl.BlockSpec((2 * QBLK, QBLK), lambda b, i: (0, 0)),
                  pl.BlockSpec((1, S, IDX_DIM), lambda b, i: (b, 0, 0)),
                  pl.BlockSpec((1, A_WIDTH // LANES, S, LANES), lambda b, i: (b, 0, 0, 0)),
                  pl.BlockSpec((1, S // LANES, A_WIDTH, LANES), lambda b, i: (b, 0, 0, 0)),
                  pl.BlockSpec((1, A_WIDTH, QBLK), lambda b, i: (b, 0, i)),
                  pl.BlockSpec((1, IDX_HEADS * IDX_DIM, QBLK), lambda b, i: (b, 0, i)),
                  pl.BlockSpec((1, IDX_HEADS, QBLK), lambda b, i: (b, 0, i))],
        out_specs=pl.BlockSpec((1, QBLK, A_WIDTH), lambda b, i: (b, i, 0)),
        out_shape=jax.ShapeDtypeStruct((B, S, A_WIDTH), BF16),
        scratch_shapes=[pltpu.VMEM((S, QBLK), F32),
                        pltpu.VMEM((S, QBLK), F32),
                        pltpu.VMEM((8, QBLK), I32),
                        pltpu.VMEM((S, QBLK), F32),
                        pltpu.VMEM((A_HEADS, TBL_PAD + 3 * QBLK, QBLK), F32),
                        pltpu.VMEM((A_WIDTH, QBLK), F32),
                        pltpu.VMEM((1, QBLK), I32),
                        pltpu.VMEM((A_HEADS, LANES, QBLK), BF16),
                        pltpu.VMEM((A_HEADS, ATT_CHUNK, QBLK), F32),
                        pltpu.VMEM((A_HEADS, ATT_CHUNK, QBLK), F32),
                        pltpu.VMEM((A_HEADS, ACC_ROWS, QBLK), F32),
                        pltpu.VMEM((A_HEADS, QBLK), F32),
                        pltpu.VMEM((A_HEADS, QBLK), F32)],
        compiler_params=_params("parallel", "arbitrary"),
        name="dsa",
    )(rel_bias, bkt, ik, kk, vT, qT, iqT, iwT)


GLA_Q = (0, 256)
GLA_K = (256, 512)
GLA_V = (512, 1024)
GLA_R = (1024, 1536)


def _gla_body(gla_ref, glr_ref, wg_ref, bg_ref, ng_ref, o_ref, st_s):
    tg = gla_ref.shape[1]
    C = G_CHUNK

    @pl.when(pl.program_id(1) == 0)
    def _():
        st_s[...] = jnp.zeros_like(st_s)

    glr = glr_ref[0]
    wg = wg_ref[...]
    glr_hi, wg_hi = glr.astype(BF16), wg.astype(BF16)
    glr_lo = (glr - glr_hi.astype(F32)).astype(BF16)
    wg_lo = (wg - wg_hi.astype(F32)).astype(BF16)
    xg = (jnp.dot(glr_hi, wg_hi, preferred_element_type=F32) + jnp.dot(glr_hi, wg_lo, preferred_element_type=F32)
          + jnp.dot(glr_lo, wg_hi, preferred_element_type=F32)) + bg_ref[...]
    logg = -(jnp.maximum(-xg, 0.0) + jnp.log1p(jnp.exp(-jnp.abs(xg)))) * (1.0 / G_TAU)

    rt = lax.broadcasted_iota(I32, (tg, tg), 0)
    ct = lax.broadcasted_iota(I32, (tg, tg), 1)
    cum = jnp.where((rt // C == ct // C) & (rt >= ct), 1.0, 0.0).astype(BF16)
    logg_hi = logg.astype(BF16)
    logg_lo = (logg - logg_hi.astype(F32)).astype(BF16)
    bc_all = (jnp.dot(cum, logg_hi, preferred_element_type=F32)
              + jnp.dot(cum, logg_lo, preferred_element_type=F32))

    bl_all = jnp.concatenate([jnp.broadcast_to(bc_all[(ck + 1) * C - 1:(ck + 1) * C, :], (C, G_KW))
                              for ck in range(tg // C)], axis=0)

    q_all = gla_ref[0, :, GLA_Q[0]:GLA_Q[1]].astype(F32) * (G_DK ** -0.5)
    k_all = gla_ref[0, :, GLA_K[0]:GLA_K[1]].astype(F32)
    q_in_all = (q_all * jnp.exp(bc_all)).astype(BF16)
    k_st_all = (k_all * jnp.exp(bl_all - bc_all)).astype(BF16)
    q_rel_all = q_all * jnp.exp(bc_all - bl_all)
    decay_all = jnp.exp(bl_all)

    ri = lax.broadcasted_iota(I32, (C, C), 0)
    ci = lax.broadcasted_iota(I32, (C, C), 1)
    tril = ri >= ci
    lane_head = lax.broadcasted_iota(I32, (C, LANES), 1) // G_DK
    st_rows = lax.broadcasted_iota(I32, (2 * G_DV, LANES), 0) // G_DV
    st_cols = lax.broadcasted_iota(I32, (2 * G_DV, LANES), 1) // G_DK
    st_diag = st_rows == st_cols
    n_ck = tg // C
    n_p = G_HEADS // 2
    units = [(ck, p) for ck in range(n_ck) for p in range(n_p)]
    rows = lambda ck: slice(ck * C, (ck + 1) * C)
    lanes = lambda p: slice(p * LANES, (p + 1) * LANES)
    v_of = lambda ck, p: gla_ref[0, rows(ck), GLA_V[0] + p * 2 * G_DV:GLA_V[0] + (p + 1) * 2 * G_DV]

    att = {}
    for ck, p in units:
        k_st = k_st_all[rows(ck), lanes(p)]
        for sub in range(2):
            qm = jnp.where(lane_head == sub, q_rel_all[rows(ck), lanes(p)], 0.0).astype(BF16)
            a = lax.dot_general(qm, k_st, NT, preferred_element_type=F32)
            att[ck, p, sub] = jnp.where(tril, a, 0.0).astype(BF16)
    o_intra = {}
    uT = {}
    for ck, p in units:
        v = v_of(ck, p)
        for sub in range(2):
            o_intra[ck, p, sub] = jnp.dot(att[ck, p, sub], v[:, sub * G_DV:(sub + 1) * G_DV],
                                          preferred_element_type=F32)
        uT[ck, p] = lax.dot_general(v, k_st_all[rows(ck), lanes(p)], TN, preferred_element_type=F32)
    o_inter = {}
    for p in range(n_p):
        st = st_s[p]
        for ck in range(n_ck):
            o_inter[ck, p] = lax.dot_general(q_in_all[rows(ck), lanes(p)], st.astype(BF16), NT,
                                             preferred_element_type=F32)
            st = st * decay_all[ck * C:ck * C + 1, lanes(p)] + jnp.where(st_diag, uT[ck, p], 0.0)
        st_s[p] = st
    for ck, p in units:
        for sub in range(2):
            hd = 2 * p + sub
            o = o_intra[ck, p, sub] + o_inter[ck, p][:, sub * G_DV:(sub + 1) * G_DV]
            y = _ln(o) * ng_ref[:, hd * G_DV:(hd + 1) * G_DV]
            g = gla_ref[0, rows(ck), GLA_R[0] + hd * G_DV:GLA_R[0] + (hd + 1) * G_DV].astype(F32)
            o_ref[0, rows(ck), hd * G_DV:(hd + 1) * G_DV] = (y * (g * _sigmoid(g))).astype(BF16)


def _gla_call(gla, glr, wg, bg, ng, tg):
    B, S, _ = gla.shape
    const = lambda b, j: (0, 0)
    return pl.pallas_call(
        _gla_body,
        grid=(B, S // tg),
        in_specs=[pl.BlockSpec((1, tg, 1536), lambda b, j: (b, j, 0)),
                  pl.BlockSpec((1, tg, G_RANK), lambda b, j: (b, j, 0)),
                  pl.BlockSpec((G_RANK, G_KW), const),
                  pl.BlockSpec((1, G_KW), const),
                  pl.BlockSpec((1, G_VW), const)],
        out_specs=pl.BlockSpec((1, tg, G_VW), lambda b, j: (b, j, 0)),
        out_shape=jax.ShapeDtypeStruct((B, S, G_VW), BF16),
        scratch_shapes=[pltpu.VMEM((G_HEADS // 2, 2 * G_DV, LANES), F32)],
        compiler_params=_params("parallel", "arbitrary"),
        name="gla",
    )(gla, glr, wg, bg, ng)


ROUTER_ROWS = 40
ROUTER_E0 = 8


def _post_body(oa_ref, ob_ref, gates_ref, x_ref, gt1_ref, sh2_ref, sc2_ref, wa_ref, wb_ref, wo_ref,
               g1_ref, b1_ref, wr_ref, br_ref, x1_ref, h2_ref, gate_ref, grp_ref):
    tm = x_ref.shape[1]
    D = x_ref.shape[2]
    ya = jnp.dot(oa_ref[0], wa_ref[...], preferred_element_type=F32)
    yb = jnp.dot(ob_ref[0], wb_ref[...], preferred_element_type=F32)
    ga = gates_ref[0, :, 0:D].astype(F32)
    gb = gates_ref[0, :, D:2 * D].astype(F32)
    merged = _sigmoid(ga) * ya + _sigmoid(gb) * yb
    y = jnp.dot(merged.astype(BF16), wo_ref[...], preferred_element_type=F32)
    x1 = _ln(DN_ALPHA * x_ref[0] + gt1_ref[0] * y) * g1_ref[...] + b1_ref[...]
    x1_ref[0] = x1
    h2 = _ln(x1) * (1.0 + sc2_ref[0]) + sh2_ref[0]
    h2_hi = h2.astype(BF16)
    h2_ref[0] = h2_hi

    h2_lo = (h2 - h2_hi.astype(F32)).astype(BF16)
    wr = wr_ref[...]
    wr_hi = wr.astype(BF16)
    wr_lo = (wr - wr_hi.astype(F32)).astype(BF16)
    lt = (lax.dot_general(wr_hi, h2_hi, NT, preferred_element_type=F32)
          + lax.dot_general(wr_hi, h2_lo, NT, preferred_element_type=F32)
          + lax.dot_general(wr_lo, h2_hi, NT, preferred_element_type=F32)) + br_ref[...]
    gl = lt[0:N_GROUPS]
    gmax = jnp.max(gl, axis=0, keepdims=True)
    g_w = 1.0 / jnp.sum(jnp.exp(gl - gmax), axis=0, keepdims=True)
    r4 = lax.broadcasted_iota(I32, (N_GROUPS, tm), 0)
    g_idx = jnp.min(jnp.where(gl == gmax, r4, N_GROUPS), axis=0, keepdims=True)
    eg = jnp.zeros((EXPERTS_PER_GROUP, tm), F32)
    for g in range(N_GROUPS):
        lo = ROUTER_E0 + g * EXPERTS_PER_GROUP
        eg = jnp.where(g_idx == g, lt[lo:lo + EXPERTS_PER_GROUP], eg)
    r8 = lax.broadcasted_iota(I32, (EXPERTS_PER_GROUP, tm), 0)
    e1 = jnp.max(eg, axis=0, keepdims=True)
    i1 = jnp.min(jnp.where(eg == e1, r8, EXPERTS_PER_GROUP), axis=0, keepdims=True)
    eg2 = jnp.where(r8 == i1, -jnp.inf, eg)
    e2 = jnp.max(eg2, axis=0, keepdims=True)
    i2 = jnp.min(jnp.where(eg2 == e2, r8, EXPERTS_PER_GROUP), axis=0, keepdims=True)
    d = jnp.exp(e2 - e1)
    w1 = g_w / (1.0 + d)
    w2 = g_w * d / (1.0 + d)
    in_group = jnp.where(r8 == i1, w1, 0.0) + jnp.where(r8 == i2, w2, 0.0)
    blocks = [jnp.where(g_idx == g, in_group, 0.0) for g in range(N_GROUPS)]
    blocks.append(jnp.zeros((LANES - N_EXPERTS, tm), F32))
    gate_ref[...] = jnp.concatenate(blocks, axis=0).T
    r8g = lax.broadcasted_iota(I32, (8, tm), 0)
    grp_ref[...] = jnp.where(r8g == g_idx, 1.0, 0.0)


def _post_call(o_a, o_b, gates, x, gt1, sh2, sc2, wa, wb, wo, g1, b1, wr, br, tm):
    B, S, D = x.shape
    nt = S // tm
    const = lambda b, t: (0, 0)
    row = lambda b, t: (b, 0, 0)
    tile = lambda b, t: (b, t, 0)
    return pl.pallas_call(
        _post_body,
        grid=(B, nt),
        in_specs=[pl.BlockSpec((1, tm, A_WIDTH), tile),
                  pl.BlockSpec((1, tm, G_VW), tile),
                  pl.BlockSpec((1, tm, 2 * D), tile),
                  pl.BlockSpec((1, tm, D), tile),
                  pl.BlockSpec((1, 1, D), row),
                  pl.BlockSpec((1, 1, D), row),
                  pl.BlockSpec((1, 1, D), row),
                  pl.BlockSpec(wa.shape, const),
                  pl.BlockSpec(wb.shape, const),
                  pl.BlockSpec(wo.shape, const),
                  pl.BlockSpec((1, D), const),
                  pl.BlockSpec((1, D), const),
                  pl.BlockSpec(wr.shape, const),
                  pl.BlockSpec(br.shape, const)],
        out_specs=(pl.BlockSpec((1, tm, D), tile),
                   pl.BlockSpec((1, tm, D), tile),
                   pl.BlockSpec((tm, LANES), lambda b, t: (b * nt + t, 0)),
                   pl.BlockSpec((8, tm), lambda b, t: (0, b * nt + t))),
        out_shape=(jax.ShapeDtypeStruct((B, S, D), F32),
                   jax.ShapeDtypeStruct((B, S, D), BF16),
                   jax.ShapeDtypeStruct((B * S, LANES), F32),
                   jax.ShapeDtypeStruct((8, B * S), F32)),
        compiler_params=_params("parallel", "parallel"),
        name="post",
    )(o_a, o_b, gates, x, gt1, sh2, sc2, wa, wb, wo, g1, b1, wr, br)


MOE_EXPERTS_PER_STEP = 4
MOE_ROW_BLOCK = 128
MOE_PERM_ROWS = 256


def _moe_body(h2_ref, gate_ref, grp_ref, x1_ref, gt2_ref, w1_ref, w3_ref, w2_ref, g2_ref, b2_ref, o_ref,
              perm_s, hs_s, gs_s, ys_s, tri_s, seg_s):
    t = pl.program_id(0)
    s = pl.program_id(1)
    ne = MOE_EXPERTS_PER_STEP
    rb = MOE_ROW_BLOCK
    tm = h2_ref.shape[0]
    tm_pad = perm_s.shape[0]
    steps_per_group = EXPERTS_PER_GROUP // ne

    @pl.when((t == 0) & (s == 0))
    def _():
        r = lax.broadcasted_iota(I32, (tm, tm), 0)
        c = lax.broadcasted_iota(I32, (tm, tm), 1)
        tri_s[...] = jnp.where(r < c, 1.0, 0.0).astype(BF16)

    @pl.when(s == 0)
    def _():
        oh = grp_ref[...]
        rank = jnp.dot(oh.astype(BF16), tri_s[...], preferred_element_type=F32)
        cnt = jnp.sum(oh, axis=1, keepdims=True)
        blocks = jnp.floor((cnt + (rb - 1)) * (1.0 / rb))
        off = jnp.zeros((1, 1), F32)
        dest = jnp.zeros((1, tm), F32)
        for g in range(N_GROUPS):
            seg_s[2 * g] = jnp.sum(off).astype(I32)
            seg_s[2 * g + 1] = jnp.sum(blocks[g:g + 1, :]).astype(I32)
            dest = dest + oh[g:g + 1, :] * (off + rank[g:g + 1, :])
            off = off + blocks[g:g + 1, :] * rb
        dest_i = dest.astype(I32)
        gate = gate_ref[...]
        g_hi = gate.astype(BF16)
        g_lo = (gate - g_hi.astype(F32)).astype(BF16)
        h2 = h2_ref[...]
        for c in range(tm_pad // MOE_PERM_ROWS):
            rows = slice(c * MOE_PERM_ROWS, (c + 1) * MOE_PERM_ROWS)
            d_idx = c * MOE_PERM_ROWS + lax.broadcasted_iota(I32, (MOE_PERM_ROWS, tm), 0)
            perm = jnp.where(d_idx == dest_i, 1.0, 0.0).astype(BF16)
            perm_s[rows, :] = perm
            hs_s[rows, :] = jnp.dot(perm, h2, preferred_element_type=F32).astype(BF16)
            gs_s[rows, :] = (jnp.dot(perm, g_hi, preferred_element_type=F32)
                             + jnp.dot(perm, g_lo, preferred_element_type=F32))
        ys_s[...] = jnp.zeros_like(ys_s)

    g = s // steps_per_group
    row0 = seg_s[2 * g]
    nblk = seg_s[2 * g + 1]

    def block(i, carry):
        r0 = pl.multiple_of(row0 + i * rb, rb)
        x = hs_s[pl.ds(r0, rb), :]
        gsel = pltpu.roll(gs_s[pl.ds(r0, rb), :], (LANES - s * ne) % LANES, axis=1)
        hid = []
        for j in range(ne):
            a = jnp.dot(x, w1_ref[j], preferred_element_type=F32)
            b = jnp.dot(x, w3_ref[j], preferred_element_type=F32)
            hid.append((a * _sigmoid(a) * b * gsel[:, j:j + 1]).astype(BF16))
        ys_s[pl.ds(r0, rb), :] += jnp.dot(jnp.concatenate(hid, axis=1), w2_ref[...], preferred_element_type=F32)
        return carry

    lax.fori_loop(0, nblk, block, 0)

    @pl.when(s == pl.num_programs(1) - 1)
    def _():
        perm = perm_s[...]
        for c in range(o_ref.shape[1] // MOE_PERM_ROWS):
            cols = slice(c * MOE_PERM_ROWS, (c + 1) * MOE_PERM_ROWS)
            o_ref[:, cols] = lax.dot_general(perm, ys_s[:, cols].astype(BF16), TN, preferred_element_type=F32)
        z = DN_ALPHA * x1_ref[...] + gt2_ref[0] * o_ref[...]
        o_ref[...] = _ln(z) * g2_ref[...] + b2_ref[...]


def _moe_call(h2, gate, grp, x1, gt2, w1, w3, w2, g2, b2, tm, S):
    T, D = h2.shape
    ne = MOE_EXPERTS_PER_STEP
    nc = N_EXPERTS // ne
    tm_pad = tm + N_GROUPS * MOE_ROW_BLOCK
    tiles_per_seq = S // tm
    tile = lambda t, c: (t, 0)
    const = lambda t, c: (0, 0)
    return pl.pallas_call(
        _moe_body,
        grid=(T // tm, nc),
        in_specs=[pl.BlockSpec((tm, D), tile),
                  pl.BlockSpec((tm, LANES), tile),
                  pl.BlockSpec((8, tm), lambda t, c: (0, t)),
                  pl.BlockSpec((tm, D), tile),
                  pl.BlockSpec((1, 1, D), lambda t, c: (t // tiles_per_seq, 0, 0)),
                  pl.BlockSpec((ne, D, D_EXPERT), lambda t, c: (c, 0, 0)),
                  pl.BlockSpec((ne, D, D_EXPERT), lambda t, c: (c, 0, 0)),
                  pl.BlockSpec((ne * D_EXPERT, D), lambda t, c: (c, 0)),
                  pl.BlockSpec((1, D), const),
                  pl.BlockSpec((1, D), const)],
        out_specs=pl.BlockSpec((tm, D), tile),
        out_shape=jax.ShapeDtypeStruct((T, D), F32),
        scratch_shapes=[pltpu.VMEM((tm_pad, tm), BF16),
                        pltpu.VMEM((tm_pad, D), BF16),
                        pltpu.VMEM((tm_pad, LANES), F32),
                        pltpu.VMEM((tm_pad, D), F32),
                        pltpu.VMEM((tm, tm), BF16),
                        pltpu.SMEM((2 * N_GROUPS,), I32)],
        compiler_params=_params("arbitrary", "arbitrary"),
        name="moe",
    )(h2, gate, grp, x1, gt2, w1, w3, w2, g2, b2)


def _pick(n, pref):
    return pref if n % pref == 0 else n


def kernel(x, c, rel_bias, w_ada, b_ada, w_in, gla_w_gate, gla_b_gate, gla_norm_g, w_branch_a, w_branch_b, w_out, ln1_g, ln1_b, w_router_group, b_router_group, w_router_expert, b_router_expert, w_exp_gate, w_exp_up, w_exp_down, ln2_g, ln2_b):
    B, S, D = x.shape
    assert S % (2 * QBLK) == 0 and D == 1024 and w_ada.shape[0] == DEPTH == 1
    l = 0

    ada = _ada_call(c, w_ada[l], b_ada[l])
    sh1, sc1, gt1, sh2, sc2, gt2 = [ada[:, i * D:(i + 1) * D].reshape(B, 1, D) for i in range(6)]

    offs = np.concatenate([[0], np.cumsum(SPLIT_SIZES)])
    seg = lambda i: w_in[l][:, offs[i]:offs[i + 1]]
    (w_aq, w_ak, w_av, w_iq, w_ik, w_iw, w_gq, w_gk, w_gv, w_gr, w_glr, w_ga, w_gb) = [seg(i) for i in range(13)]
    pad = jnp.zeros((D, TOK_SMALL[1] - TOK_SMALL[0] - IDX_DIM - G_RANK), F32)
    w_tok = jnp.concatenate([w_ak, w_gq, w_gk, w_gv, w_gr, w_ga, w_gb, w_ik, w_glr, pad], axis=1).astype(BF16)
    w_ch = jnp.concatenate([w_aq, w_av, w_iq, w_iw], axis=1).T.astype(BF16)

    tm = _pick(S, 512)
    kk, gla, gates, ik, glr, qT, vT, iqT, iwT = _inproj_call(x, sh1, sc1, w_tok, w_ch, tm)

    o_a = _dsa_call(rel_bias, ik, kk, vT, qT, iqT, iwT)
    o_b = _gla_call(gla, glr, gla_w_gate[l], gla_b_gate[l].reshape(1, G_KW), gla_norm_g[l].reshape(1, G_VW),
                    _pick(S, 256))

    wr = jnp.zeros((ROUTER_ROWS, D), F32)
    wr = wr.at[0:N_GROUPS].set(w_router_group[l].T).at[ROUTER_E0:ROUTER_E0 + N_EXPERTS].set(w_router_expert[l].T)
    br = jnp.zeros((ROUTER_ROWS, 1), F32)
    br = br.at[0:N_GROUPS, 0].set(b_router_group[l]).at[ROUTER_E0:ROUTER_E0 + N_EXPERTS, 0].set(b_router_expert[l])
    x1, h2, gate, grp = _post_call(o_a, o_b, gates, x, gt1, sh2, sc2,
                              w_branch_a[l].astype(BF16), w_branch_b[l].astype(BF16), w_out[l].astype(BF16),
                              ln1_g[l].reshape(1, D), ln1_b[l].reshape(1, D), wr, br, tm)

    tm5 = _pick(S, 1024)
    out = _moe_call(h2.reshape(B * S, D), gate, grp, x1.reshape(B * S, D), gt2,
                    w_exp_gate[l].astype(BF16), w_exp_up[l].astype(BF16),
                    w_exp_down[l].astype(BF16).reshape(N_EXPERTS * D_EXPERT, D),
                    ln2_g[l].reshape(1, D), ln2_b[l].reshape(1, D), tm5, S)
    return out.reshape(B, S, D)
```

```python
import functools
import math

import numpy as np
import jax
import jax.numpy as jnp
from jax import lax
from jax.experimental import pallas as pl
from jax.experimental.pallas import tpu as pltpu

F32 = jnp.float32
BF16 = jnp.bfloat16
I32 = jnp.int32
HIGHEST = lax.Precision.HIGHEST

A_HEADS = 8
A_HEAD_DIM = 64
A_WIDTH = A_HEADS * A_HEAD_DIM
IDX_HEADS = 16
IDX_DIM = 32
TOPK_MAX = 256
QBLK = 128
REL_BUCKETS = 32
REL_MAX_DIST = 128
G_HEADS = 4
G_DK = 64
G_DV = 128
G_KW = G_HEADS * G_DK
G_VW = G_HEADS * G_DV
G_RANK = 16
G_TAU = 16.0
G_CHUNK = 64
N_GROUPS = 4
EXPERTS_PER_GROUP = 8
N_EXPERTS = N_GROUPS * EXPERTS_PER_GROUP
D_EXPERT = 256
DEPTH = 1
DN_ALPHA = (2.0 * DEPTH) ** 0.25
LN_EPS = 1e-5
SPLIT_SIZES = (A_WIDTH, A_WIDTH, A_WIDTH, IDX_HEADS * IDX_DIM, IDX_DIM, IDX_HEADS,
               G_KW, G_KW, G_VW, G_VW, G_RANK, 1024, 1024)

LANES = 128
SUBLANES = 8
VMEM_LIMIT_BYTES = 56 * 1024 * 1024

NEG = -1e30
LOG2E = math.log2(math.e)
INT_MIN = -2 ** 31
INT_MAX = 2 ** 31 - 1
KEY_NEG_INF = -2 ** 31 + 0x7FFFFF

NT = (((1,), (1,)), ((), ()))
TN = (((0,), (0,)), ((), ()))


def _ln(x):
    mu = jnp.mean(x, axis=-1, keepdims=True)
    xc = x - mu
    var = jnp.mean(xc * xc, axis=-1, keepdims=True)
    return xc * lax.rsqrt(var + LN_EPS)


def _sigmoid(x):
    return 0.5 * jnp.tanh(0.5 * x) + 0.5


def _params(*sem):
    return pltpu.CompilerParams(dimension_semantics=sem, vmem_limit_bytes=VMEM_LIMIT_BYTES)


def _ada_body(c_ref, w_ref, b_ref, o_ref):
    c = c_ref[...]
    cond = c * _sigmoid(c)
    o_ref[...] = jnp.dot(cond, w_ref[...], preferred_element_type=F32, precision=HIGHEST) + b_ref[...]


def _ada_call(c, w, b):
    B, D = c.shape
    N = w.shape[1]
    tn = 1536
    return pl.pallas_call(
        _ada_body,
        grid=(N // tn,),
        in_specs=[pl.BlockSpec((B, D), lambda j: (0, 0)),
                  pl.BlockSpec((D, tn), lambda j: (0, j)),
                  pl.BlockSpec((1, tn), lambda j: (0, j))],
        out_specs=pl.BlockSpec((B, tn), lambda j: (0, j)),
        out_shape=jax.ShapeDtypeStruct((B, N), F32),
        compiler_params=_params("arbitrary"),
        name="ada",
    )(c, w, b.reshape(1, N))


TOK_K = (0, 512)
TOK_GLA = (512, 2048)
TOK_GATES = (2048, 4096)
TOK_SMALL = (4096, 4224)
CH_Q = (0, 512)
CH_V = (512, 1024)
CH_IQ = (1024, 1536)
CH_IW = (1536, 1552)
IW_SCALE = IDX_HEADS ** -0.5 * IDX_DIM ** -0.5


def _inproj_body(x_ref, sh_ref, sc_ref, wtok_ref, wch_ref,
                 k_ref, gla_ref, gates_ref, ik_ref, glr_ref, qT_ref, vT_ref, iqT_ref, iwT_ref):
    tm = x_ref.shape[1]
    h = (_ln(x_ref[0]) * (1.0 + sc_ref[0]) + sh_ref[0]).astype(BF16)

    def tok(ab):
        return jnp.dot(h, wtok_ref[:, ab[0]:ab[1]], preferred_element_type=F32)

    def ch(ab):
        return lax.dot_general(wch_ref[ab[0]:ab[1], :], h, NT, preferred_element_type=F32)

    kres = tok(TOK_K)
    for p in range(A_WIDTH // LANES):
        k_ref[0, p] = kres[:, p * LANES:(p + 1) * LANES].astype(BF16)
    gla_ref[0] = tok(TOK_GLA).astype(BF16)
    gates_ref[0] = tok(TOK_GATES).astype(BF16)
    small = tok(TOK_SMALL)
    ik_ref[0] = small[:, :IDX_DIM].astype(BF16)
    glr_ref[0] = small[:, IDX_DIM:IDX_DIM + G_RANK]

    qT_ref[0] = (ch(CH_Q) * (A_HEAD_DIM ** -0.5 * LOG2E)).astype(BF16)
    vres = ch(CH_V).astype(BF16)
    for j in range(tm // LANES):
        vT_ref[0, j] = vres[:, j * LANES:(j + 1) * LANES]
    iqT_ref[0] = ch(CH_IQ).astype(BF16)
    iwT_ref[0] = ch(CH_IW) * IW_SCALE


def _inproj_call(x, sh1, sc1, w_tok, w_ch, tm):
    B, S, D = x.shape
    nt = S // tm
    const = lambda b, t: (0, 0)
    out_shape = (
        jax.ShapeDtypeStruct((B, A_WIDTH // LANES, S, LANES), BF16),
        jax.ShapeDtypeStruct((B, S, 1536), BF16),
        jax.ShapeDtypeStruct((B, S, 2048), BF16),
        jax.ShapeDtypeStruct((B, S, IDX_DIM), BF16),
        jax.ShapeDtypeStruct((B, S, G_RANK), F32),
        jax.ShapeDtypeStruct((B, A_WIDTH, S), BF16),
        jax.ShapeDtypeStruct((B, S // LANES, A_WIDTH, LANES), BF16),
        jax.ShapeDtypeStruct((B, IDX_HEADS * IDX_DIM, S), BF16),
        jax.ShapeDtypeStruct((B, IDX_HEADS, S), F32),
    )
    out_specs = (
        pl.BlockSpec((1, A_WIDTH // LANES, tm, LANES), lambda b, t: (b, 0, t, 0)),
        pl.BlockSpec((1, tm, 1536), lambda b, t: (b, t, 0)),
        pl.BlockSpec((1, tm, 2048), lambda b, t: (b, t, 0)),
        pl.BlockSpec((1, tm, IDX_DIM), lambda b, t: (b, t, 0)),
        pl.BlockSpec((1, tm, G_RANK), lambda b, t: (b, t, 0)),
        pl.BlockSpec((1, A_WIDTH, tm), lambda b, t: (b, 0, t)),
        pl.BlockSpec((1, tm // LANES, A_WIDTH, LANES), lambda b, t: (b, t, 0, 0)),
        pl.BlockSpec((1, IDX_HEADS * IDX_DIM, tm), lambda b, t: (b, 0, t)),
        pl.BlockSpec((1, IDX_HEADS, tm), lambda b, t: (b, 0, t)),
    )
    return pl.pallas_call(
        _inproj_body,
        grid=(B, nt),
        in_specs=[pl.BlockSpec((1, tm, D), lambda b, t: (b, t, 0)),
                  pl.BlockSpec((1, 1, D), lambda b, t: (b, 0, 0)),
                  pl.BlockSpec((1, 1, D), lambda b, t: (b, 0, 0)),
                  pl.BlockSpec(w_tok.shape, const),
                  pl.BlockSpec(w_ch.shape, const)],
        out_specs=out_specs,
        out_shape=out_shape,
        compiler_params=_params("parallel", "parallel"),
        name="inproj",
    )(x, sh1, sc1, w_tok, w_ch)


IDX_CHUNK = 256
CNT_CHUNK = 256
SORT_GROUP = 4
ATT_CHUNK = 256
TBL_PAD = 2 * QBLK
ACC_ROWS = A_HEAD_DIM + 16


def _rel_bucket_table():
    s = np.arange(2 * QBLK)[:, None]
    t = np.arange(QBLK)[None, :]
    dist = np.maximum(t + QBLK - s, 0)
    max_exact = REL_BUCKETS // 2
    d_f = np.maximum(dist, 1).astype(np.float32)
    large = max_exact + (np.log(d_f / max_exact) / math.log(REL_MAX_DIST / max_exact)
                         * (REL_BUCKETS - max_exact)).astype(np.int32)
    large = np.minimum(large, REL_BUCKETS - 1)
    return np.where(dist < max_exact, dist, large).astype(np.int32)


def _far_bucket():
    max_exact = REL_BUCKETS // 2
    v = max_exact + int(np.float32(np.log(np.float32(QBLK + 1) / max_exact) / math.log(REL_MAX_DIST / max_exact)
                                   * (REL_BUCKETS - max_exact)))
    assert min(v, REL_BUCKETS - 1) == REL_BUCKETS - 1
    return REL_BUCKETS - 1


def _dsa_body(rb_ref, bkt_ref, ik_ref, kk_ref, vT_ref, qT_ref, iqT_ref, iwT_ref, o_ref,
              sc_s, srt_s, thr_s, madd_s, tbl_s, oT_s, xcut_s, qm_s, sa_s, sb_s, acc_s, mall_s, mblk_s,
              *, topk, idx_bits, max_cnt):
    i = pl.program_id(1)
    nck = (i + 2) // 2
    t_idx = i * QBLK + lax.broadcasted_iota(I32, (1, QBLK), 1)

    @pl.when(i == 0)
    def _():
        bkt = bkt_ref[...]
        tbl_s[...] = jnp.zeros_like(tbl_s)
        for h in range(A_HEADS):
            t = jnp.zeros((2 * QBLK, QBLK), F32)
            for k in range(REL_BUCKETS):
                t = jnp.where(bkt == k, rb_ref[k, h], t)
            tbl_s[h, TBL_PAD:TBL_PAD + 2 * QBLK, :] = (t - rb_ref[_far_bucket(), h]) * LOG2E

    def key_to_float(key):
        key = jnp.maximum(key, KEY_NEG_INF)
        return pltpu.bitcast(jnp.where(key < 0, key ^ INT_MAX, key), F32)

    def score_chunk(c, carry):
        s0 = pl.multiple_of(c * IDX_CHUNK, IDX_CHUNK)
        kc = ik_ref[0, pl.ds(s0, IDX_CHUNK), :]
        acc = jnp.zeros((IDX_CHUNK, QBLK), F32)
        for hp in range(IDX_HEADS // 2):
            r0 = hp * 2 * IDX_DIM
            rhs = jnp.concatenate([iqT_ref[0, r0:r0 + IDX_DIM, :],
                                   iqT_ref[0, r0 + IDX_DIM:r0 + 2 * IDX_DIM, :]], axis=1)
            z = jnp.dot(kc, rhs, preferred_element_type=F32)
            acc = acc + jnp.maximum(z[:, :QBLK], 0.0) * iwT_ref[0, 2 * hp:2 * hp + 1, :]
            acc = acc + jnp.maximum(z[:, QBLK:], 0.0) * iwT_ref[0, 2 * hp + 1:2 * hp + 2, :]
        s_idx = s0 + lax.broadcasted_iota(I32, (IDX_CHUNK, QBLK), 0)
        sc_s[pl.ds(s0, IDX_CHUNK), :] = jnp.where(s_idx <= t_idx, acc, -jnp.inf)
        return carry

    ncnt = nck
    lax.fori_loop(0, nck // 2, lambda c, carry: score_chunk(2 * c + 1, score_chunk(2 * c, carry)), 0)

    @pl.when(nck % 2 == 1)
    def _():
        score_chunk(nck - 1, 0)

    def count(pred):
        def body(c, cnt):
            s0 = pl.multiple_of(c * CNT_CHUNK, CNT_CHUNK)
            k = sc_s[pl.ds(s0, CNT_CHUNK), :]
            s_idx = s0 + lax.broadcasted_iota(I32, (CNT_CHUNK, QBLK), 0)
            m = jnp.where(pred(k, s_idx), 1, 0)
            return cnt + jnp.sum(m.reshape(CNT_CHUNK // SUBLANES, SUBLANES, QBLK), axis=0)
        cnt = lax.fori_loop(0, ncnt, body, jnp.zeros((SUBLANES, QBLK), I32))
        return jnp.sum(cnt, axis=0, keepdims=True)

    def search_block(n):
        groups = n * CNT_CHUNK // (SUBLANES * SORT_GROUP)

        for g in range(groups):
            v = [sc_s[(SORT_GROUP * g + u) * SUBLANES:(SORT_GROUP * g + u + 1) * SUBLANES, :] for u in range(SORT_GROUP)]
            for a, b in ((0, 1), (2, 3), (0, 2), (1, 3), (1, 2)):
                v[a], v[b] = jnp.maximum(v[a], v[b]), jnp.minimum(v[a], v[b])
            for u in range(SORT_GROUP):
                srt_s[(SORT_GROUP * g + u) * SUBLANES:(SORT_GROUP * g + u + 1) * SUBLANES, :] = v[u]

        def count_ge(cand_key):
            cand = key_to_float(cand_key)
            parts = []
            for g in range(groups):
                cnt = 0
                for u in range(SORT_GROUP):
                    tile = srt_s[(SORT_GROUP * g + u) * SUBLANES:(SORT_GROUP * g + u + 1) * SUBLANES, :]
                    cnt = jnp.where(tile >= cand, u + 1, cnt)
                parts.append(cnt)
            while len(parts) > 1:
                odd = parts[len(parts) & ~1:]
                parts = [parts[j] + parts[j + 1] for j in range(0, len(parts) - 1, 2)] + odd
            return jnp.sum(parts[0], axis=0, keepdims=True)

        c0 = count_ge(jnp.zeros((1, QBLK), I32))
        ok = c0 >= topk
        T = jnp.where(ok, 0, INT_MIN).astype(I32)
        cnt_T = jnp.where(ok, c0, n * CNT_CHUNK)

        def bit_body(j, carry):
            T, cnt_T = carry
            cand = T | jnp.left_shift(jnp.int32(1), 30 - j)
            c = count_ge(cand)
            ok = c >= topk
            return jnp.where(ok, cand, T), jnp.where(ok, c, cnt_T)

        T, cnt_T = lax.fori_loop(0, 31, bit_body, (T, cnt_T))
        thr_s[0:1, :] = T
        thr_s[1:2, :] = cnt_T

    for n in range(1, max_cnt + 1):
        pl.when(ncnt == n)(functools.partial(search_block, n))
    T_key = jnp.maximum(thr_s[0:1, :], KEY_NEG_INF)
    T = key_to_float(T_key)
    cnt_ge = thr_s[1:2, :]

    excess = jnp.where((cnt_ge > topk) & (T_key > KEY_NEG_INF), 1.0, 0.0)
    plain = (jnp.max(excess) == 0.0) & (jnp.min(T) > -jnp.inf)

    @pl.when(plain)
    def _():
        def mask_chunk(c, carry):
            s0 = pl.multiple_of(c * IDX_CHUNK, IDX_CHUNK)
            madd_s[pl.ds(s0, IDX_CHUNK), :] = jnp.where(sc_s[pl.ds(s0, IDX_CHUNK), :] >= T, 0.0, NEG)
            return carry

        lax.fori_loop(0, nck, mask_chunk, 0)

    @pl.when(jnp.logical_not(plain))
    def _():
        xcut_s[...] = jnp.full((1, QBLK), INT_MAX, I32)

        @pl.when(jnp.max(excess) > 0.0)
        def _():
            need = topk - count(lambda k, s: k > T)
            X = jnp.zeros((1, QBLK), I32)
            for b in range(idx_bits - 1, -1, -1):
                cand = X | (1 << b)
                f = count(lambda k, s: (k == T) & (s < cand))
                X = jnp.where(f < need, cand, X)
            xcut_s[...] = X

        xcut = xcut_s[...]

        def mask_chunk(c, carry):
            s0 = pl.multiple_of(c * IDX_CHUNK, IDX_CHUNK)
            k = sc_s[pl.ds(s0, IDX_CHUNK), :]
            s_idx = s0 + lax.broadcasted_iota(I32, (IDX_CHUNK, QBLK), 0)
            sel = ((k > T) | ((k == T) & (s_idx <= xcut))) & (s_idx <= t_idx)
            madd_s[pl.ds(s0, IDX_CHUNK), :] = jnp.where(sel, 0.0, NEG)
            return carry

        lax.fori_loop(0, nck, mask_chunk, 0)

    c_last = i // 2
    even = 1 - (i - 2 * c_last)
    row_head = lax.broadcasted_iota(I32, (LANES, QBLK), 0) // A_HEAD_DIM
    for h in range(A_HEADS):
        qp = qT_ref[0, (h // 2) * LANES:(h // 2 + 1) * LANES, :]
        qm_s[h] = jnp.where(row_head == h % 2, qp, jnp.zeros_like(qp))
    acc_s[...] = jnp.zeros_like(acc_s)
    ones = jnp.ones((ACC_ROWS - A_HEAD_DIM, ATT_CHUNK), BF16)

    def logits(c, s_buf, biased=True):
        s0 = pl.multiple_of(c * ATT_CHUNK, ATT_CHUNK)
        madd = madd_s[pl.ds(s0, ATT_CHUNK), :]
        off = jnp.where(c == c_last, 2 * QBLK + QBLK * even, jnp.where(c == c_last - 1, QBLK * even, 0))
        off = pl.multiple_of(off, QBLK)
        m_blk = []
        for h in range(A_HEADS):
            kc = kk_ref[0, h // 2, pl.ds(s0, ATT_CHUNK), :]
            s = jnp.dot(kc, qm_s[h], preferred_element_type=F32) + madd
            if biased:
                s = s + tbl_s[h, pl.ds(off, ATT_CHUNK), :]
            s_buf[h] = s
            m_blk.append(jnp.max(s, axis=0, keepdims=True))
        return jnp.concatenate(m_blk, axis=0)

    def accumulate(c, s_buf, m_all, m_blk):
        m_new = jnp.maximum(m_all, m_blk)
        alpha = jnp.exp2(m_all - m_new)
        for h in range(A_HEADS):
            rows = slice(h * A_HEAD_DIM, (h + 1) * A_HEAD_DIM)
            p = jnp.exp2(s_buf[h] - m_new[h:h + 1]).astype(BF16)
            vt = jnp.concatenate([vT_ref[0, 2 * c + u, rows, :] for u in range(ATT_CHUNK // QBLK)], axis=1)
            vt = jnp.concatenate([vt, ones], axis=0)
            acc_s[h] = alpha[h:h + 1] * acc_s[h] + jnp.dot(vt, p, preferred_element_type=F32)
        return m_new

    def att_body(biased, pair, carry):
        m_all, m_blk = carry
        c = 2 * pair
        m_b = logits(c + 1, sb_s, biased)
        m_all = accumulate(c, sa_s, m_all, m_blk)
        m_a = logits(c + 2, sa_s, biased)
        return accumulate(c + 1, sb_s, m_all, m_b), m_a

    n_pairs = c_last // 2
    n_far_pairs = jnp.maximum(n_pairs - 1, 0)
    carry = (jnp.full((A_HEADS, QBLK), NEG, F32), logits(0, sa_s))
    carry = lax.fori_loop(0, n_far_pairs, functools.partial(att_body, False), carry)
    m_all, m_blk = lax.fori_loop(n_far_pairs, n_pairs, functools.partial(att_body, True), carry)
    mall_s[...] = m_all
    mblk_s[...] = m_blk

    @pl.when(c_last % 2 == 1)
    def _():
        m_b = logits(c_last, sb_s)
        m_all = accumulate(c_last - 1, sa_s, mall_s[...], mblk_s[...])
        accumulate(c_last, sb_s, m_all, m_b)

    @pl.when(c_last % 2 == 0)
    def _():
        accumulate(c_last, sa_s, mall_s[...], mblk_s[...])

    for h in range(A_HEADS):
        rows = slice(h * A_HEAD_DIM, (h + 1) * A_HEAD_DIM)
        oT_s[rows, :] = acc_s[h, 0:A_HEAD_DIM, :] * (1.0 / acc_s[h, A_HEAD_DIM:A_HEAD_DIM + 1, :])
    o_ref[0] = oT_s[...].T.astype(BF16)


def _dsa_call(rel_bias, ik, kk, vT, qT, iqT, iwT):
    B, S, _ = ik.shape
    nb = S // QBLK
    topk = min(TOPK_MAX, S // 4)
    bkt = jnp.asarray(_rel_bucket_table())
    body = functools.partial(_dsa_body, topk=topk, idx_bits=int(math.log2(S)), max_cnt=S // CNT_CHUNK)
    return pl.pallas_call(
        body,
        grid=(B, nb),
        in_specs=[pl.BlockSpec(memory_space=pltpu.SMEM),
                  pl.BlockSpec((2 * QBLK, QBLK), lambda b, i: (0, 0)),
                  pl.BlockSpec((1, S, IDX_DIM), lambda b, i: (b, 0, 0)),
                  pl.BlockSpec((1, A_WIDTH // LANES, S, LANES), lambda b, i: (b, 0, 0, 0)),
                  pl.BlockSpec((1, S // LANES, A_WIDTH, LANES), lambda b, i: (b, 0, 0, 0)),
                  pl.BlockSpec((1, A_WIDTH, QBLK), lambda b, i: (b, 0, i)),
                  pl.BlockSpec((1, IDX_HEADS * IDX_DIM, QBLK), lambda b, i: (b, 0, i)),
                  pl.BlockSpec((1, IDX_HEADS, QBLK), lambda b, i: (b, 0, i))],
        out_specs=pl.BlockSpec((1, QBLK, A_WIDTH), lambda b, i: (b, i, 0)),
        out_shape=jax.ShapeDtypeStruct((B, S, A_WIDTH), BF16),
        scratch_shapes=[pltpu.VMEM((S, QBLK), F32),
                        pltpu.VMEM((S, QBLK), F32),
                        pltpu.VMEM((8, QBLK), I32),
                        pltpu.VMEM((S, QBLK), F32),
                        pltpu.VMEM((A_HEADS, TBL_PAD + 3 * QBLK, QBLK), F32),
                        pltpu.VMEM((A_WIDTH, QBLK), F32),
                        pltpu.VMEM((1, QBLK), I32),
                        pltpu.VMEM((A_HEADS, LANES, QBLK), BF16),
                        pltpu.VMEM((A_HEADS, ATT_CHUNK, QBLK), F32),
                        pltpu.VMEM((A_HEADS, ATT_CHUNK, QBLK), F32),
                        pltpu.VMEM((A_HEADS, ACC_ROWS, QBLK), F32),
                        pltpu.VMEM((A_HEADS, QBLK), F32),
                        pltpu.VMEM((A_HEADS, QBLK), F32)],
        compiler_params=_params("parallel", "arbitrary"),
        name="dsa",
    )(rel_bias, bkt, ik, kk, vT, qT, iqT, iwT)


GLA_Q = (0, 256)
GLA_K = (256, 512)
GLA_V = (512, 1024)
GLA_R = (1024, 1536)


def _gla_body(gla_ref, glr_ref, wg_ref, bg_ref, ng_ref, o_ref, st_s):
    tg = gla_ref.shape[1]
    C = G_CHUNK

    @pl.when(pl.program_id(1) == 0)
    def _():
        st_s[...] = jnp.zeros_like(st_s)

    glr = glr_ref[0]
    wg = wg_ref[...]
    glr_hi, wg_hi = glr.astype(BF16), wg.astype(BF16)
    glr_lo = (glr - glr_hi.astype(F32)).astype(BF16)
    wg_lo = (wg - wg_hi.astype(F32)).astype(BF16)
    xg = (jnp.dot(glr_hi, wg_hi, preferred_element_type=F32) + jnp.dot(glr_hi, wg_lo, preferred_element_type=F32)
          + jnp.dot(glr_lo, wg_hi, preferred_element_type=F32)) + bg_ref[...]
    logg = -(jnp.maximum(-xg, 0.0) + jnp.log1p(jnp.exp(-jnp.abs(xg)))) * (1.0 / G_TAU)

    rt = lax.broadcasted_iota(I32, (tg, tg), 0)
    ct = lax.broadcasted_iota(I32, (tg, tg), 1)
    cum = jnp.where((rt // C == ct // C) & (rt >= ct), 1.0, 0.0).astype(BF16)
    logg_hi = logg.astype(BF16)
    logg_lo = (logg - logg_hi.astype(F32)).astype(BF16)
    bc_all = (jnp.dot(cum, logg_hi, preferred_element_type=F32)
              + jnp.dot(cum, logg_lo, preferred_element_type=F32))

    bl_all = jnp.concatenate([jnp.broadcast_to(bc_all[(ck + 1) * C - 1:(ck + 1) * C, :], (C, G_KW))
                              for ck in range(tg // C)], axis=0)

    q_all = gla_ref[0, :, GLA_Q[0]:GLA_Q[1]].astype(F32) * (G_DK ** -0.5)
    k_all = gla_ref[0, :, GLA_K[0]:GLA_K[1]].astype(F32)
    q_in_all = (q_all * jnp.exp(bc_all)).astype(BF16)
    k_st_all = (k_all * jnp.exp(bl_all - bc_all)).astype(BF16)
    q_rel_all = q_all * jnp.exp(bc_all - bl_all)
    decay_all = jnp.exp(bl_all)

    ri = lax.broadcasted_iota(I32, (C, C), 0)
    ci = lax.broadcasted_iota(I32, (C, C), 1)
    tril = ri >= ci
    lane_head = lax.broadcasted_iota(I32, (C, LANES), 1) // G_DK
    st_rows = lax.broadcasted_iota(I32, (2 * G_DV, LANES), 0) // G_DV
    st_cols = lax.broadcasted_iota(I32, (2 * G_DV, LANES), 1) // G_DK
    st_diag = st_rows == st_cols
    n_ck = tg // C
    n_p = G_HEADS // 2
    units = [(ck, p) for ck in range(n_ck) for p in range(n_p)]
    rows = lambda ck: slice(ck * C, (ck + 1) * C)
    lanes = lambda p: slice(p * LANES, (p + 1) * LANES)
    v_of = lambda ck, p: gla_ref[0, rows(ck), GLA_V[0] + p * 2 * G_DV:GLA_V[0] + (p + 1) * 2 * G_DV]

    att = {}
    for ck, p in units:
        k_st = k_st_all[rows(ck), lanes(p)]
        for sub in range(2):
            qm = jnp.where(lane_head == sub, q_rel_all[rows(ck), lanes(p)], 0.0).astype(BF16)
            a = lax.dot_general(qm, k_st, NT, preferred_element_type=F32)
            att[ck, p, sub] = jnp.where(tril, a, 0.0).astype(BF16)
    o_intra = {}
    uT = {}
    for ck, p in units:
        v = v_of(ck, p)
        for sub in range(2):
            o_intra[ck, p, sub] = jnp.dot(att[ck, p, sub], v[:, sub * G_DV:(sub + 1) * G_DV],
                                          preferred_element_type=F32)
        uT[ck, p] = lax.dot_general(v, k_st_all[rows(ck), lanes(p)], TN, preferred_element_type=F32)
    o_inter = {}
    for p in range(n_p):
        st = st_s[p]
        for ck in range(n_ck):
            o_inter[ck, p] = lax.dot_general(q_in_all[rows(ck), lanes(p)], st.astype(BF16), NT,
                                             preferred_element_type=F32)
            st = st * decay_all[ck * C:ck * C + 1, lanes(p)] + jnp.where(st_diag, uT[ck, p], 0.0)
        st_s[p] = st
    for ck, p in units:
        for sub in range(2):
            hd = 2 * p + sub
            o = o_intra[ck, p, sub] + o_inter[ck, p][:, sub * G_DV:(sub + 1) * G_DV]
            y = _ln(o) * ng_ref[:, hd * G_DV:(hd + 1) * G_DV]
            g = gla_ref[0, rows(ck), GLA_R[0] + hd * G_DV:GLA_R[0] + (hd + 1) * G_DV].astype(F32)
            o_ref[0, rows(ck), hd * G_DV:(hd + 1) * G_DV] = (y * (g * _sigmoid(g))).astype(BF16)


def _gla_call(gla, glr, wg, bg, ng, tg):
    B, S, _ = gla.shape
    const = lambda b, j: (0, 0)
    return pl.pallas_call(
        _gla_body,
        grid=(B, S // tg),
        in_specs=[pl.BlockSpec((1, tg, 1536), lambda b, j: (b, j, 0)),
                  pl.BlockSpec((1, tg, G_RANK), lambda b, j: (b, j, 0)),
                  pl.BlockSpec((G_RANK, G_KW), const),
                  pl.BlockSpec((1, G_KW), const),
                  pl.BlockSpec((1, G_VW), const)],
        out_specs=pl.BlockSpec((1, tg, G_VW), lambda b, j: (b, j, 0)),
        out_shape=jax.ShapeDtypeStruct((B, S, G_VW), BF16),
        scratch_shapes=[pltpu.VMEM((G_HEADS // 2, 2 * G_DV, LANES), F32)],
        compiler_params=_params("parallel", "arbitrary"),
        name="gla",
    )(gla, glr, wg, bg, ng)


ROUTER_ROWS = 40
ROUTER_E0 = 8


def _post_body(oa_ref, ob_ref, gates_ref, x_ref, gt1_ref, sh2_ref, sc2_ref, wa_ref, wb_ref, wo_ref,
               g1_ref, b1_ref, wr_ref, br_ref, x1_ref, h2_ref, gate_ref, grp_ref):
    tm = x_ref.shape[1]
    D = x_ref.shape[2]
    ya = jnp.dot(oa_ref[0], wa_ref[...], preferred_element_type=F32)
    yb = jnp.dot(ob_ref[0], wb_ref[...], preferred_element_type=F32)
    ga = gates_ref[0, :, 0:D].astype(F32)
    gb = gates_ref[0, :, D:2 * D].astype(F32)
    merged = _sigmoid(ga) * ya + _sigmoid(gb) * yb
    y = jnp.dot(merged.astype(BF16), wo_ref[...], preferred_element_type=F32)
    x1 = _ln(DN_ALPHA * x_ref[0] + gt1_ref[0] * y) * g1_ref[...] + b1_ref[...]
    x1_ref[0] = x1
    h2 = _ln(x1) * (1.0 + sc2_ref[0]) + sh2_ref[0]
    h2_hi = h2.astype(BF16)
    h2_ref[0] = h2_hi

    h2_lo = (h2 - h2_hi.astype(F32)).astype(BF16)
    wr = wr_ref[...]
    wr_hi = wr.astype(BF16)
    wr_lo = (wr - wr_hi.astype(F32)).astype(BF16)
    lt = (lax.dot_general(wr_hi, h2_hi, NT, preferred_element_type=F32)
          + lax.dot_general(wr_hi, h2_lo, NT, preferred_element_type=F32)
          + lax.dot_general(wr_lo, h2_hi, NT, preferred_element_type=F32)) + br_ref[...]
    gl = lt[0:N_GROUPS]
    gmax = jnp.max(gl, axis=0, keepdims=True)
    g_w = 1.0 / jnp.sum(jnp.exp(gl - gmax), axis=0, keepdims=True)
    r4 = lax.broadcasted_iota(I32, (N_GROUPS, tm), 0)
    g_idx = jnp.min(jnp.where(gl == gmax, r4, N_GROUPS), axis=0, keepdims=True)
    eg = jnp.zeros((EXPERTS_PER_GROUP, tm), F32)
    for g in range(N_GROUPS):
        lo = ROUTER_E0 + g * EXPERTS_PER_GROUP
        eg = jnp.where(g_idx == g, lt[lo:lo + EXPERTS_PER_GROUP], eg)
    r8 = lax.broadcasted_iota(I32, (EXPERTS_PER_GROUP, tm), 0)
    e1 = jnp.max(eg, axis=0, keepdims=True)
    i1 = jnp.min(jnp.where(eg == e1, r8, EXPERTS_PER_GROUP), axis=0, keepdims=True)
    eg2 = jnp.where(r8 == i1, -jnp.inf, eg)
    e2 = jnp.max(eg2, axis=0, keepdims=True)
    i2 = jnp.min(jnp.where(eg2 == e2, r8, EXPERTS_PER_GROUP), axis=0, keepdims=True)
    d = jnp.exp(e2 - e1)
    w1 = g_w / (1.0 + d)
    w2 = g_w * d / (1.0 + d)
    in_group = jnp.where(r8 == i1, w1, 0.0) + jnp.where(r8 == i2, w2, 0.0)
    blocks = [jnp.where(g_idx == g, in_group, 0.0) for g in range(N_GROUPS)]
    blocks.append(jnp.zeros((LANES - N_EXPERTS, tm), F32))
    gate_ref[...] = jnp.concatenate(blocks, axis=0).T
    r8g = lax.broadcasted_iota(I32, (8, tm), 0)
    grp_ref[...] = jnp.where(r8g == g_idx, 1.0, 0.0)


def _post_call(o_a, o_b, gates, x, gt1, sh2, sc2, wa, wb, wo, g1, b1, wr, br, tm):
    B, S, D = x.shape
    nt = S // tm
    const = lambda b, t: (0, 0)
    row = lambda b, t: (b, 0, 0)
    tile = lambda b, t: (b, t, 0)
    return pl.pallas_call(
        _post_body,
        grid=(B, nt),
        in_specs=[pl.BlockSpec((1, tm, A_WIDTH), tile),
                  pl.BlockSpec((1, tm, G_VW), tile),
                  pl.BlockSpec((1, tm, 2 * D), tile),
                  pl.BlockSpec((1, tm, D), tile),
                  pl.BlockSpec((1, 1, D), row),
                  pl.BlockSpec((1, 1, D), row),
                  pl.BlockSpec((1, 1, D), row),
                  pl.BlockSpec(wa.shape, const),
                  pl.BlockSpec(wb.shape, const),
                  pl.BlockSpec(wo.shape, const),
                  pl.BlockSpec((1, D), const),
                  pl.BlockSpec((1, D), const),
                  pl.BlockSpec(wr.shape, const),
                  pl.BlockSpec(br.shape, const)],
        out_specs=(pl.BlockSpec((1, tm, D), tile),
                   pl.BlockSpec((1, tm, D), tile),
                   pl.BlockSpec((tm, LANES), lambda b, t: (b * nt + t, 0)),
                   pl.BlockSpec((8, tm), lambda b, t: (0, b * nt + t))),
        out_shape=(jax.ShapeDtypeStruct((B, S, D), F32),
                   jax.ShapeDtypeStruct((B, S, D), BF16),
                   jax.ShapeDtypeStruct((B * S, LANES), F32),
                   jax.ShapeDtypeStruct((8, B * S), F32)),
        compiler_params=_params("parallel", "parallel"),
        name="post",
    )(o_a, o_b, gates, x, gt1, sh2, sc2, wa, wb, wo, g1, b1, wr, br)


MOE_EXPERTS_PER_STEP = 4
MOE_ROW_BLOCK = 128
MOE_PERM_ROWS = 256


def _moe_body(h2_ref, gate_ref, grp_ref, x1_ref, gt2_ref, w1_ref, w3_ref, w2_ref, g2_ref, b2_ref, o_ref,
              perm_s, hs_s, gs_s, ys_s, tri_s, seg_s):
    t = pl.program_id(0)
    s = pl.program_id(1)
    ne = MOE_EXPERTS_PER_STEP
    rb = MOE_ROW_BLOCK
    tm = h2_ref.shape[0]
    tm_pad = perm_s.shape[0]
    steps_per_group = EXPERTS_PER_GROUP // ne

    @pl.when((t == 0) & (s == 0))
    def _():
        r = lax.broadcasted_iota(I32, (tm, tm), 0)
        c = lax.broadcasted_iota(I32, (tm, tm), 1)
        tri_s[...] = jnp.where(r < c, 1.0, 0.0).astype(BF16)

    @pl.when(s == 0)
    def _():
        oh = grp_ref[...]
        rank = jnp.dot(oh.astype(BF16), tri_s[...], preferred_element_type=F32)
        cnt = jnp.sum(oh, axis=1, keepdims=True)
        blocks = jnp.floor((cnt + (rb - 1)) * (1.0 / rb))
        off = jnp.zeros((1, 1), F32)
        dest = jnp.zeros((1, tm), F32)
        for g in range(N_GROUPS):
            seg_s[2 * g] = jnp.sum(off).astype(I32)
            seg_s[2 * g + 1] = jnp.sum(blocks[g:g + 1, :]).astype(I32)
            dest = dest + oh[g:g + 1, :] * (off + rank[g:g + 1, :])
            off = off + blocks[g:g + 1, :] * rb
        dest_i = dest.astype(I32)
        gate = gate_ref[...]
        g_hi = gate.astype(BF16)
        g_lo = (gate - g_hi.astype(F32)).astype(BF16)
        h2 = h2_ref[...]
        for c in range(tm_pad // MOE_PERM_ROWS):
            rows = slice(c * MOE_PERM_ROWS, (c + 1) * MOE_PERM_ROWS)
            d_idx = c * MOE_PERM_ROWS + lax.broadcasted_iota(I32, (MOE_PERM_ROWS, tm), 0)
            perm = jnp.where(d_idx == dest_i, 1.0, 0.0).astype(BF16)
            perm_s[rows, :] = perm
            hs_s[rows, :] = jnp.dot(perm, h2, preferred_element_type=F32).astype(BF16)
            gs_s[rows, :] = (jnp.dot(perm, g_hi, preferred_element_type=F32)
                             + jnp.dot(perm, g_lo, preferred_element_type=F32))
        ys_s[...] = jnp.zeros_like(ys_s)

    g = s // steps_per_group
    row0 = seg_s[2 * g]
    nblk = seg_s[2 * g + 1]

    def block(i, carry):
        r0 = pl.multiple_of(row0 + i * rb, rb)
        x = hs_s[pl.ds(r0, rb), :]
        gsel = pltpu.roll(gs_s[pl.ds(r0, rb), :], (LANES - s * ne) % LANES, axis=1)
        hid = []
        for j in range(ne):
            a = jnp.dot(x, w1_ref[j], preferred_element_type=F32)
            b = jnp.dot(x, w3_ref[j], preferred_element_type=F32)
            hid.append((a * _sigmoid(a) * b * gsel[:, j:j + 1]).astype(BF16))
        ys_s[pl.ds(r0, rb), :] += jnp.dot(jnp.concatenate(hid, axis=1), w2_ref[...], preferred_element_type=F32)
        return carry

    lax.fori_loop(0, nblk, block, 0)

    @pl.when(s == pl.num_programs(1) - 1)
    def _():
        perm = perm_s[...]
        for c in range(o_ref.shape[1] // MOE_PERM_ROWS):
            cols = slice(c * MOE_PERM_ROWS, (c + 1) * MOE_PERM_ROWS)
            o_ref[:, cols] = lax.dot_general(perm, ys_s[:, cols].astype(BF16), TN, preferred_element_type=F32)
        z = DN_ALPHA * x1_ref[...] + gt2_ref[0] * o_ref[...]
        o_ref[...] = _ln(z) * g2_ref[...] + b2_ref[...]


def _moe_call(h2, gate, grp, x1, gt2, w1, w3, w2, g2, b2, tm, S):
    T, D = h2.shape
    ne = MOE_EXPERTS_PER_STEP
    nc = N_EXPERTS // ne
    tm_pad = tm + N_GROUPS * MOE_ROW_BLOCK
    tiles_per_seq = S // tm
    tile = lambda t, c: (t, 0)
    const = lambda t, c: (0, 0)
    return pl.pallas_call(
        _moe_body,
        grid=(T // tm, nc),
        in_specs=[pl.BlockSpec((tm, D), tile),
                  pl.BlockSpec((tm, LANES), tile),
                  pl.BlockSpec((8, tm), lambda t, c: (0, t)),
                  pl.BlockSpec((tm, D), tile),
                  pl.BlockSpec((1, 1, D), lambda t, c: (t // tiles_per_seq, 0, 0)),
                  pl.BlockSpec((ne, D, D_EXPERT), lambda t, c: (c, 0, 0)),
                  pl.BlockSpec((ne, D, D_EXPERT), lambda t, c: (c, 0, 0)),
                  pl.BlockSpec((ne * D_EXPERT, D), lambda t, c: (c, 0)),
                  pl.BlockSpec((1, D), const),
                  pl.BlockSpec((1, D), const)],
        out_specs=pl.BlockSpec((tm, D), tile),
        out_shape=jax.ShapeDtypeStruct((T, D), F32),
        scratch_shapes=[pltpu.VMEM((tm_pad, tm), BF16),
                        pltpu.VMEM((tm_pad, D), BF16),
                        pltpu.VMEM((tm_pad, LANES), F32),
                        pltpu.VMEM((tm_pad, D), F32),
                        pltpu.VMEM((tm, tm), BF16),
                        pltpu.SMEM((2 * N_GROUPS,), I32)],
        compiler_params=_params("arbitrary", "arbitrary"),
        name="moe",
    )(h2, gate, grp, x1, gt2, w1, w3, w2, g2, b2)


def _pick(n, pref):
    return pref if n % pref == 0 else n


def kernel(x, c, rel_bias, w_ada, b_ada, w_in, gla_w_gate, gla_b_gate, gla_norm_g, w_branch_a, w_branch_b, w_out, ln1_g, ln1_b, w_router_group, b_router_group, w_router_expert, b_router_expert, w_exp_gate, w_exp_up, w_exp_down, ln2_g, ln2_b):
    B, S, D = x.shape
    assert S % (2 * QBLK) == 0 and D == 1024 and w_ada.shape[0] == DEPTH == 1
    l = 0

    ada = _ada_call(c, w_ada[l], b_ada[l])
    sh1, sc1, gt1, sh2, sc2, gt2 = [ada[:, i * D:(i + 1) * D].reshape(B, 1, D) for i in range(6)]

    offs = np.concatenate([[0], np.cumsum(SPLIT_SIZES)])
    seg = lambda i: w_in[l][:, offs[i]:offs[i + 1]]
    (w_aq, w_ak, w_av, w_iq, w_ik, w_iw, w_gq, w_gk, w_gv, w_gr, w_glr, w_ga, w_gb) = [seg(i) for i in range(13)]
    pad = jnp.zeros((D, TOK_SMALL[1] - TOK_SMALL[0] - IDX_DIM - G_RANK), F32)
    w_tok = jnp.concatenate([w_ak, w_gq, w_gk, w_gv, w_gr, w_ga, w_gb, w_ik, w_glr, pad], axis=1).astype(BF16)
    w_ch = jnp.concatenate([w_aq, w_av, w_iq, w_iw], axis=1).T.astype(BF16)

    tm = _pick(S, 512)
    kk, gla, gates, ik, glr, qT, vT, iqT, iwT = _inproj_call(x, sh1, sc1, w_tok, w_ch, tm)

    o_a = _dsa_call(rel_bias, ik, kk, vT, qT, iqT, iwT)
    o_b = _gla_call(gla, glr, gla_w_gate[l], gla_b_gate[l].reshape(1, G_KW), gla_norm_g[l].reshape(1, G_VW),
                    _pick(S, 256))

    wr = jnp.zeros((ROUTER_ROWS, D), F32)
    wr = wr.at[0:N_GROUPS].set(w_router_group[l].T).at[ROUTER_E0:ROUTER_E0 + N_EXPERTS].set(w_router_expert[l].T)
    br = jnp.zeros((ROUTER_ROWS, 1), F32)
    br = br.at[0:N_GROUPS, 0].set(b_router_group[l]).at[ROUTER_E0:ROUTER_E0 + N_EXPERTS, 0].set(b_router_expert[l])
    x1, h2, gate, grp = _post_call(o_a, o_b, gates, x, gt1, sh2, sc2,
                              w_branch_a[l].astype(BF16), w_branch_b[l].astype(BF16), w_out[l].astype(BF16),
                              ln1_g[l].reshape(1, D), ln1_b[l].reshape(1, D), wr, br, tm)

    tm5 = _pick(S, 1024)
    out = _moe_call(h2.reshape(B * S, D), gate, grp, x1.reshape(B * S, D), gt2,
                    w_exp_gate[l].astype(BF16), w_exp_up[l].astype(BF16),
                    w_exp_down[l].astype(BF16).reshape(N_EXPERTS * D_EXPERT, D),
                    ln2_g[l].reshape(1, D), ln2_b[l].reshape(1, D), tm5, S)
    return out.reshape(B, S, D)
```
